```python
import math
import jax
import jax.numpy as jnp
from jax import lax
import numpy as np

D_MODEL = 1024
BATCH = 16
SEQ = 4096
DEPTH = 2

MEM_TOKENS = 256
X_HEADS = 4
X_HEAD_DIM = 64
MAX_POS_OFFSET = 1024

A_HEADS = 8
NOPE_DIM = 64
ROPE_DIM = 32
QK_DIM = NOPE_DIM + ROPE_DIM
V_DIM = 64
Q_LORA = 256
KV_LORA = 128
ROPE_THETA = 10000.0
Q_BLOCK = 128

B_HEADS = 8
SSD_HEAD_DIM = 64
D_INNER = B_HEADS * SSD_HEAD_DIM
SSD_GROUPS = 2
SSD_STATE = 128
CONV_K = 4
CONV_CH = D_INNER + 2 * SSD_GROUPS * SSD_STATE
CHUNK = 128

IN_SPLITS = (Q_LORA,
             Q_LORA + KV_LORA,
             Q_LORA + KV_LORA + ROPE_DIM,
             Q_LORA + KV_LORA + ROPE_DIM + D_INNER,
             Q_LORA + KV_LORA + ROPE_DIM + D_INNER + CONV_CH)
IN_DIM = Q_LORA + KV_LORA + ROPE_DIM + D_INNER + CONV_CH + B_HEADS
MIX_WIDTH = A_HEADS * V_DIM + D_INNER

POOL_WINDOWS = (2, 4, 8, 16)
POOL_GROUP = D_MODEL // 4

MOE_GROUPS = 4
EXPERTS_PER_GROUP = 8
N_EXPERTS = MOE_GROUPS * EXPERTS_PER_GROUP
TOP_K = 2
EXPERT_FF = 256
ROW_BLOCK = 128

RMS_EPS = 1e-6

kernel_name = "hybrid_mla_ssd_pool_hmoe_trunk"


def rms_norm(u, g):
    uf = u.astype(jnp.float32)
    y = uf * lax.rsqrt(jnp.mean(uf * uf, axis=-1, keepdims=True) + RMS_EPS)
    return (y * g.astype(jnp.float32)).astype(u.dtype)


def rope_angles(positions):
    inv = ROPE_THETA ** (-jnp.arange(0, ROPE_DIM // 2, dtype=jnp.float32) * 2.0 / ROPE_DIM)
    ang = positions.astype(jnp.float32)[..., None] * inv
    return jnp.cos(ang)[:, :, None, :], jnp.sin(ang)[:, :, None, :]


def apply_rope(u, cos, sin):
    u1, u2 = jnp.split(u.astype(jnp.float32), 2, axis=-1)
    return jnp.concatenate([u1 * cos - u2 * sin, u2 * cos + u1 * sin], axis=-1).astype(u.dtype)


def causal_block_attention(q, k, v):
    _, s_len, _, dq = q.shape
    scale = dq ** -0.5
    outs = []
    for i in range(s_len // Q_BLOCK):
        lo, hi = i * Q_BLOCK, (i + 1) * Q_BLOCK
        s = jnp.einsum('bqhd,bkhd->bhqk', q[:, lo:hi], k[:, :hi]).astype(jnp.float32) * scale
        mask = (lo + jnp.arange(Q_BLOCK))[:, None] >= jnp.arange(hi)[None, :]
        p = jax.nn.softmax(jnp.where(mask, s, -jnp.inf), axis=-1).astype(v.dtype)
        outs.append(jnp.einsum('bhqk,bkhd->bqhd', p, v[:, :hi]))
    return jnp.concatenate(outs, axis=1)


def causal_depthwise_conv(u, w, b):
    c = u.shape[-1]
    y = lax.conv_general_dilated(u, w[:, None, :].astype(u.dtype), window_strides=(1,),
                                 padding=[(w.shape[0] - 1, 0)],
                                 dimension_numbers=('NWC', 'WIO', 'NWC'),
                                 feature_group_count=c)
    return y + b.astype(u.dtype)


def ssd_chunked_scan(x, dt, a, bm, cm):
    b, L, H, P = x.shape
    G, N = bm.shape[2], bm.shape[3]
    R = H // G
    nc = L // CHUNK
    xdt = (x * dt[..., None]).reshape(b, nc, CHUNK, G, R, P)
    adt = (dt * a).reshape(b, nc, CHUNK, G, R).transpose(0, 3, 4, 1, 2)
    bc = bm.reshape(b, nc, CHUNK, G, N)
    cc = cm.reshape(b, nc, CHUNK, G, N)
    a_cs = jnp.cumsum(adt, axis=-1)
    tri = jnp.tril(jnp.ones((CHUNK, CHUNK), dtype=bool))
    decay = jnp.exp(jnp.where(tri, a_cs[..., :, None] - a_cs[..., None, :], -jnp.inf))
    cb = jnp.einsum('bctgn,bcsgn->bgcts', cc, bc)
    y_diag = jnp.einsum('bgcts,bgrcts,bcsgrp->bctgrp', cb, decay, xdt)
    decay_to_end = jnp.exp(a_cs[..., -1:] - a_cs)
    chunk_states = jnp.einsum('bcsgn,bgrcs,bcsgrp->cbgrpn', bc, decay_to_end, xdt)
    chunk_decay = jnp.exp(a_cs[..., -1]).transpose(3, 0, 1, 2)

    def step(state, inp):
        s_c, d_c = inp
        return state * d_c[..., None, None] + s_c, state

    _, prev = lax.scan(step, jnp.zeros(chunk_states.shape[1:], x.dtype), (chunk_states, chunk_decay))
    y_off = jnp.einsum('bctgn,cbgrpn,bgrct->bctgrp', cc, prev, jnp.exp(a_cs))
    return (y_diag + y_off).reshape(b, L, H, P)


def latent_attention_ssd_mixer(h, cos, sin, w_in, q_lat_norm, w_uq, kv_lat_norm, w_ukv, q_norm, k_norm,
                               conv_w, conv_b, dt_bias, a_log, d_skip, ssd_norm, w_out):
    bsz, s_len, _ = h.shape
    proj = h @ w_in
    q_lat, kv_lat, k_rope, z, xbc, dt_raw = jnp.split(proj, IN_SPLITS, axis=-1)

    q = (rms_norm(q_lat, q_lat_norm) @ w_uq).reshape(bsz, s_len, A_HEADS, QK_DIM)
    kv = (rms_norm(kv_lat, kv_lat_norm) @ w_ukv).reshape(bsz, s_len, A_HEADS, NOPE_DIM + V_DIM)
    k_nope, v = kv[..., :NOPE_DIM], kv[..., NOPE_DIM:]
    k = jnp.concatenate([k_nope, jnp.broadcast_to(k_rope[:, :, None, :],
                                                  (bsz, s_len, A_HEADS, ROPE_DIM))], axis=-1)
    q = rms_norm(q, q_norm)
    k = rms_norm(k, k_norm)
    q = jnp.concatenate([q[..., :NOPE_DIM], apply_rope(q[..., NOPE_DIM:], cos, sin)], axis=-1)
    k = jnp.concatenate([k[..., :NOPE_DIM], apply_rope(k[..., NOPE_DIM:], cos, sin)], axis=-1)
    attn = causal_block_attention(q, k, v).reshape(bsz, s_len, A_HEADS * V_DIM)

    xbc = jax.nn.silu(causal_depthwise_conv(xbc, conv_w, conv_b))
    xs, bm, cm = jnp.split(xbc, [D_INNER, D_INNER + SSD_GROUPS * SSD_STATE], axis=-1)
    xs_h = xs.reshape(bsz, s_len, B_HEADS, SSD_HEAD_DIM).astype(jnp.float32)
    dt = jax.nn.softplus(dt_raw.astype(jnp.float32) + dt_bias.astype(jnp.float32))
    a = -jnp.exp(a_log.astype(jnp.float32))
    y = ssd_chunked_scan(xs_h, dt, a,
                         bm.reshape(bsz, s_len, SSD_GROUPS, SSD_STATE).astype(jnp.float32),
                         cm.reshape(bsz, s_len, SSD_GROUPS, SSD_STATE).astype(jnp.float32))
    y = y + xs_h * d_skip.astype(jnp.float32)[:, None]
    y = y.reshape(bsz, s_len, D_INNER).astype(h.dtype)
    y = rms_norm(y * jax.nn.silu(z), ssd_norm)

    return jnp.concatenate([attn, y], axis=-1) @ w_out


def multiscale_pool_mixer(h, pool_w, pool_b, pool_scale):
    bsz, s_len, d = h.shape
    hf = h.astype(jnp.float32)
    cs = jnp.pad(jnp.cumsum(hf, axis=1), ((0, 0), (1, 0), (0, 0)))
    pos_count = jnp.arange(1, s_len + 1, dtype=jnp.float32)[None, :, None]
    diffs = []
    for g, w in enumerate(POOL_WINDOWS):
        sl = slice(g * POOL_GROUP, (g + 1) * POOL_GROUP)
        csg = cs[:, :, sl]
        win_sum = csg[:, 1:] - jnp.pad(csg[:, :s_len + 1 - w], ((0, 0), (w - 1, 0), (0, 0)))
        diffs.append(win_sum / jnp.minimum(pos_count, float(w)) - hf[:, :, sl])
    dlt = jnp.stack(diffs, axis=2).astype(h.dtype)
    y = jnp.einsum('bsgc,gcd->bsgd', dlt, pool_w).reshape(bsz, s_len, d) + pool_b
    return y * pool_scale


def memory_cross_attention(hq, mem_n, wq, wkv, q_norm, k_norm, wo):
    bsz, s_len, _ = hq.shape
    m_len = mem_n.shape[1]
    q = rms_norm((hq @ wq).reshape(bsz, s_len, X_HEADS, X_HEAD_DIM), q_norm)
    kv = (mem_n @ wkv).reshape(bsz, m_len, 2, X_HEADS, X_HEAD_DIM)
    k = rms_norm(kv[:, :, 0], k_norm)
    v = kv[:, :, 1]
    s = jnp.einsum('bshd,bmhd->bhsm', q, k).astype(jnp.float32) * (X_HEAD_DIM ** -0.5)
    p = jax.nn.softmax(s, axis=-1).astype(v.dtype)
    o = jnp.einsum('bhsm,bmhd->bshd', p, v).reshape(bsz, s_len, X_HEADS * X_HEAD_DIM)
    return o @ wo


def routed_expert_ffn(hf, expert_idx, gates, w_gate, w_up, w_down):
    n_tok, d = hf.shape
    n_exp = w_gate.shape[0]
    m = n_tok * TOP_K
    flat_e = expert_idx.reshape(m)
    order = jnp.argsort(flat_e)
    sorted_e = flat_e[order]
    tok = (order // TOP_K).astype(jnp.int32)
    counts = jnp.bincount(flat_e, length=n_exp)
    padded = (counts + ROW_BLOCK - 1) // ROW_BLOCK * ROW_BLOCK
    pad_end = jnp.cumsum(padded)
    pad_start = pad_end - padded
    start = jnp.cumsum(counts) - counts
    dest = pad_start[sorted_e] + jnp.arange(m) - start[sorted_e]
    n_blocks = -(-m // ROW_BLOCK) + n_exp
    row_tok = jnp.zeros((n_blocks * ROW_BLOCK,), jnp.int32).at[dest].set(tok)
    block_e = jnp.minimum(jnp.searchsorted(pad_end, jnp.arange(n_blocks) * ROW_BLOCK, side='right'),
                          n_exp - 1)
    xb = hf[row_tok].reshape(n_blocks, ROW_BLOCK, d)

    def expert_block(args):
        xblk, e = args
        return (jax.nn.silu(xblk @ w_gate[e]) * (xblk @ w_up[e])) @ w_down[e]

    yb = lax.map(expert_block, (xb, block_e)).reshape(n_blocks * ROW_BLOCK, d)
    y_assign = yb[dest] * gates.reshape(m)[order][:, None]
    return jax.ops.segment_sum(y_assign, tok, num_segments=n_tok)


def hierarchical_moe(h, rg_w, rg_b, re_w, re_b, w_gate, w_up, w_down):
    bsz, s_len, d = h.shape
    n_tok = bsz * s_len
    hf = h.reshape(n_tok, d)
    g_prob = jax.nn.softmax((hf @ rg_w).astype(jnp.float32) + rg_b.astype(jnp.float32), axis=-1)
    g_p, g_idx = lax.top_k(g_prob, 1)
    e_logits = ((hf @ re_w).astype(jnp.float32) + re_b.astype(jnp.float32)).reshape(
        n_tok, MOE_GROUPS, EXPERTS_PER_GROUP)
    e_logits = jnp.take_along_axis(e_logits, g_idx[:, :, None], axis=1)[:, 0]
    e_p, e_idx = lax.top_k(jax.nn.softmax(e_logits, axis=-1), TOP_K)
    gates = g_p * e_p / jnp.sum(e_p, axis=-1, keepdims=True)
    expert_idx = (g_idx * EXPERTS_PER_GROUP + e_idx).astype(jnp.int32)
    y = routed_expert_ffn(hf, expert_idx, gates.astype(h.dtype), w_gate, w_up, w_down)
    return y.reshape(bsz, s_len, d)


def setup_inputs(seed: int = 0) -> dict:
    key = jax.random.key(seed)
    keys = iter(jax.random.split(key, 48))

    def nrm(shape, scale):
        return scale * jax.random.normal(next(keys), shape, jnp.float32)

    def gain(shape):
        return 1.0 + 0.05 * jax.random.normal(next(keys), shape, jnp.float32)

    ne, no = (DEPTH + 1) // 2, DEPTH // 2
    x = nrm((BATCH, SEQ, D_MODEL), 1.0)
    mem = nrm((BATCH, MEM_TOKENS, D_MODEL), 1.0)
    positions = (jax.random.randint(next(keys), (BATCH, 1), 0, MAX_POS_OFFSET, jnp.int32)
                 + jnp.arange(SEQ, dtype=jnp.int32)[None, :])
    ln_mix = gain((DEPTH, D_MODEL))
    w_in = nrm((ne, D_MODEL, IN_DIM), D_MODEL ** -0.5)
    q_lat_norm = gain((ne, Q_LORA))
    w_uq = nrm((ne, Q_LORA, A_HEADS * QK_DIM), Q_LORA ** -0.5)
    kv_lat_norm = gain((ne, KV_LORA))
    w_ukv = nrm((ne, KV_LORA, A_HEADS * (NOPE_DIM + V_DIM)), KV_LORA ** -0.5)
    q_norm = gain((ne, QK_DIM))
    k_norm = gain((ne, QK_DIM))
    conv_w = nrm((ne, CONV_K, CONV_CH), CONV_K ** -0.5)
    conv_b = nrm((ne, CONV_CH), 0.02)
    dt0 = jnp.exp(jax.random.uniform(next(keys), (ne, B_HEADS), jnp.float32,
                                     math.log(1e-3), math.log(1e-1)))
    dt_bias = dt0 + jnp.log(-jnp.expm1(-dt0))
    a_log = jnp.log(jax.random.uniform(next(keys), (ne, B_HEADS), jnp.float32, 1.0, 16.0))
    d_skip = gain((ne, B_HEADS))
    ssd_norm = gain((ne, D_INNER))
    w_out = nrm((ne, MIX_WIDTH, D_MODEL), MIX_WIDTH ** -0.5)
    pool_w = nrm((no, len(POOL_WINDOWS), POOL_GROUP, POOL_GROUP), POOL_GROUP ** -0.5)
    pool_b = nrm((no, D_MODEL), 0.02)
    pool_scale = gain((no, D_MODEL))
    ln_xq = gain((DEPTH, D_MODEL))
    ln_mem = gain((DEPTH, D_MODEL))
    xq_w = nrm((DEPTH, D_MODEL, X_HEADS * X_HEAD_DIM), D_MODEL ** -0.5)
    xkv_w = nrm((DEPTH, D_MODEL, 2 * X_HEADS * X_HEAD_DIM), D_MODEL ** -0.5)
    xq_norm = gain((DEPTH, X_HEAD_DIM))
    xk_norm = gain((DEPTH, X_HEAD_DIM))
    xo_w = nrm((DEPTH, X_HEADS * X_HEAD_DIM, D_MODEL), (X_HEADS * X_HEAD_DIM) ** -0.5)
    ln_ffn = gain((DEPTH, D_MODEL))
    rg_w = nrm((DEPTH, D_MODEL, MOE_GROUPS), D_MODEL ** -0.5)
    rg_b = nrm((DEPTH, MOE_GROUPS), 0.01)
    re_w = nrm((DEPTH, D_MODEL, N_EXPERTS), D_MODEL ** -0.5)
    re_b = nrm((DEPTH, N_EXPERTS), 0.01)
    exp_w_gate = nrm((DEPTH, N_EXPERTS, D_MODEL, EXPERT_FF), D_MODEL ** -0.5)
    exp_w_up = nrm((DEPTH, N_EXPERTS, D_MODEL, EXPERT_FF), D_MODEL ** -0.5)
    exp_w_down = nrm((DEPTH, N_EXPERTS, EXPERT_FF, D_MODEL), EXPERT_FF ** -0.5)
    return {"x": x, "mem": mem, "positions": positions, "ln_mix": ln_mix,
            "w_in": w_in, "q_lat_norm": q_lat_norm, "w_uq": w_uq, "kv_lat_norm": kv_lat_norm,
            "w_ukv": w_ukv, "q_norm": q_norm, "k_norm": k_norm, "conv_w": conv_w, "conv_b": conv_b,
            "dt_bias": dt_bias, "a_log": a_log, "d_skip": d_skip, "ssd_norm": ssd_norm, "w_out": w_out,
            "pool_w": pool_w, "pool_b": pool_b, "pool_scale": pool_scale,
            "ln_xq": ln_xq, "ln_mem": ln_mem, "xq_w": xq_w, "xkv_w": xkv_w, "xq_norm": xq_norm,
            "xk_norm": xk_norm, "xo_w": xo_w, "ln_ffn": ln_ffn, "rg_w": rg_w, "rg_b": rg_b,
            "re_w": re_w, "re_b": re_b, "exp_w_gate": exp_w_gate, "exp_w_up": exp_w_up,
            "exp_w_down": exp_w_down}


def reference(x, mem, positions, ln_mix, w_in, q_lat_norm, w_uq, kv_lat_norm, w_ukv, q_norm, k_norm,
              conv_w, conv_b, dt_bias, a_log, d_skip, ssd_norm, w_out, pool_w, pool_b, pool_scale,
              ln_xq, ln_mem, xq_w, xkv_w, xq_norm, xk_norm, xo_w, ln_ffn, rg_w, rg_b, re_w, re_b,
              exp_w_gate, exp_w_up, exp_w_down):
    cos, sin = rope_angles(positions)
    for layer in range(DEPTH):
        j = layer // 2
        h = rms_norm(x, ln_mix[layer])
        if layer % 2 == 0:
            mixed = latent_attention_ssd_mixer(h, cos, sin, w_in[j], q_lat_norm[j], w_uq[j],
                                               kv_lat_norm[j], w_ukv[j], q_norm[j], k_norm[j],
                                               conv_w[j], conv_b[j], dt_bias[j], a_log[j], d_skip[j],
                                               ssd_norm[j], w_out[j])
        else:
            mixed = multiscale_pool_mixer(h, pool_w[j], pool_b[j], pool_scale[j])
        x = x + mixed
        x = x + memory_cross_attention(rms_norm(x, ln_xq[layer]), rms_norm(mem, ln_mem[layer]),
                                       xq_w[layer], xkv_w[layer], xq_norm[layer], xk_norm[layer],
                                       xo_w[layer])
        x = x + hierarchical_moe(rms_norm(x, ln_ffn[layer]), rg_w[layer], rg_b[layer], re_w[layer],
                                 re_b[layer], exp_w_gate[layer], exp_w_up[layer], exp_w_down[layer])
    return x
```

```python
import functools

import numpy as np
import jax
import jax.numpy as jnp
from jax import lax
from jax.experimental import pallas as pl
from jax.experimental.pallas import tpu as pltpu

F32 = jnp.float32
BF16 = jnp.bfloat16
U32 = jnp.uint32
I32 = jnp.int32

RMS_EPS = 1e-6
ROPE_THETA = 10000.0

D_MODEL = 1024
X_HEADS, X_HEAD_DIM = 4, 64
A_HEADS, NOPE_DIM, ROPE_DIM, V_DIM = 8, 64, 32, 64
QK_DIM = NOPE_DIM + ROPE_DIM
Q_LORA, KV_LORA = 256, 128
B_HEADS, SSD_HEAD_DIM, SSD_GROUPS, SSD_STATE, CONV_K, CHUNK = 8, 64, 2, 128, 4, 128
D_INNER = B_HEADS * SSD_HEAD_DIM
CONV_CH = D_INNER + 2 * SSD_GROUPS * SSD_STATE
POOL_WINDOWS = (2, 4, 8, 16)
POOL_GROUP = D_MODEL // 4
MOE_GROUPS, EXPERTS_PER_GROUP, TOP_K, EXPERT_FF = 4, 8, 2, 256
N_EXPERTS = MOE_GROUPS * EXPERTS_PER_GROUP

LANES = 128
HEAD_LANES = LANES
HALF_LANES = LANES // 2
ROPE_HALF = ROPE_DIM // 2
NOPE_HALF = NOPE_DIM // 2
POOL_CARRY = 16
CONV_CARRY = 8
VMEM_LIMIT = 56 * 1024 * 1024


def _cparams(sem):
    return pltpu.CompilerParams(dimension_semantics=sem, vmem_limit_bytes=VMEM_LIMIT)


def _rms(u, g):
    return u * lax.rsqrt(jnp.mean(u * u, axis=-1, keepdims=True) + RMS_EPS) * g


def _sigmoid(u):
    return 1.0 / (1.0 + jnp.exp(-u))


def _dot(a, b):
    return jnp.dot(a, b, preferred_element_type=F32)


def _dot_nt(a, b):
    return lax.dot_general(a, b, (((1,), (1,)), ((), ())), preferred_element_type=F32)


def _head_lane(d):
    if d < NOPE_HALF:
        return d
    if d < NOPE_DIM:
        return HALF_LANES + (d - NOPE_HALF)
    r = d - NOPE_DIM
    if r < ROPE_HALF:
        return NOPE_HALF + r
    return HALF_LANES + NOPE_HALF + (r - ROPE_HALF)


def _gather_cols(w, idx):
    w_ext = jnp.concatenate([w, jnp.zeros(w.shape[:-1] + (1,), w.dtype)], axis=-1)
    idx = np.where(idx < 0, w.shape[-1], idx)
    return jnp.take(w_ext, jnp.asarray(idx, dtype=jnp.int32), axis=-1)


IN_W = 2 * D_MODEL
_OFF_QLAT, _OFF_KVLAT, _OFF_MISC, _OFF_Z, _OFF_XBC = 0, 256, 384, 512, 1024


def _win_col_index():
    idx = np.full((IN_W,), -1, np.int64)
    idx[_OFF_QLAT:_OFF_QLAT + Q_LORA] = np.arange(Q_LORA)
    idx[_OFF_KVLAT:_OFF_KVLAT + KV_LORA] = Q_LORA + np.arange(KV_LORA)
    rope0 = Q_LORA + KV_LORA
    for r in range(ROPE_DIM):
        idx[_OFF_MISC + _head_lane(NOPE_DIM + r)] = rope0 + r
    z0 = rope0 + ROPE_DIM
    idx[_OFF_Z:_OFF_Z + D_INNER] = z0 + np.arange(D_INNER)
    xbc0 = z0 + D_INNER
    idx[_OFF_XBC:_OFF_XBC + CONV_CH] = xbc0 + np.arange(CONV_CH)
    dt0 = xbc0 + CONV_CH
    idx[_OFF_MISC:_OFF_MISC + B_HEADS] = dt0 + np.arange(B_HEADS)
    return idx


def _head_col_index(per_head, offset, count):
    idx = np.full((A_HEADS * HEAD_LANES,), -1, np.int64)
    for h in range(A_HEADS):
        for d in range(count):
            idx[h * HEAD_LANES + _head_lane(d)] = h * per_head + offset + d
    return idx


def _front_even_kernel(x_ref, cos_ref, sin_ref, ln_ref, win_ref, qln_ref, wuq_ref, kvln_ref, wuk_ref, wuv_ref,
                       qg_ref, kg_ref, q_ref, k_ref, v_ref, z_ref, xbc_ref, misc_ref):
    x = x_ref[0]
    h = _rms(x, ln_ref[...]).astype(BF16)
    proj = _dot(h, win_ref[...])
    misc = proj[:, _OFF_MISC:_OFF_Z]
    z_ref[0] = proj[:, _OFF_Z:_OFF_XBC].astype(BF16)
    xbc_ref[0] = proj[:, _OFF_XBC:].astype(BF16)
    misc_ref[0] = misc
    ql = _rms(proj[:, _OFF_QLAT:_OFF_KVLAT], qln_ref[...]).astype(BF16)
    kvl = _rms(proj[:, _OFF_KVLAT:_OFF_MISC], kvln_ref[...]).astype(BF16)
    q = _dot(ql, wuq_ref[...])
    kn = _dot(kvl, wuk_ref[...])
    v_ref[0] = _dot(kvl, wuv_ref[...]).astype(BF16)
    lane = lax.broadcasted_iota(I32, (1, HEAD_LANES), 1)
    rope_lane = ((lane >= NOPE_HALF) & (lane < NOPE_HALF + ROPE_HALF)) | (
        (lane >= HALF_LANES + NOPE_HALF) & (lane < HALF_LANES + NOPE_HALF + ROPE_HALF))
    krope = jnp.where(rope_lane, misc, 0.0)
    kr_ss = jnp.sum(krope * krope, axis=-1, keepdims=True)
    cos_t, sin_t = cos_ref[0], sin_ref[0]
    qg, kg = qg_ref[...], kg_ref[...]
    scale = QK_DIM ** -0.5
    for hd in range(A_HEADS):
        sl = slice(hd * HEAD_LANES, (hd + 1) * HEAD_LANES)
        qs = q[:, sl]
        ss = jnp.sum(qs * qs, axis=-1, keepdims=True)
        qn = qs * lax.rsqrt(ss * (1.0 / QK_DIM) + RMS_EPS) * qg
        qr = qn * cos_t + pltpu.roll(qn, HALF_LANES, 1) * sin_t
        q_ref[0, hd] = (qr * scale).astype(BF16)
        ks = kn[:, sl] + krope
        ss = jnp.sum(kn[:, sl] * kn[:, sl], axis=-1, keepdims=True) + kr_ss
        kk = ks * lax.rsqrt(ss * (1.0 / QK_DIM) + RMS_EPS) * kg
        kr = kk * cos_t + pltpu.roll(kk, HALF_LANES, 1) * sin_t
        k_ref[0, hd] = kr.astype(BF16)


def _front_even(x, cos_t, sin_t, ln, win, qln, wuq, kvln, wuk, wuv, qg, kg, tm):
    b, s, d = x.shape
    grid = (b, s // tm)
    row = lambda i, j: (i, j, 0)
    fixed2 = lambda i, j: (0, 0)
    hw = A_HEADS * HEAD_LANES
    return pl.pallas_call(
        _front_even_kernel,
        grid=grid,
        in_specs=[
            pl.BlockSpec((1, tm, d), row),
            pl.BlockSpec((1, tm, HEAD_LANES), row),
            pl.BlockSpec((1, tm, HEAD_LANES), row),
            pl.BlockSpec((1, d), fixed2),
            pl.BlockSpec((d, IN_W), fixed2),
            pl.BlockSpec((1, Q_LORA), fixed2),
            pl.BlockSpec((Q_LORA, hw), fixed2),
            pl.BlockSpec((1, KV_LORA), fixed2),
            pl.BlockSpec((KV_LORA, hw), fixed2),
            pl.BlockSpec((KV_LORA, A_HEADS * V_DIM), fixed2),
            pl.BlockSpec((1, HEAD_LANES), fixed2),
            pl.BlockSpec((1, HEAD_LANES), fixed2),
        ],
        out_specs=[
            pl.BlockSpec((1, A_HEADS, tm, HEAD_LANES), lambda i, j: (i, 0, j, 0)),
            pl.BlockSpec((1, A_HEADS, tm, HEAD_LANES), lambda i, j: (i, 0, j, 0)),
            pl.BlockSpec((1, tm, A_HEADS * V_DIM), row),
            pl.BlockSpec((1, tm, D_INNER), row),
            pl.BlockSpec((1, tm, CONV_CH), row),
            pl.BlockSpec((1, tm, HEAD_LANES), row),
        ],
        out_shape=[
            jax.ShapeDtypeStruct((b, A_HEADS, s, HEAD_LANES), BF16),
            jax.ShapeDtypeStruct((b, A_HEADS, s, HEAD_LANES), BF16),
            jax.ShapeDtypeStruct((b, s, A_HEADS * V_DIM), BF16),
            jax.ShapeDtypeStruct((b, s, D_INNER), BF16),
            jax.ShapeDtypeStruct((b, s, CONV_CH), BF16),
            jax.ShapeDtypeStruct((b, s, HEAD_LANES), F32),
        ],
        compiler_params=_cparams(("parallel", "parallel")),
        name="front_even",
    )(x, cos_t, sin_t, ln, win, qln, wuq, kvln, wuk, wuv, qg, kg)


HEADS_PER_STEP = HEAD_LANES // V_DIM


def _attn_kernel(q_ref, k_ref, v_ref, o_ref, *, tq):
    qi = pl.program_id(2)
    row = lax.broadcasted_iota(I32, (tq, tq), 0)
    col = lax.broadcasted_iota(I32, (tq, tq), 1)
    outs = []
    for hh in range(HEADS_PER_STEP):
        q = q_ref[0, hh]

        def step(j, carry, masked):
            m, l, acc = carry
            kj = k_ref[0, hh, pl.ds(j * tq, tq), :]
            s = _dot_nt(q, kj)
            if masked:
                s = jnp.where(row >= col, s, -jnp.inf)
            m_new = jnp.maximum(m, jnp.max(s, axis=-1, keepdims=True))
            p = jnp.exp(s - m_new)
            alpha = jnp.exp(m - m_new)
            l = alpha * l + jnp.sum(p, axis=-1, keepdims=True)
            vj = v_ref[0, pl.ds(j * tq, tq), :]
            acc = alpha * acc + _dot(p.astype(BF16), vj)
            return m_new, l, acc

        init = (jnp.full((tq, 1), -jnp.inf, F32), jnp.zeros((tq, 1), F32), jnp.zeros((tq, HEAD_LANES), F32))
        carry = lax.fori_loop(0, qi, functools.partial(step, masked=False), init)
        _, l, acc = step(qi, carry, True)
        outs.append(acc / l)
    lane = lax.broadcasted_iota(I32, (1, HEAD_LANES), 1)
    out = outs[0]
    for hh in range(1, HEADS_PER_STEP):
        out = jnp.where(lane >= hh * V_DIM, outs[hh], out)
    o_ref[0] = out.astype(BF16)


def _attention(q, k, v, tq):
    b, nh, s, _ = q.shape
    grid = (b, nh // HEADS_PER_STEP, s // tq)
    return pl.pallas_call(
        functools.partial(_attn_kernel, tq=tq),
        grid=grid,
        in_specs=[
            pl.BlockSpec((1, HEADS_PER_STEP, tq, HEAD_LANES), lambda i, h, j: (i, h, j, 0)),
            pl.BlockSpec((1, HEADS_PER_STEP, s, HEAD_LANES), lambda i, h, j: (i, h, 0, 0)),
            pl.BlockSpec((1, s, HEAD_LANES), lambda i, h, j: (i, 0, h)),
        ],
        out_specs=pl.BlockSpec((1, tq, HEAD_LANES), lambda i, h, j: (i, j, h)),
        out_shape=jax.ShapeDtypeStruct((b, s, nh * V_DIM), BF16),
        compiler_params=_cparams(("parallel", "parallel", "parallel")),
        name="mla_attention",
    )(q, k, v)


def _ssd_kernel(xbc_ref, misc_ref, z_ref, cw_ref, cb_ref, dtb_ref, alog_ref, dskip_ref, gn_ref, y_ref,
                state_ref, carry_ref):
    c = pl.program_id(1)
    t = CHUNK

    @pl.when(c == 0)
    def _():
        state_ref[...] = jnp.zeros_like(state_ref)
        carry_ref[...] = jnp.zeros_like(carry_ref)

    xr = xbc_ref[0].astype(F32)
    xcat = jnp.concatenate([carry_ref[...], xr], axis=0)
    carry_ref[...] = xr[t - CONV_CARRY:, :]
    conv = jnp.zeros((t, CONV_CH), F32) + cb_ref[...]
    for kk in range(CONV_K):
        sh = CONV_K - 1 - kk
        shifted = xcat if sh == 0 else pltpu.roll(xcat, sh, 0)
        conv = conv + cw_ref[kk:kk + 1, :] * shifted[CONV_CARRY:, :]
    xa = conv * _sigmoid(conv)
    xs = xa[:, :D_INNER]
    gw = SSD_GROUPS * SSD_STATE
    bmat = xa[:, D_INNER:D_INNER + gw]
    cmat = xa[:, D_INNER + gw:]

    u = misc_ref[0] + dtb_ref[...]
    dt = jnp.maximum(u, 0.0) + jnp.log(1.0 + jnp.exp(-jnp.abs(u)))
    a = -jnp.exp(alog_ref[...])
    lane = lax.broadcasted_iota(I32, (1, LANES), 1)
    adt = jnp.where(lane < B_HEADS, dt * a, 0.0)
    rowi = lax.broadcasted_iota(I32, (t, LANES), 0)
    acs = adt
    sh = 1
    while sh < t:
        acs = acs + jnp.where(rowi >= sh, pltpu.roll(acs, sh, 0), 0.0)
        sh *= 2
    acs_t = acs.T
    tri = lax.broadcasted_iota(I32, (t, t), 0) >= lax.broadcasted_iota(I32, (t, t), 1)

    rep = B_HEADS // SSD_GROUPS
    ys = []
    for g in range(SSD_GROUPS):
        bg = bmat[:, g * SSD_STATE:(g + 1) * SSD_STATE]
        cg = cmat[:, g * SSD_STATE:(g + 1) * SSD_STATE]
        bg16, cg16 = bg.astype(BF16), cg.astype(BF16)
        cb = _dot_nt(cg16, bg16)
        bg_t = bg.T
        for r in range(rep):
            hd = g * rep + r
            col = acs[:, hd:hd + 1]
            rw = acs_t[hd:hd + 1, :]
            last = acs_t[hd:hd + 1, t - 1:t]
            decay = jnp.exp(jnp.where(tri, col - rw, -jnp.inf))
            xh = xs[:, hd * SSD_HEAD_DIM:(hd + 1) * SSD_HEAD_DIM]
            xdt = (xh * dt[:, hd:hd + 1]).astype(BF16)
            y_diag = _dot((cb * decay).astype(BF16), xdt)
            prev = state_ref[hd]
            y_off = _dot(cg16, prev.astype(BF16)) * jnp.exp(col)
            new_state = _dot((bg_t * jnp.exp(last - rw)).astype(BF16), xdt)
            state_ref[hd] = prev * jnp.exp(last) + new_state
            ys.append(y_diag + y_off)
    y = jnp.concatenate(ys, axis=1) + xs * dskip_ref[...]
    zf = z_ref[0].astype(F32)
    y = y * (zf * _sigmoid(zf))
    y_ref[0] = _rms(y, gn_ref[...]).astype(BF16)


def _ssd(xbc, misc, z, cw, cb, dtb, alog, dskip, gn):
    b, s, _ = xbc.shape
    grid = (b, s // CHUNK)
    row = lambda i, j: (i, j, 0)
    fixed2 = lambda i, j: (0, 0)
    return pl.pallas_call(
        _ssd_kernel,
        grid=grid,
        in_specs=[
            pl.BlockSpec((1, CHUNK, CONV_CH), row),
            pl.BlockSpec((1, CHUNK, LANES), row),
            pl.BlockSpec((1, CHUNK, D_INNER), row),
            pl.BlockSpec((CONV_K, CONV_CH), fixed2),
            pl.BlockSpec((1, CONV_CH), fixed2),
            pl.BlockSpec((1, LANES), fixed2),
            pl.BlockSpec((1, LANES), fixed2),
            pl.BlockSpec((1, D_INNER), fixed2),
            pl.BlockSpec((1, D_INNER), fixed2),
        ],
        out_specs=pl.BlockSpec((1, CHUNK, D_INNER), row),
        out_shape=jax.ShapeDtypeStruct((b, s, D_INNER), BF16),
        scratch_shapes=[
            pltpu.VMEM((B_HEADS, SSD_STATE, SSD_HEAD_DIM), F32),
            pltpu.VMEM((CONV_CARRY, CONV_CH), F32),
        ],
        compiler_params=_cparams(("parallel", "arbitrary")),
        name="ssd_scan",
    )(xbc, misc, z, cw, cb, dtb, alog, dskip, gn)


XW = X_HEADS * X_HEAD_DIM


def _mem_kv_kernel(mem_ref, ln_ref, wkv_ref, kg_ref, hsum_ref, kbd_ref, vbd_ref):
    m = mem_ref.shape[1]
    mn = _rms(mem_ref[0], ln_ref[...]).astype(BF16)
    kv = _dot(mn, wkv_ref[...])
    k, v = kv[:, :XW], kv[:, XW:]
    ss = _dot((k * k).astype(BF16), hsum_ref[...])
    kn = (k * lax.rsqrt(ss * (1.0 / X_HEAD_DIM) + RMS_EPS) * kg_ref[...]).astype(BF16)
    v16 = v.astype(BF16)
    head_of_lane = lax.shift_right_arithmetic(lax.broadcasted_iota(I32, (1, XW), 1), jnp.int32(_LOG2_XHD))
    for hd in range(X_HEADS):
        keep = head_of_lane == hd
        kbd_ref[0, hd * m:(hd + 1) * m, :] = jnp.where(keep, kn, jnp.zeros_like(kn))
        vbd_ref[0, hd * m:(hd + 1) * m, :] = jnp.where(keep, v16, jnp.zeros_like(v16))


def _mem_kv(mem, ln, wkv, kg, hsum):
    b, m, d = mem.shape
    fixed2 = lambda i: (0, 0)
    return pl.pallas_call(
        _mem_kv_kernel,
        grid=(b,),
        in_specs=[
            pl.BlockSpec((1, m, d), lambda i: (i, 0, 0)),
            pl.BlockSpec((1, d), fixed2),
            pl.BlockSpec((d, 2 * XW), fixed2),
            pl.BlockSpec((1, XW), fixed2),
            pl.BlockSpec((XW, XW), fixed2),
        ],
        out_specs=[
            pl.BlockSpec((1, X_HEADS * m, XW), lambda i: (i, 0, 0)),
            pl.BlockSpec((1, X_HEADS * m, XW), lambda i: (i, 0, 0)),
        ],
        out_shape=[
            jax.ShapeDtypeStruct((b, X_HEADS * m, XW), BF16),
            jax.ShapeDtypeStruct((b, X_HEADS * m, XW), BF16),
        ],
        compiler_params=_cparams(("parallel",)),
        name="mem_kv",
    )(mem, ln, wkv, kg, hsum)


ROUTE_LANES = LANES
_GROUP_LANE0 = 0
_EXPERT_LANE0 = MOE_GROUPS
_LOG2_EPG = EXPERTS_PER_GROUP.bit_length() - 1
_LOG2_XHD = X_HEAD_DIM.bit_length() - 1


def _pack_bf16_pairs(v):
    w = v.shape[1] // 2
    r = v.astype(BF16).astype(F32)
    hi = lax.bitcast_convert_type(r[:, :w], U32)
    lo = lax.bitcast_convert_type(r[:, w:], U32)
    return (hi & jnp.uint32(0xFFFF0000)) | (lo >> jnp.uint32(16))


def _unpack_bf16_pairs(u):
    hi = lax.bitcast_convert_type(u & jnp.uint32(0xFFFF0000), F32)
    lo = lax.bitcast_convert_type(u << jnp.uint32(16), F32)
    return hi, lo


def _tail(x1, kbd_ref, vbd_ref, lnq_ref, wq_ref, qg_ref, hsum_ref, wo_ref, lnf_ref, rwh_ref, rwl_ref, rb_ref,
          ltri_ref, x2_ref, hfp_ref, route_ref, cnt_ref):
    tm = x1.shape[0]
    m = kbd_ref.shape[1] // X_HEADS
    hq = _rms(x1, lnq_ref[...]).astype(BF16)
    q = _dot(hq, wq_ref[...])
    ss = _dot((q * q).astype(BF16), hsum_ref[...])
    qn = (q * lax.rsqrt(ss * (1.0 / X_HEAD_DIM) + RMS_EPS) * qg_ref[...] * (X_HEAD_DIM ** -0.5)).astype(BF16)
    s = _dot_nt(qn, kbd_ref[0])
    ps = []
    for hd in range(X_HEADS):
        sh = s[:, hd * m:(hd + 1) * m]
        e = jnp.exp(sh - jnp.max(sh, axis=-1, keepdims=True))
        ps.append((e / jnp.sum(e, axis=-1, keepdims=True)).astype(BF16))
    o = _dot(jnp.concatenate(ps, axis=1), vbd_ref[0]).astype(BF16)
    x2 = x1 + _dot(o, wo_ref[...])
    x2_ref[0] = x2

    hf = _rms(x2, lnf_ref[...])
    hf_hi = hf.astype(BF16)
    hfp_ref[0] = _pack_bf16_pairs(hf)
    hf_lo = (hf - hf_hi.astype(F32)).astype(BF16)
    logits = _dot(hf_hi, rwh_ref[...]) + _dot(hf_hi, rwl_ref[...]) + _dot(hf_lo, rwh_ref[...]) + rb_ref[...]

    lane_i = lax.broadcasted_iota(I32, (tm, ROUTE_LANES), 1)
    lane = lane_i.astype(F32)
    big = float(ROUTE_LANES)
    neg = -jnp.inf
    gl = jnp.where(lane_i < MOE_GROUPS, logits, neg)
    gmax = jnp.max(gl, axis=-1, keepdims=True)
    gsum = jnp.sum(jnp.exp(gl - gmax), axis=-1, keepdims=True)
    g_p = 1.0 / gsum
    g_idx = jnp.min(jnp.where(gl == gmax, lane, big), axis=-1, keepdims=True)
    e_lane = lane_i - _EXPERT_LANE0
    grp_of_lane = lax.shift_right_arithmetic(e_lane, jnp.int32(_LOG2_EPG)).astype(F32)
    in_grp = (e_lane >= 0) & (e_lane < N_EXPERTS) & (grp_of_lane == g_idx)
    el = jnp.where(in_grp, logits, neg)
    emax = jnp.max(el, axis=-1, keepdims=True)
    idx1 = jnp.min(jnp.where(el == emax, lane, big), axis=-1, keepdims=True)
    el2 = jnp.where(lane == idx1, neg, el)
    emax2 = jnp.max(el2, axis=-1, keepdims=True)
    idx2 = jnp.min(jnp.where(el2 == emax2, lane, big), axis=-1, keepdims=True)
    r2 = jnp.exp(emax2 - emax)
    gate1 = g_p / (1.0 + r2)
    gate2 = g_p * r2 / (1.0 + r2)
    e1 = idx1 - float(_EXPERT_LANE0)
    e2 = idx2 - float(_EXPERT_LANE0)

    oh1 = (lane == e1).astype(F32)
    oh2 = (lane == e2).astype(F32)
    both = oh1 + oh2
    before = _dot(ltri_ref[...], both.astype(BF16))
    rank1 = jnp.sum(before * oh1, axis=-1, keepdims=True)
    rank2 = jnp.sum(before * oh2, axis=-1, keepdims=True)
    cnt_ref[0] = jnp.broadcast_to(jnp.sum(both, axis=0, keepdims=True), cnt_ref.shape[1:])

    route = jnp.where(lane == 0, e1, 0.0)
    route = jnp.where(lane == 1, e2, route)
    route = jnp.where(lane == 2, gate1, route)
    route = jnp.where(lane == 3, gate2, route)
    route = jnp.where(lane == 4, rank1, route)
    route = jnp.where(lane == 5, rank2, route)
    route_ref[0] = route


_TAIL_IN = 12


def _post_even_kernel(x_ref, a_ref, y_ref, wout_ref, *rest):
    tail_in, outs = rest[:_TAIL_IN], rest[_TAIL_IN:]
    half = wout_ref.shape[0] // 2
    x1 = x_ref[0] + _dot(a_ref[0], wout_ref[:half, :]) + _dot(y_ref[0], wout_ref[half:, :])
    _tail(x1, *tail_in, *outs)


def _post_pool_kernel(x_ref, ln_ref, pw_ref, pb_ref, ps_ref, *rest):
    tail_in, outs, carry_ref = rest[:_TAIL_IN], rest[_TAIL_IN:-1], rest[-1]
    j = pl.program_id(1)
    tm = x_ref.shape[1]

    @pl.when(j == 0)
    def _():
        carry_ref[...] = jnp.zeros_like(carry_ref)

    x = x_ref[0]
    h = _rms(x, ln_ref[...])
    pos = (j * tm + 1 + lax.broadcasted_iota(I32, (tm, 1), 0)).astype(F32)
    mixed = []
    for g, w in enumerate(POOL_WINDOWS):
        sl = slice(g * POOL_GROUP, (g + 1) * POOL_GROUP)
        hg = h[:, sl]
        acc = jnp.concatenate([carry_ref[:, sl], hg], axis=0)
        sh = 1
        while sh < w:
            acc = acc + pltpu.roll(acc, sh, 0)
            sh *= 2
        win = acc[POOL_CARRY:, :]
        dlt = win / jnp.minimum(pos, float(w)) - hg
        mixed.append(_dot(dlt.astype(BF16), pw_ref[g]))
    carry_ref[...] = h[tm - POOL_CARRY:, :]
    y = (jnp.concatenate(mixed, axis=1) + pb_ref[...]) * ps_ref[...]
    _tail(x + y, *tail_in, *outs)


def _post(kind, front_args, front_specs, tail_args, b, s, tm, scratch):
    d = D_MODEL
    m4 = tail_args[0].shape[1]
    row = lambda i, j: (i, j, 0)
    fixed2 = lambda i, j: (0, 0)
    per_b = lambda i, j: (i, 0, 0)
    tail_specs = [
        pl.BlockSpec((1, m4, XW), per_b),
        pl.BlockSpec((1, m4, XW), per_b),
        pl.BlockSpec((1, d), fixed2),
        pl.BlockSpec((d, XW), fixed2),
        pl.BlockSpec((1, XW), fixed2),
        pl.BlockSpec((XW, XW), fixed2),
        pl.BlockSpec((XW, d), fixed2),
        pl.BlockSpec((1, d), fixed2),
        pl.BlockSpec((d, ROUTE_LANES), fixed2),
        pl.BlockSpec((d, ROUTE_LANES), fixed2),
        pl.BlockSpec((1, ROUTE_LANES), fixed2),
        pl.BlockSpec((tm, tm), fixed2),
    ]
    nt = s // tm
    kernel = _post_even_kernel if kind == "even" else _post_pool_kernel
    return pl.pallas_call(
        kernel,
        grid=(b, nt),
        in_specs=front_specs + tail_specs,
        out_specs=[
            pl.BlockSpec((1, tm, d), row),
            pl.BlockSpec((1, tm, d // 2), row),
            pl.BlockSpec((1, tm, ROUTE_LANES), row),
            pl.BlockSpec((1, 8, ROUTE_LANES), lambda i, j: (i * nt + j, 0, 0)),
        ],
        out_shape=[
            jax.ShapeDtypeStruct((b, s, d), F32),
            jax.ShapeDtypeStruct((b, s, d // 2), U32),
            jax.ShapeDtypeStruct((b, s, ROUTE_LANES), F32),
            jax.ShapeDtypeStruct((b * nt, 8, ROUTE_LANES), F32),
        ],
        scratch_shapes=scratch,
        compiler_params=_cparams(("parallel", "arbitrary")),
        name="post_" + kind,
    )(*front_args, *tail_args)


FFN_ROWS = 256
DISPATCH_TOKENS = 1024
COMBINE_TOKENS = 512
DMA_UNROLL = 8


def _dispatch_kernel(dest_ref, tail_ref, hfp_ref, xb_ref, zero_ref, sem):
    i = pl.program_id(0)
    td = hfp_ref.shape[0]

    @pl.when(i == 0)
    def _():
        zero_ref[...] = jnp.zeros_like(zero_ref)

        def zero_copy(e):
            start = pl.multiple_of(tail_ref[e], FFN_ROWS)
            return pltpu.make_async_copy(zero_ref, xb_ref.at[pl.ds(start, FFN_ROWS)], sem)

        for e in range(N_EXPERTS):
            @pl.when(tail_ref[e] >= 0)
            def _():
                zero_copy(e).start()
        for e in range(N_EXPERTS):
            @pl.when(tail_ref[e] >= 0)
            def _():
                zero_copy(e).wait()

    def issue(t, carry):
        for k in range(TOP_K):
            dst = dest_ref[0, 0, TOP_K * t + k]
            pltpu.make_async_copy(hfp_ref.at[pl.ds(t, 1)], xb_ref.at[pl.ds(dst, 1)], sem).start()
        return carry

    lax.fori_loop(0, td, issue, 0, unroll=DMA_UNROLL)

    def drain(t, carry):
        for k in range(TOP_K):
            pltpu.make_async_copy(hfp_ref.at[pl.ds(0, 1)], xb_ref.at[pl.ds(0, 1)], sem).wait()
        return carry

    lax.fori_loop(0, td, drain, 0, unroll=DMA_UNROLL)


def _dispatch(dest, seg_tail, hfp, n_rows, td):
    n, w = hfp.shape
    nsteps = n // td
    return pl.pallas_call(
        _dispatch_kernel,
        grid=(nsteps,),
        in_specs=[
            pl.BlockSpec((1, 1, TOP_K * td), lambda i: (i, 0, 0), memory_space=pltpu.SMEM),
            pl.BlockSpec(memory_space=pltpu.SMEM),
            pl.BlockSpec((td, w), lambda i: (i, 0)),
        ],
        out_specs=pl.BlockSpec(memory_space=pl.ANY),
        scratch_shapes=[pltpu.VMEM((FFN_ROWS, w), U32), pltpu.SemaphoreType.DMA],
        out_shape=jax.ShapeDtypeStruct((n_rows, w), U32),
        compiler_params=_cparams(("arbitrary",)),
        name="moe_dispatch",
    )(dest.reshape(nsteps, 1, TOP_K * td), seg_tail, hfp)


def _ffn_kernel(be_ref, bi_ref, xb_ref, wg_ref, wu_ref, wd_ref, yb_ref, wg_s, wu_s, wd_s):
    i = pl.program_id(0)
    changed = jnp.logical_or(i == 0, be_ref[i] != be_ref[jnp.maximum(i - 1, 0)])

    @pl.when(changed)
    def _():
        wg_s[...] = wg_ref[0, 0].astype(BF16)
        wu_s[...] = wu_ref[0, 0].astype(BF16)
        wd_s[...] = wd_ref[0, 0].astype(BF16)

    hi, lo = _unpack_bf16_pairs(xb_ref[...])
    hi, lo = hi.astype(BF16), lo.astype(BF16)
    half = wg_s.shape[0] // 2
    gate = _dot(hi, wg_s[:half, :]) + _dot(lo, wg_s[half:, :])
    up = _dot(hi, wu_s[:half, :]) + _dot(lo, wu_s[half:, :])
    act = (gate * _sigmoid(gate) * up).astype(BF16)
    yb_ref[...] = _pack_bf16_pairs(_dot(act, wd_s[...]))


def _expert_ffn(block_e, block_i, xb, wg, wu, wd, layer):
    n_rows, w = xb.shape
    d, ff = wg.shape[2], wg.shape[3]
    n_blk = n_rows // FFN_ROWS
    return pl.pallas_call(
        _ffn_kernel,
        grid_spec=pltpu.PrefetchScalarGridSpec(
            num_scalar_prefetch=2,
            grid=(n_blk,),
            in_specs=[
                pl.BlockSpec((FFN_ROWS, w), lambda i, be, bi: (bi[i], 0)),
                pl.BlockSpec((1, 1, d, ff), lambda i, be, bi: (layer, be[i], 0, 0)),
                pl.BlockSpec((1, 1, d, ff), lambda i, be, bi: (layer, be[i], 0, 0)),
                pl.BlockSpec((1, 1, ff, d), lambda i, be, bi: (layer, be[i], 0, 0)),
            ],
            out_specs=pl.BlockSpec((FFN_ROWS, w), lambda i, be, bi: (bi[i], 0)),
            scratch_shapes=[pltpu.VMEM((d, ff), BF16), pltpu.VMEM((d, ff), BF16), pltpu.VMEM((ff, d), BF16)],
        ),
        out_shape=jax.ShapeDtypeStruct((n_rows, w), U32),
        compiler_params=_cparams(("arbitrary",)),
        name="moe_expert_ffn",
    )(block_e, block_i, xb, wg, wu, wd)


def _combine_kernel(dest_ref, x_ref, route_ref, yb_ref, o_ref, ybuf, sem):
    tc = x_ref.shape[0]

    def issue(t, carry):
        for k in range(TOP_K):
            src = dest_ref[0, 0, TOP_K * t + k]
            pltpu.make_async_copy(yb_ref.at[pl.ds(src, 1)], ybuf.at[k, pl.ds(t, 1)], sem).start()
        return carry

    lax.fori_loop(0, tc, issue, 0, unroll=DMA_UNROLL)

    def drain(t, carry):
        for k in range(TOP_K):
            pltpu.make_async_copy(yb_ref.at[pl.ds(0, 1)], ybuf.at[0, pl.ds(0, 1)], sem).wait()
        return carry

    lax.fori_loop(0, tc, drain, 0, unroll=DMA_UNROLL)

    half = x_ref.shape[1] // 2
    route = route_ref[...]
    g1, g2 = route[:, 2:3], route[:, 3:4]
    h1, l1 = _unpack_bf16_pairs(ybuf[0])
    h2, l2 = _unpack_bf16_pairs(ybuf[1])
    o_ref[:, :half] = x_ref[:, :half] + (h1 * g1 + h2 * g2)
    o_ref[:, half:] = x_ref[:, half:] + (l1 * g1 + l2 * g2)


def _combine(dest, x2, route, yb, tc):
    n, d = x2.shape
    w = yb.shape[1]
    nsteps = n // tc
    return pl.pallas_call(
        _combine_kernel,
        grid=(nsteps,),
        in_specs=[
            pl.BlockSpec((1, 1, TOP_K * tc), lambda i: (i, 0, 0), memory_space=pltpu.SMEM),
            pl.BlockSpec((tc, d), lambda i: (i, 0)),
            pl.BlockSpec((tc, ROUTE_LANES), lambda i: (i, 0)),
            pl.BlockSpec(memory_space=pl.ANY),
        ],
        out_specs=pl.BlockSpec((tc, d), lambda i: (i, 0)),
        scratch_shapes=[pltpu.VMEM((TOP_K, tc, w), U32), pltpu.SemaphoreType.DMA],
        out_shape=jax.ShapeDtypeStruct((n, d), F32),
        compiler_params=_cparams(("arbitrary",)),
        name="moe_combine",
    )(dest.reshape(nsteps, 1, TOP_K * tc), x2, route, yb)


def _moe(x2, hfp, route, counts, wg, wu, wd, layer, tm):
    b, s, d = x2.shape
    n = b * s
    route = route.reshape(n, ROUTE_LANES)
    cnt = counts[:, 0, :N_EXPERTS].astype(I32)
    total = jnp.sum(cnt, axis=0)
    padded = (total + FFN_ROWS - 1) // FFN_ROWS * FFN_ROWS
    pad_end = jnp.cumsum(padded)
    pad_start = pad_end - padded
    tile_base = pad_start[None, :] + jnp.cumsum(cnt, axis=0) - cnt
    experts = route[:, 0:TOP_K].astype(I32)
    ranks = route[:, 4:4 + TOP_K].astype(I32)
    base_tok = jnp.repeat(tile_base, tm, axis=0)
    dest = jnp.take_along_axis(base_tok, experts, axis=1) + ranks
    dest = dest.reshape(n * TOP_K)
    n_blk = (n * TOP_K) // FFN_ROWS + N_EXPERTS
    n_rows = n_blk * FFN_ROWS
    blk_row = jnp.arange(n_blk, dtype=I32) * FFN_ROWS
    used = pad_end[-1] // FFN_ROWS
    block_i = jnp.minimum(jnp.arange(n_blk, dtype=I32), used - 1).astype(I32)
    block_e = jnp.minimum(jnp.searchsorted(pad_end, block_i * FFN_ROWS, side="right"), N_EXPERTS - 1).astype(I32)
    del blk_row
    seg_tail = jnp.where(padded > 0, pad_end - FFN_ROWS, -1).astype(I32)

    td = min(DISPATCH_TOKENS, n)
    tc = min(COMBINE_TOKENS, n)
    xb = _dispatch(dest, seg_tail, hfp.reshape(n, d // 2), n_rows, td)
    yb = _expert_ffn(block_e, block_i, xb, wg, wu, wd, layer)
    out = _combine(dest, x2.reshape(n, d), route, yb, tc)
    return out.reshape(b, s, d)


def _rope_tables(positions):
    inv = ROPE_THETA ** (-jnp.arange(0, ROPE_DIM // 2, dtype=F32) * 2.0 / ROPE_DIM)
    ang = positions.astype(F32)[..., None] * inv
    cos, sin = jnp.cos(ang), jnp.sin(ang)
    one = jnp.ones(ang.shape[:-1] + (NOPE_HALF,), F32)
    zero_n = jnp.zeros(ang.shape[:-1] + (NOPE_HALF,), F32)
    zero_p = jnp.zeros(ang.shape[:-1] + (HALF_LANES - NOPE_HALF - ROPE_HALF,), F32)
    cos_t = jnp.concatenate([one, cos, zero_p, one, cos, zero_p], axis=-1)
    sin_t = jnp.concatenate([zero_n, -sin, zero_p, zero_n, sin, zero_p], axis=-1)
    return cos_t, sin_t


def _head_gain(g):
    idx = np.full((HEAD_LANES,), -1, np.int64)
    for dd in range(QK_DIM):
        idx[_head_lane(dd)] = dd
    return _gather_cols(g[None, :], idx)


def kernel(x, mem, positions, ln_mix, w_in, q_lat_norm, w_uq, kv_lat_norm, w_ukv, q_norm, k_norm, conv_w, conv_b,
           dt_bias, a_log, d_skip, ssd_norm, w_out, pool_w, pool_b, pool_scale, ln_xq, ln_mem, xq_w, xkv_w, xq_norm,
           xk_norm, xo_w, ln_ffn, rg_w, rg_b, re_w, re_b, exp_w_gate, exp_w_up, exp_w_down):
    b, s, d = x.shape
    depth = ln_mix.shape[0]
    tm = min(512, s)
    tq = min(512, s)
    assert d == D_MODEL and s % tm == 0 and s % CHUNK == 0 and tm >= POOL_CARRY

    cos_t, sin_t = _rope_tables(positions)
    hsum = jnp.asarray(np.kron(np.eye(X_HEADS), np.ones((X_HEAD_DIM, X_HEAD_DIM))), BF16)
    ltri = jnp.asarray(np.tril(np.ones((tm, tm)), -1), BF16)
    row2 = lambda v: v.reshape(1, -1)
    lane_pad = lambda v: jnp.pad(v, (0, LANES - v.shape[0])).reshape(1, LANES)

    for layer in range(depth):
        j = layer // 2
        kbd, vbd = _mem_kv(mem, row2(ln_mem[layer]), xkv_w[layer].astype(BF16),
                           row2(jnp.tile(xk_norm[layer], X_HEADS)), hsum)
        rw = jnp.pad(jnp.concatenate([rg_w[layer], re_w[layer]], axis=1),
                     ((0, 0), (0, ROUTE_LANES - MOE_GROUPS - N_EXPERTS)))
        rw_hi = rw.astype(BF16)
        rw_lo = (rw - rw_hi.astype(F32)).astype(BF16)
        rb = lane_pad(jnp.concatenate([rg_b[layer], re_b[layer]]))
        tail_args = [kbd, vbd, row2(ln_xq[layer]), xq_w[layer].astype(BF16), row2(jnp.tile(xq_norm[layer], X_HEADS)),
                     hsum, xo_w[layer].astype(BF16), row2(ln_ffn[layer]), rw_hi, rw_lo, rb, ltri]
        row = lambda i, jj: (i, jj, 0)
        fixed2 = lambda i, jj: (0, 0)
        if layer % 2 == 0:
            win = _gather_cols(w_in[j], _win_col_index()).astype(BF16)
            wuq = _gather_cols(w_uq[j], _head_col_index(QK_DIM, 0, QK_DIM)).astype(BF16)
            wuk = _gather_cols(w_ukv[j], _head_col_index(NOPE_DIM + V_DIM, 0, NOPE_DIM)).astype(BF16)
            v_idx = np.concatenate([h * (NOPE_DIM + V_DIM) + NOPE_DIM + np.arange(V_DIM) for h in range(A_HEADS)])
            wuv = _gather_cols(w_ukv[j], v_idx).astype(BF16)
            q, k, v, z, xbc, misc = _front_even(
                x, cos_t, sin_t, row2(ln_mix[layer]), win, row2(q_lat_norm[j]), wuq, row2(kv_lat_norm[j]), wuk, wuv,
                _head_gain(q_norm[j]), _head_gain(k_norm[j]), tm)
            attn = _attention(q, k, v, tq)
            y = _ssd(xbc, misc, z, conv_w[j], row2(conv_b[j]), lane_pad(dt_bias[j]), lane_pad(a_log[j]),
                     row2(jnp.repeat(d_skip[j], SSD_HEAD_DIM)), row2(ssd_norm[j]))
            half = A_HEADS * V_DIM
            front_args = [x, attn, y, w_out[j].astype(BF16)]
            front_specs = [pl.BlockSpec((1, tm, d), row), pl.BlockSpec((1, tm, half), row),
                           pl.BlockSpec((1, tm, D_INNER), row), pl.BlockSpec((half + D_INNER, d), fixed2)]
            x2, hfp, route, counts = _post("even", front_args, front_specs, tail_args, b, s, tm, [])
        else:
            front_args = [x, row2(ln_mix[layer]), pool_w[j].astype(BF16), row2(pool_b[j]), row2(pool_scale[j])]
            front_specs = [pl.BlockSpec((1, tm, d), row), pl.BlockSpec((1, d), fixed2),
                           pl.BlockSpec((len(POOL_WINDOWS), POOL_GROUP, POOL_GROUP), lambda i, jj: (0, 0, 0)),
                           pl.BlockSpec((1, d), fixed2), pl.BlockSpec((1, d), fixed2)]
            x2, hfp, route, counts = _post("pool", front_args, front_specs, tail_args, b, s, tm,
                                           [pltpu.VMEM((POOL_CARRY, d), F32)])
        x = _moe(x2, hfp, route, counts, exp_w_gate, exp_w_up, exp_w_down, layer, tm)
    return x
```

```python
import functools

import numpy as np
import jax
import jax.numpy as jnp
from jax import lax
from jax.experimental import pallas as pl
from jax.experimental.pallas import tpu as pltpu

F32 = jnp.float32
BF16 = jnp.bfloat16
U32 = jnp.uint32
I32 = jnp.int32

RMS_EPS = 1e-6
ROPE_THETA = 10000.0

D_MODEL = 1024
X_HEADS, X_HEAD_DIM = 4, 64
A_HEADS, NOPE_DIM, ROPE_DIM, V_DIM = 8, 64, 32, 64
QK_DIM = NOPE_DIM + ROPE_DIM
Q_LORA, KV_LORA = 256, 128
B_HEADS, SSD_HEAD_DIM, SSD_GROUPS, SSD_STATE, CONV_K, CHUNK = 8, 64, 2, 128, 4, 128
D_INNER = B_HEADS * SSD_HEAD_DIM
CONV_CH = D_INNER + 2 * SSD_GROUPS * SSD_STATE
POOL_WINDOWS = (2, 4, 8, 16)
POOL_GROUP = D_MODEL // 4
MOE_GROUPS, EXPERTS_PER_GROUP, TOP_K, EXPERT_FF = 4, 8, 2, 256
N_EXPERTS = MOE_GROUPS * EXPERTS_PER_GROUP

LANES = 128
HEAD_LANES = LANES
HALF_LANES = LANES // 2
ROPE_HALF = ROPE_DIM // 2
NOPE_HALF = NOPE_DIM // 2
POOL_CARRY = 16
CONV_CARRY = 8
VMEM_LIMIT = 56 * 1024 * 1024


def _cparams(sem):
    return pltpu.CompilerParams(dimension_semantics=sem, vmem_limit_bytes=VMEM_LIMIT)


def _rms(u, g):
    return u * lax.rsqrt(jnp.mean(u * u, axis=-1, keepdims=True) + RMS_EPS) * g


def _sigmoid(u):
    return 1.0 / (1.0 + jnp.exp(-u))


def _dot(a, b):
    return jnp.dot(a, b, preferred_element_type=F32)


def _dot_nt(a, b):
    return lax.dot_general(a, b, (((1,), (1,)), ((), ())), preferred_element_type=F32)


def _head_lane(d):
    if d < NOPE_HALF:
        return d
    if d < NOPE_DIM:
        return HALF_LANES + (d - NOPE_HALF)
    r = d - NOPE_DIM
    if r < ROPE_HALF:
        return NOPE_HALF + r
    return HALF_LANES + NOPE_HALF + (r - ROPE_HALF)


def _gather_cols(w, idx):
    w_ext = jnp.concatenate([w, jnp.zeros(w.shape[:-1] + (1,), w.dtype)], axis=-1)
    idx = np.where(idx < 0, w.shape[-1], idx)
    return jnp.take(w_ext, jnp.asarray(idx, dtype=jnp.int32), axis=-1)


IN_W = 2 * D_MODEL
_OFF_QLAT, _OFF_KVLAT, _OFF_MISC, _OFF_Z, _OFF_XBC = 0, 256, 384, 512, 1024


def _win_col_index():
    idx = np.full((IN_W,), -1, np.int64)
    idx[_OFF_QLAT:_OFF_QLAT + Q_LORA] = np.arange(Q_LORA)
    idx[_OFF_KVLAT:_OFF_KVLAT + KV_LORA] = Q_LORA + np.arange(KV_LORA)
    rope0 = Q_LORA + KV_LORA
    for r in range(ROPE_DIM):
        idx[_OFF_MISC + _head_lane(NOPE_DIM + r)] = rope0 + r
    z0 = rope0 + ROPE_DIM
    idx[_OFF_Z:_OFF_Z + D_INNER] = z0 + np.arange(D_INNER)
    xbc0 = z0 + D_INNER
    idx[_OFF_XBC:_OFF_XBC + CONV_CH] = xbc0 + np.arange(CONV_CH)
    dt0 = xbc0 + CONV_CH
    idx[_OFF_MISC:_OFF_MISC + B_HEADS] = dt0 + np.arange(B_HEADS)
    return idx


def _head_col_index(per_head, offset, count):
    idx = np.full((A_HEADS * HEAD_LANES,), -1, np.int64)
    for h in range(A_HEADS):
        for d in range(count):
            idx[h * HEAD_LANES + _head_lane(d)] = h * per_head + offset + d
    return idx


SCORE_PAD_LANE = NOPE_HALF + ROPE_HALF
ONES_LANE = V_DIM
LOG2E = 1.4426950408889634


def _front_even_kernel(x_ref, pos_ref, invf_ref, ln_ref, win_ref, qln_ref, wuq_ref, kvln_ref, wuk_ref, wuv_ref,
                       qg_ref, kg_ref, koff_ref, q_ref, k_ref, v_ref, z_ref, xbc_ref, misc_ref):
    x = x_ref[0]
    h = _rms(x, ln_ref[...]).astype(BF16)
    proj = _dot(h, win_ref[...])
    misc = proj[:, _OFF_MISC:_OFF_Z]
    z_ref[0] = proj[:, _OFF_Z:_OFF_XBC].astype(BF16)
    xbc_ref[0] = proj[:, _OFF_XBC:].astype(BF16)
    misc_ref[0] = misc
    ql = _rms(proj[:, _OFF_QLAT:_OFF_KVLAT], qln_ref[...]).astype(BF16)
    kvl = _rms(proj[:, _OFF_KVLAT:_OFF_MISC], kvln_ref[...]).astype(BF16)
    q = _dot(ql, wuq_ref[...])
    kn = _dot(kvl, wuk_ref[...])
    v = _dot(kvl, wuv_ref[...])
    lane = lax.broadcasted_iota(I32, (1, HEAD_LANES), 1)
    first_half = (lane >= NOPE_HALF) & (lane < NOPE_HALF + ROPE_HALF)
    second_half = (lane >= HALF_LANES + NOPE_HALF) & (lane < HALF_LANES + NOPE_HALF + ROPE_HALF)
    krope = jnp.where(first_half | second_half, misc, 0.0)
    kr_ss = jnp.sum(krope * krope, axis=-1, keepdims=True)
    ang = pos_ref[0] * invf_ref[...]
    cos_t = jnp.cos(ang)
    sin_t = jnp.where(first_half, -jnp.sin(ang), jnp.sin(ang))
    qg, kg = qg_ref[...], kg_ref[...]
    q_one = (lane == SCORE_PAD_LANE).astype(F32)
    v_one = (lane == ONES_LANE).astype(F32)
    k_off = koff_ref[...]
    q_scale = QK_DIM ** -0.5 * LOG2E
    for hd in range(A_HEADS):
        sl = slice(hd * HEAD_LANES, (hd + 1) * HEAD_LANES)
        qs = q[:, sl]
        ss = jnp.sum(qs * qs, axis=-1, keepdims=True)
        qn = qs * lax.rsqrt(ss * (1.0 / QK_DIM) + RMS_EPS) * qg
        qr = qn * cos_t + pltpu.roll(qn, HALF_LANES, 1) * sin_t
        q_ref[0, hd] = (qr * q_scale + q_one).astype(BF16)
        ks = kn[:, sl] + krope
        ss = jnp.sum(kn[:, sl] * kn[:, sl], axis=-1, keepdims=True) + kr_ss
        kk = ks * lax.rsqrt(ss * (1.0 / QK_DIM) + RMS_EPS) * kg
        kr = kk * cos_t + pltpu.roll(kk, HALF_LANES, 1) * sin_t
        k_ref[0, hd] = (kr + k_off).astype(BF16)
        v_ref[0, hd] = (v[:, sl] + v_one).astype(BF16)


def _front_even(x, pos, invf, ln, win, qln, wuq, kvln, wuk, wuv, qg, kg, koff, tm):
    b, s, d = x.shape
    grid = (b, s // tm)
    row = lambda i, j: (i, j, 0)
    fixed2 = lambda i, j: (0, 0)
    head_row = lambda i, j: (i, 0, j, 0)
    hw = A_HEADS * HEAD_LANES
    return pl.pallas_call(
        _front_even_kernel,
        grid=grid,
        in_specs=[
            pl.BlockSpec((1, tm, d), row),
            pl.BlockSpec((1, tm, 1), row),
            pl.BlockSpec((1, HEAD_LANES), fixed2),
            pl.BlockSpec((1, d), fixed2),
            pl.BlockSpec((d, IN_W), fixed2),
            pl.BlockSpec((1, Q_LORA), fixed2),
            pl.BlockSpec((Q_LORA, hw), fixed2),
            pl.BlockSpec((1, KV_LORA), fixed2),
            pl.BlockSpec((KV_LORA, hw), fixed2),
            pl.BlockSpec((KV_LORA, hw), fixed2),
            pl.BlockSpec((1, HEAD_LANES), fixed2),
            pl.BlockSpec((1, HEAD_LANES), fixed2),
            pl.BlockSpec((1, HEAD_LANES), fixed2),
        ],
        out_specs=[
            pl.BlockSpec((1, A_HEADS, tm, HEAD_LANES), head_row),
            pl.BlockSpec((1, A_HEADS, tm, HEAD_LANES), head_row),
            pl.BlockSpec((1, A_HEADS, tm, HEAD_LANES), head_row),
            pl.BlockSpec((1, tm, D_INNER), row),
            pl.BlockSpec((1, tm, CONV_CH), row),
            pl.BlockSpec((1, tm, HEAD_LANES), row),
        ],
        out_shape=[
            jax.ShapeDtypeStruct((b, A_HEADS, s, HEAD_LANES), BF16),
            jax.ShapeDtypeStruct((b, A_HEADS, s, HEAD_LANES), BF16),
            jax.ShapeDtypeStruct((b, A_HEADS, s, HEAD_LANES), BF16),
            jax.ShapeDtypeStruct((b, s, D_INNER), BF16),
            jax.ShapeDtypeStruct((b, s, CONV_CH), BF16),
            jax.ShapeDtypeStruct((b, s, HEAD_LANES), F32),
        ],
        compiler_params=_cparams(("parallel", "parallel")),
        name="front_even",
    )(x, pos, invf, ln, win, qln, wuq, kvln, wuk, wuv, qg, kg, koff)


HEADS_PER_STEP = HEAD_LANES // V_DIM


def _attn_kernel(q_ref, k_ref, v_ref, o_ref, *, tq, online):
    qi = pl.program_id(2)
    row = lax.broadcasted_iota(I32, (tq, tq), 0)
    col = lax.broadcasted_iota(I32, (tq, tq), 1)

    def head_step(hh, j, carry, masked):
        kj = k_ref[0, hh, pl.ds(j * tq, tq), :]
        vj = v_ref[0, hh, pl.ds(j * tq, tq), :]
        s = _dot_nt(q_ref[0, hh], kj)
        if masked:
            s = jnp.where(row >= col, s, -jnp.inf)
        if online:
            m, acc = carry
            m_new = jnp.maximum(m, jnp.max(s, axis=-1, keepdims=True))
            p = jnp.exp2(s - m_new).astype(BF16)
            return m_new, jnp.exp2(m - m_new) * acc + _dot(p, vj)
        return carry + _dot(jnp.exp2(s).astype(BF16), vj)

    def step(j, carries, masked):
        return tuple(head_step(hh, j, carries[hh], masked) for hh in range(HEADS_PER_STEP))

    acc0 = jnp.zeros((tq, HEAD_LANES), F32)
    init = (jnp.full((tq, 1), -jnp.inf, F32), acc0) if online else acc0
    carries = lax.fori_loop(0, qi, functools.partial(step, masked=False), (init,) * HEADS_PER_STEP)
    carries = step(qi, carries, True)
    outs = []
    for carry in carries:
        acc = carry[1] if online else carry
        outs.append(acc / acc[:, ONES_LANE:ONES_LANE + 1])
    lane = lax.broadcasted_iota(I32, (1, HEAD_LANES), 1)
    out = outs[0]
    for hh in range(1, HEADS_PER_STEP):
        out = jnp.where(lane >= hh * V_DIM, pltpu.roll(outs[hh], hh * V_DIM, 1), out)
    o_ref[0] = out.astype(BF16)


def _attention(q, k, v, tq, online):
    b, nh, s, _ = q.shape
    grid = (b, nh // HEADS_PER_STEP, s // tq)
    kv_spec = pl.BlockSpec((1, HEADS_PER_STEP, s, HEAD_LANES), lambda i, h, j: (i, h, 0, 0))
    return pl.pallas_call(
        functools.partial(_attn_kernel, tq=tq, online=online),
        grid=grid,
        in_specs=[
            pl.BlockSpec((1, HEADS_PER_STEP, tq, HEAD_LANES), lambda i, h, j: (i, h, j, 0)),
            kv_spec,
            kv_spec,
        ],
        out_specs=pl.BlockSpec((1, tq, HEAD_LANES), lambda i, h, j: (i, j, h)),
        out_shape=jax.ShapeDtypeStruct((b, s, nh * V_DIM), BF16),
        compiler_params=_cparams(("parallel", "parallel", "parallel")),
        name="mla_attention_online" if online else "mla_attention",
    )(q, k, v)


def _ssd_kernel(xbc_ref, misc_ref, z_ref, cw_ref, cb_ref, dtb_ref, alog_ref, dskip_ref, gn_ref, y_ref,
                state_ref, carry_ref):
    c = pl.program_id(1)
    t = CHUNK

    @pl.when(c == 0)
    def _():
        state_ref[...] = jnp.zeros_like(state_ref)
        carry_ref[...] = jnp.zeros_like(carry_ref)

    xr = xbc_ref[0].astype(F32)
    xcat = jnp.concatenate([carry_ref[...], xr], axis=0)
    carry_ref[...] = xr[t - CONV_CARRY:, :]
    conv = jnp.zeros((t, CONV_CH), F32) + cb_ref[...]
    for kk in range(CONV_K):
        sh = CONV_K - 1 - kk
        shifted = xcat if sh == 0 else pltpu.roll(xcat, sh, 0)
        conv = conv + cw_ref[kk:kk + 1, :] * shifted[CONV_CARRY:, :]
    xa = conv * _sigmoid(conv)
    xs = xa[:, :D_INNER]
    gw = SSD_GROUPS * SSD_STATE
    bmat = xa[:, D_INNER:D_INNER + gw]
    cmat = xa[:, D_INNER + gw:]

    u = misc_ref[0] + dtb_ref[...]
    dt = jnp.maximum(u, 0.0) + jnp.log(1.0 + jnp.exp(-jnp.abs(u)))
    a = -jnp.exp(alog_ref[...])
    lane = lax.broadcasted_iota(I32, (1, LANES), 1)
    adt = jnp.where(lane < B_HEADS, dt * a, 0.0)
    rowi = lax.broadcasted_iota(I32, (t, LANES), 0)
    acs = adt
    sh = 1
    while sh < t:
        acs = acs + jnp.where(rowi >= sh, pltpu.roll(acs, sh, 0), 0.0)
        sh *= 2
    acs_t = acs.T
    tri = lax.broadcasted_iota(I32, (t, t), 0) >= lax.broadcasted_iota(I32, (t, t), 1)

    rep = B_HEADS // SSD_GROUPS
    ys = []
    for g in range(SSD_GROUPS):
        bg = bmat[:, g * SSD_STATE:(g + 1) * SSD_STATE]
        cg = cmat[:, g * SSD_STATE:(g + 1) * SSD_STATE]
        bg16, cg16 = bg.astype(BF16), cg.astype(BF16)
        cb = _dot_nt(cg16, bg16)
        bg_t = bg.T
        for r in range(rep):
            hd = g * rep + r
            col = acs[:, hd:hd + 1]
            rw = acs_t[hd:hd + 1, :]
            last = acs_t[hd:hd + 1, t - 1:t]
            decay = jnp.exp(jnp.where(tri, col - rw, -jnp.inf))
            xh = xs[:, hd * SSD_HEAD_DIM:(hd + 1) * SSD_HEAD_DIM]
            xdt = (xh * dt[:, hd:hd + 1]).astype(BF16)
            y_diag = _dot((cb * decay).astype(BF16), xdt)
            prev = state_ref[hd]
            y_off = _dot(cg16, prev.astype(BF16)) * jnp.exp(col)
            new_state = _dot((bg_t * jnp.exp(last - rw)).astype(BF16), xdt)
            state_ref[hd] = prev * jnp.exp(last) + new_state
            ys.append(y_diag + y_off)
    y = jnp.concatenate(ys, axis=1) + xs * dskip_ref[...]
    zf = z_ref[0].astype(F32)
    y = y * (zf * _sigmoid(zf))
    y_ref[0] = _rms(y, gn_ref[...]).astype(BF16)


def _ssd(xbc, misc, z, cw, cb, dtb, alog, dskip, gn):
    b, s, _ = xbc.shape
    grid = (b, s // CHUNK)
    row = lambda i, j: (i, j, 0)
    fixed2 = lambda i, j: (0, 0)
    return pl.pallas_call(
        _ssd_kernel,
        grid=grid,
        in_specs=[
            pl.BlockSpec((1, CHUNK, CONV_CH), row),
            pl.BlockSpec((1, CHUNK, LANES), row),
            pl.BlockSpec((1, CHUNK, D_INNER), row),
            pl.BlockSpec((CONV_K, CONV_CH), fixed2),
            pl.BlockSpec((1, CONV_CH), fixed2),
            pl.BlockSpec((1, LANES), fixed2),
            pl.BlockSpec((1, LANES), fixed2),
            pl.BlockSpec((1, D_INNER), fixed2),
            pl.BlockSpec((1, D_INNER), fixed2),
        ],
        out_specs=pl.BlockSpec((1, CHUNK, D_INNER), row),
        out_shape=jax.ShapeDtypeStruct((b, s, D_INNER), BF16),
        scratch_shapes=[
            pltpu.VMEM((B_HEADS, SSD_STATE, SSD_HEAD_DIM), F32),
            pltpu.VMEM((CONV_CARRY, CONV_CH), F32),
        ],
        compiler_params=_cparams(("parallel", "arbitrary")),
        name="ssd_scan",
    )(xbc, misc, z, cw, cb, dtb, alog, dskip, gn)


XW = X_HEADS * X_HEAD_DIM


def _mem_kv_kernel(mem_ref, ln_ref, wkv_ref, kg_ref, hsum_ref, kbd_ref, vbd_ref):
    m = mem_ref.shape[1]
    mn = _rms(mem_ref[0], ln_ref[...]).astype(BF16)
    kv = _dot(mn, wkv_ref[...])
    k, v = kv[:, :XW], kv[:, XW:]
    ss = _dot((k * k).astype(BF16), hsum_ref[...])
    kn = (k * lax.rsqrt(ss * (1.0 / X_HEAD_DIM) + RMS_EPS) * kg_ref[...]).astype(BF16)
    v16 = v.astype(BF16)
    head_of_lane = lax.shift_right_arithmetic(lax.broadcasted_iota(I32, (1, XW), 1), jnp.int32(_LOG2_XHD))
    for hd in range(X_HEADS):
        keep = head_of_lane == hd
        kbd_ref[0, hd * m:(hd + 1) * m, :] = jnp.where(keep, kn, jnp.zeros_like(kn))
        vbd_ref[0, hd * m:(hd + 1) * m, :] = jnp.where(keep, v16, jnp.zeros_like(v16))


def _mem_kv(mem, ln, wkv, kg, hsum):
    b, m, d = mem.shape
    fixed2 = lambda i: (0, 0)
    return pl.pallas_call(
        _mem_kv_kernel,
        grid=(b,),
        in_specs=[
            pl.BlockSpec((1, m, d), lambda i: (i, 0, 0)),
            pl.BlockSpec((1, d), fixed2),
            pl.BlockSpec((d, 2 * XW), fixed2),
            pl.BlockSpec((1, XW), fixed2),
            pl.BlockSpec((XW, XW), fixed2),
        ],
        out_specs=[
            pl.BlockSpec((1, X_HEADS * m, XW), lambda i: (i, 0, 0)),
            pl.BlockSpec((1, X_HEADS * m, XW), lambda i: (i, 0, 0)),
        ],
        out_shape=[
            jax.ShapeDtypeStruct((b, X_HEADS * m, XW), BF16),
            jax.ShapeDtypeStruct((b, X_HEADS * m, XW), BF16),
        ],
        compiler_params=_cparams(("parallel",)),
        name="mem_kv",
    )(mem, ln, wkv, kg, hsum)


ROUTE_LANES = LANES
_GROUP_LANE0 = 0
_EXPERT_LANE0 = MOE_GROUPS
_LOG2_EPG = EXPERTS_PER_GROUP.bit_length() - 1
_LOG2_XHD = X_HEAD_DIM.bit_length() - 1


def _pack_bf16_pairs(v):
    w = v.shape[1] // 2
    r = v.astype(BF16).astype(F32)
    hi = lax.bitcast_convert_type(r[:, :w], U32)
    lo = lax.bitcast_convert_type(r[:, w:], U32)
    return (hi & jnp.uint32(0xFFFF0000)) | (lo >> jnp.uint32(16))


def _unpack_bf16_pairs(u):
    hi = lax.bitcast_convert_type(u & jnp.uint32(0xFFFF0000), F32)
    lo = lax.bitcast_convert_type(u << jnp.uint32(16), F32)
    return hi, lo


def _tail(x1, kbd_ref, vbd_ref, lnq_ref, wq_ref, qg_ref, hsum_ref, wo_ref, lnf_ref, rwh_ref, rwl_ref, rb_ref,
          ltri_ref, x2_ref, hfp_ref, route_ref, cnt_ref):
    tm = x1.shape[0]
    m = kbd_ref.shape[1] // X_HEADS
    hq = _rms(x1, lnq_ref[...]).astype(BF16)
    q = _dot(hq, wq_ref[...])
    ss = _dot((q * q).astype(BF16), hsum_ref[...])
    qn = (q * lax.rsqrt(ss * (1.0 / X_HEAD_DIM) + RMS_EPS) * qg_ref[...] * (X_HEAD_DIM ** -0.5)).astype(BF16)
    s = _dot_nt(qn, kbd_ref[0])
    ps = []
    for hd in range(X_HEADS):
        sh = s[:, hd * m:(hd + 1) * m]
        e = jnp.exp(sh - jnp.max(sh, axis=-1, keepdims=True))
        ps.append((e / jnp.sum(e, axis=-1, keepdims=True)).astype(BF16))
    o = _dot(jnp.concatenate(ps, axis=1), vbd_ref[0]).astype(BF16)
    x2 = x1 + _dot(o, wo_ref[...])
    x2_ref[0] = x2

    hf = _rms(x2, lnf_ref[...])
    hf_hi = hf.astype(BF16)
    hfp_ref[0] = _pack_bf16_pairs(hf)
    hf_lo = (hf - hf_hi.astype(F32)).astype(BF16)
    logits = _dot(hf_hi, rwh_ref[...]) + _dot(hf_hi, rwl_ref[...]) + _dot(hf_lo, rwh_ref[...]) + rb_ref[...]

    lane_i = lax.broadcasted_iota(I32, (tm, ROUTE_LANES), 1)
    lane = lane_i.astype(F32)
    big = float(ROUTE_LANES)
    neg = -jnp.inf
    gl = jnp.where(lane_i < MOE_GROUPS, logits, neg)
    gmax = jnp.max(gl, axis=-1, keepdims=True)
    gsum = jnp.sum(jnp.exp(gl - gmax), axis=-1, keepdims=True)
    g_p = 1.0 / gsum
    g_idx = jnp.min(jnp.where(gl == gmax, lane, big), axis=-1, keepdims=True)
    e_lane = lane_i - _EXPERT_LANE0
    grp_of_lane = lax.shift_right_arithmetic(e_lane, jnp.int32(_LOG2_EPG)).astype(F32)
    in_grp = (e_lane >= 0) & (e_lane < N_EXPERTS) & (grp_of_lane == g_idx)
    el = jnp.where(in_grp, logits, neg)
    emax = jnp.max(el, axis=-1, keepdims=True)
    idx1 = jnp.min(jnp.where(el == emax, lane, big), axis=-1, keepdims=True)
    el2 = jnp.where(lane == idx1, neg, el)
    emax2 = jnp.max(el2, axis=-1, keepdims=True)
    idx2 = jnp.min(jnp.where(el2 == emax2, lane, big), axis=-1, keepdims=True)
    r2 = jnp.exp(emax2 - emax)
    gate1 = g_p / (1.0 + r2)
    gate2 = g_p * r2 / (1.0 + r2)
    e1 = idx1 - float(_EXPERT_LANE0)
    e2 = idx2 - float(_EXPERT_LANE0)

    oh1 = (lane == e1).astype(F32)
    oh2 = (lane == e2).astype(F32)
    both = oh1 + oh2
    before = _dot(ltri_ref[...], both.astype(BF16))
    rank1 = jnp.sum(before * oh1, axis=-1, keepdims=True)
    rank2 = jnp.sum(before * oh2, axis=-1, keepdims=True)
    cnt_ref[0] = jnp.broadcast_to(jnp.sum(both, axis=0, keepdims=True), cnt_ref.shape[1:])

    route = jnp.where(lane == 0, e1, 0.0)
    route = jnp.where(lane == 1, e2, route)
    route = jnp.where(lane == 2, gate1, route)
    route = jnp.where(lane == 3, gate2, route)
    route = jnp.where(lane == 4, rank1, route)
    route = jnp.where(lane == 5, rank2, route)
    route_ref[0] = route


_TAIL_IN = 12


def _post_even_kernel(x_ref, a_ref, y_ref, wout_ref, *rest):
    tail_in, outs = rest[:_TAIL_IN], rest[_TAIL_IN:]
    half = wout_ref.shape[0] // 2
    x1 = x_ref[0] + _dot(a_ref[0], wout_ref[:half, :]) + _dot(y_ref[0], wout_ref[half:, :])
    _tail(x1, *tail_in, *outs)


def _post_pool_kernel(x_ref, ln_ref, pw_ref, pb_ref, ps_ref, *rest):
    tail_in, outs, carry_ref = rest[:_TAIL_IN], rest[_TAIL_IN:-1], rest[-1]
    j = pl.program_id(1)
    tm = x_ref.shape[1]

    @pl.when(j == 0)
    def _():
        carry_ref[...] = jnp.zeros_like(carry_ref)

    x = x_ref[0]
    h = _rms(x, ln_ref[...])
    pos = (j * tm + 1 + lax.broadcasted_iota(I32, (tm, 1), 0)).astype(F32)
    mixed = []
    for g, w in enumerate(POOL_WINDOWS):
        sl = slice(g * POOL_GROUP, (g + 1) * POOL_GROUP)
        hg = h[:, sl]
        acc = jnp.concatenate([carry_ref[:, sl], hg], axis=0)
        sh = 1
        while sh < w:
            acc = acc + pltpu.roll(acc, sh, 0)
            sh *= 2
        win = acc[POOL_CARRY:, :]
        dlt = win / jnp.minimum(pos, float(w)) - hg
        mixed.append(_dot(dlt.astype(BF16), pw_ref[g]))
    carry_ref[...] = h[tm - POOL_CARRY:, :]
    y = (jnp.concatenate(mixed, axis=1) + pb_ref[...]) * ps_ref[...]
    _tail(x + y, *tail_in, *outs)


def _post(kind, front_args, front_specs, tail_args, b, s, tm, scratch):
    d = D_MODEL
    m4 = tail_args[0].shape[1]
    row = lambda i, j: (i, j, 0)
    fixed2 = lambda i, j: (0, 0)
    per_b = lambda i, j: (i, 0, 0)
    tail_specs = [
        pl.BlockSpec((1, m4, XW), per_b),
        pl.BlockSpec((1, m4, XW), per_b),
        pl.BlockSpec((1, d), fixed2),
        pl.BlockSpec((d, XW), fixed2),
        pl.BlockSpec((1, XW), fixed2),
        pl.BlockSpec((XW, XW), fixed2),
        pl.BlockSpec((XW, d), fixed2),
        pl.BlockSpec((1, d), fixed2),
        pl.BlockSpec((d, ROUTE_LANES), fixed2),
        pl.BlockSpec((d, ROUTE_LANES), fixed2),
        pl.BlockSpec((1, ROUTE_LANES), fixed2),
        pl.BlockSpec((tm, tm), fixed2),
    ]
    nt = s // tm
    kernel = _post_even_kernel if kind == "even" else _post_pool_kernel
    return pl.pallas_call(
        kernel,
        grid=(b, nt),
        in_specs=front_specs + tail_specs,
        out_specs=[
            pl.BlockSpec((1, tm, d), row),
            pl.BlockSpec((1, tm, d // 2), row),
            pl.BlockSpec((1, tm, ROUTE_LANES), row),
            pl.BlockSpec((1, 8, ROUTE_LANES), lambda i, j: (i * nt + j, 0, 0)),
        ],
        out_shape=[
            jax.ShapeDtypeStruct((b, s, d), F32),
            jax.ShapeDtypeStruct((b, s, d // 2), U32),
            jax.ShapeDtypeStruct((b, s, ROUTE_LANES), F32),
            jax.ShapeDtypeStruct((b * nt, 8, ROUTE_LANES), F32),
        ],
        scratch_shapes=scratch,
        compiler_params=_cparams(("parallel", "arbitrary")),
        name="post_" + kind,
    )(*front_args, *tail_args)


FFN_ROWS = 256
DISPATCH_TOKENS = 1024
COMBINE_TOKENS = 512
DMA_UNROLL = 8


def _dispatch_kernel(dest_ref, tail_ref, hfp_ref, xb_ref, zero_ref, sem):
    i = pl.program_id(0)
    td = hfp_ref.shape[0]

    @pl.when(i == 0)
    def _():
        zero_ref[...] = jnp.zeros_like(zero_ref)

        def zero_copy(e):
            start = pl.multiple_of(tail_ref[e], FFN_ROWS)
            return pltpu.make_async_copy(zero_ref, xb_ref.at[pl.ds(start, FFN_ROWS)], sem)

        for e in range(N_EXPERTS):
            @pl.when(tail_ref[e] >= 0)
            def _():
                zero_copy(e).start()
        for e in range(N_EXPERTS):
            @pl.when(tail_ref[e] >= 0)
            def _():
                zero_copy(e).wait()

    def issue(t, carry):
        for k in range(TOP_K):
            dst = dest_ref[0, 0, TOP_K * t + k]
            pltpu.make_async_copy(hfp_ref.at[pl.ds(t, 1)], xb_ref.at[pl.ds(dst, 1)], sem).start()
        return carry

    lax.fori_loop(0, td, issue, 0, unroll=DMA_UNROLL)

    def drain(t, carry):
        for k in range(TOP_K):
            pltpu.make_async_copy(hfp_ref.at[pl.ds(0, 1)], xb_ref.at[pl.ds(0, 1)], sem).wait()
        return carry

    lax.fori_loop(0, td, drain, 0, unroll=DMA_UNROLL)


def _dispatch(dest, seg_tail, hfp, n_rows, td):
    n, w = hfp.shape
    nsteps = n // td
    return pl.pallas_call(
        _dispatch_kernel,
        grid=(nsteps,),
        in_specs=[
            pl.BlockSpec((1, 1, TOP_K * td), lambda i: (i, 0, 0), memory_space=pltpu.SMEM),
            pl.BlockSpec(memory_space=pltpu.SMEM),
            pl.BlockSpec((td, w), lambda i: (i, 0)),
        ],
        out_specs=pl.BlockSpec(memory_space=pl.ANY),
        scratch_shapes=[pltpu.VMEM((FFN_ROWS, w), U32), pltpu.SemaphoreType.DMA],
        out_shape=jax.ShapeDtypeStruct((n_rows, w), U32),
        compiler_params=_cparams(("arbitrary",)),
        name="moe_dispatch",
    )(dest.reshape(nsteps, 1, TOP_K * td), seg_tail, hfp)


def _ffn_kernel(be_ref, bi_ref, xb_ref, wg_ref, wu_ref, wd_ref, yb_ref, wg_s, wu_s, wd_s):
    i = pl.program_id(0)
    changed = jnp.logical_or(i == 0, be_ref[i] != be_ref[jnp.maximum(i - 1, 0)])

    @pl.when(changed)
    def _():
        wg_s[...] = wg_ref[0, 0].astype(BF16)
        wu_s[...] = wu_ref[0, 0].astype(BF16)
        wd_s[...] = wd_ref[0, 0].astype(BF16)

    hi, lo = _unpack_bf16_pairs(xb_ref[...])
    hi, lo = hi.astype(BF16), lo.astype(BF16)
    half = wg_s.shape[0] // 2
    gate = _dot(hi, wg_s[:half, :]) + _dot(lo, wg_s[half:, :])
    up = _dot(hi, wu_s[:half, :]) + _dot(lo, wu_s[half:, :])
    act = (gate * _sigmoid(gate) * up).astype(BF16)
    yb_ref[...] = _pack_bf16_pairs(_dot(act, wd_s[...]))


def _expert_ffn(block_e, block_i, xb, wg, wu, wd, layer):
    n_rows, w = xb.shape
    d, ff = wg.shape[2], wg.shape[3]
    n_blk = n_rows // FFN_ROWS
    return pl.pallas_call(
        _ffn_kernel,
        grid_spec=pltpu.PrefetchScalarGridSpec(
            num_scalar_prefetch=2,
            grid=(n_blk,),
            in_specs=[
                pl.BlockSpec((FFN_ROWS, w), lambda i, be, bi: (bi[i], 0)),
                pl.BlockSpec((1, 1, d, ff), lambda i, be, bi: (layer, be[i], 0, 0)),
                pl.BlockSpec((1, 1, d, ff), lambda i, be, bi: (layer, be[i], 0, 0)),
                pl.BlockSpec((1, 1, ff, d), lambda i, be, bi: (layer, be[i], 0, 0)),
            ],
            out_specs=pl.BlockSpec((FFN_ROWS, w), lambda i, be, bi: (bi[i], 0)),
            scratch_shapes=[pltpu.VMEM((d, ff), BF16), pltpu.VMEM((d, ff), BF16), pltpu.VMEM((ff, d), BF16)],
        ),
        out_shape=jax.ShapeDtypeStruct((n_rows, w), U32),
        compiler_params=_cparams(("arbitrary",)),
        name="moe_expert_ffn",
    )(block_e, block_i, xb, wg, wu, wd)


def _combine_kernel(dest_ref, x_ref, route_ref, yb_ref, o_ref, ybuf, sem):
    tc = x_ref.shape[0]

    def issue(t, carry):
        for k in range(TOP_K):
            src = dest_ref[0, 0, TOP_K * t + k]
            pltpu.make_async_copy(yb_ref.at[pl.ds(src, 1)], ybuf.at[k, pl.ds(t, 1)], sem).start()
        return carry

    lax.fori_loop(0, tc, issue, 0, unroll=DMA_UNROLL)

    def drain(t, carry):
        for k in range(TOP_K):
            pltpu.make_async_copy(yb_ref.at[pl.ds(0, 1)], ybuf.at[0, pl.ds(0, 1)], sem).wait()
        return carry

    lax.fori_loop(0, tc, drain, 0, unroll=DMA_UNROLL)

    half = x_ref.shape[1] // 2
    route = route_ref[...]
    g1, g2 = route[:, 2:3], route[:, 3:4]
    h1, l1 = _unpack_bf16_pairs(ybuf[0])
    h2, l2 = _unpack_bf16_pairs(ybuf[1])
    o_ref[:, :half] = x_ref[:, :half] + (h1 * g1 + h2 * g2)
    o_ref[:, half:] = x_ref[:, half:] + (l1 * g1 + l2 * g2)


def _combine(dest, x2, route, yb, tc):
    n, d = x2.shape
    w = yb.shape[1]
    nsteps = n // tc
    return pl.pallas_call(
        _combine_kernel,
        grid=(nsteps,),
        in_specs=[
            pl.BlockSpec((1, 1, TOP_K * tc), lambda i: (i, 0, 0), memory_space=pltpu.SMEM),
            pl.BlockSpec((tc, d), lambda i: (i, 0)),
            pl.BlockSpec((tc, ROUTE_LANES), lambda i: (i, 0)),
            pl.BlockSpec(memory_space=pl.ANY),
        ],
        out_specs=pl.BlockSpec((tc, d), lambda i: (i, 0)),
        scratch_shapes=[pltpu.VMEM((TOP_K, tc, w), U32), pltpu.SemaphoreType.DMA],
        out_shape=jax.ShapeDtypeStruct((n, d), F32),
        compiler_params=_cparams(("arbitrary",)),
        name="moe_combine",
    )(dest.reshape(nsteps, 1, TOP_K * tc), x2, route, yb)


def _moe(x2, hfp, route, counts, wg, wu, wd, layer, tm):
    b, s, d = x2.shape
    n = b * s
    route = route.reshape(n, ROUTE_LANES)
    cnt = counts[:, 0, :N_EXPERTS].astype(I32)
    total = jnp.sum(cnt, axis=0)
    padded = (total + FFN_ROWS - 1) // FFN_ROWS * FFN_ROWS
    pad_end = jnp.cumsum(padded)
    pad_start = pad_end - padded
    tile_base = pad_start[None, :] + jnp.cumsum(cnt, axis=0) - cnt
    experts = route[:, 0:TOP_K].astype(I32)
    ranks = route[:, 4:4 + TOP_K].astype(I32)
    base_tok = jnp.repeat(tile_base, tm, axis=0)
    dest = jnp.take_along_axis(base_tok, experts, axis=1) + ranks
    dest = dest.reshape(n * TOP_K)
    n_blk = (n * TOP_K) // FFN_ROWS + N_EXPERTS
    n_rows = n_blk * FFN_ROWS
    used = pad_end[-1] // FFN_ROWS
    block_i = jnp.minimum(jnp.arange(n_blk, dtype=I32), used - 1).astype(I32)
    ended = (pad_end[None, :] <= (block_i * FFN_ROWS)[:, None]).astype(I32)
    block_e = jnp.minimum(jnp.sum(ended, axis=1), N_EXPERTS - 1).astype(I32)
    seg_tail = jnp.where(padded > 0, pad_end - FFN_ROWS, -1).astype(I32)

    td = min(DISPATCH_TOKENS, n)
    tc = min(COMBINE_TOKENS, n)
    xb = _dispatch(dest, seg_tail, hfp.reshape(n, d // 2), n_rows, td)
    yb = _expert_ffn(block_e, block_i, xb, wg, wu, wd, layer)
    out = _combine(dest, x2.reshape(n, d), route, yb, tc)
    return out.reshape(b, s, d)


def _rope_lane_freq():
    inv = ROPE_THETA ** (-jnp.arange(0, ROPE_DIM // 2, dtype=F32) * 2.0 / ROPE_DIM)
    idx = np.full((HEAD_LANES,), -1, np.int64)
    for r in range(ROPE_DIM):
        idx[_head_lane(NOPE_DIM + r)] = r % ROPE_HALF
    return _gather_cols(inv[None, :], idx)


FAST_SOFTMAX_MAX_LOG2 = 60.0


def _score_bound_log2(qg, kg):
    return 1.02 * LOG2E * QK_DIM ** 0.5 * jnp.max(jnp.abs(qg)) * jnp.max(jnp.abs(kg))


def _head_gain(g):
    idx = np.full((HEAD_LANES,), -1, np.int64)
    for dd in range(QK_DIM):
        idx[_head_lane(dd)] = dd
    return _gather_cols(g[None, :], idx)


def kernel(x, mem, positions, ln_mix, w_in, q_lat_norm, w_uq, kv_lat_norm, w_ukv, q_norm, k_norm, conv_w, conv_b,
           dt_bias, a_log, d_skip, ssd_norm, w_out, pool_w, pool_b, pool_scale, ln_xq, ln_mem, xq_w, xkv_w, xq_norm,
           xk_norm, xo_w, ln_ffn, rg_w, rg_b, re_w, re_b, exp_w_gate, exp_w_up, exp_w_down):
    b, s, d = x.shape
    depth = ln_mix.shape[0]
    tm = min(512, s)
    tq = min(512, s)
    assert d == D_MODEL and s % tm == 0 and s % CHUNK == 0 and tm >= POOL_CARRY

    pos = positions.astype(F32)[..., None]
    invf = _rope_lane_freq()
    hsum = jnp.asarray(np.kron(np.eye(X_HEADS), np.ones((X_HEAD_DIM, X_HEAD_DIM))), BF16)
    ltri = jnp.asarray(np.tril(np.ones((tm, tm)), -1), BF16)
    row2 = lambda v: v.reshape(1, -1)
    lane_pad = lambda v: jnp.pad(v, (0, LANES - v.shape[0])).reshape(1, LANES)

    for layer in range(depth):
        j = layer // 2
        kbd, vbd = _mem_kv(mem, row2(ln_mem[layer]), xkv_w[layer].astype(BF16),
                           row2(jnp.tile(xk_norm[layer], X_HEADS)), hsum)
        rw = jnp.pad(jnp.concatenate([rg_w[layer], re_w[layer]], axis=1),
                     ((0, 0), (0, ROUTE_LANES - MOE_GROUPS - N_EXPERTS)))
        rw_hi = rw.astype(BF16)
        rw_lo = (rw - rw_hi.astype(F32)).astype(BF16)
        rb = lane_pad(jnp.concatenate([rg_b[layer], re_b[layer]]))
        tail_args = [kbd, vbd, row2(ln_xq[layer]), xq_w[layer].astype(BF16), row2(jnp.tile(xq_norm[layer], X_HEADS)),
                     hsum, xo_w[layer].astype(BF16), row2(ln_ffn[layer]), rw_hi, rw_lo, rb, ltri]
        row = lambda i, jj: (i, jj, 0)
        fixed2 = lambda i, jj: (0, 0)
        if layer % 2 == 0:
            win = _gather_cols(w_in[j], _win_col_index()).astype(BF16)
            wuq = _gather_cols(w_uq[j], _head_col_index(QK_DIM, 0, QK_DIM)).astype(BF16)
            wuk = _gather_cols(w_ukv[j], _head_col_index(NOPE_DIM + V_DIM, 0, NOPE_DIM)).astype(BF16)
            v_idx = np.full((A_HEADS * HEAD_LANES,), -1, np.int64)
            for hd in range(A_HEADS):
                v_idx[hd * HEAD_LANES:hd * HEAD_LANES + V_DIM] = hd * (NOPE_DIM + V_DIM) + NOPE_DIM + np.arange(V_DIM)
            wuv = _gather_cols(w_ukv[j], v_idx).astype(BF16)
            bound = _score_bound_log2(q_norm[j], k_norm[j])
            koff = jnp.zeros((1, HEAD_LANES), F32).at[0, SCORE_PAD_LANE].set(-bound)
            q, k, v, z, xbc, misc = _front_even(
                x, pos, invf, row2(ln_mix[layer]), win, row2(q_lat_norm[j]), wuq, row2(kv_lat_norm[j]), wuk, wuv,
                _head_gain(q_norm[j]), _head_gain(k_norm[j]), koff, tm)
            attn = lax.cond(bound <= FAST_SOFTMAX_MAX_LOG2,
                            functools.partial(_attention, tq=tq, online=False),
                            functools.partial(_attention, tq=tq, online=True), q, k, v)
            y = _ssd(xbc, misc, z, conv_w[j], row2(conv_b[j]), lane_pad(dt_bias[j]), lane_pad(a_log[j]),
                     row2(jnp.repeat(d_skip[j], SSD_HEAD_DIM)), row2(ssd_norm[j]))
            half = A_HEADS * V_DIM
            front_args = [x, attn, y, w_out[j].astype(BF16)]
            front_specs = [pl.BlockSpec((1, tm, d), row), pl.BlockSpec((1, tm, half), row),
                           pl.BlockSpec((1, tm, D_INNER), row), pl.BlockSpec((half + D_INNER, d), fixed2)]
            x2, hfp, route, counts = _post("even", front_args, front_specs, tail_args, b, s, tm, [])
        else:
            front_args = [x, row2(ln_mix[layer]), pool_w[j].astype(BF16), row2(pool_b[j]), row2(pool_scale[j])]
            front_specs = [pl.BlockSpec((1, tm, d), row), pl.BlockSpec((1, d), fixed2),
                           pl.BlockSpec((len(POOL_WINDOWS), POOL_GROUP, POOL_GROUP), lambda i, jj: (0, 0, 0)),
                           pl.BlockSpec((1, d), fixed2), pl.BlockSpec((1, d), fixed2)]
            x2, hfp, route, counts = _post("pool", front_args, front_specs, tail_args, b, s, tm,
                                           [pltpu.VMEM((POOL_CARRY, d), F32)])
        x = _moe(x2, hfp, route, counts, exp_w_gate, exp_w_up, exp_w_down, layer, tm)
    return x
```

```python
import functools

import numpy as np
import jax
import jax.numpy as jnp
from jax import lax
from jax.experimental import pallas as pl
from jax.experimental.pallas import tpu as pltpu

F32 = jnp.float32
BF16 = jnp.bfloat16
U32 = jnp.uint32
I32 = jnp.int32

RMS_EPS = 1e-6
ROPE_THETA = 10000.0

D_MODEL = 1024
X_HEADS, X_HEAD_DIM = 4, 64
A_HEADS, NOPE_DIM, ROPE_DIM, V_DIM = 8, 64, 32, 64
QK_DIM = NOPE_DIM + ROPE_DIM
Q_LORA, KV_LORA = 256, 128
B_HEADS, SSD_HEAD_DIM, SSD_GROUPS, SSD_STATE, CONV_K, CHUNK = 8, 64, 2, 128, 4, 128
D_INNER = B_HEADS * SSD_HEAD_DIM
CONV_CH = D_INNER + 2 * SSD_GROUPS * SSD_STATE
POOL_WINDOWS = (2, 4, 8, 16)
POOL_GROUP = D_MODEL // 4
MOE_GROUPS, EXPERTS_PER_GROUP, TOP_K, EXPERT_FF = 4, 8, 2, 256
N_EXPERTS = MOE_GROUPS * EXPERTS_PER_GROUP

LANES = 128
HEAD_LANES = LANES
HALF_LANES = LANES // 2
ROPE_HALF = ROPE_DIM // 2
NOPE_HALF = NOPE_DIM // 2
POOL_CARRY = 16
CONV_CARRY = 8
VMEM_LIMIT = 56 * 1024 * 1024


def _cparams(sem):
    return pltpu.CompilerParams(dimension_semantics=sem, vmem_limit_bytes=VMEM_LIMIT)


def _rms(u, g):
    return u * lax.rsqrt(jnp.mean(u * u, axis=-1, keepdims=True) + RMS_EPS) * g


def _sigmoid(u):
    return 1.0 / (1.0 + jnp.exp(-u))


def _dot(a, b):
    return jnp.dot(a, b, preferred_element_type=F32)


def _dot_nt(a, b):
    return lax.dot_general(a, b, (((1,), (1,)), ((), ())), preferred_element_type=F32)


def _head_lane(d):
    if d < NOPE_HALF:
        return d
    if d < NOPE_DIM:
        return HALF_LANES + (d - NOPE_HALF)
    r = d - NOPE_DIM
    if r < ROPE_HALF:
        return NOPE_HALF + r
    return HALF_LANES + NOPE_HALF + (r - ROPE_HALF)


def _gather_cols(w, idx):
    w_ext = jnp.concatenate([w, jnp.zeros(w.shape[:-1] + (1,), w.dtype)], axis=-1)
    idx = np.where(idx < 0, w.shape[-1], idx)
    return jnp.take(w_ext, jnp.asarray(idx, dtype=jnp.int32), axis=-1)


IN_W = 2 * D_MODEL
_OFF_QLAT, _OFF_KVLAT, _OFF_MISC, _OFF_Z, _OFF_XBC = 0, 256, 384, 512, 1024


def _win_col_index():
    idx = np.full((IN_W,), -1, np.int64)
    idx[_OFF_QLAT:_OFF_QLAT + Q_LORA] = np.arange(Q_LORA)
    idx[_OFF_KVLAT:_OFF_KVLAT + KV_LORA] = Q_LORA + np.arange(KV_LORA)
    rope0 = Q_LORA + KV_LORA
    for r in range(ROPE_DIM):
        idx[_OFF_MISC + _head_lane(NOPE_DIM + r)] = rope0 + r
    z0 = rope0 + ROPE_DIM
    idx[_OFF_Z:_OFF_Z + D_INNER] = z0 + np.arange(D_INNER)
    xbc0 = z0 + D_INNER
    idx[_OFF_XBC:_OFF_XBC + CONV_CH] = xbc0 + np.arange(CONV_CH)
    dt0 = xbc0 + CONV_CH
    idx[_OFF_MISC:_OFF_MISC + B_HEADS] = dt0 + np.arange(B_HEADS)
    return idx


def _head_col_index(per_head, offset, count):
    idx = np.full((A_HEADS * HEAD_LANES,), -1, np.int64)
    for h in range(A_HEADS):
        for d in range(count):
            idx[h * HEAD_LANES + _head_lane(d)] = h * per_head + offset + d
    return idx


SCORE_PAD_LANE = NOPE_HALF + ROPE_HALF
ONES_LANE = V_DIM
LOG2E = 1.4426950408889634


def _front_even_kernel(x_ref, pos_ref, invf_ref, ln_ref, win_ref, qln_ref, wuq_ref, kvln_ref, wuk_ref, wuv_ref,
                       qg_ref, kg_ref, koff_ref, q_ref, k_ref, v_ref, z_ref, xbc_ref, misc_ref):
    x = x_ref[0]
    h = _rms(x, ln_ref[...]).astype(BF16)
    proj = _dot(h, win_ref[...])
    misc = proj[:, _OFF_MISC:_OFF_Z]
    z_ref[0] = proj[:, _OFF_Z:_OFF_XBC].astype(BF16)
    xbc_ref[0] = proj[:, _OFF_XBC:].astype(BF16)
    misc_ref[0] = misc
    ql = _rms(proj[:, _OFF_QLAT:_OFF_KVLAT], qln_ref[...]).astype(BF16)
    kvl = _rms(proj[:, _OFF_KVLAT:_OFF_MISC], kvln_ref[...]).astype(BF16)
    q = _dot(ql, wuq_ref[...])
    kn = _dot(kvl, wuk_ref[...])
    v = _dot(kvl, wuv_ref[...])
    lane = lax.broadcasted_iota(I32, (1, HEAD_LANES), 1)
    first_half = (lane >= NOPE_HALF) & (lane < NOPE_HALF + ROPE_HALF)
    second_half = (lane >= HALF_LANES + NOPE_HALF) & (lane < HALF_LANES + NOPE_HALF + ROPE_HALF)
    krope = jnp.where(first_half | second_half, misc, 0.0)
    kr_ss = jnp.sum(krope * krope, axis=-1, keepdims=True)
    ang = pos_ref[0] * invf_ref[...]
    cos_t = jnp.cos(ang)
    sin_t = jnp.where(first_half, -jnp.sin(ang), jnp.sin(ang))
    qg, kg = qg_ref[...], kg_ref[...]
    q_one = (lane == SCORE_PAD_LANE).astype(F32)
    v_one = (lane == ONES_LANE).astype(F32)
    k_off = koff_ref[...]
    q_scale = QK_DIM ** -0.5 * LOG2E
    for hd in range(A_HEADS):
        sl = slice(hd * HEAD_LANES, (hd + 1) * HEAD_LANES)
        qs = q[:, sl]
        ss = jnp.sum(qs * qs, axis=-1, keepdims=True)
        qn = qs * lax.rsqrt(ss * (1.0 / QK_DIM) + RMS_EPS) * qg
        qr = qn * cos_t + pltpu.roll(qn, HALF_LANES, 1) * sin_t
        q_ref[0, hd] = (qr * q_scale + q_one).astype(BF16)
        ks = kn[:, sl] + krope
        ss = jnp.sum(kn[:, sl] * kn[:, sl], axis=-1, keepdims=True) + kr_ss
        kk = ks * lax.rsqrt(ss * (1.0 / QK_DIM) + RMS_EPS) * kg
        kr = kk * cos_t + pltpu.roll(kk, HALF_LANES, 1) * sin_t
        k_ref[0, hd] = (kr + k_off).astype(BF16)
        v_ref[0, hd] = (v[:, sl] + v_one).astype(BF16)


def _front_even(x, pos, invf, ln, win, qln, wuq, kvln, wuk, wuv, qg, kg, koff, tm):
    b, s, d = x.shape
    grid = (b, s // tm)
    row = lambda i, j: (i, j, 0)
    fixed2 = lambda i, j: (0, 0)
    head_row = lambda i, j: (i, 0, j, 0)
    hw = A_HEADS * HEAD_LANES
    return pl.pallas_call(
        _front_even_kernel,
        grid=grid,
        in_specs=[
            pl.BlockSpec((1, tm, d), row),
            pl.BlockSpec((1, tm, 1), row),
            pl.BlockSpec((1, HEAD_LANES), fixed2),
            pl.BlockSpec((1, d), fixed2),
            pl.BlockSpec((d, IN_W), fixed2),
            pl.BlockSpec((1, Q_LORA), fixed2),
            pl.BlockSpec((Q_LORA, hw), fixed2),
            pl.BlockSpec((1, KV_LORA), fixed2),
            pl.BlockSpec((KV_LORA, hw), fixed2),
            pl.BlockSpec((KV_LORA, hw), fixed2),
            pl.BlockSpec((1, HEAD_LANES), fixed2),
            pl.BlockSpec((1, HEAD_LANES), fixed2),
            pl.BlockSpec((1, HEAD_LANES), fixed2),
        ],
        out_specs=[
            pl.BlockSpec((1, A_HEADS, tm, HEAD_LANES), head_row),
            pl.BlockSpec((1, A_HEADS, tm, HEAD_LANES), head_row),
            pl.BlockSpec((1, A_HEADS, tm, HEAD_LANES), head_row),
            pl.BlockSpec((1, tm, D_INNER), row),
            pl.BlockSpec((1, tm, CONV_CH), row),
            pl.BlockSpec((1, tm, HEAD_LANES), row),
        ],
        out_shape=[
            jax.ShapeDtypeStruct((b, A_HEADS, s, HEAD_LANES), BF16),
            jax.ShapeDtypeStruct((b, A_HEADS, s, HEAD_LANES), BF16),
            jax.ShapeDtypeStruct((b, A_HEADS, s, HEAD_LANES), BF16),
            jax.ShapeDtypeStruct((b, s, D_INNER), BF16),
            jax.ShapeDtypeStruct((b, s, CONV_CH), BF16),
            jax.ShapeDtypeStruct((b, s, HEAD_LANES), F32),
        ],
        compiler_params=_cparams(("parallel", "parallel")),
        name="front_even",
    )(x, pos, invf, ln, win, qln, wuq, kvln, wuk, wuv, qg, kg, koff)


HEADS_PER_STEP = HEAD_LANES // V_DIM


def _attn_kernel(q_ref, k_ref, v_ref, o_ref, *, tq, online):
    qi = pl.program_id(2)
    row = lax.broadcasted_iota(I32, (tq, tq), 0)
    col = lax.broadcasted_iota(I32, (tq, tq), 1)

    def head_step(hh, j, carry, masked):
        kj = k_ref[0, hh, pl.ds(j * tq, tq), :]
        vj = v_ref[0, hh, pl.ds(j * tq, tq), :]
        s = _dot_nt(q_ref[0, hh], kj)
        if masked:
            s = jnp.where(row >= col, s, -jnp.inf)
        if online:
            m, acc = carry
            m_new = jnp.maximum(m, jnp.max(s, axis=-1, keepdims=True))
            p = jnp.exp2(s - m_new).astype(BF16)
            return m_new, jnp.exp2(m - m_new) * acc + _dot(p, vj)
        return carry + _dot(jnp.exp2(s).astype(BF16), vj)

    def step(j, carries, masked):
        return tuple(head_step(hh, j, carries[hh], masked) for hh in range(HEADS_PER_STEP))

    acc0 = jnp.zeros((tq, HEAD_LANES), F32)
    init = (jnp.full((tq, 1), -jnp.inf, F32), acc0) if online else acc0
    carries = lax.fori_loop(0, qi, functools.partial(step, masked=False), (init,) * HEADS_PER_STEP)
    carries = step(qi, carries, True)
    outs = []
    for carry in carries:
        acc = carry[1] if online else carry
        outs.append(acc / acc[:, ONES_LANE:ONES_LANE + 1])
    lane = lax.broadcasted_iota(I32, (1, HEAD_LANES), 1)
    out = outs[0]
    for hh in range(1, HEADS_PER_STEP):
        out = jnp.where(lane >= hh * V_DIM, pltpu.roll(outs[hh], hh * V_DIM, 1), out)
    o_ref[0] = out.astype(BF16)


def _attention(q, k, v, tq, online):
    b, nh, s, _ = q.shape
    grid = (b, nh // HEADS_PER_STEP, s // tq)
    kv_spec = pl.BlockSpec((1, HEADS_PER_STEP, s, HEAD_LANES), lambda i, h, j: (i, h, 0, 0))
    return pl.pallas_call(
        functools.partial(_attn_kernel, tq=tq, online=online),
        grid=grid,
        in_specs=[
            pl.BlockSpec((1, HEADS_PER_STEP, tq, HEAD_LANES), lambda i, h, j: (i, h, j, 0)),
            kv_spec,
            kv_spec,
        ],
        out_specs=pl.BlockSpec((1, tq, HEAD_LANES), lambda i, h, j: (i, j, h)),
        out_shape=jax.ShapeDtypeStruct((b, s, nh * V_DIM), BF16),
        compiler_params=_cparams(("parallel", "parallel", "parallel")),
        name="mla_attention_online" if online else "mla_attention",
    )(q, k, v)


def _ssd_kernel(xbc_ref, misc_ref, z_ref, cw_ref, cb_ref, dtb_ref, alog_ref, dskip_ref, gn_ref, y_ref,
                state_ref, carry_ref):
    c = pl.program_id(1)
    t = CHUNK

    @pl.when(c == 0)
    def _():
        state_ref[...] = jnp.zeros_like(state_ref)
        carry_ref[...] = jnp.zeros_like(carry_ref)

    xr = xbc_ref[0].astype(F32)
    xcat = jnp.concatenate([carry_ref[...], xr], axis=0)
    carry_ref[...] = xr[t - CONV_CARRY:, :]
    conv = jnp.zeros((t, CONV_CH), F32) + cb_ref[...]
    for kk in range(CONV_K):
        sh = CONV_K - 1 - kk
        shifted = xcat if sh == 0 else pltpu.roll(xcat, sh, 0)
        conv = conv + cw_ref[kk:kk + 1, :] * shifted[CONV_CARRY:, :]
    xa = conv * _sigmoid(conv)
    xs = xa[:, :D_INNER]
    gw = SSD_GROUPS * SSD_STATE
    bmat = xa[:, D_INNER:D_INNER + gw]
    cmat = xa[:, D_INNER + gw:]

    u = misc_ref[0] + dtb_ref[...]
    dt = jnp.maximum(u, 0.0) + jnp.log(1.0 + jnp.exp(-jnp.abs(u)))
    a = -jnp.exp(alog_ref[...])
    lane = lax.broadcasted_iota(I32, (1, LANES), 1)
    adt = jnp.where(lane < B_HEADS, dt * a, 0.0)
    rowi = lax.broadcasted_iota(I32, (t, LANES), 0)
    acs = adt
    sh = 1
    while sh < t:
        acs = acs + jnp.where(rowi >= sh, pltpu.roll(acs, sh, 0), 0.0)
        sh *= 2
    acs_t = acs.T
    tri = lax.broadcasted_iota(I32, (t, t), 0) >= lax.broadcasted_iota(I32, (t, t), 1)

    rep = B_HEADS // SSD_GROUPS
    ys = []
    for g in range(SSD_GROUPS):
        bg = bmat[:, g * SSD_STATE:(g + 1) * SSD_STATE]
        cg = cmat[:, g * SSD_STATE:(g + 1) * SSD_STATE]
        bg16, cg16 = bg.astype(BF16), cg.astype(BF16)
        cb = _dot_nt(cg16, bg16)
        bg_t = bg.T
        for r in range(rep):
            hd = g * rep + r
            col = acs[:, hd:hd + 1]
            rw = acs_t[hd:hd + 1, :]
            last = acs_t[hd:hd + 1, t - 1:t]
            decay = jnp.exp(jnp.where(tri, col - rw, -jnp.inf))
            xh = xs[:, hd * SSD_HEAD_DIM:(hd + 1) * SSD_HEAD_DIM]
            xdt = (xh * dt[:, hd:hd + 1]).astype(BF16)
            y_diag = _dot((cb * decay).astype(BF16), xdt)
            prev = state_ref[hd]
            y_off = _dot(cg16, prev.astype(BF16)) * jnp.exp(col)
            new_state = _dot((bg_t * jnp.exp(last - rw)).astype(BF16), xdt)
            state_ref[hd] = prev * jnp.exp(last) + new_state
            ys.append(y_diag + y_off)
    y = jnp.concatenate(ys, axis=1) + xs * dskip_ref[...]
    zf = z_ref[0].astype(F32)
    y = y * (zf * _sigmoid(zf))
    y_ref[0] = _rms(y, gn_ref[...]).astype(BF16)


def _ssd(xbc, misc, z, cw, cb, dtb, alog, dskip, gn):
    b, s, _ = xbc.shape
    grid = (b, s // CHUNK)
    row = lambda i, j: (i, j, 0)
    fixed2 = lambda i, j: (0, 0)
    return pl.pallas_call(
        _ssd_kernel,
        grid=grid,
        in_specs=[
            pl.BlockSpec((1, CHUNK, CONV_CH), row),
            pl.BlockSpec((1, CHUNK, LANES), row),
            pl.BlockSpec((1, CHUNK, D_INNER), row),
            pl.BlockSpec((CONV_K, CONV_CH), fixed2),
            pl.BlockSpec((1, CONV_CH), fixed2),
            pl.BlockSpec((1, LANES), fixed2),
            pl.BlockSpec((1, LANES), fixed2),
            pl.BlockSpec((1, D_INNER), fixed2),
            pl.BlockSpec((1, D_INNER), fixed2),
        ],
        out_specs=pl.BlockSpec((1, CHUNK, D_INNER), row),
        out_shape=jax.ShapeDtypeStruct((b, s, D_INNER), BF16),
        scratch_shapes=[
            pltpu.VMEM((B_HEADS, SSD_STATE, SSD_HEAD_DIM), F32),
            pltpu.VMEM((CONV_CARRY, CONV_CH), F32),
        ],
        compiler_params=_cparams(("parallel", "arbitrary")),
        name="ssd_scan",
    )(xbc, misc, z, cw, cb, dtb, alog, dskip, gn)


XW = X_HEADS * X_HEAD_DIM


def _mem_kv_kernel(mem_ref, ln_ref, wkv_ref, kg_ref, hsum_ref, kbd_ref, vbd_ref):
    m = mem_ref.shape[1]
    mn = _rms(mem_ref[0], ln_ref[...]).astype(BF16)
    kv = _dot(mn, wkv_ref[...])
    k, v = kv[:, :XW], kv[:, XW:]
    ss = _dot((k * k).astype(BF16), hsum_ref[...])
    kn = (k * lax.rsqrt(ss * (1.0 / X_HEAD_DIM) + RMS_EPS) * kg_ref[...]).astype(BF16)
    v16 = v.astype(BF16)
    head_of_lane = lax.shift_right_arithmetic(lax.broadcasted_iota(I32, (1, XW), 1), jnp.int32(_LOG2_XHD))
    for hd in range(X_HEADS):
        keep = head_of_lane == hd
        kbd_ref[0, hd * m:(hd + 1) * m, :] = jnp.where(keep, kn, jnp.zeros_like(kn))
        vbd_ref[0, hd * m:(hd + 1) * m, :] = jnp.where(keep, v16, jnp.zeros_like(v16))


def _mem_kv(mem, ln, wkv, kg, hsum):
    b, m, d = mem.shape
    fixed2 = lambda i: (0, 0)
    return pl.pallas_call(
        _mem_kv_kernel,
        grid=(b,),
        in_specs=[
            pl.BlockSpec((1, m, d), lambda i: (i, 0, 0)),
            pl.BlockSpec((1, d), fixed2),
            pl.BlockSpec((d, 2 * XW), fixed2),
            pl.BlockSpec((1, XW), fixed2),
            pl.BlockSpec((XW, XW), fixed2),
        ],
        out_specs=[
            pl.BlockSpec((1, X_HEADS * m, XW), lambda i: (i, 0, 0)),
            pl.BlockSpec((1, X_HEADS * m, XW), lambda i: (i, 0, 0)),
        ],
        out_shape=[
            jax.ShapeDtypeStruct((b, X_HEADS * m, XW), BF16),
            jax.ShapeDtypeStruct((b, X_HEADS * m, XW), BF16),
        ],
        compiler_params=_cparams(("parallel",)),
        name="mem_kv",
    )(mem, ln, wkv, kg, hsum)


ROUTE_LANES = LANES
_GROUP_LANE0 = 0
_EXPERT_LANE0 = MOE_GROUPS
_LOG2_EPG = EXPERTS_PER_GROUP.bit_length() - 1
_LOG2_XHD = X_HEAD_DIM.bit_length() - 1


def _pack_bf16_pairs(v):
    w = v.shape[1] // 2
    r = v.astype(BF16).astype(F32)
    hi = lax.bitcast_convert_type(r[:, :w], U32)
    lo = lax.bitcast_convert_type(r[:, w:], U32)
    return (hi & jnp.uint32(0xFFFF0000)) | (lo >> jnp.uint32(16))


def _unpack_bf16_pairs(u):
    hi = lax.bitcast_convert_type(u & jnp.uint32(0xFFFF0000), F32)
    lo = lax.bitcast_convert_type(u << jnp.uint32(16), F32)
    return hi, lo


def _tail(x1, kbd_ref, vbd_ref, lnq_ref, wq_ref, qg_ref, hsum_ref, wo_ref, lnf_ref, rwh_ref, rwl_ref, rb_ref,
          ltri_ref, x2_ref, hfp_ref, route_ref, cnt_ref):
    tm = x1.shape[0]
    m = kbd_ref.shape[1] // X_HEADS
    hq = _rms(x1, lnq_ref[...]).astype(BF16)
    q = _dot(hq, wq_ref[...])
    ss = _dot((q * q).astype(BF16), hsum_ref[...])
    qn = (q * lax.rsqrt(ss * (1.0 / X_HEAD_DIM) + RMS_EPS) * qg_ref[...] * (X_HEAD_DIM ** -0.5)).astype(BF16)
    s = _dot_nt(qn, kbd_ref[0])
    ps = []
    for hd in range(X_HEADS):
        sh = s[:, hd * m:(hd + 1) * m]
        e = jnp.exp(sh - jnp.max(sh, axis=-1, keepdims=True))
        ps.append((e / jnp.sum(e, axis=-1, keepdims=True)).astype(BF16))
    o = _dot(jnp.concatenate(ps, axis=1), vbd_ref[0]).astype(BF16)
    x2 = x1 + _dot(o, wo_ref[...])
    x2_ref[0] = x2

    hf = _rms(x2, lnf_ref[...])
    hf_hi = hf.astype(BF16)
    hfp_ref[0] = _pack_bf16_pairs(hf)
    hf_lo = (hf - hf_hi.astype(F32)).astype(BF16)
    logits = _dot(hf_hi, rwh_ref[...]) + _dot(hf_hi, rwl_ref[...]) + _dot(hf_lo, rwh_ref[...]) + rb_ref[...]

    lane_i = lax.broadcasted_iota(I32, (tm, ROUTE_LANES), 1)
    lane = lane_i.astype(F32)
    big = float(ROUTE_LANES)
    neg = -jnp.inf
    gl = jnp.where(lane_i < MOE_GROUPS, logits, neg)
    gmax = jnp.max(gl, axis=-1, keepdims=True)
    gsum = jnp.sum(jnp.exp(gl - gmax), axis=-1, keepdims=True)
    g_p = 1.0 / gsum
    g_idx = jnp.min(jnp.where(gl == gmax, lane, big), axis=-1, keepdims=True)
    e_lane = lane_i - _EXPERT_LANE0
    grp_of_lane = lax.shift_right_arithmetic(e_lane, jnp.int32(_LOG2_EPG)).astype(F32)
    in_grp = (e_lane >= 0) & (e_lane < N_EXPERTS) & (grp_of_lane == g_idx)
    el = jnp.where(in_grp, logits, neg)
    emax = jnp.max(el, axis=-1, keepdims=True)
    idx1 = jnp.min(jnp.where(el == emax, lane, big), axis=-1, keepdims=True)
    el2 = jnp.where(lane == idx1, neg, el)
    emax2 = jnp.max(el2, axis=-1, keepdims=True)
    idx2 = jnp.min(jnp.where(el2 == emax2, lane, big), axis=-1, keepdims=True)
    r2 = jnp.exp(emax2 - emax)
    gate1 = g_p / (1.0 + r2)
    gate2 = g_p * r2 / (1.0 + r2)
    e1 = idx1 - float(_EXPERT_LANE0)
    e2 = idx2 - float(_EXPERT_LANE0)

    oh1 = (lane == e1).astype(F32)
    oh2 = (lane == e2).astype(F32)
    both = oh1 + oh2
    before = _dot(ltri_ref[...], both.astype(BF16))
    rank1 = jnp.sum(before * oh1, axis=-1, keepdims=True)
    rank2 = jnp.sum(before * oh2, axis=-1, keepdims=True)
    cnt_ref[0] = jnp.broadcast_to(jnp.sum(both, axis=0, keepdims=True), cnt_ref.shape[1:])

    route = jnp.where(lane == 0, e1, 0.0)
    route = jnp.where(lane == 1, e2, route)
    route = jnp.where(lane == 2, gate1, route)
    route = jnp.where(lane == 3, gate2, route)
    route = jnp.where(lane == 4, rank1, route)
    route = jnp.where(lane == 5, rank2, route)
    route_ref[0] = route


_TAIL_IN = 12


def _post_even_kernel(x_ref, a_ref, y_ref, wout_ref, *rest):
    tail_in, outs = rest[:_TAIL_IN], rest[_TAIL_IN:]
    half = wout_ref.shape[0] // 2
    x1 = x_ref[0] + _dot(a_ref[0], wout_ref[:half, :]) + _dot(y_ref[0], wout_ref[half:, :])
    _tail(x1, *tail_in, *outs)


def _post_pool_kernel(x_ref, ln_ref, pw_ref, pb_ref, ps_ref, *rest):
    tail_in, outs, carry_ref = rest[:_TAIL_IN], rest[_TAIL_IN:-1], rest[-1]
    j = pl.program_id(1)
    tm = x_ref.shape[1]

    @pl.when(j == 0)
    def _():
        carry_ref[...] = jnp.zeros_like(carry_ref)

    x = x_ref[0]
    h = _rms(x, ln_ref[...])
    pos = (j * tm + 1 + lax.broadcasted_iota(I32, (tm, 1), 0)).astype(F32)
    mixed = []
    for g, w in enumerate(POOL_WINDOWS):
        sl = slice(g * POOL_GROUP, (g + 1) * POOL_GROUP)
        hg = h[:, sl]
        acc = jnp.concatenate([carry_ref[:, sl], hg], axis=0)
        sh = 1
        while sh < w:
            acc = acc + pltpu.roll(acc, sh, 0)
            sh *= 2
        win = acc[POOL_CARRY:, :]
        dlt = win / jnp.minimum(pos, float(w)) - hg
        mixed.append(_dot(dlt.astype(BF16), pw_ref[g]))
    carry_ref[...] = h[tm - POOL_CARRY:, :]
    y = (jnp.concatenate(mixed, axis=1) + pb_ref[...]) * ps_ref[...]
    _tail(x + y, *tail_in, *outs)


def _post(kind, front_args, front_specs, tail_args, b, s, tm, scratch):
    d = D_MODEL
    m4 = tail_args[0].shape[1]
    row = lambda i, j: (i, j, 0)
    fixed2 = lambda i, j: (0, 0)
    per_b = lambda i, j: (i, 0, 0)
    tail_specs = [
        pl.BlockSpec((1, m4, XW), per_b),
        pl.BlockSpec((1, m4, XW), per_b),
        pl.BlockSpec((1, d), fixed2),
        pl.BlockSpec((d, XW), fixed2),
        pl.BlockSpec((1, XW), fixed2),
        pl.BlockSpec((XW, XW), fixed2),
        pl.BlockSpec((XW, d), fixed2),
        pl.BlockSpec((1, d), fixed2),
        pl.BlockSpec((d, ROUTE_LANES), fixed2),
        pl.BlockSpec((d, ROUTE_LANES), fixed2),
        pl.BlockSpec((1, ROUTE_LANES), fixed2),
        pl.BlockSpec((tm, tm), fixed2),
    ]
    nt = s // tm
    kernel = _post_even_kernel if kind == "even" else _post_pool_kernel
    return pl.pallas_call(
        kernel,
        grid=(b, nt),
        in_specs=front_specs + tail_specs,
        out_specs=[
            pl.BlockSpec((1, tm, d), row),
            pl.BlockSpec((1, tm, d // 2), row),
            pl.BlockSpec((1, tm, ROUTE_LANES), row),
            pl.BlockSpec((1, 8, ROUTE_LANES), lambda i, j: (i * nt + j, 0, 0)),
        ],
        out_shape=[
            jax.ShapeDtypeStruct((b, s, d), F32),
            jax.ShapeDtypeStruct((b, s, d // 2), U32),
            jax.ShapeDtypeStruct((b, s, ROUTE_LANES), F32),
            jax.ShapeDtypeStruct((b * nt, 8, ROUTE_LANES), F32),
        ],
        scratch_shapes=scratch,
        compiler_params=_cparams(("parallel", "arbitrary")),
        name="post_" + kind,
    )(*front_args, *tail_args)


FFN_ROWS = 512
DISPATCH_TOKENS = 1024
COMBINE_TOKENS = 512
DMA_UNROLL = 8


def _dispatch_kernel(dest_ref, tail_ref, hfp_ref, xb_ref, zero_ref, sem):
    i = pl.program_id(0)
    td = hfp_ref.shape[0]

    @pl.when(i == 0)
    def _():
        zero_ref[...] = jnp.zeros_like(zero_ref)

        def zero_copy(e):
            start = pl.multiple_of(tail_ref[e], FFN_ROWS)
            return pltpu.make_async_copy(zero_ref, xb_ref.at[pl.ds(start, FFN_ROWS)], sem)

        for e in range(N_EXPERTS):
            @pl.when(tail_ref[e] >= 0)
            def _():
                zero_copy(e).start()
        for e in range(N_EXPERTS):
            @pl.when(tail_ref[e] >= 0)
            def _():
                zero_copy(e).wait()

    def issue(t, carry):
        for k in range(TOP_K):
            dst = dest_ref[0, 0, TOP_K * t + k]
            pltpu.make_async_copy(hfp_ref.at[pl.ds(t, 1)], xb_ref.at[pl.ds(dst, 1)], sem).start(priority=k % 2)
        return carry

    lax.fori_loop(0, td, issue, 0, unroll=DMA_UNROLL)

    def drain(t, carry):
        for k in range(TOP_K):
            pltpu.make_async_copy(hfp_ref.at[pl.ds(0, 1)], xb_ref.at[pl.ds(0, 1)], sem).wait()
        return carry

    lax.fori_loop(0, td, drain, 0, unroll=DMA_UNROLL)


def _dispatch(dest, seg_tail, hfp, n_rows, td):
    n, w = hfp.shape
    nsteps = n // td
    return pl.pallas_call(
        _dispatch_kernel,
        grid=(nsteps,),
        in_specs=[
            pl.BlockSpec((1, 1, TOP_K * td), lambda i: (i, 0, 0), memory_space=pltpu.SMEM),
            pl.BlockSpec(memory_space=pltpu.SMEM),
            pl.BlockSpec((td, w), lambda i: (i, 0)),
        ],
        out_specs=pl.BlockSpec(memory_space=pl.ANY),
        scratch_shapes=[pltpu.VMEM((FFN_ROWS, w), U32), pltpu.SemaphoreType.DMA],
        out_shape=jax.ShapeDtypeStruct((n_rows, w), U32),
        compiler_params=_cparams(("arbitrary",)),
        name="moe_dispatch",
    )(dest.reshape(nsteps, 1, TOP_K * td), seg_tail, hfp)


def _ffn_kernel(be_ref, bi_ref, xb_ref, wg_ref, wu_ref, wd_ref, yb_ref, wg_s, wu_s, wd_s):
    i = pl.program_id(0)
    changed = jnp.logical_or(i == 0, be_ref[i] != be_ref[jnp.maximum(i - 1, 0)])

    @pl.when(changed)
    def _():
        wg_s[...] = wg_ref[0, 0].astype(BF16)
        wu_s[...] = wu_ref[0, 0].astype(BF16)
        wd_s[...] = wd_ref[0, 0].astype(BF16)

    hi, lo = _unpack_bf16_pairs(xb_ref[...])
    hi, lo = hi.astype(BF16), lo.astype(BF16)
    half = wg_s.shape[0] // 2
    gate = _dot(hi, wg_s[:half, :]) + _dot(lo, wg_s[half:, :])
    up = _dot(hi, wu_s[:half, :]) + _dot(lo, wu_s[half:, :])
    act = (gate * _sigmoid(gate) * up).astype(BF16)
    yb_ref[...] = _pack_bf16_pairs(_dot(act, wd_s[...]))


def _expert_ffn(block_e, block_i, xb, wg, wu, wd, layer):
    n_rows, w = xb.shape
    d, ff = wg.shape[2], wg.shape[3]
    n_blk = n_rows // FFN_ROWS
    return pl.pallas_call(
        _ffn_kernel,
        grid_spec=pltpu.PrefetchScalarGridSpec(
            num_scalar_prefetch=2,
            grid=(n_blk,),
            in_specs=[
                pl.BlockSpec((FFN_ROWS, w), lambda i, be, bi: (bi[i], 0)),
                pl.BlockSpec((1, 1, d, ff), lambda i, be, bi: (layer, be[i], 0, 0)),
                pl.BlockSpec((1, 1, d, ff), lambda i, be, bi: (layer, be[i], 0, 0)),
                pl.BlockSpec((1, 1, ff, d), lambda i, be, bi: (layer, be[i], 0, 0)),
            ],
            out_specs=pl.BlockSpec((FFN_ROWS, w), lambda i, be, bi: (bi[i], 0)),
            scratch_shapes=[pltpu.VMEM((d, ff), BF16), pltpu.VMEM((d, ff), BF16), pltpu.VMEM((ff, d), BF16)],
        ),
        out_shape=jax.ShapeDtypeStruct((n_rows, w), U32),
        compiler_params=_cparams(("arbitrary",)),
        name="moe_expert_ffn",
    )(block_e, block_i, xb, wg, wu, wd)


def _combine_kernel(dest_ref, x_ref, route_ref, yb_ref, o_ref, ybuf, sem):
    tc = x_ref.shape[0]

    def issue(t, carry):
        for k in range(TOP_K):
            src = dest_ref[0, 0, TOP_K * t + k]
            pltpu.make_async_copy(yb_ref.at[pl.ds(src, 1)], ybuf.at[k, pl.ds(t, 1)], sem).start(priority=k % 2)
        return carry

    lax.fori_loop(0, tc, issue, 0, unroll=DMA_UNROLL)

    def drain(t, carry):
        for k in range(TOP_K):
            pltpu.make_async_copy(yb_ref.at[pl.ds(0, 1)], ybuf.at[0, pl.ds(0, 1)], sem).wait()
        return carry

    lax.fori_loop(0, tc, drain, 0, unroll=DMA_UNROLL)

    half = x_ref.shape[1] // 2
    route = route_ref[...]
    g1, g2 = route[:, 2:3], route[:, 3:4]
    h1, l1 = _unpack_bf16_pairs(ybuf[0])
    h2, l2 = _unpack_bf16_pairs(ybuf[1])
    o_ref[:, :half] = x_ref[:, :half] + (h1 * g1 + h2 * g2)
    o_ref[:, half:] = x_ref[:, half:] + (l1 * g1 + l2 * g2)


def _combine(dest, x2, route, yb, tc):
    n, d = x2.shape
    w = yb.shape[1]
    nsteps = n // tc
    return pl.pallas_call(
        _combine_kernel,
        grid=(nsteps,),
        in_specs=[
            pl.BlockSpec((1, 1, TOP_K * tc), lambda i: (i, 0, 0), memory_space=pltpu.SMEM),
            pl.BlockSpec((tc, d), lambda i: (i, 0)),
            pl.BlockSpec((tc, ROUTE_LANES), lambda i: (i, 0)),
            pl.BlockSpec(memory_space=pl.ANY),
        ],
        out_specs=pl.BlockSpec((tc, d), lambda i: (i, 0)),
        scratch_shapes=[pltpu.VMEM((TOP_K, tc, w), U32), pltpu.SemaphoreType.DMA],
        out_shape=jax.ShapeDtypeStruct((n, d), F32),
        compiler_params=_cparams(("arbitrary",)),
        name="moe_combine",
    )(dest.reshape(nsteps, 1, TOP_K * tc), x2, route, yb)


def _moe(x2, hfp, route, counts, wg, wu, wd, layer, tm):
    b, s, d = x2.shape
    n = b * s
    route = route.reshape(n, ROUTE_LANES)
    cnt = counts[:, 0, :N_EXPERTS].astype(I32)
    total = jnp.sum(cnt, axis=0)
    padded = (total + FFN_ROWS - 1) // FFN_ROWS * FFN_ROWS
    pad_end = jnp.cumsum(padded)
    pad_start = pad_end - padded
    tile_base = pad_start[None, :] + jnp.cumsum(cnt, axis=0) - cnt
    experts = route[:, 0:TOP_K].astype(I32)
    ranks = route[:, 4:4 + TOP_K].astype(I32)
    base_tok = jnp.repeat(tile_base, tm, axis=0)
    dest = jnp.take_along_axis(base_tok, experts, axis=1) + ranks
    dest = dest.reshape(n * TOP_K)
    n_blk = (n * TOP_K) // FFN_ROWS + N_EXPERTS
    n_rows = n_blk * FFN_ROWS
    used = pad_end[-1] // FFN_ROWS
    block_i = jnp.minimum(jnp.arange(n_blk, dtype=I32), used - 1).astype(I32)
    ended = (pad_end[None, :] <= (block_i * FFN_ROWS)[:, None]).astype(I32)
    block_e = jnp.minimum(jnp.sum(ended, axis=1), N_EXPERTS - 1).astype(I32)
    seg_tail = jnp.where(padded > 0, pad_end - FFN_ROWS, -1).astype(I32)

    td = min(DISPATCH_TOKENS, n)
    tc = min(COMBINE_TOKENS, n)
    xb = _dispatch(dest, seg_tail, hfp.reshape(n, d // 2), n_rows, td)
    yb = _expert_ffn(block_e, block_i, xb, wg, wu, wd, layer)
    out = _combine(dest, x2.reshape(n, d), route, yb, tc)
    return out.reshape(b, s, d)


def _rope_lane_freq():
    inv = ROPE_THETA ** (-jnp.arange(0, ROPE_DIM // 2, dtype=F32) * 2.0 / ROPE_DIM)
    idx = np.full((HEAD_LANES,), -1, np.int64)
    for r in range(ROPE_DIM):
        idx[_head_lane(NOPE_DIM + r)] = r % ROPE_HALF
    return _gather_cols(inv[None, :], idx)


FAST_SOFTMAX_MAX_LOG2 = 60.0


def _score_bound_log2(qg, kg):
    return 1.02 * LOG2E * QK_DIM ** 0.5 * jnp.max(jnp.abs(qg)) * jnp.max(jnp.abs(kg))


def _head_gain(g):
    idx = np.full((HEAD_LANES,), -1, np.int64)
    for dd in range(QK_DIM):
        idx[_head_lane(dd)] = dd
    return _gather_cols(g[None, :], idx)


def kernel(x, mem, positions, ln_mix, w_in, q_lat_norm, w_uq, kv_lat_norm, w_ukv, q_norm, k_norm, conv_w, conv_b,
           dt_bias, a_log, d_skip, ssd_norm, w_out, pool_w, pool_b, pool_scale, ln_xq, ln_mem, xq_w, xkv_w, xq_norm,
           xk_norm, xo_w, ln_ffn, rg_w, rg_b, re_w, re_b, exp_w_gate, exp_w_up, exp_w_down):
    b, s, d = x.shape
    depth = ln_mix.shape[0]
    tm = min(512, s)
    tq = min(512, s)
    assert d == D_MODEL and s % tm == 0 and s % CHUNK == 0 and tm >= POOL_CARRY

    pos = positions.astype(F32)[..., None]
    invf = _rope_lane_freq()
    hsum = jnp.asarray(np.kron(np.eye(X_HEADS), np.ones((X_HEAD_DIM, X_HEAD_DIM))), BF16)
    ltri = jnp.asarray(np.tril(np.ones((tm, tm)), -1), BF16)
    row2 = lambda v: v.reshape(1, -1)
    lane_pad = lambda v: jnp.pad(v, (0, LANES - v.shape[0])).reshape(1, LANES)

    for layer in range(depth):
        j = layer // 2
        kbd, vbd = _mem_kv(mem, row2(ln_mem[layer]), xkv_w[layer].astype(BF16),
                           row2(jnp.tile(xk_norm[layer], X_HEADS)), hsum)
        rw = jnp.pad(jnp.concatenate([rg_w[layer], re_w[layer]], axis=1),
                     ((0, 0), (0, ROUTE_LANES - MOE_GROUPS - N_EXPERTS)))
        rw_hi = rw.astype(BF16)
        rw_lo = (rw - rw_hi.astype(F32)).astype(BF16)
        rb = lane_pad(jnp.concatenate([rg_b[layer], re_b[layer]]))
        tail_args = [kbd, vbd, row2(ln_xq[layer]), xq_w[layer].astype(BF16), row2(jnp.tile(xq_norm[layer], X_HEADS)),
                     hsum, xo_w[layer].astype(BF16), row2(ln_ffn[layer]), rw_hi, rw_lo, rb, ltri]
        row = lambda i, jj: (i, jj, 0)
        fixed2 = lambda i, jj: (0, 0)
        if layer % 2 == 0:
            win = _gather_cols(w_in[j], _win_col_index()).astype(BF16)
            wuq = _gather_cols(w_uq[j], _head_col_index(QK_DIM, 0, QK_DIM)).astype(BF16)
            wuk = _gather_cols(w_ukv[j], _head_col_index(NOPE_DIM + V_DIM, 0, NOPE_DIM)).astype(BF16)
            v_idx = np.full((A_HEADS * HEAD_LANES,), -1, np.int64)
            for hd in range(A_HEADS):
                v_idx[hd * HEAD_LANES:hd * HEAD_LANES + V_DIM] = hd * (NOPE_DIM + V_DIM) + NOPE_DIM + np.arange(V_DIM)
            wuv = _gather_cols(w_ukv[j], v_idx).astype(BF16)
            bound = _score_bound_log2(q_norm[j], k_norm[j])
            koff = jnp.zeros((1, HEAD_LANES), F32).at[0, SCORE_PAD_LANE].set(-bound)
            q, k, v, z, xbc, misc = _front_even(
                x, pos, invf, row2(ln_mix[layer]), win, row2(q_lat_norm[j]), wuq, row2(kv_lat_norm[j]), wuk, wuv,
                _head_gain(q_norm[j]), _head_gain(k_norm[j]), koff, tm)
            attn = lax.cond(bound <= FAST_SOFTMAX_MAX_LOG2,
                            functools.partial(_attention, tq=tq, online=False),
                            functools.partial(_attention, tq=tq, online=True), q, k, v)
            y = _ssd(xbc, misc, z, conv_w[j], row2(conv_b[j]), lane_pad(dt_bias[j]), lane_pad(a_log[j]),
                     row2(jnp.repeat(d_skip[j], SSD_HEAD_DIM)), row2(ssd_norm[j]))
            half = A_HEADS * V_DIM
            front_args = [x, attn, y, w_out[j].astype(BF16)]
            front_specs = [pl.BlockSpec((1, tm, d), row), pl.BlockSpec((1, tm, half), row),
                           pl.BlockSpec((1, tm, D_INNER), row), pl.BlockSpec((half + D_INNER, d), fixed2)]
            x2, hfp, route, counts = _post("even", front_args, front_specs, tail_args, b, s, tm, [])
        else:
            front_args = [x, row2(ln_mix[layer]), pool_w[j].astype(BF16), row2(pool_b[j]), row2(pool_scale[j])]
            front_specs = [pl.BlockSpec((1, tm, d), row), pl.BlockSpec((1, d), fixed2),
                           pl.BlockSpec((len(POOL_WINDOWS), POOL_GROUP, POOL_GROUP), lambda i, jj: (0, 0, 0)),
                           pl.BlockSpec((1, d), fixed2), pl.BlockSpec((1, d), fixed2)]
            x2, hfp, route, counts = _post("pool", front_args, front_specs, tail_args, b, s, tm,
                                           [pltpu.VMEM((POOL_CARRY, d), F32)])
        x = _moe(x2, hfp, route, counts, exp_w_gate, exp_w_up, exp_w_down, layer, tm)
    return x
```

```python
import functools

import numpy as np
import jax
import jax.numpy as jnp
from jax import lax
from jax.experimental import pallas as pl
from jax.experimental.pallas import tpu as pltpu
from jax.experimental.pallas import tpu_sc as plsc

F32 = jnp.float32
BF16 = jnp.bfloat16
U32 = jnp.uint32
I32 = jnp.int32

RMS_EPS = 1e-6
ROPE_THETA = 10000.0

D_MODEL = 1024
X_HEADS, X_HEAD_DIM = 4, 64
A_HEADS, NOPE_DIM, ROPE_DIM, V_DIM = 8, 64, 32, 64
QK_DIM = NOPE_DIM + ROPE_DIM
Q_LORA, KV_LORA = 256, 128
B_HEADS, SSD_HEAD_DIM, SSD_GROUPS, SSD_STATE, CONV_K, CHUNK = 8, 64, 2, 128, 4, 128
D_INNER = B_HEADS * SSD_HEAD_DIM
CONV_CH = D_INNER + 2 * SSD_GROUPS * SSD_STATE
POOL_WINDOWS = (2, 4, 8, 16)
POOL_GROUP = D_MODEL // 4
MOE_GROUPS, EXPERTS_PER_GROUP, TOP_K, EXPERT_FF = 4, 8, 2, 256
N_EXPERTS = MOE_GROUPS * EXPERTS_PER_GROUP

LANES = 128
HEAD_LANES = LANES
HALF_LANES = LANES // 2
ROPE_HALF = ROPE_DIM // 2
NOPE_HALF = NOPE_DIM // 2
POOL_CARRY = 16
CONV_CARRY = 8
VMEM_LIMIT = 56 * 1024 * 1024


def _cparams(sem):
    return pltpu.CompilerParams(dimension_semantics=sem, vmem_limit_bytes=VMEM_LIMIT)


def _rms(u, g):
    return u * lax.rsqrt(jnp.mean(u * u, axis=-1, keepdims=True) + RMS_EPS) * g


def _sigmoid(u):
    return 1.0 / (1.0 + jnp.exp(-u))


def _dot(a, b):
    return jnp.dot(a, b, preferred_element_type=F32)


def _dot_nt(a, b):
    return lax.dot_general(a, b, (((1,), (1,)), ((), ())), preferred_element_type=F32)


def _head_lane(d):
    if d < NOPE_HALF:
        return d
    if d < NOPE_DIM:
        return HALF_LANES + (d - NOPE_HALF)
    r = d - NOPE_DIM
    if r < ROPE_HALF:
        return NOPE_HALF + r
    return HALF_LANES + NOPE_HALF + (r - ROPE_HALF)


def _gather_cols(w, idx):
    w_ext = jnp.concatenate([w, jnp.zeros(w.shape[:-1] + (1,), w.dtype)], axis=-1)
    idx = np.where(idx < 0, w.shape[-1], idx)
    return jnp.take(w_ext, jnp.asarray(idx, dtype=jnp.int32), axis=-1)


IN_W = 2 * D_MODEL
_OFF_QLAT, _OFF_KVLAT, _OFF_MISC, _OFF_Z, _OFF_XBC = 0, 256, 384, 512, 1024


def _win_col_index():
    idx = np.full((IN_W,), -1, np.int64)
    idx[_OFF_QLAT:_OFF_QLAT + Q_LORA] = np.arange(Q_LORA)
    idx[_OFF_KVLAT:_OFF_KVLAT + KV_LORA] = Q_LORA + np.arange(KV_LORA)
    rope0 = Q_LORA + KV_LORA
    for r in range(ROPE_DIM):
        idx[_OFF_MISC + _head_lane(NOPE_DIM + r)] = rope0 + r
    z0 = rope0 + ROPE_DIM
    idx[_OFF_Z:_OFF_Z + D_INNER] = z0 + np.arange(D_INNER)
    xbc0 = z0 + D_INNER
    idx[_OFF_XBC:_OFF_XBC + CONV_CH] = xbc0 + np.arange(CONV_CH)
    dt0 = xbc0 + CONV_CH
    idx[_OFF_MISC:_OFF_MISC + B_HEADS] = dt0 + np.arange(B_HEADS)
    return idx


def _head_col_index(per_head, offset, count):
    idx = np.full((A_HEADS * HEAD_LANES,), -1, np.int64)
    for h in range(A_HEADS):
        for d in range(count):
            idx[h * HEAD_LANES + _head_lane(d)] = h * per_head + offset + d
    return idx


SCORE_PAD_LANE = NOPE_HALF + ROPE_HALF
ONES_LANE = V_DIM
LOG2E = 1.4426950408889634


def _front_even_kernel(x_ref, pos_ref, invf_ref, ln_ref, win_ref, qln_ref, wuq_ref, kvln_ref, wuk_ref, wuv_ref,
                       qg_ref, kg_ref, koff_ref, q_ref, k_ref, v_ref, z_ref, xbc_ref, misc_ref):
    x = x_ref[0]
    h = _rms(x, ln_ref[...]).astype(BF16)
    proj = _dot(h, win_ref[...])
    misc = proj[:, _OFF_MISC:_OFF_Z]
    z_ref[0] = proj[:, _OFF_Z:_OFF_XBC].astype(BF16)
    xbc_ref[0] = proj[:, _OFF_XBC:].astype(BF16)
    misc_ref[0] = misc
    ql = _rms(proj[:, _OFF_QLAT:_OFF_KVLAT], qln_ref[...]).astype(BF16)
    kvl = _rms(proj[:, _OFF_KVLAT:_OFF_MISC], kvln_ref[...]).astype(BF16)
    q = _dot(ql, wuq_ref[...])
    kn = _dot(kvl, wuk_ref[...])
    v = _dot(kvl, wuv_ref[...])
    lane = lax.broadcasted_iota(I32, (1, HEAD_LANES), 1)
    first_half = (lane >= NOPE_HALF) & (lane < NOPE_HALF + ROPE_HALF)
    second_half = (lane >= HALF_LANES + NOPE_HALF) & (lane < HALF_LANES + NOPE_HALF + ROPE_HALF)
    krope = jnp.where(first_half | second_half, misc, 0.0)
    kr_ss = jnp.sum(krope * krope, axis=-1, keepdims=True)
    ang = pos_ref[0] * invf_ref[...]
    cos_t = jnp.cos(ang)
    sin_t = jnp.where(first_half, -jnp.sin(ang), jnp.sin(ang))
    qg, kg = qg_ref[...], kg_ref[...]
    q_one = (lane == SCORE_PAD_LANE).astype(F32)
    v_one = (lane == ONES_LANE).astype(F32)
    k_off = koff_ref[...]
    q_scale = QK_DIM ** -0.5 * LOG2E
    for hd in range(A_HEADS):
        sl = slice(hd * HEAD_LANES, (hd + 1) * HEAD_LANES)
        qs = q[:, sl]
        ss = jnp.sum(qs * qs, axis=-1, keepdims=True)
        qn = qs * lax.rsqrt(ss * (1.0 / QK_DIM) + RMS_EPS) * qg
        qr = qn * cos_t + pltpu.roll(qn, HALF_LANES, 1) * sin_t
        q_ref[0, hd] = (qr * q_scale + q_one).astype(BF16)
        ks = kn[:, sl] + krope
        ss = jnp.sum(kn[:, sl] * kn[:, sl], axis=-1, keepdims=True) + kr_ss
        kk = ks * lax.rsqrt(ss * (1.0 / QK_DIM) + RMS_EPS) * kg
        kr = kk * cos_t + pltpu.roll(kk, HALF_LANES, 1) * sin_t
        k_ref[0, hd] = (kr + k_off).astype(BF16)
        v_ref[0, hd] = (v[:, sl] + v_one).astype(BF16)


def _front_even(x, pos, invf, ln, win, qln, wuq, kvln, wuk, wuv, qg, kg, koff, tm):
    b, s, d = x.shape
    grid = (b, s // tm)
    row = lambda i, j: (i, j, 0)
    fixed2 = lambda i, j: (0, 0)
    head_row = lambda i, j: (i, 0, j, 0)
    hw = A_HEADS * HEAD_LANES
    return pl.pallas_call(
        _front_even_kernel,
        grid=grid,
        in_specs=[
            pl.BlockSpec((1, tm, d), row),
            pl.BlockSpec((1, tm, 1), row),
            pl.BlockSpec((1, HEAD_LANES), fixed2),
            pl.BlockSpec((1, d), fixed2),
            pl.BlockSpec((d, IN_W), fixed2),
            pl.BlockSpec((1, Q_LORA), fixed2),
            pl.BlockSpec((Q_LORA, hw), fixed2),
            pl.BlockSpec((1, KV_LORA), fixed2),
            pl.BlockSpec((KV_LORA, hw), fixed2),
            pl.BlockSpec((KV_LORA, hw), fixed2),
            pl.BlockSpec((1, HEAD_LANES), fixed2),
            pl.BlockSpec((1, HEAD_LANES), fixed2),
            pl.BlockSpec((1, HEAD_LANES), fixed2),
        ],
        out_specs=[
            pl.BlockSpec((1, A_HEADS, tm, HEAD_LANES), head_row),
            pl.BlockSpec((1, A_HEADS, tm, HEAD_LANES), head_row),
            pl.BlockSpec((1, A_HEADS, tm, HEAD_LANES), head_row),
            pl.BlockSpec((1, tm, D_INNER), row),
            pl.BlockSpec((1, tm, CONV_CH), row),
            pl.BlockSpec((1, tm, HEAD_LANES), row),
        ],
        out_shape=[
            jax.ShapeDtypeStruct((b, A_HEADS, s, HEAD_LANES), BF16),
            jax.ShapeDtypeStruct((b, A_HEADS, s, HEAD_LANES), BF16),
            jax.ShapeDtypeStruct((b, A_HEADS, s, HEAD_LANES), BF16),
            jax.ShapeDtypeStruct((b, s, D_INNER), BF16),
            jax.ShapeDtypeStruct((b, s, CONV_CH), BF16),
            jax.ShapeDtypeStruct((b, s, HEAD_LANES), F32),
        ],
        compiler_params=_cparams(("parallel", "parallel")),
        name="front_even",
    )(x, pos, invf, ln, win, qln, wuq, kvln, wuk, wuv, qg, kg, koff)


HEADS_PER_STEP = HEAD_LANES // V_DIM


def _attn_kernel(q_ref, k_ref, v_ref, o_ref, *, tq, online):
    qi = pl.program_id(2)
    row = lax.broadcasted_iota(I32, (tq, tq), 0)
    col = lax.broadcasted_iota(I32, (tq, tq), 1)

    def head_step(hh, j, carry, masked):
        kj = k_ref[0, hh, pl.ds(j * tq, tq), :]
        vj = v_ref[0, hh, pl.ds(j * tq, tq), :]
        s = _dot_nt(q_ref[0, hh], kj)
        if masked:
            s = jnp.where(row >= col, s, -jnp.inf)
        if online:
            m, acc = carry
            m_new = jnp.maximum(m, jnp.max(s, axis=-1, keepdims=True))
            p = jnp.exp2(s - m_new).astype(BF16)
            return m_new, jnp.exp2(m - m_new) * acc + _dot(p, vj)
        return carry + _dot(jnp.exp2(s).astype(BF16), vj)

    def step(j, carries, masked):
        return tuple(head_step(hh, j, carries[hh], masked) for hh in range(HEADS_PER_STEP))

    acc0 = jnp.zeros((tq, HEAD_LANES), F32)
    init = (jnp.full((tq, 1), -jnp.inf, F32), acc0) if online else acc0
    carries = lax.fori_loop(0, qi, functools.partial(step, masked=False), (init,) * HEADS_PER_STEP)
    carries = step(qi, carries, True)
    outs = []
    for carry in carries:
        acc = carry[1] if online else carry
        outs.append(acc / acc[:, ONES_LANE:ONES_LANE + 1])
    lane = lax.broadcasted_iota(I32, (1, HEAD_LANES), 1)
    out = outs[0]
    for hh in range(1, HEADS_PER_STEP):
        out = jnp.where(lane >= hh * V_DIM, pltpu.roll(outs[hh], hh * V_DIM, 1), out)
    o_ref[0] = out.astype(BF16)


def _attention(q, k, v, tq, online):
    b, nh, s, _ = q.shape
    grid = (b, nh // HEADS_PER_STEP, s // tq)
    kv_spec = pl.BlockSpec((1, HEADS_PER_STEP, s, HEAD_LANES), lambda i, h, j: (i, h, 0, 0))
    return pl.pallas_call(
        functools.partial(_attn_kernel, tq=tq, online=online),
        grid=grid,
        in_specs=[
            pl.BlockSpec((1, HEADS_PER_STEP, tq, HEAD_LANES), lambda i, h, j: (i, h, j, 0)),
            kv_spec,
            kv_spec,
        ],
        out_specs=pl.BlockSpec((1, tq, HEAD_LANES), lambda i, h, j: (i, j, h)),
        out_shape=jax.ShapeDtypeStruct((b, s, nh * V_DIM), BF16),
        compiler_params=_cparams(("parallel", "parallel", "parallel")),
        name="mla_attention_online" if online else "mla_attention",
    )(q, k, v)


def _ssd_kernel(xbc_ref, misc_ref, z_ref, cw_ref, cb_ref, dtb_ref, alog_ref, dskip_ref, gn_ref, y_ref,
                state_ref, carry_ref):
    c = pl.program_id(1)
    t = CHUNK

    @pl.when(c == 0)
    def _():
        state_ref[...] = jnp.zeros_like(state_ref)
        carry_ref[...] = jnp.zeros_like(carry_ref)

    xr = xbc_ref[0].astype(F32)
    xcat = jnp.concatenate([carry_ref[...], xr], axis=0)
    carry_ref[...] = xr[t - CONV_CARRY:, :]
    conv = jnp.zeros((t, CONV_CH), F32) + cb_ref[...]
    for kk in range(CONV_K):
        sh = CONV_K - 1 - kk
        shifted = xcat if sh == 0 else pltpu.roll(xcat, sh, 0)
        conv = conv + cw_ref[kk:kk + 1, :] * shifted[CONV_CARRY:, :]
    xa = conv * _sigmoid(conv)
    xs = xa[:, :D_INNER]
    gw = SSD_GROUPS * SSD_STATE
    bmat = xa[:, D_INNER:D_INNER + gw]
    cmat = xa[:, D_INNER + gw:]

    u = misc_ref[0] + dtb_ref[...]
    dt = jnp.maximum(u, 0.0) + jnp.log(1.0 + jnp.exp(-jnp.abs(u)))
    a = -jnp.exp(alog_ref[...])
    lane = lax.broadcasted_iota(I32, (1, LANES), 1)
    adt = jnp.where(lane < B_HEADS, dt * a, 0.0)
    rowi = lax.broadcasted_iota(I32, (t, LANES), 0)
    acs = adt
    sh = 1
    while sh < t:
        acs = acs + jnp.where(rowi >= sh, pltpu.roll(acs, sh, 0), 0.0)
        sh *= 2
    acs_t = acs.T
    tri = lax.broadcasted_iota(I32, (t, t), 0) >= lax.broadcasted_iota(I32, (t, t), 1)

    rep = B_HEADS // SSD_GROUPS
    ys = []
    for g in range(SSD_GROUPS):
        bg = bmat[:, g * SSD_STATE:(g + 1) * SSD_STATE]
        cg = cmat[:, g * SSD_STATE:(g + 1) * SSD_STATE]
        bg16, cg16 = bg.astype(BF16), cg.astype(BF16)
        cb = _dot_nt(cg16, bg16)
        bg_t = bg.T
        for r in range(rep):
            hd = g * rep + r
            col = acs[:, hd:hd + 1]
            rw = acs_t[hd:hd + 1, :]
            last = acs_t[hd:hd + 1, t - 1:t]
            decay = jnp.exp(jnp.where(tri, col - rw, -jnp.inf))
            xh = xs[:, hd * SSD_HEAD_DIM:(hd + 1) * SSD_HEAD_DIM]
            xdt = (xh * dt[:, hd:hd + 1]).astype(BF16)
            y_diag = _dot((cb * decay).astype(BF16), xdt)
            prev = state_ref[hd]
            y_off = _dot(cg16, prev.astype(BF16)) * jnp.exp(col)
            new_state = _dot((bg_t * jnp.exp(last - rw)).astype(BF16), xdt)
            state_ref[hd] = prev * jnp.exp(last) + new_state
            ys.append(y_diag + y_off)
    y = jnp.concatenate(ys, axis=1) + xs * dskip_ref[...]
    zf = z_ref[0].astype(F32)
    y = y * (zf * _sigmoid(zf))
    y_ref[0] = _rms(y, gn_ref[...]).astype(BF16)


def _ssd(xbc, misc, z, cw, cb, dtb, alog, dskip, gn):
    b, s, _ = xbc.shape
    grid = (b, s // CHUNK)
    row = lambda i, j: (i, j, 0)
    fixed2 = lambda i, j: (0, 0)
    return pl.pallas_call(
        _ssd_kernel,
        grid=grid,
        in_specs=[
            pl.BlockSpec((1, CHUNK, CONV_CH), row),
            pl.BlockSpec((1, CHUNK, LANES), row),
            pl.BlockSpec((1, CHUNK, D_INNER), row),
            pl.BlockSpec((CONV_K, CONV_CH), fixed2),
            pl.BlockSpec((1, CONV_CH), fixed2),
            pl.BlockSpec((1, LANES), fixed2),
            pl.BlockSpec((1, LANES), fixed2),
            pl.BlockSpec((1, D_INNER), fixed2),
            pl.BlockSpec((1, D_INNER), fixed2),
        ],
        out_specs=pl.BlockSpec((1, CHUNK, D_INNER), row),
        out_shape=jax.ShapeDtypeStruct((b, s, D_INNER), BF16),
        scratch_shapes=[
            pltpu.VMEM((B_HEADS, SSD_STATE, SSD_HEAD_DIM), F32),
            pltpu.VMEM((CONV_CARRY, CONV_CH), F32),
        ],
        compiler_params=_cparams(("parallel", "arbitrary")),
        name="ssd_scan",
    )(xbc, misc, z, cw, cb, dtb, alog, dskip, gn)


XW = X_HEADS * X_HEAD_DIM


def _mem_kv_kernel(mem_ref, ln_ref, wkv_ref, kg_ref, hsum_ref, kbd_ref, vbd_ref):
    m = mem_ref.shape[1]
    mn = _rms(mem_ref[0], ln_ref[...]).astype(BF16)
    kv = _dot(mn, wkv_ref[...])
    k, v = kv[:, :XW], kv[:, XW:]
    ss = _dot((k * k).astype(BF16), hsum_ref[...])
    kn = (k * lax.rsqrt(ss * (1.0 / X_HEAD_DIM) + RMS_EPS) * kg_ref[...]).astype(BF16)
    v16 = v.astype(BF16)
    head_of_lane = lax.shift_right_arithmetic(lax.broadcasted_iota(I32, (1, XW), 1), jnp.int32(_LOG2_XHD))
    for hd in range(X_HEADS):
        keep = head_of_lane == hd
        kbd_ref[0, hd * m:(hd + 1) * m, :] = jnp.where(keep, kn, jnp.zeros_like(kn))
        vbd_ref[0, hd * m:(hd + 1) * m, :] = jnp.where(keep, v16, jnp.zeros_like(v16))


def _mem_kv(mem, ln, wkv, kg, hsum):
    b, m, d = mem.shape
    fixed2 = lambda i: (0, 0)
    return pl.pallas_call(
        _mem_kv_kernel,
        grid=(b,),
        in_specs=[
            pl.BlockSpec((1, m, d), lambda i: (i, 0, 0)),
            pl.BlockSpec((1, d), fixed2),
            pl.BlockSpec((d, 2 * XW), fixed2),
            pl.BlockSpec((1, XW), fixed2),
            pl.BlockSpec((XW, XW), fixed2),
        ],
        out_specs=[
            pl.BlockSpec((1, X_HEADS * m, XW), lambda i: (i, 0, 0)),
            pl.BlockSpec((1, X_HEADS * m, XW), lambda i: (i, 0, 0)),
        ],
        out_shape=[
            jax.ShapeDtypeStruct((b, X_HEADS * m, XW), BF16),
            jax.ShapeDtypeStruct((b, X_HEADS * m, XW), BF16),
        ],
        compiler_params=_cparams(("parallel",)),
        name="mem_kv",
    )(mem, ln, wkv, kg, hsum)


ROUTE_LANES = LANES
_GROUP_LANE0 = 0
_EXPERT_LANE0 = MOE_GROUPS
_LOG2_EPG = EXPERTS_PER_GROUP.bit_length() - 1
_LOG2_XHD = X_HEAD_DIM.bit_length() - 1


def _pack_bf16_pairs(v):
    w = v.shape[1] // 2
    r = v.astype(BF16).astype(F32)
    hi = lax.bitcast_convert_type(r[:, :w], U32)
    lo = lax.bitcast_convert_type(r[:, w:], U32)
    return (hi & jnp.uint32(0xFFFF0000)) | (lo >> jnp.uint32(16))


def _unpack_bf16_pairs(u):
    hi = lax.bitcast_convert_type(u & jnp.uint32(0xFFFF0000), F32)
    lo = lax.bitcast_convert_type(u << jnp.uint32(16), F32)
    return hi, lo


def _tail(x1, kbd_ref, vbd_ref, lnq_ref, wq_ref, qg_ref, hsum_ref, wo_ref, lnf_ref, rwh_ref, rwl_ref, rb_ref,
          ltri_ref, x2_ref, hfp_ref, route_ref, cnt_ref):
    tm = x1.shape[0]
    m = kbd_ref.shape[1] // X_HEADS
    hq = _rms(x1, lnq_ref[...]).astype(BF16)
    q = _dot(hq, wq_ref[...])
    ss = _dot((q * q).astype(BF16), hsum_ref[...])
    qn = (q * lax.rsqrt(ss * (1.0 / X_HEAD_DIM) + RMS_EPS) * qg_ref[...] * (X_HEAD_DIM ** -0.5)).astype(BF16)
    s = _dot_nt(qn, kbd_ref[0])
    ps = []
    for hd in range(X_HEADS):
        sh = s[:, hd * m:(hd + 1) * m]
        e = jnp.exp(sh - jnp.max(sh, axis=-1, keepdims=True))
        ps.append((e / jnp.sum(e, axis=-1, keepdims=True)).astype(BF16))
    o = _dot(jnp.concatenate(ps, axis=1), vbd_ref[0]).astype(BF16)
    x2 = x1 + _dot(o, wo_ref[...])
    x2_ref[0] = x2

    hf = _rms(x2, lnf_ref[...])
    hf_hi = hf.astype(BF16)
    hfp_ref[0] = _pack_bf16_pairs(hf)
    hf_lo = (hf - hf_hi.astype(F32)).astype(BF16)
    logits = _dot(hf_hi, rwh_ref[...]) + _dot(hf_hi, rwl_ref[...]) + _dot(hf_lo, rwh_ref[...]) + rb_ref[...]

    lane_i = lax.broadcasted_iota(I32, (tm, ROUTE_LANES), 1)
    lane = lane_i.astype(F32)
    big = float(ROUTE_LANES)
    neg = -jnp.inf
    gl = jnp.where(lane_i < MOE_GROUPS, logits, neg)
    gmax = jnp.max(gl, axis=-1, keepdims=True)
    gsum = jnp.sum(jnp.exp(gl - gmax), axis=-1, keepdims=True)
    g_p = 1.0 / gsum
    g_idx = jnp.min(jnp.where(gl == gmax, lane, big), axis=-1, keepdims=True)
    e_lane = lane_i - _EXPERT_LANE0
    grp_of_lane = lax.shift_right_arithmetic(e_lane, jnp.int32(_LOG2_EPG)).astype(F32)
    in_grp = (e_lane >= 0) & (e_lane < N_EXPERTS) & (grp_of_lane == g_idx)
    el = jnp.where(in_grp, logits, neg)
    emax = jnp.max(el, axis=-1, keepdims=True)
    idx1 = jnp.min(jnp.where(el == emax, lane, big), axis=-1, keepdims=True)
    el2 = jnp.where(lane == idx1, neg, el)
    emax2 = jnp.max(el2, axis=-1, keepdims=True)
    idx2 = jnp.min(jnp.where(el2 == emax2, lane, big), axis=-1, keepdims=True)
    r2 = jnp.exp(emax2 - emax)
    gate1 = g_p / (1.0 + r2)
    gate2 = g_p * r2 / (1.0 + r2)
    e1 = idx1 - float(_EXPERT_LANE0)
    e2 = idx2 - float(_EXPERT_LANE0)

    oh1 = (lane == e1).astype(F32)
    oh2 = (lane == e2).astype(F32)
    both = oh1 + oh2
    before = _dot(ltri_ref[...], both.astype(BF16))
    rank1 = jnp.sum(before * oh1, axis=-1, keepdims=True)
    rank2 = jnp.sum(before * oh2, axis=-1, keepdims=True)
    cnt_ref[0] = jnp.broadcast_to(jnp.sum(both, axis=0, keepdims=True), cnt_ref.shape[1:])

    route = jnp.where(lane == 0, e1, 0.0)
    route = jnp.where(lane == 1, e2, route)
    route = jnp.where(lane == 2, gate1, route)
    route = jnp.where(lane == 3, gate2, route)
    route = jnp.where(lane == 4, rank1, route)
    route = jnp.where(lane == 5, rank2, route)
    route_ref[0] = route


_TAIL_IN = 12


def _post_even_kernel(x_ref, a_ref, y_ref, wout_ref, *rest):
    tail_in, outs = rest[:_TAIL_IN], rest[_TAIL_IN:]
    half = wout_ref.shape[0] // 2
    x1 = x_ref[0] + _dot(a_ref[0], wout_ref[:half, :]) + _dot(y_ref[0], wout_ref[half:, :])
    _tail(x1, *tail_in, *outs)


def _post_pool_kernel(x_ref, ln_ref, pw_ref, pb_ref, ps_ref, *rest):
    tail_in, outs, carry_ref = rest[:_TAIL_IN], rest[_TAIL_IN:-1], rest[-1]
    j = pl.program_id(1)
    tm = x_ref.shape[1]

    @pl.when(j == 0)
    def _():
        carry_ref[...] = jnp.zeros_like(carry_ref)

    x = x_ref[0]
    h = _rms(x, ln_ref[...])
    pos = (j * tm + 1 + lax.broadcasted_iota(I32, (tm, 1), 0)).astype(F32)
    mixed = []
    for g, w in enumerate(POOL_WINDOWS):
        sl = slice(g * POOL_GROUP, (g + 1) * POOL_GROUP)
        hg = h[:, sl]
        acc = jnp.concatenate([carry_ref[:, sl], hg], axis=0)
        sh = 1
        while sh < w:
            acc = acc + pltpu.roll(acc, sh, 0)
            sh *= 2
        win = acc[POOL_CARRY:, :]
        dlt = win / jnp.minimum(pos, float(w)) - hg
        mixed.append(_dot(dlt.astype(BF16), pw_ref[g]))
    carry_ref[...] = h[tm - POOL_CARRY:, :]
    y = (jnp.concatenate(mixed, axis=1) + pb_ref[...]) * ps_ref[...]
    _tail(x + y, *tail_in, *outs)


def _post(kind, front_args, front_specs, tail_args, b, s, tm, scratch):
    d = D_MODEL
    m4 = tail_args[0].shape[1]
    row = lambda i, j: (i, j, 0)
    fixed2 = lambda i, j: (0, 0)
    per_b = lambda i, j: (i, 0, 0)
    tail_specs = [
        pl.BlockSpec((1, m4, XW), per_b),
        pl.BlockSpec((1, m4, XW), per_b),
        pl.BlockSpec((1, d), fixed2),
        pl.BlockSpec((d, XW), fixed2),
        pl.BlockSpec((1, XW), fixed2),
        pl.BlockSpec((XW, XW), fixed2),
        pl.BlockSpec((XW, d), fixed2),
        pl.BlockSpec((1, d), fixed2),
        pl.BlockSpec((d, ROUTE_LANES), fixed2),
        pl.BlockSpec((d, ROUTE_LANES), fixed2),
        pl.BlockSpec((1, ROUTE_LANES), fixed2),
        pl.BlockSpec((tm, tm), fixed2),
    ]
    nt = s // tm
    kernel = _post_even_kernel if kind == "even" else _post_pool_kernel
    return pl.pallas_call(
        kernel,
        grid=(b, nt),
        in_specs=front_specs + tail_specs,
        out_specs=[
            pl.BlockSpec((1, tm, d), row),
            pl.BlockSpec((1, tm, d // 2), row),
            pl.BlockSpec((1, tm, ROUTE_LANES), row),
            pl.BlockSpec((1, 8, ROUTE_LANES), lambda i, j: (i * nt + j, 0, 0)),
        ],
        out_shape=[
            jax.ShapeDtypeStruct((b, s, d), F32),
            jax.ShapeDtypeStruct((b, s, d // 2), U32),
            jax.ShapeDtypeStruct((b, s, ROUTE_LANES), F32),
            jax.ShapeDtypeStruct((b * nt, 8, ROUTE_LANES), F32),
        ],
        scratch_shapes=scratch,
        compiler_params=_cparams(("parallel", "arbitrary")),
        name="post_" + kind,
    )(*front_args, *tail_args)


FFN_ROWS = 512
DISPATCH_TOKENS = 1024
COMBINE_TOKENS = 512
DMA_UNROLL = 8


def _dispatch_kernel(dest_ref, tail_ref, hfp_ref, xb_ref, zero_ref, sem):
    i = pl.program_id(0)
    td = hfp_ref.shape[0]

    @pl.when(i == 0)
    def _():
        zero_ref[...] = jnp.zeros_like(zero_ref)

        def zero_copy(e):
            start = pl.multiple_of(tail_ref[e], FFN_ROWS)
            return pltpu.make_async_copy(zero_ref, xb_ref.at[pl.ds(start, FFN_ROWS)], sem)

        for e in range(N_EXPERTS):
            @pl.when(tail_ref[e] >= 0)
            def _():
                zero_copy(e).start()
        for e in range(N_EXPERTS):
            @pl.when(tail_ref[e] >= 0)
            def _():
                zero_copy(e).wait()

    def issue(t, carry):
        for k in range(TOP_K):
            dst = dest_ref[0, 0, TOP_K * t + k]
            pltpu.make_async_copy(hfp_ref.at[pl.ds(t, 1)], xb_ref.at[pl.ds(dst, 1)], sem).start(priority=k % 2)
        return carry

    lax.fori_loop(0, td, issue, 0, unroll=DMA_UNROLL)

    def drain(t, carry):
        for k in range(TOP_K):
            pltpu.make_async_copy(hfp_ref.at[pl.ds(0, 1)], xb_ref.at[pl.ds(0, 1)], sem).wait()
        return carry

    lax.fori_loop(0, td, drain, 0, unroll=DMA_UNROLL)


def _dispatch(dest, seg_tail, hfp, n_rows, td):
    n, w = hfp.shape
    nsteps = n // td
    return pl.pallas_call(
        _dispatch_kernel,
        grid=(nsteps,),
        in_specs=[
            pl.BlockSpec((1, 1, TOP_K * td), lambda i: (i, 0, 0), memory_space=pltpu.SMEM),
            pl.BlockSpec(memory_space=pltpu.SMEM),
            pl.BlockSpec((td, w), lambda i: (i, 0)),
        ],
        out_specs=pl.BlockSpec(memory_space=pl.ANY),
        scratch_shapes=[pltpu.VMEM((FFN_ROWS, w), U32), pltpu.SemaphoreType.DMA],
        out_shape=jax.ShapeDtypeStruct((n_rows, w), U32),
        compiler_params=_cparams(("arbitrary",)),
        name="moe_dispatch",
    )(dest.reshape(nsteps, 1, TOP_K * td), seg_tail, hfp)


ROW_PARTS = 2


def _ffn_kernel(be_ref, bi_ref, xb_ref, wg_ref, wu_ref, wd_ref, *refs):
    yb_refs, (wg_s, wu_s, wd_s) = refs[:ROW_PARTS], refs[ROW_PARTS:]
    i = pl.program_id(0)
    changed = jnp.logical_or(i == 0, be_ref[i] != be_ref[jnp.maximum(i - 1, 0)])

    @pl.when(changed)
    def _():
        wg_s[...] = wg_ref[0, 0].astype(BF16)
        wu_s[...] = wu_ref[0, 0].astype(BF16)
        wd_s[...] = wd_ref[0, 0].astype(BF16)

    hi, lo = _unpack_bf16_pairs(xb_ref[...])
    hi, lo = hi.astype(BF16), lo.astype(BF16)
    half = wg_s.shape[0] // 2
    gate = _dot(hi, wg_s[:half, :]) + _dot(lo, wg_s[half:, :])
    up = _dot(hi, wu_s[:half, :]) + _dot(lo, wu_s[half:, :])
    act = (gate * _sigmoid(gate) * up).astype(BF16)
    packed = _pack_bf16_pairs(_dot(act, wd_s[...]))
    pw = packed.shape[1] // ROW_PARTS
    for c in range(ROW_PARTS):
        yb_refs[c][...] = packed[:, c * pw:(c + 1) * pw]


def _expert_ffn(block_e, block_i, xb, wg, wu, wd, layer):
    n_rows, w = xb.shape
    d, ff = wg.shape[2], wg.shape[3]
    n_blk = n_rows // FFN_ROWS
    return pl.pallas_call(
        _ffn_kernel,
        grid_spec=pltpu.PrefetchScalarGridSpec(
            num_scalar_prefetch=2,
            grid=(n_blk,),
            in_specs=[
                pl.BlockSpec((FFN_ROWS, w), lambda i, be, bi: (bi[i], 0)),
                pl.BlockSpec((1, 1, d, ff), lambda i, be, bi: (layer, be[i], 0, 0)),
                pl.BlockSpec((1, 1, d, ff), lambda i, be, bi: (layer, be[i], 0, 0)),
                pl.BlockSpec((1, 1, ff, d), lambda i, be, bi: (layer, be[i], 0, 0)),
            ],
            out_specs=[pl.BlockSpec((FFN_ROWS, w // ROW_PARTS), lambda i, be, bi: (bi[i], 0))] * ROW_PARTS,
            scratch_shapes=[pltpu.VMEM((d, ff), BF16), pltpu.VMEM((d, ff), BF16), pltpu.VMEM((ff, d), BF16)],
        ),
        out_shape=[jax.ShapeDtypeStruct((n_rows, w // ROW_PARTS), U32)] * ROW_PARTS,
        compiler_params=_cparams(("arbitrary",)),
        name="moe_expert_ffn",
    )(block_e, block_i, xb, wg, wu, wd)


SC_GATHER_WINDOW = 128


def _sc_gather_rows(table, idx):
    m, w = idx.shape[0], table.shape[1]
    mesh = plsc.VectorSubcoreMesh(core_axis_name="core", subcore_axis_name="subcore")

    @pl.kernel(out_type=jax.ShapeDtypeStruct((m, w), table.dtype), mesh=mesh, name="moe_row_gather")
    def gather(t_hbm, i_hbm, o_hbm):
        def body(i_vmem, o_vmem):
            pltpu.sync_copy(t_hbm.at[i_vmem.at[0]], o_vmem)

        pltpu.emit_pipeline(
            body,
            grid=(m // SC_GATHER_WINDOW,),
            in_specs=[pl.BlockSpec((1, SC_GATHER_WINDOW), lambda i: (0, i))],
            out_specs=[pl.BlockSpec((SC_GATHER_WINDOW, w), lambda i: (i, 0))],
            core_axis_name=("core", "subcore"),
            dimension_semantics=(pltpu.PARALLEL,),
        )(i_hbm, o_hbm)

    return gather(table, idx.reshape(1, m))


def _combine_kernel(x_ref, route_ref, *refs):
    y_refs, o_ref = refs[:-1], refs[-1]
    half = x_ref.shape[1] // 2
    route = route_ref[...]
    g1, g2 = route[:, 2:3], route[:, 3:4]
    for c in range(ROW_PARTS):
        w = y_refs[2 * c].shape[1]
        h1, l1 = _unpack_bf16_pairs(y_refs[2 * c][...])
        h2, l2 = _unpack_bf16_pairs(y_refs[2 * c + 1][...])
        hs, ls = slice(c * w, (c + 1) * w), slice(half + c * w, half + (c + 1) * w)
        o_ref[:, hs] = x_ref[:, hs] + (h1 * g1 + h2 * g2)
        o_ref[:, ls] = x_ref[:, ls] + (l1 * g1 + l2 * g2)


def _combine(x2, route, ytoks, tc):
    n, d = x2.shape
    w = ytoks[0].shape[1]
    nsteps = n // tc
    y_specs, y_args = [], []
    for ytok in ytoks:
        y_specs += [pl.BlockSpec((tc, w), lambda i: (i, 0)), pl.BlockSpec((tc, w), lambda i: (i + nsteps, 0))]
        y_args += [ytok, ytok]
    return pl.pallas_call(
        _combine_kernel,
        grid=(nsteps,),
        in_specs=[pl.BlockSpec((tc, d), lambda i: (i, 0)), pl.BlockSpec((tc, ROUTE_LANES), lambda i: (i, 0))] + y_specs,
        out_specs=pl.BlockSpec((tc, d), lambda i: (i, 0)),
        out_shape=jax.ShapeDtypeStruct((n, d), F32),
        compiler_params=_cparams(("parallel",)),
        name="moe_combine",
    )(x2, route, *y_args)


def _moe(x2, hfp, route, counts, wg, wu, wd, layer, tm):
    b, s, d = x2.shape
    n = b * s
    route = route.reshape(n, ROUTE_LANES)
    cnt = counts[:, 0, :N_EXPERTS].astype(I32)
    total = jnp.sum(cnt, axis=0)
    padded = (total + FFN_ROWS - 1) // FFN_ROWS * FFN_ROWS
    pad_end = jnp.cumsum(padded)
    pad_start = pad_end - padded
    tile_base = pad_start[None, :] + jnp.cumsum(cnt, axis=0) - cnt
    experts = route[:, 0:TOP_K].astype(I32)
    ranks = route[:, 4:4 + TOP_K].astype(I32)
    base_tok = jnp.repeat(tile_base, tm, axis=0)
    dest = jnp.take_along_axis(base_tok, experts, axis=1) + ranks
    dest_by_slot = dest.T.reshape(n * TOP_K)
    dest = dest.reshape(n * TOP_K)
    n_blk = (n * TOP_K) // FFN_ROWS + N_EXPERTS
    n_rows = n_blk * FFN_ROWS
    used = pad_end[-1] // FFN_ROWS
    block_i = jnp.minimum(jnp.arange(n_blk, dtype=I32), used - 1).astype(I32)
    ended = (pad_end[None, :] <= (block_i * FFN_ROWS)[:, None]).astype(I32)
    block_e = jnp.minimum(jnp.sum(ended, axis=1), N_EXPERTS - 1).astype(I32)
    seg_tail = jnp.where(padded > 0, pad_end - FFN_ROWS, -1).astype(I32)

    td = min(DISPATCH_TOKENS, n)
    tc = min(COMBINE_TOKENS, n)
    xb = _dispatch(dest, seg_tail, hfp.reshape(n, d // 2), n_rows, td)
    yb = _expert_ffn(block_e, block_i, xb, wg, wu, wd, layer)
    ytoks = [_sc_gather_rows(part, dest_by_slot) for part in yb]
    out = _combine(x2.reshape(n, d), route, ytoks, tc)
    return out.reshape(b, s, d)


def _rope_lane_freq():
    inv = ROPE_THETA ** (-jnp.arange(0, ROPE_DIM // 2, dtype=F32) * 2.0 / ROPE_DIM)
    idx = np.full((HEAD_LANES,), -1, np.int64)
    for r in range(ROPE_DIM):
        idx[_head_lane(NOPE_DIM + r)] = r % ROPE_HALF
    return _gather_cols(inv[None, :], idx)


FAST_SOFTMAX_MAX_LOG2 = 60.0


def _score_bound_log2(qg, kg):
    return 1.02 * LOG2E * QK_DIM ** 0.5 * jnp.max(jnp.abs(qg)) * jnp.max(jnp.abs(kg))


def _head_gain(g):
    idx = np.full((HEAD_LANES,), -1, np.int64)
    for dd in range(QK_DIM):
        idx[_head_lane(dd)] = dd
    return _gather_cols(g[None, :], idx)


def kernel(x, mem, positions, ln_mix, w_in, q_lat_norm, w_uq, kv_lat_norm, w_ukv, q_norm, k_norm, conv_w, conv_b,
           dt_bias, a_log, d_skip, ssd_norm, w_out, pool_w, pool_b, pool_scale, ln_xq, ln_mem, xq_w, xkv_w, xq_norm,
           xk_norm, xo_w, ln_ffn, rg_w, rg_b, re_w, re_b, exp_w_gate, exp_w_up, exp_w_down):
    b, s, d = x.shape
    depth = ln_mix.shape[0]
    tm = min(512, s)
    tq = min(512, s)
    assert d == D_MODEL and s % tm == 0 and s % CHUNK == 0 and tm >= POOL_CARRY

    pos = positions.astype(F32)[..., None]
    invf = _rope_lane_freq()
    hsum = jnp.asarray(np.kron(np.eye(X_HEADS), np.ones((X_HEAD_DIM, X_HEAD_DIM))), BF16)
    ltri = jnp.asarray(np.tril(np.ones((tm, tm)), -1), BF16)
    row2 = lambda v: v.reshape(1, -1)
    lane_pad = lambda v: jnp.pad(v, (0, LANES - v.shape[0])).reshape(1, LANES)

    for layer in range(depth):
        j = layer // 2
        kbd, vbd = _mem_kv(mem, row2(ln_mem[layer]), xkv_w[layer].astype(BF16),
                           row2(jnp.tile(xk_norm[layer], X_HEADS)), hsum)
        rw = jnp.pad(jnp.concatenate([rg_w[layer], re_w[layer]], axis=1),
                     ((0, 0), (0, ROUTE_LANES - MOE_GROUPS - N_EXPERTS)))
        rw_hi = rw.astype(BF16)
        rw_lo = (rw - rw_hi.astype(F32)).astype(BF16)
        rb = lane_pad(jnp.concatenate([rg_b[layer], re_b[layer]]))
        tail_args = [kbd, vbd, row2(ln_xq[layer]), xq_w[layer].astype(BF16), row2(jnp.tile(xq_norm[layer], X_HEADS)),
                     hsum, xo_w[layer].astype(BF16), row2(ln_ffn[layer]), rw_hi, rw_lo, rb, ltri]
        row = lambda i, jj: (i, jj, 0)
        fixed2 = lambda i, jj: (0, 0)
        if layer % 2 == 0:
            win = _gather_cols(w_in[j], _win_col_index()).astype(BF16)
            wuq = _gather_cols(w_uq[j], _head_col_index(QK_DIM, 0, QK_DIM)).astype(BF16)
            wuk = _gather_cols(w_ukv[j], _head_col_index(NOPE_DIM + V_DIM, 0, NOPE_DIM)).astype(BF16)
            v_idx = np.full((A_HEADS * HEAD_LANES,), -1, np.int64)
            for hd in range(A_HEADS):
                v_idx[hd * HEAD_LANES:hd * HEAD_LANES + V_DIM] = hd * (NOPE_DIM + V_DIM) + NOPE_DIM + np.arange(V_DIM)
            wuv = _gather_cols(w_ukv[j], v_idx).astype(BF16)
            bound = _score_bound_log2(q_norm[j], k_norm[j])
            koff = jnp.zeros((1, HEAD_LANES), F32).at[0, SCORE_PAD_LANE].set(-bound)
            q, k, v, z, xbc, misc = _front_even(
                x, pos, invf, row2(ln_mix[layer]), win, row2(q_lat_norm[j]), wuq, row2(kv_lat_norm[j]), wuk, wuv,
                _head_gain(q_norm[j]), _head_gain(k_norm[j]), koff, tm)
            attn = lax.cond(bound <= FAST_SOFTMAX_MAX_LOG2,
                            functools.partial(_attention, tq=tq, online=False),
                            functools.partial(_attention, tq=tq, online=True), q, k, v)
            y = _ssd(xbc, misc, z, conv_w[j], row2(conv_b[j]), lane_pad(dt_bias[j]), lane_pad(a_log[j]),
                     row2(jnp.repeat(d_skip[j], SSD_HEAD_DIM)), row2(ssd_norm[j]))
            half = A_HEADS * V_DIM
            front_args = [x, attn, y, w_out[j].astype(BF16)]
            front_specs = [pl.BlockSpec((1, tm, d), row), pl.BlockSpec((1, tm, half), row),
                           pl.BlockSpec((1, tm, D_INNER), row), pl.BlockSpec((half + D_INNER, d), fixed2)]
            x2, hfp, route, counts = _post("even", front_args, front_specs, tail_args, b, s, tm, [])
        else:
            front_args = [x, row2(ln_mix[layer]), pool_w[j].astype(BF16), row2(pool_b[j]), row2(pool_scale[j])]
            front_specs = [pl.BlockSpec((1, tm, d), row), pl.BlockSpec((1, d), fixed2),
                           pl.BlockSpec((len(POOL_WINDOWS), POOL_GROUP, POOL_GROUP), lambda i, jj: (0, 0, 0)),
                           pl.BlockSpec((1, d), fixed2), pl.BlockSpec((1, d), fixed2)]
            x2, hfp, route, counts = _post("pool", front_args, front_specs, tail_args, b, s, tm,
                                           [pltpu.VMEM((POOL_CARRY, d), F32)])
        x = _moe(x2, hfp, route, counts, exp_w_gate, exp_w_up, exp_w_down, layer, tm)
    return x
```

```python
import functools

import numpy as np
import jax
import jax.numpy as jnp
from jax import lax
from jax.experimental import pallas as pl
from jax.experimental.pallas import tpu as pltpu
from jax.experimental.pallas import tpu_sc as plsc

F32 = jnp.float32
BF16 = jnp.bfloat16
U32 = jnp.uint32
I32 = jnp.int32

RMS_EPS = 1e-6
ROPE_THETA = 10000.0

D_MODEL = 1024
X_HEADS, X_HEAD_DIM = 4, 64
A_HEADS, NOPE_DIM, ROPE_DIM, V_DIM = 8, 64, 32, 64
QK_DIM = NOPE_DIM + ROPE_DIM
Q_LORA, KV_LORA = 256, 128
B_HEADS, SSD_HEAD_DIM, SSD_GROUPS, SSD_STATE, CONV_K, CHUNK = 8, 64, 2, 128, 4, 128
D_INNER = B_HEADS * SSD_HEAD_DIM
CONV_CH = D_INNER + 2 * SSD_GROUPS * SSD_STATE
POOL_WINDOWS = (2, 4, 8, 16)
POOL_GROUP = D_MODEL // 4
MOE_GROUPS, EXPERTS_PER_GROUP, TOP_K, EXPERT_FF = 4, 8, 2, 256
N_EXPERTS = MOE_GROUPS * EXPERTS_PER_GROUP

LANES = 128
HEAD_LANES = LANES
HALF_LANES = LANES // 2
ROPE_HALF = ROPE_DIM // 2
NOPE_HALF = NOPE_DIM // 2
POOL_CARRY = 16
CONV_CARRY = 8
VMEM_LIMIT = 56 * 1024 * 1024


def _cparams(sem):
    return pltpu.CompilerParams(dimension_semantics=sem, vmem_limit_bytes=VMEM_LIMIT)


def _rms(u, g):
    return u * lax.rsqrt(jnp.mean(u * u, axis=-1, keepdims=True) + RMS_EPS) * g


def _sigmoid(u):
    return 1.0 / (1.0 + jnp.exp(-u))


def _dot(a, b):
    return jnp.dot(a, b, preferred_element_type=F32)


def _dot_nt(a, b):
    return lax.dot_general(a, b, (((1,), (1,)), ((), ())), preferred_element_type=F32)


def _head_lane(d):
    if d < NOPE_HALF:
        return d
    if d < NOPE_DIM:
        return HALF_LANES + (d - NOPE_HALF)
    r = d - NOPE_DIM
    if r < ROPE_HALF:
        return NOPE_HALF + r
    return HALF_LANES + NOPE_HALF + (r - ROPE_HALF)


def _gather_cols(w, idx):
    w_ext = jnp.concatenate([w, jnp.zeros(w.shape[:-1] + (1,), w.dtype)], axis=-1)
    idx = np.where(idx < 0, w.shape[-1], idx)
    return jnp.take(w_ext, jnp.asarray(idx, dtype=jnp.int32), axis=-1)


IN_W = 2 * D_MODEL
_OFF_QLAT, _OFF_KVLAT, _OFF_MISC, _OFF_Z, _OFF_XBC = 0, 256, 384, 512, 1024


def _win_col_index():
    idx = np.full((IN_W,), -1, np.int64)
    idx[_OFF_QLAT:_OFF_QLAT + Q_LORA] = np.arange(Q_LORA)
    idx[_OFF_KVLAT:_OFF_KVLAT + KV_LORA] = Q_LORA + np.arange(KV_LORA)
    rope0 = Q_LORA + KV_LORA
    for r in range(ROPE_DIM):
        idx[_OFF_MISC + _head_lane(NOPE_DIM + r)] = rope0 + r
    z0 = rope0 + ROPE_DIM
    idx[_OFF_Z:_OFF_Z + D_INNER] = z0 + np.arange(D_INNER)
    xbc0 = z0 + D_INNER
    idx[_OFF_XBC:_OFF_XBC + CONV_CH] = xbc0 + np.arange(CONV_CH)
    dt0 = xbc0 + CONV_CH
    idx[_OFF_MISC:_OFF_MISC + B_HEADS] = dt0 + np.arange(B_HEADS)
    return idx


def _head_col_index(per_head, offset, count):
    idx = np.full((A_HEADS * HEAD_LANES,), -1, np.int64)
    for h in range(A_HEADS):
        for d in range(count):
            idx[h * HEAD_LANES + _head_lane(d)] = h * per_head + offset + d
    return idx


SCORE_PAD_LANE = NOPE_HALF + ROPE_HALF
ONES_LANE = V_DIM
LOG2E = 1.4426950408889634


def _front_even_kernel(x_ref, pos_ref, invf_ref, ln_ref, win_ref, qln_ref, wuq_ref, kvln_ref, wuk_ref, wuv_ref,
                       qg_ref, kg_ref, koff_ref, q_ref, k_ref, v_ref, z_ref, xbc_ref, misc_ref):
    x = x_ref[0]
    h = _rms(x, ln_ref[...]).astype(BF16)
    proj = _dot(h, win_ref[...])
    misc = proj[:, _OFF_MISC:_OFF_Z]
    z_ref[0] = proj[:, _OFF_Z:_OFF_XBC].astype(BF16)
    xbc_ref[0] = proj[:, _OFF_XBC:].astype(BF16)
    misc_ref[0] = misc
    ql = _rms(proj[:, _OFF_QLAT:_OFF_KVLAT], qln_ref[...]).astype(BF16)
    kvl = _rms(proj[:, _OFF_KVLAT:_OFF_MISC], kvln_ref[...]).astype(BF16)
    q = _dot(ql, wuq_ref[...])
    kn = _dot(kvl, wuk_ref[...])
    v = _dot(kvl, wuv_ref[...])
    lane = lax.broadcasted_iota(I32, (1, HEAD_LANES), 1)
    first_half = (lane >= NOPE_HALF) & (lane < NOPE_HALF + ROPE_HALF)
    second_half = (lane >= HALF_LANES + NOPE_HALF) & (lane < HALF_LANES + NOPE_HALF + ROPE_HALF)
    krope = jnp.where(first_half | second_half, misc, 0.0)
    kr_ss = jnp.sum(krope * krope, axis=-1, keepdims=True)
    ang = pos_ref[0] * invf_ref[...]
    cos_t = jnp.cos(ang)
    sin_t = jnp.where(first_half, -jnp.sin(ang), jnp.sin(ang))
    qg, kg = qg_ref[...], kg_ref[...]
    q_one = (lane == SCORE_PAD_LANE).astype(F32)
    v_one = (lane == ONES_LANE).astype(F32)
    k_off = koff_ref[...]
    q_scale = QK_DIM ** -0.5 * LOG2E
    for hd in range(A_HEADS):
        sl = slice(hd * HEAD_LANES, (hd + 1) * HEAD_LANES)
        qs = q[:, sl]
        ss = jnp.sum(qs * qs, axis=-1, keepdims=True)
        qn = qs * lax.rsqrt(ss * (1.0 / QK_DIM) + RMS_EPS) * qg
        qr = qn * cos_t + pltpu.roll(qn, HALF_LANES, 1) * sin_t
        q_ref[0, hd] = (qr * q_scale + q_one).astype(BF16)
        ks = kn[:, sl] + krope
        ss = jnp.sum(kn[:, sl] * kn[:, sl], axis=-1, keepdims=True) + kr_ss
        kk = ks * lax.rsqrt(ss * (1.0 / QK_DIM) + RMS_EPS) * kg
        kr = kk * cos_t + pltpu.roll(kk, HALF_LANES, 1) * sin_t
        k_ref[0, hd] = (kr + k_off).astype(BF16)
        v_ref[0, hd] = (v[:, sl] + v_one).astype(BF16)


def _front_even(x, pos, invf, ln, win, qln, wuq, kvln, wuk, wuv, qg, kg, koff, tm):
    b, s, d = x.shape
    grid = (b, s // tm)
    row = lambda i, j: (i, j, 0)
    fixed2 = lambda i, j: (0, 0)
    head_row = lambda i, j: (i, 0, j, 0)
    hw = A_HEADS * HEAD_LANES
    return pl.pallas_call(
        _front_even_kernel,
        grid=grid,
        in_specs=[
            pl.BlockSpec((1, tm, d), row),
            pl.BlockSpec((1, tm, 1), row),
            pl.BlockSpec((1, HEAD_LANES), fixed2),
            pl.BlockSpec((1, d), fixed2),
            pl.BlockSpec((d, IN_W), fixed2),
            pl.BlockSpec((1, Q_LORA), fixed2),
            pl.BlockSpec((Q_LORA, hw), fixed2),
            pl.BlockSpec((1, KV_LORA), fixed2),
            pl.BlockSpec((KV_LORA, hw), fixed2),
            pl.BlockSpec((KV_LORA, hw), fixed2),
            pl.BlockSpec((1, HEAD_LANES), fixed2),
            pl.BlockSpec((1, HEAD_LANES), fixed2),
            pl.BlockSpec((1, HEAD_LANES), fixed2),
        ],
        out_specs=[
            pl.BlockSpec((1, A_HEADS, tm, HEAD_LANES), head_row),
            pl.BlockSpec((1, A_HEADS, tm, HEAD_LANES), head_row),
            pl.BlockSpec((1, A_HEADS, tm, HEAD_LANES), head_row),
            pl.BlockSpec((1, tm, D_INNER), row),
            pl.BlockSpec((1, tm, CONV_CH), row),
            pl.BlockSpec((1, tm, HEAD_LANES), row),
        ],
        out_shape=[
            jax.ShapeDtypeStruct((b, A_HEADS, s, HEAD_LANES), BF16),
            jax.ShapeDtypeStruct((b, A_HEADS, s, HEAD_LANES), BF16),
            jax.ShapeDtypeStruct((b, A_HEADS, s, HEAD_LANES), BF16),
            jax.ShapeDtypeStruct((b, s, D_INNER), BF16),
            jax.ShapeDtypeStruct((b, s, CONV_CH), BF16),
            jax.ShapeDtypeStruct((b, s, HEAD_LANES), F32),
        ],
        compiler_params=_cparams(("parallel", "parallel")),
        name="front_even",
    )(x, pos, invf, ln, win, qln, wuq, kvln, wuk, wuv, qg, kg, koff)


HEADS_PER_STEP = HEAD_LANES // V_DIM


def _attn_kernel(q_ref, k_ref, v_ref, o_ref, *, tq, online):
    qi = pl.program_id(2)
    row = lax.broadcasted_iota(I32, (tq, tq), 0)
    col = lax.broadcasted_iota(I32, (tq, tq), 1)

    def head_step(hh, j, carry, masked):
        kj = k_ref[0, hh, pl.ds(j * tq, tq), :]
        vj = v_ref[0, hh, pl.ds(j * tq, tq), :]
        s = _dot_nt(q_ref[0, hh], kj)
        if masked:
            s = jnp.where(row >= col, s, -jnp.inf)
        if online:
            m, acc = carry
            m_new = jnp.maximum(m, jnp.max(s, axis=-1, keepdims=True))
            p = jnp.exp2(s - m_new).astype(BF16)
            return m_new, jnp.exp2(m - m_new) * acc + _dot(p, vj)
        return carry + _dot(jnp.exp2(s).astype(BF16), vj)

    def step(j, carries, masked):
        return tuple(head_step(hh, j, carries[hh], masked) for hh in range(HEADS_PER_STEP))

    acc0 = jnp.zeros((tq, HEAD_LANES), F32)
    init = (jnp.full((tq, 1), -jnp.inf, F32), acc0) if online else acc0
    carries = lax.fori_loop(0, qi, functools.partial(step, masked=False), (init,) * HEADS_PER_STEP)
    carries = step(qi, carries, True)
    outs = []
    for carry in carries:
        acc = carry[1] if online else carry
        outs.append(acc / acc[:, ONES_LANE:ONES_LANE + 1])
    lane = lax.broadcasted_iota(I32, (1, HEAD_LANES), 1)
    out = outs[0]
    for hh in range(1, HEADS_PER_STEP):
        out = jnp.where(lane >= hh * V_DIM, pltpu.roll(outs[hh], hh * V_DIM, 1), out)
    o_ref[0] = out.astype(BF16)


def _attention(q, k, v, tq, online):
    b, nh, s, _ = q.shape
    grid = (b, nh // HEADS_PER_STEP, s // tq)
    kv_spec = pl.BlockSpec((1, HEADS_PER_STEP, s, HEAD_LANES), lambda i, h, j: (i, h, 0, 0))
    return pl.pallas_call(
        functools.partial(_attn_kernel, tq=tq, online=online),
        grid=grid,
        in_specs=[
            pl.BlockSpec((1, HEADS_PER_STEP, tq, HEAD_LANES), lambda i, h, j: (i, h, j, 0)),
            kv_spec,
            kv_spec,
        ],
        out_specs=pl.BlockSpec((1, tq, HEAD_LANES), lambda i, h, j: (i, j, h)),
        out_shape=jax.ShapeDtypeStruct((b, s, nh * V_DIM), BF16),
        compiler_params=_cparams(("parallel", "parallel", "parallel")),
        name="mla_attention_online" if online else "mla_attention",
    )(q, k, v)


def _ssd_kernel(xbc_ref, misc_ref, z_ref, cw_ref, cb_ref, dtb_ref, alog_ref, dskip_ref, gn_ref, y_ref,
                state_ref, carry_ref):
    c = pl.program_id(1)
    t = CHUNK

    @pl.when(c == 0)
    def _():
        state_ref[...] = jnp.zeros_like(state_ref)
        carry_ref[...] = jnp.zeros_like(carry_ref)

    xr = xbc_ref[0].astype(F32)
    xcat = jnp.concatenate([carry_ref[...], xr], axis=0)
    carry_ref[...] = xr[t - CONV_CARRY:, :]
    conv = jnp.zeros((t, CONV_CH), F32) + cb_ref[...]
    for kk in range(CONV_K):
        sh = CONV_K - 1 - kk
        shifted = xcat if sh == 0 else pltpu.roll(xcat, sh, 0)
        conv = conv + cw_ref[kk:kk + 1, :] * shifted[CONV_CARRY:, :]
    xa = conv * _sigmoid(conv)
    xs = xa[:, :D_INNER]
    gw = SSD_GROUPS * SSD_STATE
    bmat = xa[:, D_INNER:D_INNER + gw]
    cmat = xa[:, D_INNER + gw:]

    u = misc_ref[0] + dtb_ref[...]
    dt = jnp.maximum(u, 0.0) + jnp.log(1.0 + jnp.exp(-jnp.abs(u)))
    a = -jnp.exp(alog_ref[...])
    lane = lax.broadcasted_iota(I32, (1, LANES), 1)
    adt = jnp.where(lane < B_HEADS, dt * a, 0.0)
    rowi = lax.broadcasted_iota(I32, (t, LANES), 0)
    acs = adt
    sh = 1
    while sh < t:
        acs = acs + jnp.where(rowi >= sh, pltpu.roll(acs, sh, 0), 0.0)
        sh *= 2
    acs_t = acs.T
    tri = lax.broadcasted_iota(I32, (t, t), 0) >= lax.broadcasted_iota(I32, (t, t), 1)

    rep = B_HEADS // SSD_GROUPS
    ys = []
    for g in range(SSD_GROUPS):
        bg = bmat[:, g * SSD_STATE:(g + 1) * SSD_STATE]
        cg = cmat[:, g * SSD_STATE:(g + 1) * SSD_STATE]
        bg16, cg16 = bg.astype(BF16), cg.astype(BF16)
        cb = _dot_nt(cg16, bg16)
        bg_t = bg.T
        for r in range(rep):
            hd = g * rep + r
            col = acs[:, hd:hd + 1]
            rw = acs_t[hd:hd + 1, :]
            last = acs_t[hd:hd + 1, t - 1:t]
            decay = jnp.exp(jnp.where(tri, col - rw, -jnp.inf))
            xh = xs[:, hd * SSD_HEAD_DIM:(hd + 1) * SSD_HEAD_DIM]
            xdt = (xh * dt[:, hd:hd + 1]).astype(BF16)
            y_diag = _dot((cb * decay).astype(BF16), xdt)
            prev = state_ref[hd]
            y_off = _dot(cg16, prev.astype(BF16)) * jnp.exp(col)
            new_state = _dot((bg_t * jnp.exp(last - rw)).astype(BF16), xdt)
            state_ref[hd] = prev * jnp.exp(last) + new_state
            ys.append(y_diag + y_off)
    y = jnp.concatenate(ys, axis=1) + xs * dskip_ref[...]
    zf = z_ref[0].astype(F32)
    y = y * (zf * _sigmoid(zf))
    y_ref[0] = _rms(y, gn_ref[...]).astype(BF16)


def _ssd(xbc, misc, z, cw, cb, dtb, alog, dskip, gn):
    b, s, _ = xbc.shape
    grid = (b, s // CHUNK)
    row = lambda i, j: (i, j, 0)
    fixed2 = lambda i, j: (0, 0)
    return pl.pallas_call(
        _ssd_kernel,
        grid=grid,
        in_specs=[
            pl.BlockSpec((1, CHUNK, CONV_CH), row),
            pl.BlockSpec((1, CHUNK, LANES), row),
            pl.BlockSpec((1, CHUNK, D_INNER), row),
            pl.BlockSpec((CONV_K, CONV_CH), fixed2),
            pl.BlockSpec((1, CONV_CH), fixed2),
            pl.BlockSpec((1, LANES), fixed2),
            pl.BlockSpec((1, LANES), fixed2),
            pl.BlockSpec((1, D_INNER), fixed2),
            pl.BlockSpec((1, D_INNER), fixed2),
        ],
        out_specs=pl.BlockSpec((1, CHUNK, D_INNER), row),
        out_shape=jax.ShapeDtypeStruct((b, s, D_INNER), BF16),
        scratch_shapes=[
            pltpu.VMEM((B_HEADS, SSD_STATE, SSD_HEAD_DIM), F32),
            pltpu.VMEM((CONV_CARRY, CONV_CH), F32),
        ],
        compiler_params=_cparams(("parallel", "arbitrary")),
        name="ssd_scan",
    )(xbc, misc, z, cw, cb, dtb, alog, dskip, gn)


XW = X_HEADS * X_HEAD_DIM


def _mem_kv_kernel(mem_ref, ln_ref, wkv_ref, kg_ref, hsum_ref, kbd_ref, vbd_ref):
    m = mem_ref.shape[1]
    mn = _rms(mem_ref[0], ln_ref[...]).astype(BF16)
    kv = _dot(mn, wkv_ref[...])
    k, v = kv[:, :XW], kv[:, XW:]
    ss = _dot((k * k).astype(BF16), hsum_ref[...])
    kn = (k * lax.rsqrt(ss * (1.0 / X_HEAD_DIM) + RMS_EPS) * kg_ref[...]).astype(BF16)
    v16 = v.astype(BF16)
    head_of_lane = lax.shift_right_arithmetic(lax.broadcasted_iota(I32, (1, XW), 1), jnp.int32(_LOG2_XHD))
    for hd in range(X_HEADS):
        keep = head_of_lane == hd
        kbd_ref[0, hd * m:(hd + 1) * m, :] = jnp.where(keep, kn, jnp.zeros_like(kn))
        vbd_ref[0, hd * m:(hd + 1) * m, :] = jnp.where(keep, v16, jnp.zeros_like(v16))


def _mem_kv(mem, ln, wkv, kg, hsum):
    b, m, d = mem.shape
    fixed2 = lambda i: (0, 0)
    return pl.pallas_call(
        _mem_kv_kernel,
        grid=(b,),
        in_specs=[
            pl.BlockSpec((1, m, d), lambda i: (i, 0, 0)),
            pl.BlockSpec((1, d), fixed2),
            pl.BlockSpec((d, 2 * XW), fixed2),
            pl.BlockSpec((1, XW), fixed2),
            pl.BlockSpec((XW, XW), fixed2),
        ],
        out_specs=[
            pl.BlockSpec((1, X_HEADS * m, XW), lambda i: (i, 0, 0)),
            pl.BlockSpec((1, X_HEADS * m, XW), lambda i: (i, 0, 0)),
        ],
        out_shape=[
            jax.ShapeDtypeStruct((b, X_HEADS * m, XW), BF16),
            jax.ShapeDtypeStruct((b, X_HEADS * m, XW), BF16),
        ],
        compiler_params=_cparams(("parallel",)),
        name="mem_kv",
    )(mem, ln, wkv, kg, hsum)


ROUTE_LANES = LANES
ROW_PARTS = 2
_GROUP_LANE0 = 0
_EXPERT_LANE0 = MOE_GROUPS
_LOG2_EPG = EXPERTS_PER_GROUP.bit_length() - 1
_LOG2_XHD = X_HEAD_DIM.bit_length() - 1


def _pack_bf16_pairs(v):
    w = v.shape[1] // 2
    r = v.astype(BF16).astype(F32)
    hi = lax.bitcast_convert_type(r[:, :w], U32)
    lo = lax.bitcast_convert_type(r[:, w:], U32)
    return (hi & jnp.uint32(0xFFFF0000)) | (lo >> jnp.uint32(16))


def _unpack_bf16_pairs(u):
    hi = lax.bitcast_convert_type(u & jnp.uint32(0xFFFF0000), F32)
    lo = lax.bitcast_convert_type(u << jnp.uint32(16), F32)
    return hi, lo


def _tail(x1, kbd_ref, vbd_ref, lnq_ref, wq_ref, qg_ref, hsum_ref, wo_ref, lnf_ref, rwh_ref, rwl_ref, rb_ref,
          ltri_ref, x2_ref, *out_refs):
    hfp_refs, (route_ref, cnt_ref) = out_refs[:ROW_PARTS], out_refs[ROW_PARTS:]
    tm = x1.shape[0]
    m = kbd_ref.shape[1] // X_HEADS
    hq = _rms(x1, lnq_ref[...]).astype(BF16)
    q = _dot(hq, wq_ref[...])
    ss = _dot((q * q).astype(BF16), hsum_ref[...])
    qn = (q * lax.rsqrt(ss * (1.0 / X_HEAD_DIM) + RMS_EPS) * qg_ref[...] * (X_HEAD_DIM ** -0.5)).astype(BF16)
    s = _dot_nt(qn, kbd_ref[0])
    ps = []
    for hd in range(X_HEADS):
        sh = s[:, hd * m:(hd + 1) * m]
        e = jnp.exp(sh - jnp.max(sh, axis=-1, keepdims=True))
        ps.append((e / jnp.sum(e, axis=-1, keepdims=True)).astype(BF16))
    o = _dot(jnp.concatenate(ps, axis=1), vbd_ref[0]).astype(BF16)
    x2 = x1 + _dot(o, wo_ref[...])
    x2_ref[0] = x2

    hf = _rms(x2, lnf_ref[...])
    hf_hi = hf.astype(BF16)
    packed = _pack_bf16_pairs(hf)
    pw = packed.shape[1] // ROW_PARTS
    for c in range(ROW_PARTS):
        hfp_refs[c][0] = packed[:, c * pw:(c + 1) * pw]
    hf_lo = (hf - hf_hi.astype(F32)).astype(BF16)
    logits = _dot(hf_hi, rwh_ref[...]) + _dot(hf_hi, rwl_ref[...]) + _dot(hf_lo, rwh_ref[...]) + rb_ref[...]

    lane_i = lax.broadcasted_iota(I32, (tm, ROUTE_LANES), 1)
    lane = lane_i.astype(F32)
    big = float(ROUTE_LANES)
    neg = -jnp.inf
    gl = jnp.where(lane_i < MOE_GROUPS, logits, neg)
    gmax = jnp.max(gl, axis=-1, keepdims=True)
    gsum = jnp.sum(jnp.exp(gl - gmax), axis=-1, keepdims=True)
    g_p = 1.0 / gsum
    g_idx = jnp.min(jnp.where(gl == gmax, lane, big), axis=-1, keepdims=True)
    e_lane = lane_i - _EXPERT_LANE0
    grp_of_lane = lax.shift_right_arithmetic(e_lane, jnp.int32(_LOG2_EPG)).astype(F32)
    in_grp = (e_lane >= 0) & (e_lane < N_EXPERTS) & (grp_of_lane == g_idx)
    el = jnp.where(in_grp, logits, neg)
    emax = jnp.max(el, axis=-1, keepdims=True)
    idx1 = jnp.min(jnp.where(el == emax, lane, big), axis=-1, keepdims=True)
    el2 = jnp.where(lane == idx1, neg, el)
    emax2 = jnp.max(el2, axis=-1, keepdims=True)
    idx2 = jnp.min(jnp.where(el2 == emax2, lane, big), axis=-1, keepdims=True)
    r2 = jnp.exp(emax2 - emax)
    gate1 = g_p / (1.0 + r2)
    gate2 = g_p * r2 / (1.0 + r2)
    e1 = idx1 - float(_EXPERT_LANE0)
    e2 = idx2 - float(_EXPERT_LANE0)

    oh1 = (lane == e1).astype(F32)
    oh2 = (lane == e2).astype(F32)
    both = oh1 + oh2
    before = _dot(ltri_ref[...], both.astype(BF16))
    rank1 = jnp.sum(before * oh1, axis=-1, keepdims=True)
    rank2 = jnp.sum(before * oh2, axis=-1, keepdims=True)
    cnt_ref[0] = jnp.broadcast_to(jnp.sum(both, axis=0, keepdims=True), cnt_ref.shape[1:])

    route = jnp.where(lane == 0, e1, 0.0)
    route = jnp.where(lane == 1, e2, route)
    route = jnp.where(lane == 2, gate1, route)
    route = jnp.where(lane == 3, gate2, route)
    route = jnp.where(lane == 4, rank1, route)
    route = jnp.where(lane == 5, rank2, route)
    route_ref[0] = route


_TAIL_IN = 12


def _post_even_kernel(x_ref, a_ref, y_ref, wout_ref, *rest):
    tail_in, outs = rest[:_TAIL_IN], rest[_TAIL_IN:]
    half = wout_ref.shape[0] // 2
    x1 = x_ref[0] + _dot(a_ref[0], wout_ref[:half, :]) + _dot(y_ref[0], wout_ref[half:, :])
    _tail(x1, *tail_in, *outs)


def _post_pool_kernel(x_ref, ln_ref, pw_ref, pb_ref, ps_ref, *rest):
    tail_in, outs, carry_ref = rest[:_TAIL_IN], rest[_TAIL_IN:-1], rest[-1]
    j = pl.program_id(1)
    tm = x_ref.shape[1]

    @pl.when(j == 0)
    def _():
        carry_ref[...] = jnp.zeros_like(carry_ref)

    x = x_ref[0]
    h = _rms(x, ln_ref[...])
    pos = (j * tm + 1 + lax.broadcasted_iota(I32, (tm, 1), 0)).astype(F32)
    mixed = []
    for g, w in enumerate(POOL_WINDOWS):
        sl = slice(g * POOL_GROUP, (g + 1) * POOL_GROUP)
        hg = h[:, sl]
        acc = jnp.concatenate([carry_ref[:, sl], hg], axis=0)
        sh = 1
        while sh < w:
            acc = acc + pltpu.roll(acc, sh, 0)
            sh *= 2
        win = acc[POOL_CARRY:, :]
        dlt = win / jnp.minimum(pos, float(w)) - hg
        mixed.append(_dot(dlt.astype(BF16), pw_ref[g]))
    carry_ref[...] = h[tm - POOL_CARRY:, :]
    y = (jnp.concatenate(mixed, axis=1) + pb_ref[...]) * ps_ref[...]
    _tail(x + y, *tail_in, *outs)


def _post(kind, front_args, front_specs, tail_args, b, s, tm, scratch):
    d = D_MODEL
    m4 = tail_args[0].shape[1]
    row = lambda i, j: (i, j, 0)
    fixed2 = lambda i, j: (0, 0)
    per_b = lambda i, j: (i, 0, 0)
    tail_specs = [
        pl.BlockSpec((1, m4, XW), per_b),
        pl.BlockSpec((1, m4, XW), per_b),
        pl.BlockSpec((1, d), fixed2),
        pl.BlockSpec((d, XW), fixed2),
        pl.BlockSpec((1, XW), fixed2),
        pl.BlockSpec((XW, XW), fixed2),
        pl.BlockSpec((XW, d), fixed2),
        pl.BlockSpec((1, d), fixed2),
        pl.BlockSpec((d, ROUTE_LANES), fixed2),
        pl.BlockSpec((d, ROUTE_LANES), fixed2),
        pl.BlockSpec((1, ROUTE_LANES), fixed2),
        pl.BlockSpec((tm, tm), fixed2),
    ]
    nt = s // tm
    pw = d // 2 // ROW_PARTS
    kernel = _post_even_kernel if kind == "even" else _post_pool_kernel
    return pl.pallas_call(
        kernel,
        grid=(b, nt),
        in_specs=front_specs + tail_specs,
        out_specs=[pl.BlockSpec((1, tm, d), row)]
        + [pl.BlockSpec((1, tm, pw), row)] * ROW_PARTS
        + [pl.BlockSpec((1, tm, ROUTE_LANES), row),
           pl.BlockSpec((1, 8, ROUTE_LANES), lambda i, j: (i * nt + j, 0, 0))],
        out_shape=[jax.ShapeDtypeStruct((b, s, d), F32)]
        + [jax.ShapeDtypeStruct((b, s, pw), U32)] * ROW_PARTS
        + [jax.ShapeDtypeStruct((b, s, ROUTE_LANES), F32),
           jax.ShapeDtypeStruct((b * nt, 8, ROUTE_LANES), F32)],
        scratch_shapes=scratch,
        compiler_params=_cparams(("parallel", "arbitrary")),
        name="post_" + kind,
    )(*front_args, *tail_args)


FFN_ROWS = 512
COMBINE_TOKENS = 512
SC_GATHER_WINDOW = 128


def _sc_gather_rows(table, idx):
    m, w = idx.shape[0], table.shape[1]
    mesh = plsc.VectorSubcoreMesh(core_axis_name="core", subcore_axis_name="subcore")

    @pl.kernel(out_type=jax.ShapeDtypeStruct((m, w), table.dtype), mesh=mesh, name="moe_row_gather")
    def gather(t_hbm, i_hbm, o_hbm):
        def body(i_vmem, o_vmem):
            pltpu.sync_copy(t_hbm.at[i_vmem.at[0]], o_vmem)

        pltpu.emit_pipeline(
            body,
            grid=(m // SC_GATHER_WINDOW,),
            in_specs=[pl.BlockSpec((1, SC_GATHER_WINDOW), lambda i: (0, i))],
            out_specs=[pl.BlockSpec((SC_GATHER_WINDOW, w), lambda i: (i, 0))],
            core_axis_name=("core", "subcore"),
            dimension_semantics=(pltpu.PARALLEL,),
        )(i_hbm, o_hbm)

    return gather(table, idx.reshape(1, m))


def _sc_scatter_rows(src, dests, pad_rows, n_rows):
    n, w = src.shape
    win = SC_GATHER_WINDOW
    mesh = plsc.VectorSubcoreMesh(core_axis_name="core", subcore_axis_name="subcore")
    idx_spec = pl.BlockSpec((1, win), lambda i: (0, i))
    split = dict(core_axis_name=("core", "subcore"), dimension_semantics=(pltpu.PARALLEL,))

    @pl.kernel(out_type=jax.ShapeDtypeStruct((n_rows, w), src.dtype), mesh=mesh, name="moe_row_scatter")
    def scatter(s_hbm, z_hbm, p_hbm, *rest):
        d_hbms, o_hbm = rest[:-1], rest[-1]

        def body(s_vmem, *i_vmems):
            for i_vmem in i_vmems:
                pltpu.sync_copy(s_vmem, o_hbm.at[i_vmem.at[0]])

        pltpu.emit_pipeline(
            body, grid=(n // win,),
            in_specs=[pl.BlockSpec((win, w), lambda i: (i, 0))] + [idx_spec] * len(dests),
            out_specs=[], **split)(s_hbm, *d_hbms)

        def zero_body(z_vmem, i_vmem):
            pltpu.sync_copy(z_vmem, o_hbm.at[i_vmem.at[0]])

        pltpu.emit_pipeline(
            zero_body, grid=(pad_rows.shape[0] // win,),
            in_specs=[pl.BlockSpec((win, w), lambda i: (0, 0)), idx_spec],
            out_specs=[], **split)(z_hbm, p_hbm)

    zeros = jnp.zeros((win, w), src.dtype)
    return scatter(src, zeros, pad_rows.reshape(1, -1), *[dd.reshape(1, n) for dd in dests])


def _ffn_kernel(be_ref, bi_ref, *refs):
    xb_refs, (wg_ref, wu_ref, wd_ref) = refs[:ROW_PARTS], refs[ROW_PARTS:ROW_PARTS + 3]
    yb_refs, (wg_s, wu_s, wd_s) = refs[ROW_PARTS + 3:2 * ROW_PARTS + 3], refs[2 * ROW_PARTS + 3:]
    i = pl.program_id(0)
    changed = jnp.logical_or(i == 0, be_ref[i] != be_ref[jnp.maximum(i - 1, 0)])

    @pl.when(changed)
    def _():
        wg_s[...] = wg_ref[0, 0].astype(BF16)
        wu_s[...] = wu_ref[0, 0].astype(BF16)
        wd_s[...] = wd_ref[0, 0].astype(BF16)

    half = wg_s.shape[0] // 2
    gate = up = None
    for c in range(ROW_PARTS):
        hi, lo = _unpack_bf16_pairs(xb_refs[c][...])
        hi, lo = hi.astype(BF16), lo.astype(BF16)
        pw = hi.shape[1]
        hs, ls = slice(c * pw, (c + 1) * pw), slice(half + c * pw, half + (c + 1) * pw)
        g = _dot(hi, wg_s[hs, :]) + _dot(lo, wg_s[ls, :])
        u = _dot(hi, wu_s[hs, :]) + _dot(lo, wu_s[ls, :])
        gate, up = (g, u) if gate is None else (gate + g, up + u)
    act = (gate * _sigmoid(gate) * up).astype(BF16)
    packed = _pack_bf16_pairs(_dot(act, wd_s[...]))
    pw = packed.shape[1] // ROW_PARTS
    for c in range(ROW_PARTS):
        yb_refs[c][...] = packed[:, c * pw:(c + 1) * pw]


def _expert_ffn(block_e, block_i, xbs, wg, wu, wd, layer):
    n_rows, pw = xbs[0].shape
    d, ff = wg.shape[2], wg.shape[3]
    n_blk = n_rows // FFN_ROWS
    row_spec = pl.BlockSpec((FFN_ROWS, pw), lambda i, be, bi: (bi[i], 0))
    return pl.pallas_call(
        _ffn_kernel,
        grid_spec=pltpu.PrefetchScalarGridSpec(
            num_scalar_prefetch=2,
            grid=(n_blk,),
            in_specs=[row_spec] * ROW_PARTS + [
                pl.BlockSpec((1, 1, d, ff), lambda i, be, bi: (layer, be[i], 0, 0)),
                pl.BlockSpec((1, 1, d, ff), lambda i, be, bi: (layer, be[i], 0, 0)),
                pl.BlockSpec((1, 1, ff, d), lambda i, be, bi: (layer, be[i], 0, 0)),
            ],
            out_specs=[row_spec] * ROW_PARTS,
            scratch_shapes=[pltpu.VMEM((d, ff), BF16), pltpu.VMEM((d, ff), BF16), pltpu.VMEM((ff, d), BF16)],
        ),
        out_shape=[jax.ShapeDtypeStruct((n_rows, pw), U32)] * ROW_PARTS,
        compiler_params=_cparams(("arbitrary",)),
        name="moe_expert_ffn",
    )(block_e, block_i, *xbs, wg, wu, wd)


def _combine_kernel(x_ref, route_ref, *refs):
    y_refs, o_ref = refs[:-1], refs[-1]
    half = x_ref.shape[1] // 2
    route = route_ref[...]
    g1, g2 = route[:, 2:3], route[:, 3:4]
    for c in range(ROW_PARTS):
        w = y_refs[2 * c].shape[1]
        h1, l1 = _unpack_bf16_pairs(y_refs[2 * c][...])
        h2, l2 = _unpack_bf16_pairs(y_refs[2 * c + 1][...])
        hs, ls = slice(c * w, (c + 1) * w), slice(half + c * w, half + (c + 1) * w)
        o_ref[:, hs] = x_ref[:, hs] + (h1 * g1 + h2 * g2)
        o_ref[:, ls] = x_ref[:, ls] + (l1 * g1 + l2 * g2)


def _combine(x2, route, ytoks, tc):
    n, d = x2.shape
    w = ytoks[0].shape[1]
    nsteps = n // tc
    y_specs, y_args = [], []
    for ytok in ytoks:
        y_specs += [pl.BlockSpec((tc, w), lambda i: (i, 0)), pl.BlockSpec((tc, w), lambda i: (i + nsteps, 0))]
        y_args += [ytok, ytok]
    return pl.pallas_call(
        _combine_kernel,
        grid=(nsteps,),
        in_specs=[pl.BlockSpec((tc, d), lambda i: (i, 0)), pl.BlockSpec((tc, ROUTE_LANES), lambda i: (i, 0))] + y_specs,
        out_specs=pl.BlockSpec((tc, d), lambda i: (i, 0)),
        out_shape=jax.ShapeDtypeStruct((n, d), F32),
        compiler_params=_cparams(("parallel",)),
        name="moe_combine",
    )(x2, route, *y_args)


def _moe(x2, hfps, route, counts, wg, wu, wd, layer, tm):
    b, s, d = x2.shape
    n = b * s
    route = route.reshape(n, ROUTE_LANES)
    cnt = counts[:, 0, :N_EXPERTS].astype(I32)
    total = jnp.sum(cnt, axis=0)
    padded = (total + FFN_ROWS - 1) // FFN_ROWS * FFN_ROWS
    pad_end = jnp.cumsum(padded)
    pad_start = pad_end - padded
    tile_base = pad_start[None, :] + jnp.cumsum(cnt, axis=0) - cnt
    experts = route[:, 0:TOP_K].astype(I32)
    ranks = route[:, 4:4 + TOP_K].astype(I32)
    base_tok = jnp.repeat(tile_base, tm, axis=0)
    dest = jnp.take_along_axis(base_tok, experts, axis=1) + ranks
    dest_by_slot = dest.T.reshape(n * TOP_K)
    n_blk = (n * TOP_K) // FFN_ROWS + N_EXPERTS
    n_rows = n_blk * FFN_ROWS
    used = pad_end[-1] // FFN_ROWS
    block_i = jnp.minimum(jnp.arange(n_blk, dtype=I32), used - 1).astype(I32)
    ended = (pad_end[None, :] <= (block_i * FFN_ROWS)[:, None]).astype(I32)
    block_e = jnp.minimum(jnp.sum(ended, axis=1), N_EXPERTS - 1).astype(I32)
    seg_len = jnp.concatenate([padded - total, (n_rows - pad_end[-1])[None]])
    seg_first = jnp.concatenate([pad_start + total, pad_end[-1:]])
    seg_end = jnp.cumsum(seg_len)
    jpad = jnp.arange(n_rows - n * TOP_K, dtype=I32)
    seg = jnp.sum((seg_end[None, :] <= jpad[:, None]).astype(I32), axis=1)
    pad_rows = (seg_first[seg] + jpad - (seg_end - seg_len)[seg]).astype(I32)

    tc = min(COMBINE_TOKENS, n)
    dests = [dest[:, k] for k in range(TOP_K)]
    xbs = [_sc_scatter_rows(part.reshape(n, part.shape[-1]), dests, pad_rows, n_rows) for part in hfps]
    yb = _expert_ffn(block_e, block_i, xbs, wg, wu, wd, layer)
    ytoks = [_sc_gather_rows(part, dest_by_slot) for part in yb]
    out = _combine(x2.reshape(n, d), route, ytoks, tc)
    return out.reshape(b, s, d)


def _rope_lane_freq():
    inv = ROPE_THETA ** (-jnp.arange(0, ROPE_DIM // 2, dtype=F32) * 2.0 / ROPE_DIM)
    idx = np.full((HEAD_LANES,), -1, np.int64)
    for r in range(ROPE_DIM):
        idx[_head_lane(NOPE_DIM + r)] = r % ROPE_HALF
    return _gather_cols(inv[None, :], idx)


FAST_SOFTMAX_MAX_LOG2 = 60.0


def _score_bound_log2(qg, kg):
    return 1.02 * LOG2E * QK_DIM ** 0.5 * jnp.max(jnp.abs(qg)) * jnp.max(jnp.abs(kg))


def _head_gain(g):
    idx = np.full((HEAD_LANES,), -1, np.int64)
    for dd in range(QK_DIM):
        idx[_head_lane(dd)] = dd
    return _gather_cols(g[None, :], idx)


def kernel(x, mem, positions, ln_mix, w_in, q_lat_norm, w_uq, kv_lat_norm, w_ukv, q_norm, k_norm, conv_w, conv_b,
           dt_bias, a_log, d_skip, ssd_norm, w_out, pool_w, pool_b, pool_scale, ln_xq, ln_mem, xq_w, xkv_w, xq_norm,
           xk_norm, xo_w, ln_ffn, rg_w, rg_b, re_w, re_b, exp_w_gate, exp_w_up, exp_w_down):
    b, s, d = x.shape
    depth = ln_mix.shape[0]
    tm = min(512, s)
    tq = min(512, s)
    assert d == D_MODEL and s % tm == 0 and s % CHUNK == 0 and tm >= POOL_CARRY

    pos = positions.astype(F32)[..., None]
    invf = _rope_lane_freq()
    hsum = jnp.asarray(np.kron(np.eye(X_HEADS), np.ones((X_HEAD_DIM, X_HEAD_DIM))), BF16)
    ltri = jnp.asarray(np.tril(np.ones((tm, tm)), -1), BF16)
    row2 = lambda v: v.reshape(1, -1)
    lane_pad = lambda v: jnp.pad(v, (0, LANES - v.shape[0])).reshape(1, LANES)

    for layer in range(depth):
        j = layer // 2
        kbd, vbd = _mem_kv(mem, row2(ln_mem[layer]), xkv_w[layer].astype(BF16),
                           row2(jnp.tile(xk_norm[layer], X_HEADS)), hsum)
        rw = jnp.pad(jnp.concatenate([rg_w[layer], re_w[layer]], axis=1),
                     ((0, 0), (0, ROUTE_LANES - MOE_GROUPS - N_EXPERTS)))
        rw_hi = rw.astype(BF16)
        rw_lo = (rw - rw_hi.astype(F32)).astype(BF16)
        rb = lane_pad(jnp.concatenate([rg_b[layer], re_b[layer]]))
        tail_args = [kbd, vbd, row2(ln_xq[layer]), xq_w[layer].astype(BF16), row2(jnp.tile(xq_norm[layer], X_HEADS)),
                     hsum, xo_w[layer].astype(BF16), row2(ln_ffn[layer]), rw_hi, rw_lo, rb, ltri]
        row = lambda i, jj: (i, jj, 0)
        fixed2 = lambda i, jj: (0, 0)
        if layer % 2 == 0:
            win = _gather_cols(w_in[j], _win_col_index()).astype(BF16)
            wuq = _gather_cols(w_uq[j], _head_col_index(QK_DIM, 0, QK_DIM)).astype(BF16)
            wuk = _gather_cols(w_ukv[j], _head_col_index(NOPE_DIM + V_DIM, 0, NOPE_DIM)).astype(BF16)
            v_idx = np.full((A_HEADS * HEAD_LANES,), -1, np.int64)
            for hd in range(A_HEADS):
                v_idx[hd * HEAD_LANES:hd * HEAD_LANES + V_DIM] = hd * (NOPE_DIM + V_DIM) + NOPE_DIM + np.arange(V_DIM)
            wuv = _gather_cols(w_ukv[j], v_idx).astype(BF16)
            bound = _score_bound_log2(q_norm[j], k_norm[j])
            koff = jnp.zeros((1, HEAD_LANES), F32).at[0, SCORE_PAD_LANE].set(-bound)
            q, k, v, z, xbc, misc = _front_even(
                x, pos, invf, row2(ln_mix[layer]), win, row2(q_lat_norm[j]), wuq, row2(kv_lat_norm[j]), wuk, wuv,
                _head_gain(q_norm[j]), _head_gain(k_norm[j]), koff, tm)
            attn = lax.cond(bound <= FAST_SOFTMAX_MAX_LOG2,
                            functools.partial(_attention, tq=tq, online=False),
                            functools.partial(_attention, tq=tq, online=True), q, k, v)
            y = _ssd(xbc, misc, z, conv_w[j], row2(conv_b[j]), lane_pad(dt_bias[j]), lane_pad(a_log[j]),
                     row2(jnp.repeat(d_skip[j], SSD_HEAD_DIM)), row2(ssd_norm[j]))
            half = A_HEADS * V_DIM
            front_args = [x, attn, y, w_out[j].astype(BF16)]
            front_specs = [pl.BlockSpec((1, tm, d), row), pl.BlockSpec((1, tm, half), row),
                           pl.BlockSpec((1, tm, D_INNER), row), pl.BlockSpec((half + D_INNER, d), fixed2)]
            x2, *hfps, route, counts = _post("even", front_args, front_specs, tail_args, b, s, tm, [])
        else:
            front_args = [x, row2(ln_mix[layer]), pool_w[j].astype(BF16), row2(pool_b[j]), row2(pool_scale[j])]
            front_specs = [pl.BlockSpec((1, tm, d), row), pl.BlockSpec((1, d), fixed2),
                           pl.BlockSpec((len(POOL_WINDOWS), POOL_GROUP, POOL_GROUP), lambda i, jj: (0, 0, 0)),
                           pl.BlockSpec((1, d), fixed2), pl.BlockSpec((1, d), fixed2)]
            x2, *hfps, route, counts = _post("pool", front_args, front_specs, tail_args, b, s, tm,
                                           [pltpu.VMEM((POOL_CARRY, d), F32)])
        x = _moe(x2, hfps, route, counts, exp_w_gate, exp_w_up, exp_w_down, layer, tm)
    return x
```

```python
import functools

import numpy as np
import jax
import jax.numpy as jnp
from jax import lax
from jax.experimental import pallas as pl
from jax.experimental.pallas import tpu as pltpu
from jax.experimental.pallas import tpu_sc as plsc

F32 = jnp.float32
BF16 = jnp.bfloat16
U32 = jnp.uint32
I32 = jnp.int32

RMS_EPS = 1e-6
ROPE_THETA = 10000.0

D_MODEL = 1024
X_HEADS, X_HEAD_DIM = 4, 64
A_HEADS, NOPE_DIM, ROPE_DIM, V_DIM = 8, 64, 32, 64
QK_DIM = NOPE_DIM + ROPE_DIM
Q_LORA, KV_LORA = 256, 128
B_HEADS, SSD_HEAD_DIM, SSD_GROUPS, SSD_STATE, CONV_K, CHUNK = 8, 64, 2, 128, 4, 128
D_INNER = B_HEADS * SSD_HEAD_DIM
CONV_CH = D_INNER + 2 * SSD_GROUPS * SSD_STATE
POOL_WINDOWS = (2, 4, 8, 16)
POOL_GROUP = D_MODEL // 4
MOE_GROUPS, EXPERTS_PER_GROUP, TOP_K, EXPERT_FF = 4, 8, 2, 256
N_EXPERTS = MOE_GROUPS * EXPERTS_PER_GROUP

LANES = 128
HEAD_LANES = LANES
HALF_LANES = LANES // 2
ROPE_HALF = ROPE_DIM // 2
NOPE_HALF = NOPE_DIM // 2
POOL_CARRY = 16
CONV_CARRY = 8
VMEM_LIMIT = 56 * 1024 * 1024


def _cparams(sem):
    return pltpu.CompilerParams(dimension_semantics=sem, vmem_limit_bytes=VMEM_LIMIT)


def _rms(u, g):
    return u * lax.rsqrt(jnp.mean(u * u, axis=-1, keepdims=True) + RMS_EPS) * g


def _sigmoid(u):
    return 1.0 / (1.0 + jnp.exp(-u))


def _dot(a, b):
    return jnp.dot(a, b, preferred_element_type=F32)


def _dot_nt(a, b):
    return lax.dot_general(a, b, (((1,), (1,)), ((), ())), preferred_element_type=F32)


def _head_lane(d):
    if d < NOPE_HALF:
        return d
    if d < NOPE_DIM:
        return HALF_LANES + (d - NOPE_HALF)
    r = d - NOPE_DIM
    if r < ROPE_HALF:
        return NOPE_HALF + r
    return HALF_LANES + NOPE_HALF + (r - ROPE_HALF)


def _gather_cols(w, idx):
    w_ext = jnp.concatenate([w, jnp.zeros(w.shape[:-1] + (1,), w.dtype)], axis=-1)
    idx = np.where(idx < 0, w.shape[-1], idx)
    return jnp.take(w_ext, jnp.asarray(idx, dtype=jnp.int32), axis=-1)


IN_W = 2 * D_MODEL
_OFF_QLAT, _OFF_KVLAT, _OFF_MISC, _OFF_Z, _OFF_XBC = 0, 256, 384, 512, 1024


def _win_col_index():
    idx = np.full((IN_W,), -1, np.int64)
    idx[_OFF_QLAT:_OFF_QLAT + Q_LORA] = np.arange(Q_LORA)
    idx[_OFF_KVLAT:_OFF_KVLAT + KV_LORA] = Q_LORA + np.arange(KV_LORA)
    rope0 = Q_LORA + KV_LORA
    for r in range(ROPE_DIM):
        idx[_OFF_MISC + _head_lane(NOPE_DIM + r)] = rope0 + r
    z0 = rope0 + ROPE_DIM
    idx[_OFF_Z:_OFF_Z + D_INNER] = z0 + np.arange(D_INNER)
    xbc0 = z0 + D_INNER
    idx[_OFF_XBC:_OFF_XBC + CONV_CH] = xbc0 + np.arange(CONV_CH)
    dt0 = xbc0 + CONV_CH
    idx[_OFF_MISC:_OFF_MISC + B_HEADS] = dt0 + np.arange(B_HEADS)
    return idx


def _head_col_index(per_head, offset, count):
    idx = np.full((A_HEADS * HEAD_LANES,), -1, np.int64)
    for h in range(A_HEADS):
        for d in range(count):
            idx[h * HEAD_LANES + _head_lane(d)] = h * per_head + offset + d
    return idx


SCORE_PAD_LANE = NOPE_HALF + ROPE_HALF
ONES_LANE = V_DIM
LOG2E = 1.4426950408889634


def _front_even_kernel(x_ref, pos_ref, lc_ref, ln_ref, win_ref, qln_ref, wuq_ref, wuqr_ref, kvln_ref, wuk_ref, wuv_ref,
                       q_ref, k_ref, v_ref, z_ref, xbc_ref, misc_ref):
    x = x_ref[0]
    h = _rms(x, ln_ref[...]).astype(BF16)
    proj = _dot(h, win_ref[...])
    misc = proj[:, _OFF_MISC:_OFF_Z]
    z_ref[0] = proj[:, _OFF_Z:_OFF_XBC].astype(BF16)
    xbc_ref[0] = proj[:, _OFF_XBC:].astype(BF16)
    misc_ref[0] = misc
    ql = _rms(proj[:, _OFF_QLAT:_OFF_KVLAT], qln_ref[...]).astype(BF16)
    kvl = _rms(proj[:, _OFF_KVLAT:_OFF_MISC], kvln_ref[...]).astype(BF16)
    q = _dot(ql, wuq_ref[...])
    kn = _dot(kvl, wuk_ref[...])
    v = _dot(kvl, wuv_ref[...])
    lane = lax.broadcasted_iota(I32, (1, HEAD_LANES), 1)
    first_half = (lane >= NOPE_HALF) & (lane < NOPE_HALF + ROPE_HALF)
    second_half = (lane >= HALF_LANES + NOPE_HALF) & (lane < HALF_LANES + NOPE_HALF + ROPE_HALF)
    lc = lc_ref[...]
    qg, qg_p, kg, kg_p, k_off, invf = (lc[i:i + 1] for i in range(6))
    ones = jnp.ones((HEAD_LANES, HEAD_LANES), BF16)

    def lane_sumsq(u):
        return _dot((u * u).astype(BF16), ones)

    krope = jnp.where(first_half | second_half, misc, 0.0)
    kr_ss = lane_sumsq(krope)
    ang = pos_ref[0] * invf
    cos_t = jnp.cos(ang)
    sin_t = jnp.where(first_half, -jnp.sin(ang), jnp.sin(ang))
    q_one = (lane == SCORE_PAD_LANE).astype(F32)
    v_one = (lane == ONES_LANE).astype(F32)
    q_scale = QK_DIM ** -0.5 * LOG2E
    qa, qb = qg * cos_t * q_scale, qg_p * sin_t * q_scale
    ka, kc = kg * cos_t, pltpu.roll(krope, HALF_LANES, 1) * (kg_p * sin_t)
    q_p = _dot(ql, wuqr_ref[...])
    for hd in range(A_HEADS):
        sl = slice(hd * HEAD_LANES, (hd + 1) * HEAD_LANES)
        qs = q[:, sl]
        inv = lax.rsqrt(lane_sumsq(qs) * (1.0 / QK_DIM) + RMS_EPS)
        q_ref[0, hd] = ((qs * qa + q_p[:, sl] * qb) * inv + q_one).astype(BF16)
        kns = kn[:, sl]
        inv = lax.rsqrt((lane_sumsq(kns) + kr_ss) * (1.0 / QK_DIM) + RMS_EPS)
        k_ref[0, hd] = (((kns + krope) * ka + kc) * inv + k_off).astype(BF16)
        v_ref[0, hd] = (v[:, sl] + v_one).astype(BF16)


def _front_even(x, pos, lane_consts, ln, win, qln, wuq, wuq_p, kvln, wuk, wuv, tm):
    b, s, d = x.shape
    grid = (b, s // tm)
    row = lambda i, j: (i, j, 0)
    fixed2 = lambda i, j: (0, 0)
    head_row = lambda i, j: (i, 0, j, 0)
    hw = A_HEADS * HEAD_LANES
    return pl.pallas_call(
        _front_even_kernel,
        grid=grid,
        in_specs=[
            pl.BlockSpec((1, tm, d), row),
            pl.BlockSpec((1, tm, 1), row),
            pl.BlockSpec((8, HEAD_LANES), fixed2),
            pl.BlockSpec((1, d), fixed2),
            pl.BlockSpec((d, IN_W), fixed2),
            pl.BlockSpec((1, Q_LORA), fixed2),
            pl.BlockSpec((Q_LORA, hw), fixed2),
            pl.BlockSpec((Q_LORA, hw), fixed2),
            pl.BlockSpec((1, KV_LORA), fixed2),
            pl.BlockSpec((KV_LORA, hw), fixed2),
            pl.BlockSpec((KV_LORA, hw), fixed2),
        ],
        out_specs=[
            pl.BlockSpec((1, A_HEADS, tm, HEAD_LANES), head_row),
            pl.BlockSpec((1, A_HEADS, tm, HEAD_LANES), head_row),
            pl.BlockSpec((1, A_HEADS, tm, HEAD_LANES), head_row),
            pl.BlockSpec((1, tm, D_INNER), row),
            pl.BlockSpec((1, tm, CONV_CH), row),
            pl.BlockSpec((1, tm, HEAD_LANES), row),
        ],
        out_shape=[
            jax.ShapeDtypeStruct((b, A_HEADS, s, HEAD_LANES), BF16),
            jax.ShapeDtypeStruct((b, A_HEADS, s, HEAD_LANES), BF16),
            jax.ShapeDtypeStruct((b, A_HEADS, s, HEAD_LANES), BF16),
            jax.ShapeDtypeStruct((b, s, D_INNER), BF16),
            jax.ShapeDtypeStruct((b, s, CONV_CH), BF16),
            jax.ShapeDtypeStruct((b, s, HEAD_LANES), F32),
        ],
        compiler_params=_cparams(("parallel", "parallel")),
        name="front_even",
    )(x, pos, lane_consts, ln, win, qln, wuq, wuq_p, kvln, wuk, wuv)


HEADS_PER_STEP = HEAD_LANES // V_DIM


def _attn_kernel(q_ref, k_ref, v_ref, o_ref, *, tq, online):
    qi = pl.program_id(2)
    row = lax.broadcasted_iota(I32, (tq, tq), 0)
    col = lax.broadcasted_iota(I32, (tq, tq), 1)

    def head_step(hh, j, carry, masked):
        kj = k_ref[0, hh, pl.ds(j * tq, tq), :]
        vj = v_ref[0, hh, pl.ds(j * tq, tq), :]
        s = _dot_nt(q_ref[0, hh], kj)
        if masked:
            s = jnp.where(row >= col, s, -jnp.inf)
        if online:
            m, acc = carry
            m_new = jnp.maximum(m, jnp.max(s, axis=-1, keepdims=True))
            p = jnp.exp2(s - m_new).astype(BF16)
            return m_new, jnp.exp2(m - m_new) * acc + _dot(p, vj)
        return carry + _dot(jnp.exp2(s).astype(BF16), vj)

    def step(j, carries, masked):
        return tuple(head_step(hh, j, carries[hh], masked) for hh in range(HEADS_PER_STEP))

    acc0 = jnp.zeros((tq, HEAD_LANES), F32)
    init = (jnp.full((tq, 1), -jnp.inf, F32), acc0) if online else acc0
    carries = lax.fori_loop(0, qi, functools.partial(step, masked=False), (init,) * HEADS_PER_STEP)
    carries = step(qi, carries, True)
    outs = []
    for carry in carries:
        acc = carry[1] if online else carry
        outs.append(acc / acc[:, ONES_LANE:ONES_LANE + 1])
    lane = lax.broadcasted_iota(I32, (1, HEAD_LANES), 1)
    out = outs[0]
    for hh in range(1, HEADS_PER_STEP):
        out = jnp.where(lane >= hh * V_DIM, pltpu.roll(outs[hh], hh * V_DIM, 1), out)
    o_ref[0] = out.astype(BF16)


def _attention(q, k, v, tq, online):
    b, nh, s, _ = q.shape
    grid = (b, nh // HEADS_PER_STEP, s // tq)
    kv_spec = pl.BlockSpec((1, HEADS_PER_STEP, s, HEAD_LANES), lambda i, h, j: (i, h, 0, 0))
    return pl.pallas_call(
        functools.partial(_attn_kernel, tq=tq, online=online),
        grid=grid,
        in_specs=[
            pl.BlockSpec((1, HEADS_PER_STEP, tq, HEAD_LANES), lambda i, h, j: (i, h, j, 0)),
            kv_spec,
            kv_spec,
        ],
        out_specs=pl.BlockSpec((1, tq, HEAD_LANES), lambda i, h, j: (i, j, h)),
        out_shape=jax.ShapeDtypeStruct((b, s, nh * V_DIM), BF16),
        compiler_params=_cparams(("parallel", "parallel", "parallel")),
        name="mla_attention_online" if online else "mla_attention",
    )(q, k, v)


def _ssd_kernel(xbc_ref, misc_ref, z_ref, cw_ref, cb_ref, dtb_ref, alog_ref, dskip_ref, gn_ref, y_ref,
                state_ref, carry_ref):
    c = pl.program_id(1)
    t = CHUNK

    @pl.when(c == 0)
    def _():
        state_ref[...] = jnp.zeros_like(state_ref)
        carry_ref[...] = jnp.zeros_like(carry_ref)

    xr = xbc_ref[0].astype(F32)
    xcat = jnp.concatenate([carry_ref[...], xr], axis=0)
    carry_ref[...] = xr[t - CONV_CARRY:, :]
    conv = jnp.zeros((t, CONV_CH), F32) + cb_ref[...]
    for kk in range(CONV_K):
        sh = CONV_K - 1 - kk
        shifted = xcat if sh == 0 else pltpu.roll(xcat, sh, 0)
        conv = conv + cw_ref[kk:kk + 1, :] * shifted[CONV_CARRY:, :]
    xa = conv * _sigmoid(conv)
    xs = xa[:, :D_INNER]
    gw = SSD_GROUPS * SSD_STATE
    bmat = xa[:, D_INNER:D_INNER + gw]
    cmat = xa[:, D_INNER + gw:]

    u = misc_ref[0] + dtb_ref[...]
    dt = jnp.maximum(u, 0.0) + jnp.log(1.0 + jnp.exp(-jnp.abs(u)))
    a = -jnp.exp(alog_ref[...])
    lane = lax.broadcasted_iota(I32, (1, LANES), 1)
    adt = jnp.where(lane < B_HEADS, dt * a, 0.0)
    rowi = lax.broadcasted_iota(I32, (t, LANES), 0)
    acs = adt
    sh = 1
    while sh < t:
        acs = acs + jnp.where(rowi >= sh, pltpu.roll(acs, sh, 0), 0.0)
        sh *= 2
    acs_t = acs.T
    tri = lax.broadcasted_iota(I32, (t, t), 0) >= lax.broadcasted_iota(I32, (t, t), 1)

    rep = B_HEADS // SSD_GROUPS
    ys = []
    for g in range(SSD_GROUPS):
        bg = bmat[:, g * SSD_STATE:(g + 1) * SSD_STATE]
        cg = cmat[:, g * SSD_STATE:(g + 1) * SSD_STATE]
        bg16, cg16 = bg.astype(BF16), cg.astype(BF16)
        cb = _dot_nt(cg16, bg16)
        bg_t = bg.T
        for r in range(rep):
            hd = g * rep + r
            col = acs[:, hd:hd + 1]
            rw = acs_t[hd:hd + 1, :]
            last = acs_t[hd:hd + 1, t - 1:t]
            decay = jnp.exp(jnp.where(tri, col - rw, -jnp.inf))
            xh = xs[:, hd * SSD_HEAD_DIM:(hd + 1) * SSD_HEAD_DIM]
            xdt = (xh * dt[:, hd:hd + 1]).astype(BF16)
            y_diag = _dot((cb * decay).astype(BF16), xdt)
            prev = state_ref[hd]
            y_off = _dot(cg16, prev.astype(BF16)) * jnp.exp(col)
            new_state = _dot((bg_t * jnp.exp(last - rw)).astype(BF16), xdt)
            state_ref[hd] = prev * jnp.exp(last) + new_state
            ys.append(y_diag + y_off)
    y = jnp.concatenate(ys, axis=1) + xs * dskip_ref[...]
    zf = z_ref[0].astype(F32)
    y = y * (zf * _sigmoid(zf))
    y_ref[0] = _rms(y, gn_ref[...]).astype(BF16)


def _ssd(xbc, misc, z, cw, cb, dtb, alog, dskip, gn):
    b, s, _ = xbc.shape
    grid = (b, s // CHUNK)
    row = lambda i, j: (i, j, 0)
    fixed2 = lambda i, j: (0, 0)
    return pl.pallas_call(
        _ssd_kernel,
        grid=grid,
        in_specs=[
            pl.BlockSpec((1, CHUNK, CONV_CH), row),
            pl.BlockSpec((1, CHUNK, LANES), row),
            pl.BlockSpec((1, CHUNK, D_INNER), row),
            pl.BlockSpec((CONV_K, CONV_CH), fixed2),
            pl.BlockSpec((1, CONV_CH), fixed2),
            pl.BlockSpec((1, LANES), fixed2),
            pl.BlockSpec((1, LANES), fixed2),
            pl.BlockSpec((1, D_INNER), fixed2),
            pl.BlockSpec((1, D_INNER), fixed2),
        ],
        out_specs=pl.BlockSpec((1, CHUNK, D_INNER), row),
        out_shape=jax.ShapeDtypeStruct((b, s, D_INNER), BF16),
        scratch_shapes=[
            pltpu.VMEM((B_HEADS, SSD_STATE, SSD_HEAD_DIM), F32),
            pltpu.VMEM((CONV_CARRY, CONV_CH), F32),
        ],
        compiler_params=_cparams(("parallel", "arbitrary")),
        name="ssd_scan",
    )(xbc, misc, z, cw, cb, dtb, alog, dskip, gn)


XW = X_HEADS * X_HEAD_DIM


def _mem_kv_kernel(mem_ref, ln_ref, wkv_ref, kg_ref, hsum_ref, kbd_ref, vbd_ref):
    m = mem_ref.shape[1]
    mn = _rms(mem_ref[0], ln_ref[...]).astype(BF16)
    kv = _dot(mn, wkv_ref[...])
    k, v = kv[:, :XW], kv[:, XW:]
    ss = _dot((k * k).astype(BF16), hsum_ref[...])
    kn = (k * lax.rsqrt(ss * (1.0 / X_HEAD_DIM) + RMS_EPS) * kg_ref[...]).astype(BF16)
    v16 = v.astype(BF16)
    head_of_lane = lax.shift_right_arithmetic(lax.broadcasted_iota(I32, (1, XW), 1), jnp.int32(_LOG2_XHD))
    for hd in range(X_HEADS):
        keep = head_of_lane == hd
        kbd_ref[0, hd * m:(hd + 1) * m, :] = jnp.where(keep, kn, jnp.zeros_like(kn))
        vbd_ref[0, hd * m:(hd + 1) * m, :] = jnp.where(keep, v16, jnp.zeros_like(v16))


def _mem_kv(mem, ln, wkv, kg, hsum):
    b, m, d = mem.shape
    fixed2 = lambda i: (0, 0)
    return pl.pallas_call(
        _mem_kv_kernel,
        grid=(b,),
        in_specs=[
            pl.BlockSpec((1, m, d), lambda i: (i, 0, 0)),
            pl.BlockSpec((1, d), fixed2),
            pl.BlockSpec((d, 2 * XW), fixed2),
            pl.BlockSpec((1, XW), fixed2),
            pl.BlockSpec((XW, XW), fixed2),
        ],
        out_specs=[
            pl.BlockSpec((1, X_HEADS * m, XW), lambda i: (i, 0, 0)),
            pl.BlockSpec((1, X_HEADS * m, XW), lambda i: (i, 0, 0)),
        ],
        out_shape=[
            jax.ShapeDtypeStruct((b, X_HEADS * m, XW), BF16),
            jax.ShapeDtypeStruct((b, X_HEADS * m, XW), BF16),
        ],
        compiler_params=_cparams(("parallel",)),
        name="mem_kv",
    )(mem, ln, wkv, kg, hsum)


ROUTE_LANES = LANES
ROW_PARTS = 2
_GROUP_LANE0 = 0
_EXPERT_LANE0 = MOE_GROUPS
_LOG2_EPG = EXPERTS_PER_GROUP.bit_length() - 1
_LOG2_XHD = X_HEAD_DIM.bit_length() - 1


def _pack_bf16_pairs(v):
    w = v.shape[1] // 2
    r = v.astype(BF16).astype(F32)
    hi = lax.bitcast_convert_type(r[:, :w], U32)
    lo = lax.bitcast_convert_type(r[:, w:], U32)
    return (hi & jnp.uint32(0xFFFF0000)) | (lo >> jnp.uint32(16))


def _unpack_bf16_pairs(u):
    hi = lax.bitcast_convert_type(u & jnp.uint32(0xFFFF0000), F32)
    lo = lax.bitcast_convert_type(u << jnp.uint32(16), F32)
    return hi, lo


def _tail(x1, kbd_ref, vbd_ref, lnq_ref, wq_ref, qg_ref, hsum_ref, wo_ref, lnf_ref, rwh_ref, rwl_ref, rb_ref,
          ltri_ref, x2_ref, *out_refs):
    hfp_refs, (route_ref, cnt_ref) = out_refs[:ROW_PARTS], out_refs[ROW_PARTS:]
    tm = x1.shape[0]
    m = kbd_ref.shape[1] // X_HEADS
    hq = _rms(x1, lnq_ref[...]).astype(BF16)
    q = _dot(hq, wq_ref[...])
    ss = _dot((q * q).astype(BF16), hsum_ref[...])
    qn = (q * lax.rsqrt(ss * (1.0 / X_HEAD_DIM) + RMS_EPS) * qg_ref[...] * (X_HEAD_DIM ** -0.5)).astype(BF16)
    s = _dot_nt(qn, kbd_ref[0])
    ps = []
    for hd in range(X_HEADS):
        sh = s[:, hd * m:(hd + 1) * m]
        e = jnp.exp(sh - jnp.max(sh, axis=-1, keepdims=True))
        ps.append((e / jnp.sum(e, axis=-1, keepdims=True)).astype(BF16))
    o = _dot(jnp.concatenate(ps, axis=1), vbd_ref[0]).astype(BF16)
    x2 = x1 + _dot(o, wo_ref[...])
    x2_ref[0] = x2

    hf = _rms(x2, lnf_ref[...])
    hf_hi = hf.astype(BF16)
    packed = _pack_bf16_pairs(hf)
    pw = packed.shape[1] // ROW_PARTS
    for c in range(ROW_PARTS):
        hfp_refs[c][0] = packed[:, c * pw:(c + 1) * pw]
    hf_lo = (hf - hf_hi.astype(F32)).astype(BF16)
    logits = _dot(hf_hi, rwh_ref[...]) + _dot(hf_hi, rwl_ref[...]) + _dot(hf_lo, rwh_ref[...]) + rb_ref[...]

    lane_i = lax.broadcasted_iota(I32, (tm, ROUTE_LANES), 1)
    lane = lane_i.astype(F32)
    big = float(ROUTE_LANES)
    neg = -jnp.inf
    gl = jnp.where(lane_i < MOE_GROUPS, logits, neg)
    gmax = jnp.max(gl, axis=-1, keepdims=True)
    gsum = jnp.sum(jnp.exp(gl - gmax), axis=-1, keepdims=True)
    g_p = 1.0 / gsum
    g_idx = jnp.min(jnp.where(gl == gmax, lane, big), axis=-1, keepdims=True)
    e_lane = lane_i - _EXPERT_LANE0
    grp_of_lane = lax.shift_right_arithmetic(e_lane, jnp.int32(_LOG2_EPG)).astype(F32)
    in_grp = (e_lane >= 0) & (e_lane < N_EXPERTS) & (grp_of_lane == g_idx)
    el = jnp.where(in_grp, logits, neg)
    emax = jnp.max(el, axis=-1, keepdims=True)
    idx1 = jnp.min(jnp.where(el == emax, lane, big), axis=-1, keepdims=True)
    el2 = jnp.where(lane == idx1, neg, el)
    emax2 = jnp.max(el2, axis=-1, keepdims=True)
    idx2 = jnp.min(jnp.where(el2 == emax2, lane, big), axis=-1, keepdims=True)
    r2 = jnp.exp(emax2 - emax)
    gate1 = g_p / (1.0 + r2)
    gate2 = g_p * r2 / (1.0 + r2)
    e1 = idx1 - float(_EXPERT_LANE0)
    e2 = idx2 - float(_EXPERT_LANE0)

    oh1 = (lane == e1).astype(F32)
    oh2 = (lane == e2).astype(F32)
    both = oh1 + oh2
    before = _dot(ltri_ref[...], both.astype(BF16))
    rank1 = jnp.sum(before * oh1, axis=-1, keepdims=True)
    rank2 = jnp.sum(before * oh2, axis=-1, keepdims=True)
    cnt_ref[0] = jnp.broadcast_to(jnp.sum(both, axis=0, keepdims=True), cnt_ref.shape[1:])

    route = jnp.where(lane == 0, e1, 0.0)
    route = jnp.where(lane == 1, e2, route)
    route = jnp.where(lane == 2, gate1, route)
    route = jnp.where(lane == 3, gate2, route)
    route = jnp.where(lane == 4, rank1, route)
    route = jnp.where(lane == 5, rank2, route)
    route_ref[0] = route


_TAIL_IN = 12


def _post_even_kernel(x_ref, a_ref, y_ref, wout_ref, *rest):
    tail_in, outs = rest[:_TAIL_IN], rest[_TAIL_IN:]
    half = wout_ref.shape[0] // 2
    x1 = x_ref[0] + _dot(a_ref[0], wout_ref[:half, :]) + _dot(y_ref[0], wout_ref[half:, :])
    _tail(x1, *tail_in, *outs)


def _post_pool_kernel(x_ref, ln_ref, pw_ref, pb_ref, ps_ref, *rest):
    tail_in, outs, carry_ref = rest[:_TAIL_IN], rest[_TAIL_IN:-1], rest[-1]
    j = pl.program_id(1)
    tm = x_ref.shape[1]

    @pl.when(j == 0)
    def _():
        carry_ref[...] = jnp.zeros_like(carry_ref)

    x = x_ref[0]
    h = _rms(x, ln_ref[...])
    pos = (j * tm + 1 + lax.broadcasted_iota(I32, (tm, 1), 0)).astype(F32)
    mixed = []
    for g, w in enumerate(POOL_WINDOWS):
        sl = slice(g * POOL_GROUP, (g + 1) * POOL_GROUP)
        hg = h[:, sl]
        acc = jnp.concatenate([carry_ref[:, sl], hg], axis=0)
        sh = 1
        while sh < w:
            acc = acc + pltpu.roll(acc, sh, 0)
            sh *= 2
        win = acc[POOL_CARRY:, :]
        dlt = win / jnp.minimum(pos, float(w)) - hg
        mixed.append(_dot(dlt.astype(BF16), pw_ref[g]))
    carry_ref[...] = h[tm - POOL_CARRY:, :]
    y = (jnp.concatenate(mixed, axis=1) + pb_ref[...]) * ps_ref[...]
    _tail(x + y, *tail_in, *outs)


def _post(kind, front_args, front_specs, tail_args, b, s, tm, scratch):
    d = D_MODEL
    m4 = tail_args[0].shape[1]
    row = lambda i, j: (i, j, 0)
    fixed2 = lambda i, j: (0, 0)
    per_b = lambda i, j: (i, 0, 0)
    tail_specs = [
        pl.BlockSpec((1, m4, XW), per_b),
        pl.BlockSpec((1, m4, XW), per_b),
        pl.BlockSpec((1, d), fixed2),
        pl.BlockSpec((d, XW), fixed2),
        pl.BlockSpec((1, XW), fixed2),
        pl.BlockSpec((XW, XW), fixed2),
        pl.BlockSpec((XW, d), fixed2),
        pl.BlockSpec((1, d), fixed2),
        pl.BlockSpec((d, ROUTE_LANES), fixed2),
        pl.BlockSpec((d, ROUTE_LANES), fixed2),
        pl.BlockSpec((1, ROUTE_LANES), fixed2),
        pl.BlockSpec((tm, tm), fixed2),
    ]
    nt = s // tm
    pw = d // 2 // ROW_PARTS
    kernel = _post_even_kernel if kind == "even" else _post_pool_kernel
    return pl.pallas_call(
        kernel,
        grid=(b, nt),
        in_specs=front_specs + tail_specs,
        out_specs=[pl.BlockSpec((1, tm, d), row)]
        + [pl.BlockSpec((1, tm, pw), row)] * ROW_PARTS
        + [pl.BlockSpec((1, tm, ROUTE_LANES), row),
           pl.BlockSpec((1, 8, ROUTE_LANES), lambda i, j: (i * nt + j, 0, 0))],
        out_shape=[jax.ShapeDtypeStruct((b, s, d), F32)]
        + [jax.ShapeDtypeStruct((b, s, pw), U32)] * ROW_PARTS
        + [jax.ShapeDtypeStruct((b, s, ROUTE_LANES), F32),
           jax.ShapeDtypeStruct((b * nt, 8, ROUTE_LANES), F32)],
        scratch_shapes=scratch,
        compiler_params=_cparams(("parallel", "arbitrary")),
        name="post_" + kind,
    )(*front_args, *tail_args)


FFN_ROWS = 512
COMBINE_TOKENS = 512
SC_GATHER_WINDOW = 128


def _sc_gather_rows(table, idx):
    m, w = idx.shape[0], table.shape[1]
    mesh = plsc.VectorSubcoreMesh(core_axis_name="core", subcore_axis_name="subcore")

    @pl.kernel(out_type=jax.ShapeDtypeStruct((m, w), table.dtype), mesh=mesh, name="moe_row_gather")
    def gather(t_hbm, i_hbm, o_hbm):
        def body(i_vmem, o_vmem):
            pltpu.sync_copy(t_hbm.at[i_vmem.at[0]], o_vmem)

        pltpu.emit_pipeline(
            body,
            grid=(m // SC_GATHER_WINDOW,),
            in_specs=[pl.BlockSpec((1, SC_GATHER_WINDOW), lambda i: (0, i))],
            out_specs=[pl.BlockSpec((SC_GATHER_WINDOW, w), lambda i: (i, 0))],
            core_axis_name=("core", "subcore"),
            dimension_semantics=(pltpu.PARALLEL,),
        )(i_hbm, o_hbm)

    return gather(table, idx.reshape(1, m))


def _sc_scatter_rows(src, dests, pad_rows, n_rows):
    n, w = src.shape
    win = SC_GATHER_WINDOW
    mesh = plsc.VectorSubcoreMesh(core_axis_name="core", subcore_axis_name="subcore")
    idx_spec = pl.BlockSpec((1, win), lambda i: (0, i))
    split = dict(core_axis_name=("core", "subcore"), dimension_semantics=(pltpu.PARALLEL,))

    @pl.kernel(out_type=jax.ShapeDtypeStruct((n_rows, w), src.dtype), mesh=mesh, name="moe_row_scatter")
    def scatter(s_hbm, z_hbm, p_hbm, *rest):
        d_hbms, o_hbm = rest[:-1], rest[-1]

        def body(s_vmem, *i_vmems):
            for i_vmem in i_vmems:
                pltpu.sync_copy(s_vmem, o_hbm.at[i_vmem.at[0]])

        pltpu.emit_pipeline(
            body, grid=(n // win,),
            in_specs=[pl.BlockSpec((win, w), lambda i: (i, 0))] + [idx_spec] * len(dests),
            out_specs=[], **split)(s_hbm, *d_hbms)

        def zero_body(z_vmem, i_vmem):
            pltpu.sync_copy(z_vmem, o_hbm.at[i_vmem.at[0]])

        pltpu.emit_pipeline(
            zero_body, grid=(pad_rows.shape[0] // win,),
            in_specs=[pl.BlockSpec((win, w), lambda i: (0, 0)), idx_spec],
            out_specs=[], **split)(z_hbm, p_hbm)

    zeros = jnp.zeros((win, w), src.dtype)
    return scatter(src, zeros, pad_rows.reshape(1, -1), *[dd.reshape(1, n) for dd in dests])


def _ffn_kernel(be_ref, bi_ref, *refs):
    xb_refs, (wg_ref, wu_ref, wd_ref) = refs[:ROW_PARTS], refs[ROW_PARTS:ROW_PARTS + 3]
    yb_refs, (wg_s, wu_s, wd_s) = refs[ROW_PARTS + 3:2 * ROW_PARTS + 3], refs[2 * ROW_PARTS + 3:]
    i = pl.program_id(0)
    changed = jnp.logical_or(i == 0, be_ref[i] != be_ref[jnp.maximum(i - 1, 0)])

    @pl.when(changed)
    def _():
        wg_s[...] = wg_ref[0, 0].astype(BF16)
        wu_s[...] = wu_ref[0, 0].astype(BF16)
        wd_s[...] = wd_ref[0, 0].astype(BF16)

    half = wg_s.shape[0] // 2
    gate = up = None
    for c in range(ROW_PARTS):
        hi, lo = _unpack_bf16_pairs(xb_refs[c][...])
        hi, lo = hi.astype(BF16), lo.astype(BF16)
        pw = hi.shape[1]
        hs, ls = slice(c * pw, (c + 1) * pw), slice(half + c * pw, half + (c + 1) * pw)
        g = _dot(hi, wg_s[hs, :]) + _dot(lo, wg_s[ls, :])
        u = _dot(hi, wu_s[hs, :]) + _dot(lo, wu_s[ls, :])
        gate, up = (g, u) if gate is None else (gate + g, up + u)
    act = (gate * _sigmoid(gate) * up).astype(BF16)
    packed = _pack_bf16_pairs(_dot(act, wd_s[...]))
    pw = packed.shape[1] // ROW_PARTS
    for c in range(ROW_PARTS):
        yb_refs[c][...] = packed[:, c * pw:(c + 1) * pw]


def _expert_ffn(block_e, block_i, xbs, wg, wu, wd, layer):
    n_rows, pw = xbs[0].shape
    d, ff = wg.shape[2], wg.shape[3]
    n_blk = n_rows // FFN_ROWS
    row_spec = pl.BlockSpec((FFN_ROWS, pw), lambda i, be, bi: (bi[i], 0))
    return pl.pallas_call(
        _ffn_kernel,
        grid_spec=pltpu.PrefetchScalarGridSpec(
            num_scalar_prefetch=2,
            grid=(n_blk,),
            in_specs=[row_spec] * ROW_PARTS + [
                pl.BlockSpec((1, 1, d, ff), lambda i, be, bi: (layer, be[i], 0, 0)),
                pl.BlockSpec((1, 1, d, ff), lambda i, be, bi: (layer, be[i], 0, 0)),
                pl.BlockSpec((1, 1, ff, d), lambda i, be, bi: (layer, be[i], 0, 0)),
            ],
            out_specs=[row_spec] * ROW_PARTS,
            scratch_shapes=[pltpu.VMEM((d, ff), BF16), pltpu.VMEM((d, ff), BF16), pltpu.VMEM((ff, d), BF16)],
        ),
        out_shape=[jax.ShapeDtypeStruct((n_rows, pw), U32)] * ROW_PARTS,
        compiler_params=_cparams(("arbitrary",)),
        name="moe_expert_ffn",
    )(block_e, block_i, *xbs, wg, wu, wd)


def _combine_kernel(x_ref, route_ref, *refs):
    y_refs, o_ref = refs[:-1], refs[-1]
    half = x_ref.shape[1] // 2
    route = route_ref[...]
    g1, g2 = route[:, 2:3], route[:, 3:4]
    for c in range(ROW_PARTS):
        w = y_refs[2 * c].shape[1]
        h1, l1 = _unpack_bf16_pairs(y_refs[2 * c][...])
        h2, l2 = _unpack_bf16_pairs(y_refs[2 * c + 1][...])
        hs, ls = slice(c * w, (c + 1) * w), slice(half + c * w, half + (c + 1) * w)
        o_ref[:, hs] = x_ref[:, hs] + (h1 * g1 + h2 * g2)
        o_ref[:, ls] = x_ref[:, ls] + (l1 * g1 + l2 * g2)


def _combine(x2, route, ytoks, tc):
    n, d = x2.shape
    w = ytoks[0].shape[1]
    nsteps = n // tc
    y_specs, y_args = [], []
    for ytok in ytoks:
        y_specs += [pl.BlockSpec((tc, w), lambda i: (i, 0)), pl.BlockSpec((tc, w), lambda i: (i + nsteps, 0))]
        y_args += [ytok, ytok]
    return pl.pallas_call(
        _combine_kernel,
        grid=(nsteps,),
        in_specs=[pl.BlockSpec((tc, d), lambda i: (i, 0)), pl.BlockSpec((tc, ROUTE_LANES), lambda i: (i, 0))] + y_specs,
        out_specs=pl.BlockSpec((tc, d), lambda i: (i, 0)),
        out_shape=jax.ShapeDtypeStruct((n, d), F32),
        compiler_params=_cparams(("parallel",)),
        name="moe_combine",
    )(x2, route, *y_args)


def _moe(x2, hfps, route, counts, wg, wu, wd, layer, tm):
    b, s, d = x2.shape
    n = b * s
    route = route.reshape(n, ROUTE_LANES)
    cnt = counts[:, 0, :N_EXPERTS].astype(I32)
    total = jnp.sum(cnt, axis=0)
    padded = (total + FFN_ROWS - 1) // FFN_ROWS * FFN_ROWS
    pad_end = jnp.cumsum(padded)
    pad_start = pad_end - padded
    tile_base = pad_start[None, :] + jnp.cumsum(cnt, axis=0) - cnt
    experts = route[:, 0:TOP_K].astype(I32)
    ranks = route[:, 4:4 + TOP_K].astype(I32)
    base_tok = jnp.repeat(tile_base, tm, axis=0)
    dest = jnp.take_along_axis(base_tok, experts, axis=1) + ranks
    dest_by_slot = dest.T.reshape(n * TOP_K)
    n_blk = (n * TOP_K) // FFN_ROWS + N_EXPERTS
    n_rows = n_blk * FFN_ROWS
    used = pad_end[-1] // FFN_ROWS
    block_i = jnp.minimum(jnp.arange(n_blk, dtype=I32), used - 1).astype(I32)
    ended = (pad_end[None, :] <= (block_i * FFN_ROWS)[:, None]).astype(I32)
    block_e = jnp.minimum(jnp.sum(ended, axis=1), N_EXPERTS - 1).astype(I32)
    seg_len = jnp.concatenate([padded - total, (n_rows - pad_end[-1])[None]])
    seg_first = jnp.concatenate([pad_start + total, pad_end[-1:]])
    seg_end = jnp.cumsum(seg_len)
    jpad = jnp.arange(n_rows - n * TOP_K, dtype=I32)
    seg = jnp.sum((seg_end[None, :] <= jpad[:, None]).astype(I32), axis=1)
    pad_rows = (seg_first[seg] + jpad - (seg_end - seg_len)[seg]).astype(I32)

    tc = min(COMBINE_TOKENS, n)
    dests = [dest[:, k] for k in range(TOP_K)]
    xbs = [_sc_scatter_rows(part.reshape(n, part.shape[-1]), dests, pad_rows, n_rows) for part in hfps]
    yb = _expert_ffn(block_e, block_i, xbs, wg, wu, wd, layer)
    ytoks = [_sc_gather_rows(part, dest_by_slot) for part in yb]
    out = _combine(x2.reshape(n, d), route, ytoks, tc)
    return out.reshape(b, s, d)


def _rope_lane_freq():
    inv = ROPE_THETA ** (-jnp.arange(0, ROPE_DIM // 2, dtype=F32) * 2.0 / ROPE_DIM)
    idx = np.full((HEAD_LANES,), -1, np.int64)
    for r in range(ROPE_DIM):
        idx[_head_lane(NOPE_DIM + r)] = r % ROPE_HALF
    return _gather_cols(inv[None, :], idx)


FAST_SOFTMAX_MAX_LOG2 = 60.0


def _score_bound_log2(qg, kg):
    return 1.02 * LOG2E * QK_DIM ** 0.5 * jnp.max(jnp.abs(qg)) * jnp.max(jnp.abs(kg))


def _partner_lanes(idx):
    out = np.full_like(idx, -1)
    for base in range(0, idx.shape[0], HEAD_LANES):
        for r in range(ROPE_DIM):
            lane = _head_lane(NOPE_DIM + r)
            out[base + lane] = idx[base + (lane + HALF_LANES) % HEAD_LANES]
    return out


def kernel(x, mem, positions, ln_mix, w_in, q_lat_norm, w_uq, kv_lat_norm, w_ukv, q_norm, k_norm, conv_w, conv_b,
           dt_bias, a_log, d_skip, ssd_norm, w_out, pool_w, pool_b, pool_scale, ln_xq, ln_mem, xq_w, xkv_w, xq_norm,
           xk_norm, xo_w, ln_ffn, rg_w, rg_b, re_w, re_b, exp_w_gate, exp_w_up, exp_w_down):
    b, s, d = x.shape
    depth = ln_mix.shape[0]
    tm = min(512, s)
    tq = min(512, s)
    assert d == D_MODEL and s % tm == 0 and s % CHUNK == 0 and tm >= POOL_CARRY

    pos = positions.astype(F32)[..., None]
    invf = _rope_lane_freq()
    hsum = jnp.asarray(np.kron(np.eye(X_HEADS), np.ones((X_HEAD_DIM, X_HEAD_DIM))), BF16)
    ltri = jnp.asarray(np.tril(np.ones((tm, tm)), -1), BF16)
    row2 = lambda v: v.reshape(1, -1)
    lane_pad = lambda v: jnp.pad(v, (0, LANES - v.shape[0])).reshape(1, LANES)

    for layer in range(depth):
        j = layer // 2
        kbd, vbd = _mem_kv(mem, row2(ln_mem[layer]), xkv_w[layer].astype(BF16),
                           row2(jnp.tile(xk_norm[layer], X_HEADS)), hsum)
        rw = jnp.pad(jnp.concatenate([rg_w[layer], re_w[layer]], axis=1),
                     ((0, 0), (0, ROUTE_LANES - MOE_GROUPS - N_EXPERTS)))
        rw_hi = rw.astype(BF16)
        rw_lo = (rw - rw_hi.astype(F32)).astype(BF16)
        rb = lane_pad(jnp.concatenate([rg_b[layer], re_b[layer]]))
        tail_args = [kbd, vbd, row2(ln_xq[layer]), xq_w[layer].astype(BF16), row2(jnp.tile(xq_norm[layer], X_HEADS)),
                     hsum, xo_w[layer].astype(BF16), row2(ln_ffn[layer]), rw_hi, rw_lo, rb, ltri]
        row = lambda i, jj: (i, jj, 0)
        fixed2 = lambda i, jj: (0, 0)
        if layer % 2 == 0:
            win = _gather_cols(w_in[j], _win_col_index()).astype(BF16)
            q_idx = _head_col_index(QK_DIM, 0, QK_DIM)
            wuq = _gather_cols(w_uq[j], q_idx).astype(BF16)
            wuq_p = _gather_cols(w_uq[j], _partner_lanes(q_idx)).astype(BF16)
            wuk = _gather_cols(w_ukv[j], _head_col_index(NOPE_DIM + V_DIM, 0, NOPE_DIM)).astype(BF16)
            v_idx = np.full((A_HEADS * HEAD_LANES,), -1, np.int64)
            for hd in range(A_HEADS):
                v_idx[hd * HEAD_LANES:hd * HEAD_LANES + V_DIM] = hd * (NOPE_DIM + V_DIM) + NOPE_DIM + np.arange(V_DIM)
            wuv = _gather_cols(w_ukv[j], v_idx).astype(BF16)
            bound = _score_bound_log2(q_norm[j], k_norm[j])
            koff = jnp.zeros((1, HEAD_LANES), F32).at[0, SCORE_PAD_LANE].set(-bound)
            gain_idx = _head_col_index(QK_DIM, 0, QK_DIM)[:HEAD_LANES]
            lane_consts = jnp.concatenate(
                [_gather_cols(g[None, :], idx) for g in (q_norm[j], k_norm[j])
                 for idx in (gain_idx, _partner_lanes(gain_idx))]
                + [koff, invf, jnp.zeros((2, HEAD_LANES), F32)], axis=0)
            q, k, v, z, xbc, misc = _front_even(
                x, pos, lane_consts, row2(ln_mix[layer]), win, row2(q_lat_norm[j]), wuq, wuq_p,
                row2(kv_lat_norm[j]), wuk, wuv, tm)
            attn = lax.cond(bound <= FAST_SOFTMAX_MAX_LOG2,
                            functools.partial(_attention, tq=tq, online=False),
                            functools.partial(_attention, tq=tq, online=True), q, k, v)
            y = _ssd(xbc, misc, z, conv_w[j], row2(conv_b[j]), lane_pad(dt_bias[j]), lane_pad(a_log[j]),
                     row2(jnp.repeat(d_skip[j], SSD_HEAD_DIM)), row2(ssd_norm[j]))
            half = A_HEADS * V_DIM
            front_args = [x, attn, y, w_out[j].astype(BF16)]
            front_specs = [pl.BlockSpec((1, tm, d), row), pl.BlockSpec((1, tm, half), row),
                           pl.BlockSpec((1, tm, D_INNER), row), pl.BlockSpec((half + D_INNER, d), fixed2)]
            x2, *hfps, route, counts = _post("even", front_args, front_specs, tail_args, b, s, tm, [])
        else:
            front_args = [x, row2(ln_mix[layer]), pool_w[j].astype(BF16), row2(pool_b[j]), row2(pool_scale[j])]
            front_specs = [pl.BlockSpec((1, tm, d), row), pl.BlockSpec((1, d), fixed2),
                           pl.BlockSpec((len(POOL_WINDOWS), POOL_GROUP, POOL_GROUP), lambda i, jj: (0, 0, 0)),
                           pl.BlockSpec((1, d), fixed2), pl.BlockSpec((1, d), fixed2)]
            x2, *hfps, route, counts = _post("pool", front_args, front_specs, tail_args, b, s, tm,
                                           [pltpu.VMEM((POOL_CARRY, d), F32)])
        x = _moe(x2, hfps, route, counts, exp_w_gate, exp_w_up, exp_w_down, layer, tm)
    return x
```

```python
import functools

import numpy as np
import jax
import jax.numpy as jnp
from jax import lax
from jax.experimental import pallas as pl
from jax.experimental.pallas import tpu as pltpu
from jax.experimental.pallas import tpu_sc as plsc

F32 = jnp.float32
BF16 = jnp.bfloat16
U32 = jnp.uint32
I32 = jnp.int32

RMS_EPS = 1e-6
ROPE_THETA = 10000.0

D_MODEL = 1024
X_HEADS, X_HEAD_DIM = 4, 64
A_HEADS, NOPE_DIM, ROPE_DIM, V_DIM = 8, 64, 32, 64
QK_DIM = NOPE_DIM + ROPE_DIM
Q_LORA, KV_LORA = 256, 128
B_HEADS, SSD_HEAD_DIM, SSD_GROUPS, SSD_STATE, CONV_K, CHUNK = 8, 64, 2, 128, 4, 128
D_INNER = B_HEADS * SSD_HEAD_DIM
CONV_CH = D_INNER + 2 * SSD_GROUPS * SSD_STATE
POOL_WINDOWS = (2, 4, 8, 16)
POOL_GROUP = D_MODEL // 4
MOE_GROUPS, EXPERTS_PER_GROUP, TOP_K, EXPERT_FF = 4, 8, 2, 256
N_EXPERTS = MOE_GROUPS * EXPERTS_PER_GROUP

LANES = 128
HEAD_LANES = LANES
HALF_LANES = LANES // 2
ROPE_HALF = ROPE_DIM // 2
NOPE_HALF = NOPE_DIM // 2
POOL_CARRY = 16
CONV_CARRY = 8
VMEM_LIMIT = 56 * 1024 * 1024


def _cparams(sem):
    return pltpu.CompilerParams(dimension_semantics=sem, vmem_limit_bytes=VMEM_LIMIT)


def _rms(u, g):
    return u * lax.rsqrt(jnp.mean(u * u, axis=-1, keepdims=True) + RMS_EPS) * g


def _sigmoid(u):
    return 1.0 / (1.0 + jnp.exp(-u))


def _dot(a, b):
    return jnp.dot(a, b, preferred_element_type=F32)


def _dot_nt(a, b):
    return lax.dot_general(a, b, (((1,), (1,)), ((), ())), preferred_element_type=F32)


def _head_lane(d):
    if d < NOPE_HALF:
        return d
    if d < NOPE_DIM:
        return HALF_LANES + (d - NOPE_HALF)
    r = d - NOPE_DIM
    if r < ROPE_HALF:
        return NOPE_HALF + r
    return HALF_LANES + NOPE_HALF + (r - ROPE_HALF)


def _gather_cols(w, idx):
    w_ext = jnp.concatenate([w, jnp.zeros(w.shape[:-1] + (1,), w.dtype)], axis=-1)
    idx = np.where(idx < 0, w.shape[-1], idx)
    return jnp.take(w_ext, jnp.asarray(idx, dtype=jnp.int32), axis=-1)


IN_W = 2 * D_MODEL
_OFF_QLAT, _OFF_KVLAT, _OFF_MISC, _OFF_Z, _OFF_XBC = 0, 256, 384, 512, 1024


def _win_col_index():
    idx = np.full((IN_W,), -1, np.int64)
    idx[_OFF_QLAT:_OFF_QLAT + Q_LORA] = np.arange(Q_LORA)
    idx[_OFF_KVLAT:_OFF_KVLAT + KV_LORA] = Q_LORA + np.arange(KV_LORA)
    rope0 = Q_LORA + KV_LORA
    for r in range(ROPE_DIM):
        idx[_OFF_MISC + _head_lane(NOPE_DIM + r)] = rope0 + r
    z0 = rope0 + ROPE_DIM
    idx[_OFF_Z:_OFF_Z + D_INNER] = z0 + np.arange(D_INNER)
    xbc0 = z0 + D_INNER
    idx[_OFF_XBC:_OFF_XBC + CONV_CH] = xbc0 + np.arange(CONV_CH)
    dt0 = xbc0 + CONV_CH
    idx[_OFF_MISC:_OFF_MISC + B_HEADS] = dt0 + np.arange(B_HEADS)
    return idx


def _head_col_index(per_head, offset, count):
    idx = np.full((A_HEADS * HEAD_LANES,), -1, np.int64)
    for h in range(A_HEADS):
        for d in range(count):
            idx[h * HEAD_LANES + _head_lane(d)] = h * per_head + offset + d
    return idx


SCORE_PAD_LANE = NOPE_HALF + ROPE_HALF
ONES_LANE = V_DIM
LOG2E = 1.4426950408889634


def _front_even_kernel(x_ref, pos_ref, lc_ref, ln_ref, win_ref, qln_ref, wuq_ref, wuqr_ref, kvln_ref, wuk_ref, wuv_ref,
                       q_ref, k_ref, v_ref, z_ref, xbc_ref, misc_ref):
    x = x_ref[0]
    h = _rms(x, ln_ref[...]).astype(BF16)
    proj = _dot(h, win_ref[...])
    misc = proj[:, _OFF_MISC:_OFF_Z]
    z_ref[0] = proj[:, _OFF_Z:_OFF_XBC].astype(BF16)
    xbc_ref[0] = proj[:, _OFF_XBC:].astype(BF16)
    misc_ref[0] = misc
    ql = _rms(proj[:, _OFF_QLAT:_OFF_KVLAT], qln_ref[...]).astype(BF16)
    kvl = _rms(proj[:, _OFF_KVLAT:_OFF_MISC], kvln_ref[...]).astype(BF16)
    q = _dot(ql, wuq_ref[...])
    kn = _dot(kvl, wuk_ref[...])
    v = _dot(kvl, wuv_ref[...])
    lane = lax.broadcasted_iota(I32, (1, HEAD_LANES), 1)
    first_half = (lane >= NOPE_HALF) & (lane < NOPE_HALF + ROPE_HALF)
    second_half = (lane >= HALF_LANES + NOPE_HALF) & (lane < HALF_LANES + NOPE_HALF + ROPE_HALF)
    lc = lc_ref[...]
    qg, qg_p, kg, kg_p, k_off, invf = (lc[i:i + 1] for i in range(6))
    ones = jnp.ones((HEAD_LANES, HEAD_LANES), BF16)

    def lane_sumsq(u):
        return _dot((u * u).astype(BF16), ones)

    krope = jnp.where(first_half | second_half, misc, 0.0)
    kr_ss = lane_sumsq(krope)
    ang = pos_ref[0] * invf
    cos_t = jnp.cos(ang)
    sin_t = jnp.where(first_half, -jnp.sin(ang), jnp.sin(ang))
    q_one = (lane == SCORE_PAD_LANE).astype(F32)
    v_one = (lane == ONES_LANE).astype(F32)
    q_scale = QK_DIM ** -0.5 * LOG2E
    qa, qb = qg * cos_t * q_scale, qg_p * sin_t * q_scale
    ka, kc = kg * cos_t, pltpu.roll(krope, HALF_LANES, 1) * (kg_p * sin_t)
    q_p = _dot(ql, wuqr_ref[...])
    for hd in range(A_HEADS):
        sl = slice(hd * HEAD_LANES, (hd + 1) * HEAD_LANES)
        qs = q[:, sl]
        inv = lax.rsqrt(lane_sumsq(qs) * (1.0 / QK_DIM) + RMS_EPS)
        q_ref[0, hd] = ((qs * qa + q_p[:, sl] * qb) * inv + q_one).astype(BF16)
        kns = kn[:, sl]
        inv = lax.rsqrt((lane_sumsq(kns) + kr_ss) * (1.0 / QK_DIM) + RMS_EPS)
        k_ref[0, hd] = (((kns + krope) * ka + kc) * inv + k_off).astype(BF16)
        v_ref[0, hd] = (v[:, sl] + v_one).astype(BF16)


def _front_even(x, pos, lane_consts, ln, win, qln, wuq, wuq_p, kvln, wuk, wuv, tm):
    b, s, d = x.shape
    grid = (b, s // tm)
    row = lambda i, j: (i, j, 0)
    fixed2 = lambda i, j: (0, 0)
    head_row = lambda i, j: (i, 0, j, 0)
    hw = A_HEADS * HEAD_LANES
    return pl.pallas_call(
        _front_even_kernel,
        grid=grid,
        in_specs=[
            pl.BlockSpec((1, tm, d), row),
            pl.BlockSpec((1, tm, 1), row),
            pl.BlockSpec((8, HEAD_LANES), fixed2),
            pl.BlockSpec((1, d), fixed2),
            pl.BlockSpec((d, IN_W), fixed2),
            pl.BlockSpec((1, Q_LORA), fixed2),
            pl.BlockSpec((Q_LORA, hw), fixed2),
            pl.BlockSpec((Q_LORA, hw), fixed2),
            pl.BlockSpec((1, KV_LORA), fixed2),
            pl.BlockSpec((KV_LORA, hw), fixed2),
            pl.BlockSpec((KV_LORA, hw), fixed2),
        ],
        out_specs=[
            pl.BlockSpec((1, A_HEADS, tm, HEAD_LANES), head_row),
            pl.BlockSpec((1, A_HEADS, tm, HEAD_LANES), head_row),
            pl.BlockSpec((1, A_HEADS, tm, HEAD_LANES), head_row),
            pl.BlockSpec((1, tm, D_INNER), row),
            pl.BlockSpec((1, tm, CONV_CH), row),
            pl.BlockSpec((1, tm, HEAD_LANES), row),
        ],
        out_shape=[
            jax.ShapeDtypeStruct((b, A_HEADS, s, HEAD_LANES), BF16),
            jax.ShapeDtypeStruct((b, A_HEADS, s, HEAD_LANES), BF16),
            jax.ShapeDtypeStruct((b, A_HEADS, s, HEAD_LANES), BF16),
            jax.ShapeDtypeStruct((b, s, D_INNER), BF16),
            jax.ShapeDtypeStruct((b, s, CONV_CH), BF16),
            jax.ShapeDtypeStruct((b, s, HEAD_LANES), F32),
        ],
        compiler_params=_cparams(("parallel", "parallel")),
        name="front_even",
    )(x, pos, lane_consts, ln, win, qln, wuq, wuq_p, kvln, wuk, wuv)


HEADS_PER_STEP = HEAD_LANES // V_DIM


def _attn_kernel(q_ref, k_ref, v_ref, o_ref, *, tq, online):
    qi = pl.program_id(2)
    row = lax.broadcasted_iota(I32, (tq, tq), 0)
    col = lax.broadcasted_iota(I32, (tq, tq), 1)

    def head_step(hh, j, carry, masked):
        kj = k_ref[0, hh, pl.ds(j * tq, tq), :]
        vj = v_ref[0, hh, pl.ds(j * tq, tq), :]
        s = _dot_nt(q_ref[0, hh], kj)
        if masked:
            s = jnp.where(row >= col, s, -jnp.inf)
        if online:
            m, acc = carry
            m_new = jnp.maximum(m, jnp.max(s, axis=-1, keepdims=True))
            p = jnp.exp2(s - m_new).astype(BF16)
            return m_new, jnp.exp2(m - m_new) * acc + _dot(p, vj)
        return carry + _dot(jnp.exp2(s).astype(BF16), vj)

    def step(j, carries, masked):
        return tuple(head_step(hh, j, carries[hh], masked) for hh in range(HEADS_PER_STEP))

    acc0 = jnp.zeros((tq, HEAD_LANES), F32)
    init = (jnp.full((tq, 1), -jnp.inf, F32), acc0) if online else acc0
    carries = lax.fori_loop(0, qi, functools.partial(step, masked=False), (init,) * HEADS_PER_STEP)
    carries = step(qi, carries, True)
    outs = []
    for carry in carries:
        acc = carry[1] if online else carry
        outs.append(acc / acc[:, ONES_LANE:ONES_LANE + 1])
    lane = lax.broadcasted_iota(I32, (1, HEAD_LANES), 1)
    out = outs[0]
    for hh in range(1, HEADS_PER_STEP):
        out = jnp.where(lane >= hh * V_DIM, pltpu.roll(outs[hh], hh * V_DIM, 1), out)
    o_ref[0] = out.astype(BF16)


def _attention(q, k, v, tq, online):
    b, nh, s, _ = q.shape
    grid = (b, nh // HEADS_PER_STEP, s // tq)
    kv_spec = pl.BlockSpec((1, HEADS_PER_STEP, s, HEAD_LANES), lambda i, h, j: (i, h, 0, 0))
    return pl.pallas_call(
        functools.partial(_attn_kernel, tq=tq, online=online),
        grid=grid,
        in_specs=[
            pl.BlockSpec((1, HEADS_PER_STEP, tq, HEAD_LANES), lambda i, h, j: (i, h, j, 0)),
            kv_spec,
            kv_spec,
        ],
        out_specs=pl.BlockSpec((1, tq, HEAD_LANES), lambda i, h, j: (i, j, h)),
        out_shape=jax.ShapeDtypeStruct((b, s, nh * V_DIM), BF16),
        compiler_params=_cparams(("parallel", "parallel", "parallel")),
        name="mla_attention_online" if online else "mla_attention",
    )(q, k, v)


def _ssd_kernel(xbc_ref, misc_ref, z_ref, cw_ref, cb_ref, dtb_ref, alog_ref, dskip_ref, gn_ref, y_ref,
                state_ref, carry_ref):
    c = pl.program_id(1)
    t = CHUNK

    @pl.when(c == 0)
    def _():
        state_ref[...] = jnp.zeros_like(state_ref)
        carry_ref[...] = jnp.zeros_like(carry_ref)

    xr = xbc_ref[0].astype(F32)
    xcat = jnp.concatenate([carry_ref[...], xr], axis=0)
    carry_ref[...] = xr[t - CONV_CARRY:, :]
    conv = jnp.zeros((t, CONV_CH), F32) + cb_ref[...]
    for kk in range(CONV_K):
        sh = CONV_K - 1 - kk
        shifted = xcat if sh == 0 else pltpu.roll(xcat, sh, 0)
        conv = conv + cw_ref[kk:kk + 1, :] * shifted[CONV_CARRY:, :]
    xa = conv * _sigmoid(conv)
    xs = xa[:, :D_INNER]
    gw = SSD_GROUPS * SSD_STATE
    bmat = xa[:, D_INNER:D_INNER + gw]
    cmat = xa[:, D_INNER + gw:]

    u = misc_ref[0] + dtb_ref[...]
    dt = jnp.maximum(u, 0.0) + jnp.log(1.0 + jnp.exp(-jnp.abs(u)))
    a = -jnp.exp(alog_ref[...])
    lane = lax.broadcasted_iota(I32, (1, LANES), 1)
    adt = jnp.where(lane < B_HEADS, dt * a, 0.0)
    rowi = lax.broadcasted_iota(I32, (t, LANES), 0)
    acs = adt
    sh = 1
    while sh < t:
        acs = acs + jnp.where(rowi >= sh, pltpu.roll(acs, sh, 0), 0.0)
        sh *= 2
    acs_t = acs.T
    tri = lax.broadcasted_iota(I32, (t, t), 0) >= lax.broadcasted_iota(I32, (t, t), 1)

    rep = B_HEADS // SSD_GROUPS
    ys = []
    for g in range(SSD_GROUPS):
        bg = bmat[:, g * SSD_STATE:(g + 1) * SSD_STATE]
        cg = cmat[:, g * SSD_STATE:(g + 1) * SSD_STATE]
        bg16, cg16 = bg.astype(BF16), cg.astype(BF16)
        cb = _dot_nt(cg16, bg16)
        bg_t = bg.T
        for r in range(rep):
            hd = g * rep + r
            col = acs[:, hd:hd + 1]
            rw = acs_t[hd:hd + 1, :]
            last = acs_t[hd:hd + 1, t - 1:t]
            decay = jnp.exp(jnp.where(tri, col - rw, -jnp.inf))
            xh = xs[:, hd * SSD_HEAD_DIM:(hd + 1) * SSD_HEAD_DIM]
            xdt = (xh * dt[:, hd:hd + 1]).astype(BF16)
            y_diag = _dot((cb * decay).astype(BF16), xdt)
            prev = state_ref[hd]
            y_off = _dot(cg16, prev.astype(BF16)) * jnp.exp(col)
            new_state = _dot((bg_t * jnp.exp(last - rw)).astype(BF16), xdt)
            state_ref[hd] = prev * jnp.exp(last) + new_state
            ys.append(y_diag + y_off)
    y = jnp.concatenate(ys, axis=1) + xs * dskip_ref[...]
    zf = z_ref[0].astype(F32)
    y = y * (zf * _sigmoid(zf))
    y_ref[0] = _rms(y, gn_ref[...]).astype(BF16)


def _ssd(xbc, misc, z, cw, cb, dtb, alog, dskip, gn):
    b, s, _ = xbc.shape
    grid = (b, s // CHUNK)
    row = lambda i, j: (i, j, 0)
    fixed2 = lambda i, j: (0, 0)
    return pl.pallas_call(
        _ssd_kernel,
        grid=grid,
        in_specs=[
            pl.BlockSpec((1, CHUNK, CONV_CH), row),
            pl.BlockSpec((1, CHUNK, LANES), row),
            pl.BlockSpec((1, CHUNK, D_INNER), row),
            pl.BlockSpec((CONV_K, CONV_CH), fixed2),
            pl.BlockSpec((1, CONV_CH), fixed2),
            pl.BlockSpec((1, LANES), fixed2),
            pl.BlockSpec((1, LANES), fixed2),
            pl.BlockSpec((1, D_INNER), fixed2),
            pl.BlockSpec((1, D_INNER), fixed2),
        ],
        out_specs=pl.BlockSpec((1, CHUNK, D_INNER), row),
        out_shape=jax.ShapeDtypeStruct((b, s, D_INNER), BF16),
        scratch_shapes=[
            pltpu.VMEM((B_HEADS, SSD_STATE, SSD_HEAD_DIM), F32),
            pltpu.VMEM((CONV_CARRY, CONV_CH), F32),
        ],
        compiler_params=_cparams(("parallel", "arbitrary")),
        name="ssd_scan",
    )(xbc, misc, z, cw, cb, dtb, alog, dskip, gn)


XW = X_HEADS * X_HEAD_DIM


def _mem_kv_kernel(mem_ref, ln_ref, wkv_ref, kg_ref, hsum_ref, kbd_ref, vbd_ref):
    m = mem_ref.shape[1]
    mn = _rms(mem_ref[0], ln_ref[...]).astype(BF16)
    kv = _dot(mn, wkv_ref[...])
    k, v = kv[:, :XW], kv[:, XW:]
    ss = _dot((k * k).astype(BF16), hsum_ref[...])
    kn = (k * lax.rsqrt(ss * (1.0 / X_HEAD_DIM) + RMS_EPS) * kg_ref[...]).astype(BF16)
    v16 = v.astype(BF16)
    head_of_lane = lax.shift_right_arithmetic(lax.broadcasted_iota(I32, (1, XW), 1), jnp.int32(_LOG2_XHD))
    for hd in range(X_HEADS):
        keep = head_of_lane == hd
        kbd_ref[0, hd * m:(hd + 1) * m, :] = jnp.where(keep, kn, jnp.zeros_like(kn))
        vbd_ref[0, hd * m:(hd + 1) * m, :] = jnp.where(keep, v16, jnp.zeros_like(v16))


def _mem_kv(mem, ln, wkv, kg, hsum):
    b, m, d = mem.shape
    fixed2 = lambda i: (0, 0)
    return pl.pallas_call(
        _mem_kv_kernel,
        grid=(b,),
        in_specs=[
            pl.BlockSpec((1, m, d), lambda i: (i, 0, 0)),
            pl.BlockSpec((1, d), fixed2),
            pl.BlockSpec((d, 2 * XW), fixed2),
            pl.BlockSpec((1, XW), fixed2),
            pl.BlockSpec((XW, XW), fixed2),
        ],
        out_specs=[
            pl.BlockSpec((1, X_HEADS * m, XW), lambda i: (i, 0, 0)),
            pl.BlockSpec((1, X_HEADS * m, XW), lambda i: (i, 0, 0)),
        ],
        out_shape=[
            jax.ShapeDtypeStruct((b, X_HEADS * m, XW), BF16),
            jax.ShapeDtypeStruct((b, X_HEADS * m, XW), BF16),
        ],
        compiler_params=_cparams(("parallel",)),
        name="mem_kv",
    )(mem, ln, wkv, kg, hsum)


ROUTE_LANES = LANES
ROW_PARTS = 2
_GROUP_LANE0 = 0
_EXPERT_LANE0 = MOE_GROUPS
_LOG2_EPG = EXPERTS_PER_GROUP.bit_length() - 1
_LOG2_XHD = X_HEAD_DIM.bit_length() - 1


def _pack_bf16_pairs(v):
    w = v.shape[1] // 2
    r = v.astype(BF16).astype(F32)
    hi = lax.bitcast_convert_type(r[:, :w], U32)
    lo = lax.bitcast_convert_type(r[:, w:], U32)
    return (hi & jnp.uint32(0xFFFF0000)) | (lo >> jnp.uint32(16))


def _unpack_bf16_pairs(u):
    hi = lax.bitcast_convert_type(u & jnp.uint32(0xFFFF0000), F32)
    lo = lax.bitcast_convert_type(u << jnp.uint32(16), F32)
    return hi, lo


TAIL_SPLIT = 1


def _tail_rows(x1, rows, kbd_ref, vbd_ref, lnq_ref, wq_ref, qg_ref, hsum_ref, wo_ref, lnf_ref, rwh_ref, rwl_ref,
               rb_ref, x2_ref, hfp_refs):
    tm = x1.shape[0]
    m = kbd_ref.shape[1] // X_HEADS
    hq = _rms(x1, lnq_ref[...]).astype(BF16)
    q = _dot(hq, wq_ref[...])
    ss = _dot((q * q).astype(BF16), hsum_ref[...])
    qn = (q * lax.rsqrt(ss * (1.0 / X_HEAD_DIM) + RMS_EPS) * qg_ref[...] * (X_HEAD_DIM ** -0.5)).astype(BF16)
    s = _dot_nt(qn, kbd_ref[0])
    ps = []
    for hd in range(X_HEADS):
        sh = s[:, hd * m:(hd + 1) * m]
        e = jnp.exp(sh - jnp.max(sh, axis=-1, keepdims=True))
        ps.append((e / jnp.sum(e, axis=-1, keepdims=True)).astype(BF16))
    o = _dot(jnp.concatenate(ps, axis=1), vbd_ref[0]).astype(BF16)
    x2 = x1 + _dot(o, wo_ref[...])
    x2_ref[0, rows, :] = x2

    hf = _rms(x2, lnf_ref[...])
    hf_hi = hf.astype(BF16)
    packed = _pack_bf16_pairs(hf)
    pw = packed.shape[1] // ROW_PARTS
    for c in range(ROW_PARTS):
        hfp_refs[c][0, rows, :] = packed[:, c * pw:(c + 1) * pw]
    hf_lo = (hf - hf_hi.astype(F32)).astype(BF16)
    logits = _dot(hf_hi, rwh_ref[...]) + _dot(hf_hi, rwl_ref[...]) + _dot(hf_lo, rwh_ref[...]) + rb_ref[...]

    lane_i = lax.broadcasted_iota(I32, (tm, ROUTE_LANES), 1)
    lane = lane_i.astype(F32)
    big = float(ROUTE_LANES)
    neg = -jnp.inf
    gl = jnp.where(lane_i < MOE_GROUPS, logits, neg)
    gmax = jnp.max(gl, axis=-1, keepdims=True)
    gsum = jnp.sum(jnp.exp(gl - gmax), axis=-1, keepdims=True)
    g_p = 1.0 / gsum
    g_idx = jnp.min(jnp.where(gl == gmax, lane, big), axis=-1, keepdims=True)
    e_lane = lane_i - _EXPERT_LANE0
    grp_of_lane = lax.shift_right_arithmetic(e_lane, jnp.int32(_LOG2_EPG)).astype(F32)
    in_grp = (e_lane >= 0) & (e_lane < N_EXPERTS) & (grp_of_lane == g_idx)
    el = jnp.where(in_grp, logits, neg)
    emax = jnp.max(el, axis=-1, keepdims=True)
    idx1 = jnp.min(jnp.where(el == emax, lane, big), axis=-1, keepdims=True)
    el2 = jnp.where(lane == idx1, neg, el)
    emax2 = jnp.max(el2, axis=-1, keepdims=True)
    idx2 = jnp.min(jnp.where(el2 == emax2, lane, big), axis=-1, keepdims=True)
    r2 = jnp.exp(emax2 - emax)
    gate1 = g_p / (1.0 + r2)
    gate2 = g_p * r2 / (1.0 + r2)
    e1 = idx1 - float(_EXPERT_LANE0)
    e2 = idx2 - float(_EXPERT_LANE0)

    route = jnp.where(lane == 0, e1, 0.0)
    route = jnp.where(lane == 1, e2, route)
    route = jnp.where(lane == 2, gate1, route)
    route = jnp.where(lane == 3, gate2, route)
    return route, (lane == e1).astype(F32), (lane == e2).astype(F32)


def _tail(x1, kbd_ref, vbd_ref, lnq_ref, wq_ref, qg_ref, hsum_ref, wo_ref, lnf_ref, rwh_ref, rwl_ref, rb_ref,
          ltri_ref, x2_ref, *out_refs):
    hfp_refs, (route_ref, route_t_ref, cnt_ref) = out_refs[:ROW_PARTS], out_refs[ROW_PARTS:]
    tm = x1.shape[0]
    tr = tm // TAIL_SPLIT
    parts = [_tail_rows(x1[r * tr:(r + 1) * tr], slice(r * tr, (r + 1) * tr), kbd_ref, vbd_ref, lnq_ref, wq_ref,
                        qg_ref, hsum_ref, wo_ref, lnf_ref, rwh_ref, rwl_ref, rb_ref, x2_ref, hfp_refs)
             for r in range(TAIL_SPLIT)]
    route, oh1, oh2 = (jnp.concatenate([p[i] for p in parts], axis=0) for i in range(3))
    both = oh1 + oh2
    before = _dot(ltri_ref[...], both.astype(BF16))
    rank1 = jnp.sum(before * oh1, axis=-1, keepdims=True)
    rank2 = jnp.sum(before * oh2, axis=-1, keepdims=True)
    cnt_ref[0] = jnp.broadcast_to(jnp.sum(both, axis=0, keepdims=True), cnt_ref.shape[1:])
    lane = lax.broadcasted_iota(I32, (tm, ROUTE_LANES), 1)
    route = jnp.where(lane == 4, rank1, route)
    route = jnp.where(lane == 5, rank2, route)
    route_ref[0] = route
    route_t_ref[0] = route.T[:route_t_ref.shape[1], :]


_TAIL_IN = 12


def _post_even_kernel(x_ref, a_ref, y_ref, wout_ref, *rest):
    tail_in, outs = rest[:_TAIL_IN], rest[_TAIL_IN:]
    half = wout_ref.shape[0] // 2
    x1 = x_ref[0] + _dot(a_ref[0], wout_ref[:half, :]) + _dot(y_ref[0], wout_ref[half:, :])
    _tail(x1, *tail_in, *outs)


def _add_expert_rows(x, route, ys):
    g1, g2 = route[:, 2:3], route[:, 3:4]
    his, los = [], []
    for y1, y2 in ys:
        h1, l1 = _unpack_bf16_pairs(y1)
        h2, l2 = _unpack_bf16_pairs(y2)
        his.append(h1 * g1 + h2 * g2)
        los.append(l1 * g1 + l2 * g2)
    return x + jnp.concatenate(his + los, axis=1)


N_PENDING = 1 + 2 * ROW_PARTS


def _post_pool_kernel(x_ref, *rest, pending):
    if pending:
        route_prev_ref, y_refs, rest = rest[0], rest[1:N_PENDING], rest[N_PENDING:]
    (ln_ref, pw_ref, pb_ref, ps_ref), rest = rest[:4], rest[4:]
    tail_in, outs, carry_ref = rest[:_TAIL_IN], rest[_TAIL_IN:-1], rest[-1]
    j = pl.program_id(1)
    tm = x_ref.shape[1]

    @pl.when(j == 0)
    def _():
        carry_ref[...] = jnp.zeros_like(carry_ref)

    x = x_ref[0]
    if pending:
        x = _add_expert_rows(x, route_prev_ref[0], [(y_refs[2 * c][...], y_refs[2 * c + 1][...])
                                                    for c in range(ROW_PARTS)])
    h = _rms(x, ln_ref[...])
    pos = (j * tm + 1 + lax.broadcasted_iota(I32, (tm, 1), 0)).astype(F32)
    mixed = []
    for g, w in enumerate(POOL_WINDOWS):
        sl = slice(g * POOL_GROUP, (g + 1) * POOL_GROUP)
        hg = h[:, sl]
        acc = jnp.concatenate([carry_ref[:, sl], hg], axis=0)
        sh = 1
        while sh < w:
            acc = acc + pltpu.roll(acc, sh, 0)
            sh *= 2
        win = acc[POOL_CARRY:, :]
        dlt = win / jnp.minimum(pos, float(w)) - hg
        mixed.append(_dot(dlt.astype(BF16), pw_ref[g]))
    carry_ref[...] = h[tm - POOL_CARRY:, :]
    y = (jnp.concatenate(mixed, axis=1) + pb_ref[...]) * ps_ref[...]
    _tail(x + y, *tail_in, *outs)


def _post(kind, front_args, front_specs, tail_args, b, s, tm, scratch):
    d = D_MODEL
    m4 = tail_args[0].shape[1]
    row = lambda i, j: (i, j, 0)
    fixed2 = lambda i, j: (0, 0)
    per_b = lambda i, j: (i, 0, 0)
    tail_specs = [
        pl.BlockSpec((1, m4, XW), per_b),
        pl.BlockSpec((1, m4, XW), per_b),
        pl.BlockSpec((1, d), fixed2),
        pl.BlockSpec((d, XW), fixed2),
        pl.BlockSpec((1, XW), fixed2),
        pl.BlockSpec((XW, XW), fixed2),
        pl.BlockSpec((XW, d), fixed2),
        pl.BlockSpec((1, d), fixed2),
        pl.BlockSpec((d, ROUTE_LANES), fixed2),
        pl.BlockSpec((d, ROUTE_LANES), fixed2),
        pl.BlockSpec((1, ROUTE_LANES), fixed2),
        pl.BlockSpec((tm, tm), fixed2),
    ]
    nt = s // tm
    pw = d // 2 // ROW_PARTS
    kernel = {"even": _post_even_kernel,
              "pool": functools.partial(_post_pool_kernel, pending=False),
              "pool_pending": functools.partial(_post_pool_kernel, pending=True)}[kind]
    return pl.pallas_call(
        kernel,
        grid=(b, nt),
        in_specs=front_specs + tail_specs,
        out_specs=[pl.BlockSpec((1, tm, d), row)]
        + [pl.BlockSpec((1, tm, pw), row)] * ROW_PARTS
        + [pl.BlockSpec((1, tm, ROUTE_LANES), row),
           pl.BlockSpec((1, 8, tm), lambda i, j: (i * nt + j, 0, 0)),
           pl.BlockSpec((1, 8, ROUTE_LANES), lambda i, j: (i * nt + j, 0, 0))],
        out_shape=[jax.ShapeDtypeStruct((b, s, d), F32)]
        + [jax.ShapeDtypeStruct((b, s, pw), U32)] * ROW_PARTS
        + [jax.ShapeDtypeStruct((b, s, ROUTE_LANES), F32),
           jax.ShapeDtypeStruct((b * nt, 8, tm), F32),
           jax.ShapeDtypeStruct((b * nt, 8, ROUTE_LANES), F32)],
        scratch_shapes=scratch,
        compiler_params=_cparams(("parallel", "arbitrary")),
        name="post_" + kind,
    )(*front_args, *tail_args)


FFN_ROWS = 512
COMBINE_TOKENS = 512
SC_GATHER_WINDOW = 128


def _sc_gather_rows(table, idx):
    m, w = idx.shape[0], table.shape[1]
    mesh = plsc.VectorSubcoreMesh(core_axis_name="core", subcore_axis_name="subcore")

    @pl.kernel(out_type=jax.ShapeDtypeStruct((m, w), table.dtype), mesh=mesh, name="moe_row_gather")
    def gather(t_hbm, i_hbm, o_hbm):
        def body(i_vmem, o_vmem):
            pltpu.sync_copy(t_hbm.at[i_vmem.at[0]], o_vmem)

        pltpu.emit_pipeline(
            body,
            grid=(m // SC_GATHER_WINDOW,),
            in_specs=[pl.BlockSpec((1, SC_GATHER_WINDOW), lambda i: (0, i))],
            out_specs=[pl.BlockSpec((SC_GATHER_WINDOW, w), lambda i: (i, 0))],
            core_axis_name=("core", "subcore"),
            dimension_semantics=(pltpu.PARALLEL,),
        )(i_hbm, o_hbm)

    return gather(table, idx.reshape(1, m))


def _sc_scatter_rows(src, dests, pad_rows, n_rows):
    n, w = src.shape
    win = SC_GATHER_WINDOW
    mesh = plsc.VectorSubcoreMesh(core_axis_name="core", subcore_axis_name="subcore")
    idx_spec = pl.BlockSpec((1, win), lambda i: (0, i))
    split = dict(core_axis_name=("core", "subcore"), dimension_semantics=(pltpu.PARALLEL,))

    @pl.kernel(out_type=jax.ShapeDtypeStruct((n_rows, w), src.dtype), mesh=mesh, name="moe_row_scatter")
    def scatter(s_hbm, z_hbm, p_hbm, *rest):
        d_hbms, o_hbm = rest[:-1], rest[-1]

        def body(s_vmem, *i_vmems):
            for i_vmem in i_vmems:
                pltpu.sync_copy(s_vmem, o_hbm.at[i_vmem.at[0]])

        pltpu.emit_pipeline(
            body, grid=(n // win,),
            in_specs=[pl.BlockSpec((win, w), lambda i: (i, 0))] + [idx_spec] * len(dests),
            out_specs=[], **split)(s_hbm, *d_hbms)

        def zero_body(z_vmem, i_vmem):
            pltpu.sync_copy(z_vmem, o_hbm.at[i_vmem.at[0]])

        pltpu.emit_pipeline(
            zero_body, grid=(pad_rows.shape[0] // win,),
            in_specs=[pl.BlockSpec((win, w), lambda i: (0, 0)), idx_spec],
            out_specs=[], **split)(z_hbm, p_hbm)

    zeros = jnp.zeros((win, w), src.dtype)
    return scatter(src, zeros, pad_rows.reshape(1, -1), *[dd.reshape(1, n) for dd in dests])


def _ffn_kernel(be_ref, bi_ref, *refs):
    xb_refs, (wg_ref, wu_ref, wd_ref) = refs[:ROW_PARTS], refs[ROW_PARTS:ROW_PARTS + 3]
    yb_refs, (wg_s, wu_s, wd_s) = refs[ROW_PARTS + 3:2 * ROW_PARTS + 3], refs[2 * ROW_PARTS + 3:]
    i = pl.program_id(0)
    changed = jnp.logical_or(i == 0, be_ref[i] != be_ref[jnp.maximum(i - 1, 0)])

    @pl.when(changed)
    def _():
        wg_s[...] = wg_ref[0, 0].astype(BF16)
        wu_s[...] = wu_ref[0, 0].astype(BF16)
        wd_s[...] = wd_ref[0, 0].astype(BF16)

    half = wg_s.shape[0] // 2
    gate = up = None
    for c in range(ROW_PARTS):
        hi, lo = _unpack_bf16_pairs(xb_refs[c][...])
        hi, lo = hi.astype(BF16), lo.astype(BF16)
        pw = hi.shape[1]
        hs, ls = slice(c * pw, (c + 1) * pw), slice(half + c * pw, half + (c + 1) * pw)
        g = _dot(hi, wg_s[hs, :]) + _dot(lo, wg_s[ls, :])
        u = _dot(hi, wu_s[hs, :]) + _dot(lo, wu_s[ls, :])
        gate, up = (g, u) if gate is None else (gate + g, up + u)
    act = (gate * _sigmoid(gate) * up).astype(BF16)
    packed = _pack_bf16_pairs(_dot(act, wd_s[...]))
    pw = packed.shape[1] // ROW_PARTS
    for c in range(ROW_PARTS):
        yb_refs[c][...] = packed[:, c * pw:(c + 1) * pw]


def _expert_ffn(block_e, block_i, xbs, wg, wu, wd, layer):
    n_rows, pw = xbs[0].shape
    d, ff = wg.shape[2], wg.shape[3]
    n_blk = n_rows // FFN_ROWS
    row_spec = pl.BlockSpec((FFN_ROWS, pw), lambda i, be, bi: (bi[i], 0))
    return pl.pallas_call(
        _ffn_kernel,
        grid_spec=pltpu.PrefetchScalarGridSpec(
            num_scalar_prefetch=2,
            grid=(n_blk,),
            in_specs=[row_spec] * ROW_PARTS + [
                pl.BlockSpec((1, 1, d, ff), lambda i, be, bi: (layer, be[i], 0, 0)),
                pl.BlockSpec((1, 1, d, ff), lambda i, be, bi: (layer, be[i], 0, 0)),
                pl.BlockSpec((1, 1, ff, d), lambda i, be, bi: (layer, be[i], 0, 0)),
            ],
            out_specs=[row_spec] * ROW_PARTS,
            scratch_shapes=[pltpu.VMEM((d, ff), BF16), pltpu.VMEM((d, ff), BF16), pltpu.VMEM((ff, d), BF16)],
        ),
        out_shape=[jax.ShapeDtypeStruct((n_rows, pw), U32)] * ROW_PARTS,
        compiler_params=_cparams(("arbitrary",)),
        name="moe_expert_ffn",
    )(block_e, block_i, *xbs, wg, wu, wd)


def _combine_kernel(x_ref, route_ref, *refs):
    y_refs, o_ref = refs[:-1], refs[-1]
    o_ref[...] = _add_expert_rows(x_ref[...], route_ref[...],
                                  [(y_refs[2 * c][...], y_refs[2 * c + 1][...]) for c in range(ROW_PARTS)])


def _combine(x2, route, ytoks, tc):
    n, d = x2.shape
    w = ytoks[0].shape[1]
    nsteps = n // tc
    y_specs, y_args = [], []
    for ytok in ytoks:
        y_specs += [pl.BlockSpec((tc, w), lambda i: (i, 0)), pl.BlockSpec((tc, w), lambda i: (i + nsteps, 0))]
        y_args += [ytok, ytok]
    return pl.pallas_call(
        _combine_kernel,
        grid=(nsteps,),
        in_specs=[pl.BlockSpec((tc, d), lambda i: (i, 0)), pl.BlockSpec((tc, ROUTE_LANES), lambda i: (i, 0))] + y_specs,
        out_specs=pl.BlockSpec((tc, d), lambda i: (i, 0)),
        out_shape=jax.ShapeDtypeStruct((n, d), F32),
        compiler_params=_cparams(("parallel",)),
        name="moe_combine",
    )(x2, route, *y_args)


def _moe(n, hfps, route_t, counts, wg, wu, wd, layer):
    cnt = counts[:, 0, :N_EXPERTS].astype(I32)
    total = jnp.sum(cnt, axis=0)
    padded = (total + FFN_ROWS - 1) // FFN_ROWS * FFN_ROWS
    pad_end = jnp.cumsum(padded)
    pad_start = pad_end - padded
    tile_base = pad_start[None, :] + jnp.cumsum(cnt, axis=0) - cnt
    expert_ids = jnp.arange(N_EXPERTS, dtype=I32)
    dests = []
    for k in range(TOP_K):
        ek = route_t[:, k, :].astype(I32)
        base = jnp.sum(jnp.where(ek[:, :, None] == expert_ids, tile_base[:, None, :], 0), axis=-1)
        dests.append((base + route_t[:, 4 + k, :].astype(I32)).reshape(n))
    dest_by_slot = jnp.concatenate(dests)
    n_blk = (n * TOP_K) // FFN_ROWS + N_EXPERTS
    n_rows = n_blk * FFN_ROWS
    used = pad_end[-1] // FFN_ROWS
    block_i = jnp.minimum(jnp.arange(n_blk, dtype=I32), used - 1).astype(I32)
    ended = (pad_end[None, :] <= (block_i * FFN_ROWS)[:, None]).astype(I32)
    block_e = jnp.minimum(jnp.sum(ended, axis=1), N_EXPERTS - 1).astype(I32)
    seg_len = jnp.concatenate([padded - total, (n_rows - pad_end[-1])[None]])
    seg_first = jnp.concatenate([pad_start + total, pad_end[-1:]])
    seg_end = jnp.cumsum(seg_len)
    jpad = jnp.arange(n_rows - n * TOP_K, dtype=I32)
    seg = jnp.sum((seg_end[None, :] <= jpad[:, None]).astype(I32), axis=1)
    pad_rows = (seg_first[seg] + jpad - (seg_end - seg_len)[seg]).astype(I32)

    xbs = [_sc_scatter_rows(part.reshape(n, part.shape[-1]), dests, pad_rows, n_rows) for part in hfps]
    yb = _expert_ffn(block_e, block_i, xbs, wg, wu, wd, layer)
    return [_sc_gather_rows(part, dest_by_slot) for part in yb]


def _apply_moe(x2, route, ytoks):
    b, s, d = x2.shape
    n = b * s
    out = _combine(x2.reshape(n, d), route.reshape(n, ROUTE_LANES), ytoks, min(COMBINE_TOKENS, n))
    return out.reshape(b, s, d)


def _rope_lane_freq():
    inv = ROPE_THETA ** (-jnp.arange(0, ROPE_DIM // 2, dtype=F32) * 2.0 / ROPE_DIM)
    idx = np.full((HEAD_LANES,), -1, np.int64)
    for r in range(ROPE_DIM):
        idx[_head_lane(NOPE_DIM + r)] = r % ROPE_HALF
    return _gather_cols(inv[None, :], idx)


FAST_SOFTMAX_MAX_LOG2 = 60.0


def _score_bound_log2(qg, kg):
    return 1.02 * LOG2E * QK_DIM ** 0.5 * jnp.max(jnp.abs(qg)) * jnp.max(jnp.abs(kg))


def _partner_lanes(idx):
    out = np.full_like(idx, -1)
    for base in range(0, idx.shape[0], HEAD_LANES):
        for r in range(ROPE_DIM):
            lane = _head_lane(NOPE_DIM + r)
            out[base + lane] = idx[base + (lane + HALF_LANES) % HEAD_LANES]
    return out


def kernel(x, mem, positions, ln_mix, w_in, q_lat_norm, w_uq, kv_lat_norm, w_ukv, q_norm, k_norm, conv_w, conv_b,
           dt_bias, a_log, d_skip, ssd_norm, w_out, pool_w, pool_b, pool_scale, ln_xq, ln_mem, xq_w, xkv_w, xq_norm,
           xk_norm, xo_w, ln_ffn, rg_w, rg_b, re_w, re_b, exp_w_gate, exp_w_up, exp_w_down):
    b, s, d = x.shape
    depth = ln_mix.shape[0]
    tm = min(512, s)
    tq = min(512, s)
    assert d == D_MODEL and s % tm == 0 and s % CHUNK == 0 and tm >= POOL_CARRY

    pos = positions.astype(F32)[..., None]
    invf = _rope_lane_freq()
    hsum = jnp.asarray(np.kron(np.eye(X_HEADS), np.ones((X_HEAD_DIM, X_HEAD_DIM))), BF16)
    ltri = jnp.asarray(np.tril(np.ones((tm, tm)), -1), BF16)
    row2 = lambda v: v.reshape(1, -1)
    lane_pad = lambda v: jnp.pad(v, (0, LANES - v.shape[0])).reshape(1, LANES)

    pending = None
    for layer in range(depth):
        j = layer // 2
        kbd, vbd = _mem_kv(mem, row2(ln_mem[layer]), xkv_w[layer].astype(BF16),
                           row2(jnp.tile(xk_norm[layer], X_HEADS)), hsum)
        rw = jnp.pad(jnp.concatenate([rg_w[layer], re_w[layer]], axis=1),
                     ((0, 0), (0, ROUTE_LANES - MOE_GROUPS - N_EXPERTS)))
        rw_hi = rw.astype(BF16)
        rw_lo = (rw - rw_hi.astype(F32)).astype(BF16)
        rb = lane_pad(jnp.concatenate([rg_b[layer], re_b[layer]]))
        tail_args = [kbd, vbd, row2(ln_xq[layer]), xq_w[layer].astype(BF16), row2(jnp.tile(xq_norm[layer], X_HEADS)),
                     hsum, xo_w[layer].astype(BF16), row2(ln_ffn[layer]), rw_hi, rw_lo, rb, ltri]
        row = lambda i, jj: (i, jj, 0)
        fixed2 = lambda i, jj: (0, 0)
        if layer % 2 == 0:
            win = _gather_cols(w_in[j], _win_col_index()).astype(BF16)
            q_idx = _head_col_index(QK_DIM, 0, QK_DIM)
            wuq = _gather_cols(w_uq[j], q_idx).astype(BF16)
            wuq_p = _gather_cols(w_uq[j], _partner_lanes(q_idx)).astype(BF16)
            wuk = _gather_cols(w_ukv[j], _head_col_index(NOPE_DIM + V_DIM, 0, NOPE_DIM)).astype(BF16)
            v_idx = np.full((A_HEADS * HEAD_LANES,), -1, np.int64)
            for hd in range(A_HEADS):
                v_idx[hd * HEAD_LANES:hd * HEAD_LANES + V_DIM] = hd * (NOPE_DIM + V_DIM) + NOPE_DIM + np.arange(V_DIM)
            wuv = _gather_cols(w_ukv[j], v_idx).astype(BF16)
            bound = _score_bound_log2(q_norm[j], k_norm[j])
            koff = jnp.zeros((1, HEAD_LANES), F32).at[0, SCORE_PAD_LANE].set(-bound)
            gain_idx = _head_col_index(QK_DIM, 0, QK_DIM)[:HEAD_LANES]
            lane_consts = jnp.concatenate(
                [_gather_cols(g[None, :], idx) for g in (q_norm[j], k_norm[j])
                 for idx in (gain_idx, _partner_lanes(gain_idx))]
                + [koff, invf, jnp.zeros((2, HEAD_LANES), F32)], axis=0)
            q, k, v, z, xbc, misc = _front_even(
                x, pos, lane_consts, row2(ln_mix[layer]), win, row2(q_lat_norm[j]), wuq, wuq_p,
                row2(kv_lat_norm[j]), wuk, wuv, tm)
            attn = lax.cond(bound <= FAST_SOFTMAX_MAX_LOG2,
                            functools.partial(_attention, tq=tq, online=False),
                            functools.partial(_attention, tq=tq, online=True), q, k, v)
            y = _ssd(xbc, misc, z, conv_w[j], row2(conv_b[j]), lane_pad(dt_bias[j]), lane_pad(a_log[j]),
                     row2(jnp.repeat(d_skip[j], SSD_HEAD_DIM)), row2(ssd_norm[j]))
            half = A_HEADS * V_DIM
            front_args = [x, attn, y, w_out[j].astype(BF16)]
            front_specs = [pl.BlockSpec((1, tm, d), row), pl.BlockSpec((1, tm, half), row),
                           pl.BlockSpec((1, tm, D_INNER), row), pl.BlockSpec((half + D_INNER, d), fixed2)]
            x2, *hfps, route, route_t, counts = _post("even", front_args, front_specs, tail_args, b, s, tm, [])
        else:
            front_args, front_specs = [x], [pl.BlockSpec((1, tm, d), row)]
            if pending is not None:
                route_prev, ytoks = pending
                nt = s // tm
                pw = ytoks[0].shape[1]
                front_args += [route_prev]
                front_specs += [pl.BlockSpec((1, tm, ROUTE_LANES), row)]
                for ytok in ytoks:
                    front_args += [ytok, ytok]
                    front_specs += [pl.BlockSpec((tm, pw), lambda i, jj: (i * nt + jj, 0)),
                                    pl.BlockSpec((tm, pw), lambda i, jj: (i * nt + jj + b * nt, 0))]
            front_args += [row2(ln_mix[layer]), pool_w[j].astype(BF16), row2(pool_b[j]), row2(pool_scale[j])]
            front_specs += [pl.BlockSpec((1, d), fixed2),
                            pl.BlockSpec((len(POOL_WINDOWS), POOL_GROUP, POOL_GROUP), lambda i, jj: (0, 0, 0)),
                            pl.BlockSpec((1, d), fixed2), pl.BlockSpec((1, d), fixed2)]
            x2, *hfps, route, route_t, counts = _post(
                "pool" if pending is None else "pool_pending", front_args, front_specs, tail_args, b, s, tm,
                [pltpu.VMEM((POOL_CARRY, d), F32)])
        ytoks = _moe(b * s, hfps, route_t, counts, exp_w_gate, exp_w_up, exp_w_down, layer)
        x, pending = x2, (route, ytoks)
        if layer + 1 == depth or (layer + 1) % 2 == 0:
            x, pending = _apply_moe(x2, route, ytoks), None
    return x
```

```python
import functools

import numpy as np
import jax
import jax.numpy as jnp
from jax import lax
from jax.experimental import pallas as pl
from jax.experimental.pallas import tpu as pltpu
from jax.experimental.pallas import tpu_sc as plsc

F32 = jnp.float32
BF16 = jnp.bfloat16
U32 = jnp.uint32
I32 = jnp.int32

RMS_EPS = 1e-6
ROPE_THETA = 10000.0

D_MODEL = 1024
X_HEADS, X_HEAD_DIM = 4, 64
A_HEADS, NOPE_DIM, ROPE_DIM, V_DIM = 8, 64, 32, 64
QK_DIM = NOPE_DIM + ROPE_DIM
Q_LORA, KV_LORA = 256, 128
B_HEADS, SSD_HEAD_DIM, SSD_GROUPS, SSD_STATE, CONV_K, CHUNK = 8, 64, 2, 128, 4, 128
D_INNER = B_HEADS * SSD_HEAD_DIM
CONV_CH = D_INNER + 2 * SSD_GROUPS * SSD_STATE
POOL_WINDOWS = (2, 4, 8, 16)
POOL_GROUP = D_MODEL // 4
MOE_GROUPS, EXPERTS_PER_GROUP, TOP_K, EXPERT_FF = 4, 8, 2, 256
N_EXPERTS = MOE_GROUPS * EXPERTS_PER_GROUP

LANES = 128
HEAD_LANES = LANES
HALF_LANES = LANES // 2
ROPE_HALF = ROPE_DIM // 2
NOPE_HALF = NOPE_DIM // 2
POOL_CARRY = 16
CONV_CARRY = 8
VMEM_LIMIT = 56 * 1024 * 1024


def _cparams(sem):
    return pltpu.CompilerParams(dimension_semantics=sem, vmem_limit_bytes=VMEM_LIMIT)


def _rms(u, g):
    return u * lax.rsqrt(jnp.mean(u * u, axis=-1, keepdims=True) + RMS_EPS) * g


def _sigmoid(u):
    return 1.0 / (1.0 + jnp.exp(-u))


def _dot(a, b):
    return jnp.dot(a, b, preferred_element_type=F32)


def _dot_nt(a, b):
    return lax.dot_general(a, b, (((1,), (1,)), ((), ())), preferred_element_type=F32)


def _head_lane(d):
    if d < NOPE_HALF:
        return d
    if d < NOPE_DIM:
        return HALF_LANES + (d - NOPE_HALF)
    r = d - NOPE_DIM
    if r < ROPE_HALF:
        return NOPE_HALF + r
    return HALF_LANES + NOPE_HALF + (r - ROPE_HALF)


def _gather_cols(w, idx):
    w_ext = jnp.concatenate([w, jnp.zeros(w.shape[:-1] + (1,), w.dtype)], axis=-1)
    idx = np.where(idx < 0, w.shape[-1], idx)
    return jnp.take(w_ext, jnp.asarray(idx, dtype=jnp.int32), axis=-1)


IN_W = 2 * D_MODEL
_OFF_QLAT, _OFF_KVLAT, _OFF_MISC, _OFF_Z, _OFF_XBC = 0, 256, 384, 512, 1024


def _win_col_index():
    idx = np.full((IN_W,), -1, np.int64)
    idx[_OFF_QLAT:_OFF_QLAT + Q_LORA] = np.arange(Q_LORA)
    idx[_OFF_KVLAT:_OFF_KVLAT + KV_LORA] = Q_LORA + np.arange(KV_LORA)
    rope0 = Q_LORA + KV_LORA
    for r in range(ROPE_DIM):
        idx[_OFF_MISC + _head_lane(NOPE_DIM + r)] = rope0 + r
    z0 = rope0 + ROPE_DIM
    idx[_OFF_Z:_OFF_Z + D_INNER] = z0 + np.arange(D_INNER)
    xbc0 = z0 + D_INNER
    idx[_OFF_XBC:_OFF_XBC + CONV_CH] = xbc0 + np.arange(CONV_CH)
    dt0 = xbc0 + CONV_CH
    idx[_OFF_MISC:_OFF_MISC + B_HEADS] = dt0 + np.arange(B_HEADS)
    return idx


def _head_col_index(per_head, offset, count):
    idx = np.full((A_HEADS * HEAD_LANES,), -1, np.int64)
    for h in range(A_HEADS):
        for d in range(count):
            idx[h * HEAD_LANES + _head_lane(d)] = h * per_head + offset + d
    return idx


SCORE_PAD_LANE = NOPE_HALF + ROPE_HALF
ONES_LANE = V_DIM
LOG2E = 1.4426950408889634


def _front_even_kernel(x_ref, pos_ref, lc_ref, ln_ref, win_ref, qln_ref, wuq_ref, wuqr_ref, kvln_ref, wuk_ref, wuv_ref,
                       q_ref, k_ref, v_ref, z_ref, xbc_ref, misc_ref):
    x = x_ref[0]
    h = _rms(x, ln_ref[...]).astype(BF16)
    proj = _dot(h, win_ref[...])
    misc = proj[:, _OFF_MISC:_OFF_Z]
    z_ref[0] = proj[:, _OFF_Z:_OFF_XBC].astype(BF16)
    xbc_ref[0] = proj[:, _OFF_XBC:].astype(BF16)
    misc_ref[0] = misc
    ql = _rms(proj[:, _OFF_QLAT:_OFF_KVLAT], qln_ref[...]).astype(BF16)
    kvl = _rms(proj[:, _OFF_KVLAT:_OFF_MISC], kvln_ref[...]).astype(BF16)
    q = _dot(ql, wuq_ref[...])
    kn = _dot(kvl, wuk_ref[...])
    v = _dot(kvl, wuv_ref[...])
    lane = lax.broadcasted_iota(I32, (1, HEAD_LANES), 1)
    first_half = (lane >= NOPE_HALF) & (lane < NOPE_HALF + ROPE_HALF)
    second_half = (lane >= HALF_LANES + NOPE_HALF) & (lane < HALF_LANES + NOPE_HALF + ROPE_HALF)
    lc = lc_ref[...]
    qg, qg_p, kg, kg_p, k_off, invf = (lc[i:i + 1] for i in range(6))
    ones = jnp.ones((HEAD_LANES, HEAD_LANES), BF16)

    def lane_sumsq(u):
        return _dot((u * u).astype(BF16), ones)

    krope = jnp.where(first_half | second_half, misc, 0.0)
    kr_ss = lane_sumsq(krope)
    ang = pos_ref[0] * invf
    cos_t = jnp.cos(ang)
    sin_t = jnp.where(first_half, -jnp.sin(ang), jnp.sin(ang))
    q_one = (lane == SCORE_PAD_LANE).astype(F32)
    v_one = (lane == ONES_LANE).astype(F32)
    q_scale = QK_DIM ** -0.5 * LOG2E
    qa, qb = qg * cos_t * q_scale, qg_p * sin_t * q_scale
    ka, kc = kg * cos_t, pltpu.roll(krope, HALF_LANES, 1) * (kg_p * sin_t)
    q_p = _dot(ql, wuqr_ref[...])
    for hd in range(A_HEADS):
        sl = slice(hd * HEAD_LANES, (hd + 1) * HEAD_LANES)
        qs = q[:, sl]
        inv = lax.rsqrt(lane_sumsq(qs) * (1.0 / QK_DIM) + RMS_EPS)
        q_ref[0, hd] = ((qs * qa + q_p[:, sl] * qb) * inv + q_one).astype(BF16)
        kns = kn[:, sl]
        inv = lax.rsqrt((lane_sumsq(kns) + kr_ss) * (1.0 / QK_DIM) + RMS_EPS)
        k_ref[0, hd] = (((kns + krope) * ka + kc) * inv + k_off).astype(BF16)
        v_ref[0, hd] = (v[:, sl] + v_one).astype(BF16)


def _front_even(x, pos, lane_consts, ln, win, qln, wuq, wuq_p, kvln, wuk, wuv, tm):
    b, s, d = x.shape
    grid = (b, s // tm)
    row = lambda i, j: (i, j, 0)
    fixed2 = lambda i, j: (0, 0)
    head_row = lambda i, j: (i, 0, j, 0)
    hw = A_HEADS * HEAD_LANES
    return pl.pallas_call(
        _front_even_kernel,
        grid=grid,
        in_specs=[
            pl.BlockSpec((1, tm, d), row),
            pl.BlockSpec((1, tm, 1), row),
            pl.BlockSpec((8, HEAD_LANES), fixed2),
            pl.BlockSpec((1, d), fixed2),
            pl.BlockSpec((d, IN_W), fixed2),
            pl.BlockSpec((1, Q_LORA), fixed2),
            pl.BlockSpec((Q_LORA, hw), fixed2),
            pl.BlockSpec((Q_LORA, hw), fixed2),
            pl.BlockSpec((1, KV_LORA), fixed2),
            pl.BlockSpec((KV_LORA, hw), fixed2),
            pl.BlockSpec((KV_LORA, hw), fixed2),
        ],
        out_specs=[
            pl.BlockSpec((1, A_HEADS, tm, HEAD_LANES), head_row),
            pl.BlockSpec((1, A_HEADS, tm, HEAD_LANES), head_row),
            pl.BlockSpec((1, A_HEADS, tm, HEAD_LANES), head_row),
            pl.BlockSpec((1, tm, D_INNER), row),
            pl.BlockSpec((1, tm, CONV_CH), row),
            pl.BlockSpec((1, tm, HEAD_LANES), row),
        ],
        out_shape=[
            jax.ShapeDtypeStruct((b, A_HEADS, s, HEAD_LANES), BF16),
            jax.ShapeDtypeStruct((b, A_HEADS, s, HEAD_LANES), BF16),
            jax.ShapeDtypeStruct((b, A_HEADS, s, HEAD_LANES), BF16),
            jax.ShapeDtypeStruct((b, s, D_INNER), BF16),
            jax.ShapeDtypeStruct((b, s, CONV_CH), BF16),
            jax.ShapeDtypeStruct((b, s, HEAD_LANES), F32),
        ],
        compiler_params=_cparams(("parallel", "parallel")),
        name="front_even",
    )(x, pos, lane_consts, ln, win, qln, wuq, wuq_p, kvln, wuk, wuv)


HEADS_PER_STEP = 8


def _attn_kernel(q_ref, k_ref, v_ref, o_ref, *, tq, online):
    qi = pl.program_id(2)
    row = lax.broadcasted_iota(I32, (tq, tq), 0)
    col = lax.broadcasted_iota(I32, (tq, tq), 1)

    def head_step(hh, j, carry, masked):
        kj = k_ref[0, hh, pl.ds(j * tq, tq), :]
        vj = v_ref[0, hh, pl.ds(j * tq, tq), :]
        s = _dot_nt(q_ref[0, hh], kj)
        if masked:
            s = jnp.where(row >= col, s, -jnp.inf)
        if online:
            m, acc = carry
            m_new = jnp.maximum(m, jnp.max(s, axis=-1, keepdims=True))
            p = jnp.exp2(s - m_new).astype(BF16)
            return m_new, jnp.exp2(m - m_new) * acc + _dot(p, vj)
        return carry + _dot(jnp.exp2(s).astype(BF16), vj)

    def step(j, carries, masked):
        return tuple(head_step(hh, j, carries[hh], masked) for hh in range(HEADS_PER_STEP))

    acc0 = jnp.zeros((tq, HEAD_LANES), F32)
    init = (jnp.full((tq, 1), -jnp.inf, F32), acc0) if online else acc0
    carries = lax.fori_loop(0, qi, functools.partial(step, masked=False), (init,) * HEADS_PER_STEP)
    carries = step(qi, carries, True)
    outs = []
    for carry in carries:
        acc = carry[1] if online else carry
        outs.append(acc / acc[:, ONES_LANE:ONES_LANE + 1])
    lane = lax.broadcasted_iota(I32, (1, HEAD_LANES), 1)
    per_group = HEAD_LANES // V_DIM
    groups = []
    for g0 in range(0, HEADS_PER_STEP, per_group):
        out = outs[g0]
        for hh in range(1, per_group):
            out = jnp.where(lane >= hh * V_DIM, pltpu.roll(outs[g0 + hh], hh * V_DIM, 1), out)
        groups.append(out.astype(BF16))
    o_ref[0] = jnp.concatenate(groups, axis=1)


def _attention(q, k, v, tq, online):
    b, nh, s, _ = q.shape
    grid = (b, nh // HEADS_PER_STEP, s // tq)
    kv_spec = pl.BlockSpec((1, HEADS_PER_STEP, s, HEAD_LANES), lambda i, h, j: (i, h, 0, 0))
    return pl.pallas_call(
        functools.partial(_attn_kernel, tq=tq, online=online),
        grid=grid,
        in_specs=[
            pl.BlockSpec((1, HEADS_PER_STEP, tq, HEAD_LANES), lambda i, h, j: (i, h, j, 0)),
            kv_spec,
            kv_spec,
        ],
        out_specs=pl.BlockSpec((1, tq, HEADS_PER_STEP * V_DIM), lambda i, h, j: (i, j, h)),
        out_shape=jax.ShapeDtypeStruct((b, s, nh * V_DIM), BF16),
        compiler_params=_cparams(("parallel", "parallel", "parallel")),
        name="mla_attention_online" if online else "mla_attention",
    )(q, k, v)


def _ssd_kernel(xbc_ref, misc_ref, z_ref, cw_ref, cb_ref, dtb_ref, alog_ref, dskip_ref, gn_ref, y_ref,
                state_ref, carry_ref):
    c = pl.program_id(1)
    t = CHUNK

    @pl.when(c == 0)
    def _():
        state_ref[...] = jnp.zeros_like(state_ref)
        carry_ref[...] = jnp.zeros_like(carry_ref)

    xr = xbc_ref[0].astype(F32)
    xcat = jnp.concatenate([carry_ref[...], xr], axis=0)
    carry_ref[...] = xr[t - CONV_CARRY:, :]
    conv = jnp.zeros((t, CONV_CH), F32) + cb_ref[...]
    for kk in range(CONV_K):
        sh = CONV_K - 1 - kk
        shifted = xcat if sh == 0 else pltpu.roll(xcat, sh, 0)
        conv = conv + cw_ref[kk:kk + 1, :] * shifted[CONV_CARRY:, :]
    xa = conv * _sigmoid(conv)
    xs = xa[:, :D_INNER]
    gw = SSD_GROUPS * SSD_STATE
    bmat = xa[:, D_INNER:D_INNER + gw]
    cmat = xa[:, D_INNER + gw:]

    u = misc_ref[0] + dtb_ref[...]
    dt = jnp.maximum(u, 0.0) + jnp.log(1.0 + jnp.exp(-jnp.abs(u)))
    a = -jnp.exp(alog_ref[...])
    lane = lax.broadcasted_iota(I32, (1, LANES), 1)
    adt = jnp.where(lane < B_HEADS, dt * a, 0.0)
    rowi = lax.broadcasted_iota(I32, (t, LANES), 0)
    acs = adt
    sh = 1
    while sh < t:
        acs = acs + jnp.where(rowi >= sh, pltpu.roll(acs, sh, 0), 0.0)
        sh *= 2
    acs_t = acs.T
    tri = lax.broadcasted_iota(I32, (t, t), 0) >= lax.broadcasted_iota(I32, (t, t), 1)

    rep = B_HEADS // SSD_GROUPS
    ys = []
    for g in range(SSD_GROUPS):
        bg = bmat[:, g * SSD_STATE:(g + 1) * SSD_STATE]
        cg = cmat[:, g * SSD_STATE:(g + 1) * SSD_STATE]
        bg16, cg16 = bg.astype(BF16), cg.astype(BF16)
        cb = _dot_nt(cg16, bg16)
        bg_t = bg.T
        for r in range(rep):
            hd = g * rep + r
            col = acs[:, hd:hd + 1]
            rw = acs_t[hd:hd + 1, :]
            last = acs_t[hd:hd + 1, t - 1:t]
            decay = jnp.exp(jnp.where(tri, col - rw, -jnp.inf))
            xh = xs[:, hd * SSD_HEAD_DIM:(hd + 1) * SSD_HEAD_DIM]
            xdt = (xh * dt[:, hd:hd + 1]).astype(BF16)
            y_diag = _dot((cb * decay).astype(BF16), xdt)
            prev = state_ref[hd]
            y_off = _dot(cg16, prev.astype(BF16)) * jnp.exp(col)
            new_state = _dot((bg_t * jnp.exp(last - rw)).astype(BF16), xdt)
            state_ref[hd] = prev * jnp.exp(last) + new_state
            ys.append(y_diag + y_off)
    y = jnp.concatenate(ys, axis=1) + xs * dskip_ref[...]
    zf = z_ref[0].astype(F32)
    y = y * (zf * _sigmoid(zf))
    y_ref[0] = _rms(y, gn_ref[...]).astype(BF16)


def _ssd(xbc, misc, z, cw, cb, dtb, alog, dskip, gn):
    b, s, _ = xbc.shape
    grid = (b, s // CHUNK)
    row = lambda i, j: (i, j, 0)
    fixed2 = lambda i, j: (0, 0)
    return pl.pallas_call(
        _ssd_kernel,
        grid=grid,
        in_specs=[
            pl.BlockSpec((1, CHUNK, CONV_CH), row),
            pl.BlockSpec((1, CHUNK, LANES), row),
            pl.BlockSpec((1, CHUNK, D_INNER), row),
            pl.BlockSpec((CONV_K, CONV_CH), fixed2),
            pl.BlockSpec((1, CONV_CH), fixed2),
            pl.BlockSpec((1, LANES), fixed2),
            pl.BlockSpec((1, LANES), fixed2),
            pl.BlockSpec((1, D_INNER), fixed2),
            pl.BlockSpec((1, D_INNER), fixed2),
        ],
        out_specs=pl.BlockSpec((1, CHUNK, D_INNER), row),
        out_shape=jax.ShapeDtypeStruct((b, s, D_INNER), BF16),
        scratch_shapes=[
            pltpu.VMEM((B_HEADS, SSD_STATE, SSD_HEAD_DIM), F32),
            pltpu.VMEM((CONV_CARRY, CONV_CH), F32),
        ],
        compiler_params=_cparams(("parallel", "arbitrary")),
        name="ssd_scan",
    )(xbc, misc, z, cw, cb, dtb, alog, dskip, gn)


XW = X_HEADS * X_HEAD_DIM


def _mem_kv_kernel(mem_ref, ln_ref, wkv_ref, kg_ref, hsum_ref, kbd_ref, vbd_ref):
    m = mem_ref.shape[1]
    mn = _rms(mem_ref[0], ln_ref[...]).astype(BF16)
    kv = _dot(mn, wkv_ref[...])
    k, v = kv[:, :XW], kv[:, XW:]
    ss = _dot((k * k).astype(BF16), hsum_ref[...])
    kn = (k * lax.rsqrt(ss * (1.0 / X_HEAD_DIM) + RMS_EPS) * kg_ref[...]).astype(BF16)
    v16 = v.astype(BF16)
    head_of_lane = lax.shift_right_arithmetic(lax.broadcasted_iota(I32, (1, XW), 1), jnp.int32(_LOG2_XHD))
    for hd in range(X_HEADS):
        keep = head_of_lane == hd
        kbd_ref[0, hd * m:(hd + 1) * m, :] = jnp.where(keep, kn, jnp.zeros_like(kn))
        vbd_ref[0, hd * m:(hd + 1) * m, :] = jnp.where(keep, v16, jnp.zeros_like(v16))


def _mem_kv(mem, ln, wkv, kg, hsum):
    b, m, d = mem.shape
    fixed2 = lambda i: (0, 0)
    return pl.pallas_call(
        _mem_kv_kernel,
        grid=(b,),
        in_specs=[
            pl.BlockSpec((1, m, d), lambda i: (i, 0, 0)),
            pl.BlockSpec((1, d), fixed2),
            pl.BlockSpec((d, 2 * XW), fixed2),
            pl.BlockSpec((1, XW), fixed2),
            pl.BlockSpec((XW, XW), fixed2),
        ],
        out_specs=[
            pl.BlockSpec((1, X_HEADS * m, XW), lambda i: (i, 0, 0)),
            pl.BlockSpec((1, X_HEADS * m, XW), lambda i: (i, 0, 0)),
        ],
        out_shape=[
            jax.ShapeDtypeStruct((b, X_HEADS * m, XW), BF16),
            jax.ShapeDtypeStruct((b, X_HEADS * m, XW), BF16),
        ],
        compiler_params=_cparams(("parallel",)),
        name="mem_kv",
    )(mem, ln, wkv, kg, hsum)


ROUTE_LANES = LANES
ROW_PARTS = 2
_GROUP_LANE0 = 0
_EXPERT_LANE0 = MOE_GROUPS
_LOG2_EPG = EXPERTS_PER_GROUP.bit_length() - 1
_LOG2_XHD = X_HEAD_DIM.bit_length() - 1


def _pack_bf16_pairs(v):
    w = v.shape[1] // 2
    r = v.astype(BF16).astype(F32)
    hi = lax.bitcast_convert_type(r[:, :w], U32)
    lo = lax.bitcast_convert_type(r[:, w:], U32)
    return (hi & jnp.uint32(0xFFFF0000)) | (lo >> jnp.uint32(16))


def _unpack_bf16_pairs(u):
    hi = lax.bitcast_convert_type(u & jnp.uint32(0xFFFF0000), F32)
    lo = lax.bitcast_convert_type(u << jnp.uint32(16), F32)
    return hi, lo


TAIL_SPLIT = 1


def _tail_rows(x1, rows, kbd_ref, vbd_ref, lnq_ref, wq_ref, qg_ref, hsum_ref, wo_ref, lnf_ref, rwh_ref, rwl_ref,
               rb_ref, x2_ref, hfp_refs):
    tm = x1.shape[0]
    m = kbd_ref.shape[1] // X_HEADS
    hq = _rms(x1, lnq_ref[...]).astype(BF16)
    q = _dot(hq, wq_ref[...])
    ss = _dot((q * q).astype(BF16), hsum_ref[...])
    qn = (q * lax.rsqrt(ss * (1.0 / X_HEAD_DIM) + RMS_EPS) * qg_ref[...] * (X_HEAD_DIM ** -0.5)).astype(BF16)
    s = _dot_nt(qn, kbd_ref[0])
    ps = []
    for hd in range(X_HEADS):
        sh = s[:, hd * m:(hd + 1) * m]
        e = jnp.exp(sh - jnp.max(sh, axis=-1, keepdims=True))
        ps.append((e / jnp.sum(e, axis=-1, keepdims=True)).astype(BF16))
    o = _dot(jnp.concatenate(ps, axis=1), vbd_ref[0]).astype(BF16)
    x2 = x1 + _dot(o, wo_ref[...])
    x2_ref[0, rows, :] = x2

    hf = _rms(x2, lnf_ref[...])
    hf_hi = hf.astype(BF16)
    packed = _pack_bf16_pairs(hf)
    pw = packed.shape[1] // ROW_PARTS
    for c in range(ROW_PARTS):
        hfp_refs[c][0, rows, :] = packed[:, c * pw:(c + 1) * pw]
    hf_lo = (hf - hf_hi.astype(F32)).astype(BF16)
    logits = _dot(hf_hi, rwh_ref[...]) + _dot(hf_hi, rwl_ref[...]) + _dot(hf_lo, rwh_ref[...]) + rb_ref[...]

    lane_i = lax.broadcasted_iota(I32, (tm, ROUTE_LANES), 1)
    lane = lane_i.astype(F32)
    big = float(ROUTE_LANES)
    neg = -jnp.inf
    gl = jnp.where(lane_i < MOE_GROUPS, logits, neg)
    gmax = jnp.max(gl, axis=-1, keepdims=True)
    gsum = jnp.sum(jnp.exp(gl - gmax), axis=-1, keepdims=True)
    g_p = 1.0 / gsum
    g_idx = jnp.min(jnp.where(gl == gmax, lane, big), axis=-1, keepdims=True)
    e_lane = lane_i - _EXPERT_LANE0
    grp_of_lane = lax.shift_right_arithmetic(e_lane, jnp.int32(_LOG2_EPG)).astype(F32)
    in_grp = (e_lane >= 0) & (e_lane < N_EXPERTS) & (grp_of_lane == g_idx)
    el = jnp.where(in_grp, logits, neg)
    emax = jnp.max(el, axis=-1, keepdims=True)
    idx1 = jnp.min(jnp.where(el == emax, lane, big), axis=-1, keepdims=True)
    el2 = jnp.where(lane == idx1, neg, el)
    emax2 = jnp.max(el2, axis=-1, keepdims=True)
    idx2 = jnp.min(jnp.where(el2 == emax2, lane, big), axis=-1, keepdims=True)
    r2 = jnp.exp(emax2 - emax)
    gate1 = g_p / (1.0 + r2)
    gate2 = g_p * r2 / (1.0 + r2)
    e1 = idx1 - float(_EXPERT_LANE0)
    e2 = idx2 - float(_EXPERT_LANE0)

    route = jnp.where(lane == 0, e1, 0.0)
    route = jnp.where(lane == 1, e2, route)
    route = jnp.where(lane == 2, gate1, route)
    route = jnp.where(lane == 3, gate2, route)
    return route, (lane == e1).astype(F32), (lane == e2).astype(F32)


def _tail(x1, kbd_ref, vbd_ref, lnq_ref, wq_ref, qg_ref, hsum_ref, wo_ref, lnf_ref, rwh_ref, rwl_ref, rb_ref,
          ltri_ref, x2_ref, *out_refs):
    hfp_refs, (route_ref, route_t_ref, cnt_ref) = out_refs[:ROW_PARTS], out_refs[ROW_PARTS:]
    tm = x1.shape[0]
    tr = tm // TAIL_SPLIT
    parts = [_tail_rows(x1[r * tr:(r + 1) * tr], slice(r * tr, (r + 1) * tr), kbd_ref, vbd_ref, lnq_ref, wq_ref,
                        qg_ref, hsum_ref, wo_ref, lnf_ref, rwh_ref, rwl_ref, rb_ref, x2_ref, hfp_refs)
             for r in range(TAIL_SPLIT)]
    route, oh1, oh2 = (jnp.concatenate([p[i] for p in parts], axis=0) for i in range(3))
    both = oh1 + oh2
    before = _dot(ltri_ref[...], both.astype(BF16))
    rank1 = jnp.sum(before * oh1, axis=-1, keepdims=True)
    rank2 = jnp.sum(before * oh2, axis=-1, keepdims=True)
    cnt_ref[0] = jnp.broadcast_to(jnp.sum(both, axis=0, keepdims=True), cnt_ref.shape[1:])
    lane = lax.broadcasted_iota(I32, (tm, ROUTE_LANES), 1)
    route = jnp.where(lane == 4, rank1, route)
    route = jnp.where(lane == 5, rank2, route)
    route_ref[0] = route
    route_t_ref[0] = route.T[:route_t_ref.shape[1], :]


_TAIL_IN = 12


def _post_even_kernel(x_ref, a_ref, y_ref, wout_ref, *rest):
    tail_in, outs = rest[:_TAIL_IN], rest[_TAIL_IN:]
    half = wout_ref.shape[0] // 2
    x1 = x_ref[0] + _dot(a_ref[0], wout_ref[:half, :]) + _dot(y_ref[0], wout_ref[half:, :])
    _tail(x1, *tail_in, *outs)


def _add_expert_rows(x, route, ys):
    g1, g2 = route[:, 2:3], route[:, 3:4]
    his, los = [], []
    for y1, y2 in ys:
        h1, l1 = _unpack_bf16_pairs(y1)
        h2, l2 = _unpack_bf16_pairs(y2)
        his.append(h1 * g1 + h2 * g2)
        los.append(l1 * g1 + l2 * g2)
    return x + jnp.concatenate(his + los, axis=1)


N_PENDING = 1 + 2 * ROW_PARTS


def _post_pool_kernel(x_ref, *rest, pending):
    if pending:
        route_prev_ref, y_refs, rest = rest[0], rest[1:N_PENDING], rest[N_PENDING:]
    (ln_ref, pw_ref, pb_ref, ps_ref), rest = rest[:4], rest[4:]
    tail_in, outs, carry_ref = rest[:_TAIL_IN], rest[_TAIL_IN:-1], rest[-1]
    j = pl.program_id(1)
    tm = x_ref.shape[1]

    @pl.when(j == 0)
    def _():
        carry_ref[...] = jnp.zeros_like(carry_ref)

    x = x_ref[0]
    if pending:
        x = _add_expert_rows(x, route_prev_ref[0], [(y_refs[2 * c][...], y_refs[2 * c + 1][...])
                                                    for c in range(ROW_PARTS)])
    h = _rms(x, ln_ref[...])
    pos = (j * tm + 1 + lax.broadcasted_iota(I32, (tm, 1), 0)).astype(F32)
    mixed = []
    for g, w in enumerate(POOL_WINDOWS):
        sl = slice(g * POOL_GROUP, (g + 1) * POOL_GROUP)
        hg = h[:, sl]
        acc = jnp.concatenate([carry_ref[:, sl], hg], axis=0)
        sh = 1
        while sh < w:
            acc = acc + pltpu.roll(acc, sh, 0)
            sh *= 2
        win = acc[POOL_CARRY:, :]
        dlt = win / jnp.minimum(pos, float(w)) - hg
        mixed.append(_dot(dlt.astype(BF16), pw_ref[g]))
    carry_ref[...] = h[tm - POOL_CARRY:, :]
    y = (jnp.concatenate(mixed, axis=1) + pb_ref[...]) * ps_ref[...]
    _tail(x + y, *tail_in, *outs)


def _post(kind, front_args, front_specs, tail_args, b, s, tm, scratch):
    d = D_MODEL
    m4 = tail_args[0].shape[1]
    row = lambda i, j: (i, j, 0)
    fixed2 = lambda i, j: (0, 0)
    per_b = lambda i, j: (i, 0, 0)
    tail_specs = [
        pl.BlockSpec((1, m4, XW), per_b),
        pl.BlockSpec((1, m4, XW), per_b),
        pl.BlockSpec((1, d), fixed2),
        pl.BlockSpec((d, XW), fixed2),
        pl.BlockSpec((1, XW), fixed2),
        pl.BlockSpec((XW, XW), fixed2),
        pl.BlockSpec((XW, d), fixed2),
        pl.BlockSpec((1, d), fixed2),
        pl.BlockSpec((d, ROUTE_LANES), fixed2),
        pl.BlockSpec((d, ROUTE_LANES), fixed2),
        pl.BlockSpec((1, ROUTE_LANES), fixed2),
        pl.BlockSpec((tm, tm), fixed2),
    ]
    nt = s // tm
    pw = d // 2 // ROW_PARTS
    kernel = {"even": _post_even_kernel,
              "pool": functools.partial(_post_pool_kernel, pending=False),
              "pool_pending": functools.partial(_post_pool_kernel, pending=True)}[kind]
    return pl.pallas_call(
        kernel,
        grid=(b, nt),
        in_specs=front_specs + tail_specs,
        out_specs=[pl.BlockSpec((1, tm, d), row)]
        + [pl.BlockSpec((1, tm, pw), row)] * ROW_PARTS
        + [pl.BlockSpec((1, tm, ROUTE_LANES), row),
           pl.BlockSpec((1, 8, tm), lambda i, j: (i * nt + j, 0, 0)),
           pl.BlockSpec((1, 8, ROUTE_LANES), lambda i, j: (i * nt + j, 0, 0))],
        out_shape=[jax.ShapeDtypeStruct((b, s, d), F32)]
        + [jax.ShapeDtypeStruct((b, s, pw), U32)] * ROW_PARTS
        + [jax.ShapeDtypeStruct((b, s, ROUTE_LANES), F32),
           jax.ShapeDtypeStruct((b * nt, 8, tm), F32),
           jax.ShapeDtypeStruct((b * nt, 8, ROUTE_LANES), F32)],
        scratch_shapes=scratch,
        compiler_params=_cparams(("parallel", "arbitrary")),
        name="post_" + kind,
    )(*front_args, *tail_args)


FFN_ROWS = 1024
COMBINE_TOKENS = 512
SC_GATHER_WINDOW = 128


def _sc_gather_rows(table, idx):
    m, w = idx.shape[0], table.shape[1]
    mesh = plsc.VectorSubcoreMesh(core_axis_name="core", subcore_axis_name="subcore")

    @pl.kernel(out_type=jax.ShapeDtypeStruct((m, w), table.dtype), mesh=mesh, name="moe_row_gather")
    def gather(t_hbm, i_hbm, o_hbm):
        def body(i_vmem, o_vmem):
            pltpu.sync_copy(t_hbm.at[i_vmem.at[0]], o_vmem)

        pltpu.emit_pipeline(
            body,
            grid=(m // SC_GATHER_WINDOW,),
            in_specs=[pl.BlockSpec((1, SC_GATHER_WINDOW), lambda i: (0, i))],
            out_specs=[pl.BlockSpec((SC_GATHER_WINDOW, w), lambda i: (i, 0))],
            core_axis_name=("core", "subcore"),
            dimension_semantics=(pltpu.PARALLEL,),
        )(i_hbm, o_hbm)

    return gather(table, idx.reshape(1, m))


def _sc_scatter_rows(src, dests, pad_rows, n_rows):
    n, w = src.shape
    win = SC_GATHER_WINDOW
    mesh = plsc.VectorSubcoreMesh(core_axis_name="core", subcore_axis_name="subcore")
    idx_spec = pl.BlockSpec((1, win), lambda i: (0, i))
    split = dict(core_axis_name=("core", "subcore"), dimension_semantics=(pltpu.PARALLEL,))

    @pl.kernel(out_type=jax.ShapeDtypeStruct((n_rows, w), src.dtype), mesh=mesh, name="moe_row_scatter")
    def scatter(s_hbm, z_hbm, p_hbm, *rest):
        d_hbms, o_hbm = rest[:-1], rest[-1]

        def body(s_vmem, *i_vmems):
            for i_vmem in i_vmems:
                pltpu.sync_copy(s_vmem, o_hbm.at[i_vmem.at[0]])

        pltpu.emit_pipeline(
            body, grid=(n // win,),
            in_specs=[pl.BlockSpec((win, w), lambda i: (i, 0))] + [idx_spec] * len(dests),
            out_specs=[], **split)(s_hbm, *d_hbms)

        def zero_body(z_vmem, i_vmem):
            pltpu.sync_copy(z_vmem, o_hbm.at[i_vmem.at[0]])

        pltpu.emit_pipeline(
            zero_body, grid=(pad_rows.shape[0] // win,),
            in_specs=[pl.BlockSpec((win, w), lambda i: (0, 0)), idx_spec],
            out_specs=[], **split)(z_hbm, p_hbm)

    zeros = jnp.zeros((win, w), src.dtype)
    return scatter(src, zeros, pad_rows.reshape(1, -1), *[dd.reshape(1, n) for dd in dests])


def _ffn_kernel(be_ref, bi_ref, *refs):
    xb_refs, (wg_ref, wu_ref, wd_ref) = refs[:ROW_PARTS], refs[ROW_PARTS:ROW_PARTS + 3]
    yb_refs, (wg_s, wu_s, wd_s) = refs[ROW_PARTS + 3:2 * ROW_PARTS + 3], refs[2 * ROW_PARTS + 3:]
    i = pl.program_id(0)
    changed = jnp.logical_or(i == 0, be_ref[i] != be_ref[jnp.maximum(i - 1, 0)])

    @pl.when(changed)
    def _():
        wg_s[...] = wg_ref[0, 0].astype(BF16)
        wu_s[...] = wu_ref[0, 0].astype(BF16)
        wd_s[...] = wd_ref[0, 0].astype(BF16)

    @pl.when(bi_ref[i] == i)
    def _():
        half = wg_s.shape[0] // 2
        gate = up = None
        for c in range(ROW_PARTS):
            hi, lo = _unpack_bf16_pairs(xb_refs[c][...])
            hi, lo = hi.astype(BF16), lo.astype(BF16)
            pw = hi.shape[1]
            hs, ls = slice(c * pw, (c + 1) * pw), slice(half + c * pw, half + (c + 1) * pw)
            g = _dot(hi, wg_s[hs, :]) + _dot(lo, wg_s[ls, :])
            u = _dot(hi, wu_s[hs, :]) + _dot(lo, wu_s[ls, :])
            gate, up = (g, u) if gate is None else (gate + g, up + u)
        act = (gate * _sigmoid(gate) * up).astype(BF16)
        packed = _pack_bf16_pairs(_dot(act, wd_s[...]))
        pw = packed.shape[1] // ROW_PARTS
        for c in range(ROW_PARTS):
            yb_refs[c][...] = packed[:, c * pw:(c + 1) * pw]


def _expert_ffn(block_e, block_i, xbs, wg, wu, wd, layer):
    n_rows, pw = xbs[0].shape
    d, ff = wg.shape[2], wg.shape[3]
    n_blk = n_rows // FFN_ROWS
    row_spec = pl.BlockSpec((FFN_ROWS, pw), lambda i, be, bi: (bi[i], 0))
    return pl.pallas_call(
        _ffn_kernel,
        grid_spec=pltpu.PrefetchScalarGridSpec(
            num_scalar_prefetch=2,
            grid=(n_blk,),
            in_specs=[row_spec] * ROW_PARTS + [
                pl.BlockSpec((1, 1, d, ff), lambda i, be, bi: (layer, be[i], 0, 0)),
                pl.BlockSpec((1, 1, d, ff), lambda i, be, bi: (layer, be[i], 0, 0)),
                pl.BlockSpec((1, 1, ff, d), lambda i, be, bi: (layer, be[i], 0, 0)),
            ],
            out_specs=[row_spec] * ROW_PARTS,
            scratch_shapes=[pltpu.VMEM((d, ff), BF16), pltpu.VMEM((d, ff), BF16), pltpu.VMEM((ff, d), BF16)],
        ),
        out_shape=[jax.ShapeDtypeStruct((n_rows, pw), U32)] * ROW_PARTS,
        compiler_params=_cparams(("arbitrary",)),
        name="moe_expert_ffn",
    )(block_e, block_i, *xbs, wg, wu, wd)


def _combine_kernel(x_ref, route_ref, *refs):
    y_refs, o_ref = refs[:-1], refs[-1]
    o_ref[...] = _add_expert_rows(x_ref[...], route_ref[...],
                                  [(y_refs[2 * c][...], y_refs[2 * c + 1][...]) for c in range(ROW_PARTS)])


def _combine(x2, route, ytoks, tc):
    n, d = x2.shape
    w = ytoks[0].shape[1]
    nsteps = n // tc
    y_specs, y_args = [], []
    for ytok in ytoks:
        y_specs += [pl.BlockSpec((tc, w), lambda i: (i, 0)), pl.BlockSpec((tc, w), lambda i: (i + nsteps, 0))]
        y_args += [ytok, ytok]
    return pl.pallas_call(
        _combine_kernel,
        grid=(nsteps,),
        in_specs=[pl.BlockSpec((tc, d), lambda i: (i, 0)), pl.BlockSpec((tc, ROUTE_LANES), lambda i: (i, 0))] + y_specs,
        out_specs=pl.BlockSpec((tc, d), lambda i: (i, 0)),
        out_shape=jax.ShapeDtypeStruct((n, d), F32),
        compiler_params=_cparams(("parallel",)),
        name="moe_combine",
    )(x2, route, *y_args)


def _moe(n, hfps, route_t, counts, wg, wu, wd, layer):
    cnt = counts[:, 0, :N_EXPERTS].astype(I32)
    total = jnp.sum(cnt, axis=0)
    padded = (total + FFN_ROWS - 1) // FFN_ROWS * FFN_ROWS
    pad_end = jnp.cumsum(padded)
    pad_start = pad_end - padded
    tile_base = pad_start[None, :] + jnp.cumsum(cnt, axis=0) - cnt
    expert_ids = jnp.arange(N_EXPERTS, dtype=I32)
    dests = []
    for k in range(TOP_K):
        ek = route_t[:, k, :].astype(I32)
        base = jnp.sum(jnp.where(ek[:, :, None] == expert_ids, tile_base[:, None, :], 0), axis=-1)
        dests.append((base + route_t[:, 4 + k, :].astype(I32)).reshape(n))
    dest_by_slot = jnp.concatenate(dests)
    n_blk = (n * TOP_K) // FFN_ROWS + N_EXPERTS
    n_rows = n_blk * FFN_ROWS
    used = pad_end[-1] // FFN_ROWS
    block_i = jnp.minimum(jnp.arange(n_blk, dtype=I32), used - 1).astype(I32)
    ended = (pad_end[None, :] <= (block_i * FFN_ROWS)[:, None]).astype(I32)
    block_e = jnp.minimum(jnp.sum(ended, axis=1), N_EXPERTS - 1).astype(I32)
    seg_len = jnp.concatenate([padded - total, (n_rows - pad_end[-1])[None]])
    seg_first = jnp.concatenate([pad_start + total, pad_end[-1:]])
    seg_end = jnp.cumsum(seg_len)
    jpad = jnp.arange(n_rows - n * TOP_K, dtype=I32)
    seg = jnp.sum((seg_end[None, :] <= jpad[:, None]).astype(I32), axis=1)
    pad_rows = (seg_first[seg] + jpad - (seg_end - seg_len)[seg]).astype(I32)

    xbs = [_sc_scatter_rows(part.reshape(n, part.shape[-1]), dests, pad_rows, n_rows) for part in hfps]
    yb = _expert_ffn(block_e, block_i, xbs, wg, wu, wd, layer)
    return [_sc_gather_rows(part, dest_by_slot) for part in yb]


def _apply_moe(x2, route, ytoks):
    b, s, d = x2.shape
    n = b * s
    out = _combine(x2.reshape(n, d), route.reshape(n, ROUTE_LANES), ytoks, min(COMBINE_TOKENS, n))
    return out.reshape(b, s, d)


def _rope_lane_freq():
    inv = ROPE_THETA ** (-jnp.arange(0, ROPE_DIM // 2, dtype=F32) * 2.0 / ROPE_DIM)
    idx = np.full((HEAD_LANES,), -1, np.int64)
    for r in range(ROPE_DIM):
        idx[_head_lane(NOPE_DIM + r)] = r % ROPE_HALF
    return _gather_cols(inv[None, :], idx)


FAST_SOFTMAX_MAX_LOG2 = 60.0


def _score_bound_log2(qg, kg):
    return 1.02 * LOG2E * QK_DIM ** 0.5 * jnp.max(jnp.abs(qg)) * jnp.max(jnp.abs(kg))


def _partner_lanes(idx):
    out = np.full_like(idx, -1)
    for base in range(0, idx.shape[0], HEAD_LANES):
        for r in range(ROPE_DIM):
            lane = _head_lane(NOPE_DIM + r)
            out[base + lane] = idx[base + (lane + HALF_LANES) % HEAD_LANES]
    return out


def kernel(x, mem, positions, ln_mix, w_in, q_lat_norm, w_uq, kv_lat_norm, w_ukv, q_norm, k_norm, conv_w, conv_b,
           dt_bias, a_log, d_skip, ssd_norm, w_out, pool_w, pool_b, pool_scale, ln_xq, ln_mem, xq_w, xkv_w, xq_norm,
           xk_norm, xo_w, ln_ffn, rg_w, rg_b, re_w, re_b, exp_w_gate, exp_w_up, exp_w_down):
    b, s, d = x.shape
    depth = ln_mix.shape[0]
    tm = min(512, s)
    tq = min(512, s)
    assert d == D_MODEL and s % tm == 0 and s % CHUNK == 0 and tm >= POOL_CARRY

    pos = positions.astype(F32)[..., None]
    invf = _rope_lane_freq()
    hsum = jnp.asarray(np.kron(np.eye(X_HEADS), np.ones((X_HEAD_DIM, X_HEAD_DIM))), BF16)
    ltri = jnp.asarray(np.tril(np.ones((tm, tm)), -1), BF16)
    row2 = lambda v: v.reshape(1, -1)
    lane_pad = lambda v: jnp.pad(v, (0, LANES - v.shape[0])).reshape(1, LANES)

    pending = None
    for layer in range(depth):
        j = layer // 2
        kbd, vbd = _mem_kv(mem, row2(ln_mem[layer]), xkv_w[layer].astype(BF16),
                           row2(jnp.tile(xk_norm[layer], X_HEADS)), hsum)
        rw = jnp.pad(jnp.concatenate([rg_w[layer], re_w[layer]], axis=1),
                     ((0, 0), (0, ROUTE_LANES - MOE_GROUPS - N_EXPERTS)))
        rw_hi = rw.astype(BF16)
        rw_lo = (rw - rw_hi.astype(F32)).astype(BF16)
        rb = lane_pad(jnp.concatenate([rg_b[layer], re_b[layer]]))
        tail_args = [kbd, vbd, row2(ln_xq[layer]), xq_w[layer].astype(BF16), row2(jnp.tile(xq_norm[layer], X_HEADS)),
                     hsum, xo_w[layer].astype(BF16), row2(ln_ffn[layer]), rw_hi, rw_lo, rb, ltri]
        row = lambda i, jj: (i, jj, 0)
        fixed2 = lambda i, jj: (0, 0)
        if layer % 2 == 0:
            win = _gather_cols(w_in[j], _win_col_index()).astype(BF16)
            q_idx = _head_col_index(QK_DIM, 0, QK_DIM)
            wuq = _gather_cols(w_uq[j], q_idx).astype(BF16)
            wuq_p = _gather_cols(w_uq[j], _partner_lanes(q_idx)).astype(BF16)
            wuk = _gather_cols(w_ukv[j], _head_col_index(NOPE_DIM + V_DIM, 0, NOPE_DIM)).astype(BF16)
            v_idx = np.full((A_HEADS * HEAD_LANES,), -1, np.int64)
            for hd in range(A_HEADS):
                v_idx[hd * HEAD_LANES:hd * HEAD_LANES + V_DIM] = hd * (NOPE_DIM + V_DIM) + NOPE_DIM + np.arange(V_DIM)
            wuv = _gather_cols(w_ukv[j], v_idx).astype(BF16)
            bound = _score_bound_log2(q_norm[j], k_norm[j])
            koff = jnp.zeros((1, HEAD_LANES), F32).at[0, SCORE_PAD_LANE].set(-bound)
            gain_idx = _head_col_index(QK_DIM, 0, QK_DIM)[:HEAD_LANES]
            lane_consts = jnp.concatenate(
                [_gather_cols(g[None, :], idx) for g in (q_norm[j], k_norm[j])
                 for idx in (gain_idx, _partner_lanes(gain_idx))]
                + [koff, invf, jnp.zeros((2, HEAD_LANES), F32)], axis=0)
            q, k, v, z, xbc, misc = _front_even(
                x, pos, lane_consts, row2(ln_mix[layer]), win, row2(q_lat_norm[j]), wuq, wuq_p,
                row2(kv_lat_norm[j]), wuk, wuv, tm)
            attn = lax.cond(bound <= FAST_SOFTMAX_MAX_LOG2,
                            functools.partial(_attention, tq=tq, online=False),
                            functools.partial(_attention, tq=tq, online=True), q, k, v)
            y = _ssd(xbc, misc, z, conv_w[j], row2(conv_b[j]), lane_pad(dt_bias[j]), lane_pad(a_log[j]),
                     row2(jnp.repeat(d_skip[j], SSD_HEAD_DIM)), row2(ssd_norm[j]))
            half = A_HEADS * V_DIM
            front_args = [x, attn, y, w_out[j].astype(BF16)]
            front_specs = [pl.BlockSpec((1, tm, d), row), pl.BlockSpec((1, tm, half), row),
                           pl.BlockSpec((1, tm, D_INNER), row), pl.BlockSpec((half + D_INNER, d), fixed2)]
            x2, *hfps, route, route_t, counts = _post("even", front_args, front_specs, tail_args, b, s, tm, [])
        else:
            front_args, front_specs = [x], [pl.BlockSpec((1, tm, d), row)]
            if pending is not None:
                route_prev, ytoks = pending
                nt = s // tm
                pw = ytoks[0].shape[1]
                front_args += [route_prev]
                front_specs += [pl.BlockSpec((1, tm, ROUTE_LANES), row)]
                for ytok in ytoks:
                    front_args += [ytok, ytok]
                    front_specs += [pl.BlockSpec((tm, pw), lambda i, jj: (i * nt + jj, 0)),
                                    pl.BlockSpec((tm, pw), lambda i, jj: (i * nt + jj + b * nt, 0))]
            front_args += [row2(ln_mix[layer]), pool_w[j].astype(BF16), row2(pool_b[j]), row2(pool_scale[j])]
            front_specs += [pl.BlockSpec((1, d), fixed2),
                            pl.BlockSpec((len(POOL_WINDOWS), POOL_GROUP, POOL_GROUP), lambda i, jj: (0, 0, 0)),
                            pl.BlockSpec((1, d), fixed2), pl.BlockSpec((1, d), fixed2)]
            x2, *hfps, route, route_t, counts = _post(
                "pool" if pending is None else "pool_pending", front_args, front_specs, tail_args, b, s, tm,
                [pltpu.VMEM((POOL_CARRY, d), F32)])
        ytoks = _moe(b * s, hfps, route_t, counts, exp_w_gate, exp_w_up, exp_w_down, layer)
        x, pending = x2, (route, ytoks)
        if layer + 1 == depth or (layer + 1) % 2 == 0:
            x, pending = _apply_moe(x2, route, ytoks), None
    return x
```

```python
import functools

import numpy as np
import jax
import jax.numpy as jnp
from jax import lax
from jax.experimental import pallas as pl
from jax.experimental.pallas import tpu as pltpu
from jax.experimental.pallas import tpu_sc as plsc

F32 = jnp.float32
BF16 = jnp.bfloat16
U32 = jnp.uint32
I32 = jnp.int32

RMS_EPS = 1e-6
ROPE_THETA = 10000.0

D_MODEL = 1024
X_HEADS, X_HEAD_DIM = 4, 64
A_HEADS, NOPE_DIM, ROPE_DIM, V_DIM = 8, 64, 32, 64
QK_DIM = NOPE_DIM + ROPE_DIM
Q_LORA, KV_LORA = 256, 128
B_HEADS, SSD_HEAD_DIM, SSD_GROUPS, SSD_STATE, CONV_K, CHUNK = 8, 64, 2, 128, 4, 128
D_INNER = B_HEADS * SSD_HEAD_DIM
CONV_CH = D_INNER + 2 * SSD_GROUPS * SSD_STATE
POOL_WINDOWS = (2, 4, 8, 16)
POOL_GROUP = D_MODEL // 4
MOE_GROUPS, EXPERTS_PER_GROUP, TOP_K, EXPERT_FF = 4, 8, 2, 256
N_EXPERTS = MOE_GROUPS * EXPERTS_PER_GROUP

LANES = 128
HEAD_LANES = LANES
HALF_LANES = LANES // 2
ROPE_HALF = ROPE_DIM // 2
NOPE_HALF = NOPE_DIM // 2
POOL_CARRY = 16
CONV_CARRY = 8
VMEM_LIMIT = 56 * 1024 * 1024


def _cparams(sem):
    return pltpu.CompilerParams(dimension_semantics=sem, vmem_limit_bytes=VMEM_LIMIT)


def _rms(u, g):
    return u * lax.rsqrt(jnp.mean(u * u, axis=-1, keepdims=True) + RMS_EPS) * g


def _sigmoid(u):
    return 1.0 / (1.0 + jnp.exp(-u))


def _dot(a, b):
    return jnp.dot(a, b, preferred_element_type=F32)


def _dot_nt(a, b):
    return lax.dot_general(a, b, (((1,), (1,)), ((), ())), preferred_element_type=F32)


def _head_lane(d):
    if d < NOPE_HALF:
        return d
    if d < NOPE_DIM:
        return HALF_LANES + (d - NOPE_HALF)
    r = d - NOPE_DIM
    if r < ROPE_HALF:
        return NOPE_HALF + r
    return HALF_LANES + NOPE_HALF + (r - ROPE_HALF)


def _gather_cols(w, idx):
    w_ext = jnp.concatenate([w, jnp.zeros(w.shape[:-1] + (1,), w.dtype)], axis=-1)
    idx = np.where(idx < 0, w.shape[-1], idx)
    return jnp.take(w_ext, jnp.asarray(idx, dtype=jnp.int32), axis=-1)


IN_W = 2 * D_MODEL
_OFF_QLAT, _OFF_KVLAT, _OFF_MISC, _OFF_Z, _OFF_XBC = 0, 256, 384, 512, 1024


def _win_col_index():
    idx = np.full((IN_W,), -1, np.int64)
    idx[_OFF_QLAT:_OFF_QLAT + Q_LORA] = np.arange(Q_LORA)
    idx[_OFF_KVLAT:_OFF_KVLAT + KV_LORA] = Q_LORA + np.arange(KV_LORA)
    rope0 = Q_LORA + KV_LORA
    for r in range(ROPE_DIM):
        idx[_OFF_MISC + _head_lane(NOPE_DIM + r)] = rope0 + r
    z0 = rope0 + ROPE_DIM
    idx[_OFF_Z:_OFF_Z + D_INNER] = z0 + np.arange(D_INNER)
    xbc0 = z0 + D_INNER
    idx[_OFF_XBC:_OFF_XBC + CONV_CH] = xbc0 + np.arange(CONV_CH)
    dt0 = xbc0 + CONV_CH
    idx[_OFF_MISC:_OFF_MISC + B_HEADS] = dt0 + np.arange(B_HEADS)
    return idx


def _head_col_index(per_head, offset, count):
    idx = np.full((A_HEADS * HEAD_LANES,), -1, np.int64)
    for h in range(A_HEADS):
        for d in range(count):
            idx[h * HEAD_LANES + _head_lane(d)] = h * per_head + offset + d
    return idx


SCORE_PAD_LANE = NOPE_HALF + ROPE_HALF
ONES_LANE = V_DIM
LOG2E = 1.4426950408889634


def _front_even_kernel(x_ref, pos_ref, lc_ref, ln_ref, win_ref, qln_ref, wuq_ref, wuqr_ref, kvln_ref, wuk_ref, wuv_ref,
                       q_ref, k_ref, v_ref, z_ref, xbc_ref, misc_ref):
    x = x_ref[0]
    h = _rms(x, ln_ref[...]).astype(BF16)
    proj = _dot(h, win_ref[...])
    misc = proj[:, _OFF_MISC:_OFF_Z]
    z_ref[0] = proj[:, _OFF_Z:_OFF_XBC].astype(BF16)
    xbc_ref[0] = proj[:, _OFF_XBC:].astype(BF16)
    misc_ref[0] = misc
    ql = _rms(proj[:, _OFF_QLAT:_OFF_KVLAT], qln_ref[...]).astype(BF16)
    kvl = _rms(proj[:, _OFF_KVLAT:_OFF_MISC], kvln_ref[...]).astype(BF16)
    q = _dot(ql, wuq_ref[...])
    kn = _dot(kvl, wuk_ref[...])
    v = _dot(kvl, wuv_ref[...])
    lane = lax.broadcasted_iota(I32, (1, HEAD_LANES), 1)
    first_half = (lane >= NOPE_HALF) & (lane < NOPE_HALF + ROPE_HALF)
    second_half = (lane >= HALF_LANES + NOPE_HALF) & (lane < HALF_LANES + NOPE_HALF + ROPE_HALF)
    lc = lc_ref[...]
    qg, qg_p, kg, kg_p, k_off, invf = (lc[i:i + 1] for i in range(6))
    ones = jnp.ones((HEAD_LANES, HEAD_LANES), BF16)

    def lane_sumsq(u):
        return _dot((u * u).astype(BF16), ones)

    krope = jnp.where(first_half | second_half, misc, 0.0)
    kr_ss = lane_sumsq(krope)
    ang = pos_ref[0] * invf
    cos_t = jnp.cos(ang)
    sin_t = jnp.where(first_half, -jnp.sin(ang), jnp.sin(ang))
    q_one = (lane == SCORE_PAD_LANE).astype(F32)
    v_one = (lane == ONES_LANE).astype(F32)
    q_scale = QK_DIM ** -0.5 * LOG2E
    qa, qb = qg * cos_t * q_scale, qg_p * sin_t * q_scale
    ka, kc = kg * cos_t, pltpu.roll(krope, HALF_LANES, 1) * (kg_p * sin_t)
    q_p = _dot(ql, wuqr_ref[...])
    for hd in range(A_HEADS):
        sl = slice(hd * HEAD_LANES, (hd + 1) * HEAD_LANES)
        qs = q[:, sl]
        inv = lax.rsqrt(lane_sumsq(qs) * (1.0 / QK_DIM) + RMS_EPS)
        q_ref[0, hd] = ((qs * qa + q_p[:, sl] * qb) * inv + q_one).astype(BF16)
        kns = kn[:, sl]
        inv = lax.rsqrt((lane_sumsq(kns) + kr_ss) * (1.0 / QK_DIM) + RMS_EPS)
        k_ref[0, hd] = (((kns + krope) * ka + kc) * inv + k_off).astype(BF16)
        v_ref[0, hd] = (v[:, sl] + v_one).astype(BF16)


def _front_even(x, pos, lane_consts, ln, win, qln, wuq, wuq_p, kvln, wuk, wuv, tm):
    b, s, d = x.shape
    grid = (b, s // tm)
    row = lambda i, j: (i, j, 0)
    fixed2 = lambda i, j: (0, 0)
    head_row = lambda i, j: (i, 0, j, 0)
    hw = A_HEADS * HEAD_LANES
    return pl.pallas_call(
        _front_even_kernel,
        grid=grid,
        in_specs=[
            pl.BlockSpec((1, tm, d), row),
            pl.BlockSpec((1, tm, 1), row),
            pl.BlockSpec((8, HEAD_LANES), fixed2),
            pl.BlockSpec((1, d), fixed2),
            pl.BlockSpec((d, IN_W), fixed2),
            pl.BlockSpec((1, Q_LORA), fixed2),
            pl.BlockSpec((Q_LORA, hw), fixed2),
            pl.BlockSpec((Q_LORA, hw), fixed2),
            pl.BlockSpec((1, KV_LORA), fixed2),
            pl.BlockSpec((KV_LORA, hw), fixed2),
            pl.BlockSpec((KV_LORA, hw), fixed2),
        ],
        out_specs=[
            pl.BlockSpec((1, A_HEADS, tm, HEAD_LANES), head_row),
            pl.BlockSpec((1, A_HEADS, tm, HEAD_LANES), head_row),
            pl.BlockSpec((1, A_HEADS, tm, HEAD_LANES), head_row),
            pl.BlockSpec((1, tm, D_INNER), row),
            pl.BlockSpec((1, tm, CONV_CH), row),
            pl.BlockSpec((1, tm, HEAD_LANES), row),
        ],
        out_shape=[
            jax.ShapeDtypeStruct((b, A_HEADS, s, HEAD_LANES), BF16),
            jax.ShapeDtypeStruct((b, A_HEADS, s, HEAD_LANES), BF16),
            jax.ShapeDtypeStruct((b, A_HEADS, s, HEAD_LANES), BF16),
            jax.ShapeDtypeStruct((b, s, D_INNER), BF16),
            jax.ShapeDtypeStruct((b, s, CONV_CH), BF16),
            jax.ShapeDtypeStruct((b, s, HEAD_LANES), F32),
        ],
        compiler_params=_cparams(("parallel", "parallel")),
        name="front_even",
    )(x, pos, lane_consts, ln, win, qln, wuq, wuq_p, kvln, wuk, wuv)


HEADS_PER_STEP = 8


def _attn_kernel(q_ref, k_ref, v_ref, o_ref, *, tq, online):
    qi = pl.program_id(2)
    row = lax.broadcasted_iota(I32, (tq, tq), 0)
    col = lax.broadcasted_iota(I32, (tq, tq), 1)

    def head_step(hh, j, carry, masked):
        kj = k_ref[0, hh, pl.ds(j * tq, tq), :]
        vj = v_ref[0, hh, pl.ds(j * tq, tq), :]
        s = _dot_nt(q_ref[0, hh], kj)
        if masked:
            s = jnp.where(row >= col, s, -jnp.inf)
        if online:
            m, acc = carry
            m_new = jnp.maximum(m, jnp.max(s, axis=-1, keepdims=True))
            p = jnp.exp2(s - m_new).astype(BF16)
            return m_new, jnp.exp2(m - m_new) * acc + _dot(p, vj)
        return carry + _dot(jnp.exp2(s).astype(BF16), vj)

    def step(j, carries, masked):
        return tuple(head_step(hh, j, carries[hh], masked) for hh in range(HEADS_PER_STEP))

    acc0 = jnp.zeros((tq, HEAD_LANES), F32)
    init = (jnp.full((tq, 1), -jnp.inf, F32), acc0) if online else acc0
    carries = lax.fori_loop(0, qi, functools.partial(step, masked=False), (init,) * HEADS_PER_STEP)
    carries = step(qi, carries, True)
    outs = []
    for carry in carries:
        acc = carry[1] if online else carry
        outs.append(acc / acc[:, ONES_LANE:ONES_LANE + 1])
    lane = lax.broadcasted_iota(I32, (1, HEAD_LANES), 1)
    per_group = HEAD_LANES // V_DIM
    groups = []
    for g0 in range(0, HEADS_PER_STEP, per_group):
        out = outs[g0]
        for hh in range(1, per_group):
            out = jnp.where(lane >= hh * V_DIM, pltpu.roll(outs[g0 + hh], hh * V_DIM, 1), out)
        groups.append(out.astype(BF16))
    o_ref[0] = jnp.concatenate(groups, axis=1)


def _attention(q, k, v, tq, online):
    b, nh, s, _ = q.shape
    grid = (b, nh // HEADS_PER_STEP, s // tq)
    kv_spec = pl.BlockSpec((1, HEADS_PER_STEP, s, HEAD_LANES), lambda i, h, j: (i, h, 0, 0))
    return pl.pallas_call(
        functools.partial(_attn_kernel, tq=tq, online=online),
        grid=grid,
        in_specs=[
            pl.BlockSpec((1, HEADS_PER_STEP, tq, HEAD_LANES), lambda i, h, j: (i, h, j, 0)),
            kv_spec,
            kv_spec,
        ],
        out_specs=pl.BlockSpec((1, tq, HEADS_PER_STEP * V_DIM), lambda i, h, j: (i, j, h)),
        out_shape=jax.ShapeDtypeStruct((b, s, nh * V_DIM), BF16),
        compiler_params=_cparams(("parallel", "parallel", "parallel")),
        name="mla_attention_online" if online else "mla_attention",
    )(q, k, v)


def _ssd_kernel(xbc_ref, misc_ref, z_ref, cw_ref, cb_ref, dtb_ref, alog_ref, dskip_ref, gn_ref, y_ref,
                state_ref, carry_ref):
    c = pl.program_id(1)
    t = CHUNK
    rows = y_ref.shape[1]

    @pl.when(c == 0)
    def _():
        state_ref[...] = jnp.zeros_like(state_ref)
        carry_ref[...] = jnp.zeros_like(carry_ref)

    xr = xbc_ref[0].astype(F32)
    xcat = jnp.concatenate([carry_ref[...], xr], axis=0)
    carry_ref[...] = xr[rows - CONV_CARRY:, :]
    conv = jnp.zeros((rows, CONV_CH), F32) + cb_ref[...]
    for kk in range(CONV_K):
        sh = CONV_K - 1 - kk
        shifted = xcat if sh == 0 else pltpu.roll(xcat, sh, 0)
        conv = conv + cw_ref[kk:kk + 1, :] * shifted[CONV_CARRY:, :]
    xa = conv * _sigmoid(conv)
    xs = xa[:, :D_INNER]
    gw = SSD_GROUPS * SSD_STATE
    bmat = xa[:, D_INNER:D_INNER + gw]
    cmat = xa[:, D_INNER + gw:]

    u = misc_ref[0] + dtb_ref[...]
    dt = jnp.maximum(u, 0.0) + jnp.log(1.0 + jnp.exp(-jnp.abs(u)))
    a = -jnp.exp(alog_ref[...])
    lane = lax.broadcasted_iota(I32, (1, LANES), 1)
    adt_all = jnp.where(lane < B_HEADS, dt * a, 0.0)
    rowi = lax.broadcasted_iota(I32, (t, LANES), 0)
    tri = lax.broadcasted_iota(I32, (t, t), 0) >= lax.broadcasted_iota(I32, (t, t), 1)
    rep = B_HEADS // SSD_GROUPS
    y_chunks = []
    for ci in range(rows // t):
        sl = slice(ci * t, (ci + 1) * t)
        acs = adt_all[sl]
        sh = 1
        while sh < t:
            acs = acs + jnp.where(rowi >= sh, pltpu.roll(acs, sh, 0), 0.0)
            sh *= 2
        acs_t = acs.T
        ys = []
        for g in range(SSD_GROUPS):
            bg = bmat[sl, g * SSD_STATE:(g + 1) * SSD_STATE]
            cg = cmat[sl, g * SSD_STATE:(g + 1) * SSD_STATE]
            bg16, cg16 = bg.astype(BF16), cg.astype(BF16)
            cb = _dot_nt(cg16, bg16)
            bg_t = bg.T
            for r in range(rep):
                hd = g * rep + r
                col = acs[:, hd:hd + 1]
                rw = acs_t[hd:hd + 1, :]
                last = acs_t[hd:hd + 1, t - 1:t]
                decay = jnp.exp(jnp.where(tri, col - rw, -jnp.inf))
                xh = xs[sl, hd * SSD_HEAD_DIM:(hd + 1) * SSD_HEAD_DIM]
                xdt = (xh * dt[sl, hd:hd + 1]).astype(BF16)
                y_diag = _dot((cb * decay).astype(BF16), xdt)
                prev = state_ref[hd]
                y_off = _dot(cg16, prev.astype(BF16)) * jnp.exp(col)
                new_state = _dot((bg_t * jnp.exp(last - rw)).astype(BF16), xdt)
                state_ref[hd] = prev * jnp.exp(last) + new_state
                ys.append(y_diag + y_off)
        y_chunks.append(jnp.concatenate(ys, axis=1))
    y = jnp.concatenate(y_chunks, axis=0) + xs * dskip_ref[...]
    zf = z_ref[0].astype(F32)
    y = y * (zf * _sigmoid(zf))
    y_ref[0] = _rms(y, gn_ref[...]).astype(BF16)


SSD_CHUNKS_PER_STEP = 4


def _ssd(xbc, misc, z, cw, cb, dtb, alog, dskip, gn):
    b, s, _ = xbc.shape
    rows = CHUNK * SSD_CHUNKS_PER_STEP if s % (CHUNK * SSD_CHUNKS_PER_STEP) == 0 else CHUNK
    grid = (b, s // rows)
    row = lambda i, j: (i, j, 0)
    fixed2 = lambda i, j: (0, 0)
    return pl.pallas_call(
        _ssd_kernel,
        grid=grid,
        in_specs=[
            pl.BlockSpec((1, rows, CONV_CH), row),
            pl.BlockSpec((1, rows, LANES), row),
            pl.BlockSpec((1, rows, D_INNER), row),
            pl.BlockSpec((CONV_K, CONV_CH), fixed2),
            pl.BlockSpec((1, CONV_CH), fixed2),
            pl.BlockSpec((1, LANES), fixed2),
            pl.BlockSpec((1, LANES), fixed2),
            pl.BlockSpec((1, D_INNER), fixed2),
            pl.BlockSpec((1, D_INNER), fixed2),
        ],
        out_specs=pl.BlockSpec((1, rows, D_INNER), row),
        out_shape=jax.ShapeDtypeStruct((b, s, D_INNER), BF16),
        scratch_shapes=[
            pltpu.VMEM((B_HEADS, SSD_STATE, SSD_HEAD_DIM), F32),
            pltpu.VMEM((CONV_CARRY, CONV_CH), F32),
        ],
        compiler_params=_cparams(("parallel", "arbitrary")),
        name="ssd_scan",
    )(xbc, misc, z, cw, cb, dtb, alog, dskip, gn)


XW = X_HEADS * X_HEAD_DIM


def _mem_kv_kernel(mem_ref, ln_ref, wkv_ref, kg_ref, hsum_ref, kbd_ref, vbd_ref):
    m = mem_ref.shape[1]
    mn = _rms(mem_ref[0], ln_ref[...]).astype(BF16)
    kv = _dot(mn, wkv_ref[...])
    k, v = kv[:, :XW], kv[:, XW:]
    ss = _dot((k * k).astype(BF16), hsum_ref[...])
    kn = (k * lax.rsqrt(ss * (1.0 / X_HEAD_DIM) + RMS_EPS) * kg_ref[...]).astype(BF16)
    v16 = v.astype(BF16)
    head_of_lane = lax.shift_right_arithmetic(lax.broadcasted_iota(I32, (1, XW), 1), jnp.int32(_LOG2_XHD))
    for hd in range(X_HEADS):
        keep = head_of_lane == hd
        kbd_ref[0, hd * m:(hd + 1) * m, :] = jnp.where(keep, kn, jnp.zeros_like(kn))
        vbd_ref[0, hd * m:(hd + 1) * m, :] = jnp.where(keep, v16, jnp.zeros_like(v16))


def _mem_kv(mem, ln, wkv, kg, hsum):
    b, m, d = mem.shape
    fixed2 = lambda i: (0, 0)
    return pl.pallas_call(
        _mem_kv_kernel,
        grid=(b,),
        in_specs=[
            pl.BlockSpec((1, m, d), lambda i: (i, 0, 0)),
            pl.BlockSpec((1, d), fixed2),
            pl.BlockSpec((d, 2 * XW), fixed2),
            pl.BlockSpec((1, XW), fixed2),
            pl.BlockSpec((XW, XW), fixed2),
        ],
        out_specs=[
            pl.BlockSpec((1, X_HEADS * m, XW), lambda i: (i, 0, 0)),
            pl.BlockSpec((1, X_HEADS * m, XW), lambda i: (i, 0, 0)),
        ],
        out_shape=[
            jax.ShapeDtypeStruct((b, X_HEADS * m, XW), BF16),
            jax.ShapeDtypeStruct((b, X_HEADS * m, XW), BF16),
        ],
        compiler_params=_cparams(("parallel",)),
        name="mem_kv",
    )(mem, ln, wkv, kg, hsum)


ROUTE_LANES = LANES
ROW_PARTS = 2
_GROUP_LANE0 = 0
_EXPERT_LANE0 = MOE_GROUPS
_LOG2_EPG = EXPERTS_PER_GROUP.bit_length() - 1
_LOG2_XHD = X_HEAD_DIM.bit_length() - 1


def _pack_bf16_pairs(v):
    w = v.shape[1] // 2
    r = v.astype(BF16).astype(F32)
    hi = lax.bitcast_convert_type(r[:, :w], U32)
    lo = lax.bitcast_convert_type(r[:, w:], U32)
    return (hi & jnp.uint32(0xFFFF0000)) | (lo >> jnp.uint32(16))


def _unpack_bf16_pairs(u):
    hi = lax.bitcast_convert_type(u & jnp.uint32(0xFFFF0000), F32)
    lo = lax.bitcast_convert_type(u << jnp.uint32(16), F32)
    return hi, lo


TAIL_SPLIT = 1


def _tail_rows(x1, rows, kbd_ref, vbd_ref, lnq_ref, wq_ref, qg_ref, hsum_ref, wo_ref, lnf_ref, rwh_ref, rwl_ref,
               rb_ref, x2_ref, hfp_refs):
    tm = x1.shape[0]
    m = kbd_ref.shape[1] // X_HEADS
    hq = _rms(x1, lnq_ref[...]).astype(BF16)
    q = _dot(hq, wq_ref[...])
    ss = _dot((q * q).astype(BF16), hsum_ref[...])
    qn = (q * lax.rsqrt(ss * (1.0 / X_HEAD_DIM) + RMS_EPS) * qg_ref[...] * (X_HEAD_DIM ** -0.5)).astype(BF16)
    s = _dot_nt(qn, kbd_ref[0])
    ps = []
    for hd in range(X_HEADS):
        sh = s[:, hd * m:(hd + 1) * m]
        e = jnp.exp(sh - jnp.max(sh, axis=-1, keepdims=True))
        ps.append((e / jnp.sum(e, axis=-1, keepdims=True)).astype(BF16))
    o = _dot(jnp.concatenate(ps, axis=1), vbd_ref[0]).astype(BF16)
    x2 = x1 + _dot(o, wo_ref[...])
    x2_ref[0, rows, :] = x2

    hf = _rms(x2, lnf_ref[...])
    hf_hi = hf.astype(BF16)
    packed = _pack_bf16_pairs(hf)
    pw = packed.shape[1] // ROW_PARTS
    for c in range(ROW_PARTS):
        hfp_refs[c][0, rows, :] = packed[:, c * pw:(c + 1) * pw]
    hf_lo = (hf - hf_hi.astype(F32)).astype(BF16)
    logits = _dot(hf_hi, rwh_ref[...]) + _dot(hf_hi, rwl_ref[...]) + _dot(hf_lo, rwh_ref[...]) + rb_ref[...]

    lane_i = lax.broadcasted_iota(I32, (tm, ROUTE_LANES), 1)
    lane = lane_i.astype(F32)
    big = float(ROUTE_LANES)
    neg = -jnp.inf
    gl = jnp.where(lane_i < MOE_GROUPS, logits, neg)
    gmax = jnp.max(gl, axis=-1, keepdims=True)
    gsum = jnp.sum(jnp.exp(gl - gmax), axis=-1, keepdims=True)
    g_p = 1.0 / gsum
    g_idx = jnp.min(jnp.where(gl == gmax, lane, big), axis=-1, keepdims=True)
    e_lane = lane_i - _EXPERT_LANE0
    grp_of_lane = lax.shift_right_arithmetic(e_lane, jnp.int32(_LOG2_EPG)).astype(F32)
    in_grp = (e_lane >= 0) & (e_lane < N_EXPERTS) & (grp_of_lane == g_idx)
    el = jnp.where(in_grp, logits, neg)
    emax = jnp.max(el, axis=-1, keepdims=True)
    idx1 = jnp.min(jnp.where(el == emax, lane, big), axis=-1, keepdims=True)
    el2 = jnp.where(lane == idx1, neg, el)
    emax2 = jnp.max(el2, axis=-1, keepdims=True)
    idx2 = jnp.min(jnp.where(el2 == emax2, lane, big), axis=-1, keepdims=True)
    r2 = jnp.exp(emax2 - emax)
    gate1 = g_p / (1.0 + r2)
    gate2 = g_p * r2 / (1.0 + r2)
    e1 = idx1 - float(_EXPERT_LANE0)
    e2 = idx2 - float(_EXPERT_LANE0)

    route = jnp.where(lane == 0, e1, 0.0)
    route = jnp.where(lane == 1, e2, route)
    route = jnp.where(lane == 2, gate1, route)
    route = jnp.where(lane == 3, gate2, route)
    return route, (lane == e1).astype(F32), (lane == e2).astype(F32)


def _tail(x1, kbd_ref, vbd_ref, lnq_ref, wq_ref, qg_ref, hsum_ref, wo_ref, lnf_ref, rwh_ref, rwl_ref, rb_ref,
          ltri_ref, x2_ref, *out_refs):
    hfp_refs, (route_ref, route_t_ref, cnt_ref) = out_refs[:ROW_PARTS], out_refs[ROW_PARTS:]
    tm = x1.shape[0]
    tr = tm // TAIL_SPLIT
    parts = [_tail_rows(x1[r * tr:(r + 1) * tr], slice(r * tr, (r + 1) * tr), kbd_ref, vbd_ref, lnq_ref, wq_ref,
                        qg_ref, hsum_ref, wo_ref, lnf_ref, rwh_ref, rwl_ref, rb_ref, x2_ref, hfp_refs)
             for r in range(TAIL_SPLIT)]
    route, oh1, oh2 = (jnp.concatenate([p[i] for p in parts], axis=0) for i in range(3))
    both = oh1 + oh2
    before = _dot(ltri_ref[...], both.astype(BF16))
    rank1 = jnp.sum(before * oh1, axis=-1, keepdims=True)
    rank2 = jnp.sum(before * oh2, axis=-1, keepdims=True)
    cnt_ref[0] = jnp.broadcast_to(jnp.sum(both, axis=0, keepdims=True), cnt_ref.shape[1:])
    lane = lax.broadcasted_iota(I32, (tm, ROUTE_LANES), 1)
    route = jnp.where(lane == 4, rank1, route)
    route = jnp.where(lane == 5, rank2, route)
    route_ref[0] = route
    route_t_ref[0] = route.T[:route_t_ref.shape[1], :]


_TAIL_IN = 12


def _post_even_kernel(x_ref, a_ref, y_ref, wout_ref, *rest):
    tail_in, outs = rest[:_TAIL_IN], rest[_TAIL_IN:]
    half = wout_ref.shape[0] // 2
    x1 = x_ref[0] + _dot(a_ref[0], wout_ref[:half, :]) + _dot(y_ref[0], wout_ref[half:, :])
    _tail(x1, *tail_in, *outs)


def _add_expert_rows(x, route, ys):
    g1, g2 = route[:, 2:3], route[:, 3:4]
    his, los = [], []
    for y1, y2 in ys:
        h1, l1 = _unpack_bf16_pairs(y1)
        h2, l2 = _unpack_bf16_pairs(y2)
        his.append(h1 * g1 + h2 * g2)
        los.append(l1 * g1 + l2 * g2)
    return x + jnp.concatenate(his + los, axis=1)


N_PENDING = 1 + 2 * ROW_PARTS


def _post_pool_kernel(x_ref, *rest, pending):
    if pending:
        route_prev_ref, y_refs, rest = rest[0], rest[1:N_PENDING], rest[N_PENDING:]
    (ln_ref, pw_ref, pb_ref, ps_ref), rest = rest[:4], rest[4:]
    tail_in, outs, carry_ref = rest[:_TAIL_IN], rest[_TAIL_IN:-1], rest[-1]
    j = pl.program_id(1)
    tm = x_ref.shape[1]

    @pl.when(j == 0)
    def _():
        carry_ref[...] = jnp.zeros_like(carry_ref)

    x = x_ref[0]
    if pending:
        x = _add_expert_rows(x, route_prev_ref[0], [(y_refs[2 * c][...], y_refs[2 * c + 1][...])
                                                    for c in range(ROW_PARTS)])
    h = _rms(x, ln_ref[...])
    pos = (j * tm + 1 + lax.broadcasted_iota(I32, (tm, 1), 0)).astype(F32)
    mixed = []
    for g, w in enumerate(POOL_WINDOWS):
        sl = slice(g * POOL_GROUP, (g + 1) * POOL_GROUP)
        hg = h[:, sl]
        acc = jnp.concatenate([carry_ref[:, sl], hg], axis=0)
        sh = 1
        while sh < w:
            acc = acc + pltpu.roll(acc, sh, 0)
            sh *= 2
        win = acc[POOL_CARRY:, :]
        dlt = win / jnp.minimum(pos, float(w)) - hg
        mixed.append(_dot(dlt.astype(BF16), pw_ref[g]))
    carry_ref[...] = h[tm - POOL_CARRY:, :]
    y = (jnp.concatenate(mixed, axis=1) + pb_ref[...]) * ps_ref[...]
    _tail(x + y, *tail_in, *outs)


def _post(kind, front_args, front_specs, tail_args, b, s, tm, scratch):
    d = D_MODEL
    m4 = tail_args[0].shape[1]
    row = lambda i, j: (i, j, 0)
    fixed2 = lambda i, j: (0, 0)
    per_b = lambda i, j: (i, 0, 0)
    tail_specs = [
        pl.BlockSpec((1, m4, XW), per_b),
        pl.BlockSpec((1, m4, XW), per_b),
        pl.BlockSpec((1, d), fixed2),
        pl.BlockSpec((d, XW), fixed2),
        pl.BlockSpec((1, XW), fixed2),
        pl.BlockSpec((XW, XW), fixed2),
        pl.BlockSpec((XW, d), fixed2),
        pl.BlockSpec((1, d), fixed2),
        pl.BlockSpec((d, ROUTE_LANES), fixed2),
        pl.BlockSpec((d, ROUTE_LANES), fixed2),
        pl.BlockSpec((1, ROUTE_LANES), fixed2),
        pl.BlockSpec((tm, tm), fixed2),
    ]
    nt = s // tm
    pw = d // 2 // ROW_PARTS
    kernel = {"even": _post_even_kernel,
              "pool": functools.partial(_post_pool_kernel, pending=False),
              "pool_pending": functools.partial(_post_pool_kernel, pending=True)}[kind]
    return pl.pallas_call(
        kernel,
        grid=(b, nt),
        in_specs=front_specs + tail_specs,
        out_specs=[pl.BlockSpec((1, tm, d), row)]
        + [pl.BlockSpec((1, tm, pw), row)] * ROW_PARTS
        + [pl.BlockSpec((1, tm, ROUTE_LANES), row),
           pl.BlockSpec((1, 8, tm), lambda i, j: (i * nt + j, 0, 0)),
           pl.BlockSpec((1, 8, ROUTE_LANES), lambda i, j: (i * nt + j, 0, 0))],
        out_shape=[jax.ShapeDtypeStruct((b, s, d), F32)]
        + [jax.ShapeDtypeStruct((b, s, pw), U32)] * ROW_PARTS
        + [jax.ShapeDtypeStruct((b, s, ROUTE_LANES), F32),
           jax.ShapeDtypeStruct((b * nt, 8, tm), F32),
           jax.ShapeDtypeStruct((b * nt, 8, ROUTE_LANES), F32)],
        scratch_shapes=scratch,
        compiler_params=_cparams(("parallel", "arbitrary")),
        name="post_" + kind,
    )(*front_args, *tail_args)


FFN_ROWS = 1024
COMBINE_TOKENS = 512
SC_GATHER_WINDOW = 128


def _sc_gather_rows(table, idx):
    m, w = idx.shape[0], table.shape[1]
    mesh = plsc.VectorSubcoreMesh(core_axis_name="core", subcore_axis_name="subcore")

    @pl.kernel(out_type=jax.ShapeDtypeStruct((m, w), table.dtype), mesh=mesh, name="moe_row_gather")
    def gather(t_hbm, i_hbm, o_hbm):
        def body(i_vmem, o_vmem):
            pltpu.sync_copy(t_hbm.at[i_vmem.at[0]], o_vmem)

        pltpu.emit_pipeline(
            body,
            grid=(m // SC_GATHER_WINDOW,),
            in_specs=[pl.BlockSpec((1, SC_GATHER_WINDOW), lambda i: (0, i))],
            out_specs=[pl.BlockSpec((SC_GATHER_WINDOW, w), lambda i: (i, 0))],
            core_axis_name=("core", "subcore"),
            dimension_semantics=(pltpu.PARALLEL,),
        )(i_hbm, o_hbm)

    return gather(table, idx.reshape(1, m))


def _sc_scatter_rows(src, dests, pad_rows, n_rows):
    n, w = src.shape
    win = SC_GATHER_WINDOW
    mesh = plsc.VectorSubcoreMesh(core_axis_name="core", subcore_axis_name="subcore")
    idx_spec = pl.BlockSpec((1, win), lambda i: (0, i))
    split = dict(core_axis_name=("core", "subcore"), dimension_semantics=(pltpu.PARALLEL,))

    @pl.kernel(out_type=jax.ShapeDtypeStruct((n_rows, w), src.dtype), mesh=mesh, name="moe_row_scatter")
    def scatter(s_hbm, z_hbm, p_hbm, *rest):
        d_hbms, o_hbm = rest[:-1], rest[-1]

        def body(s_vmem, *i_vmems):
            for i_vmem in i_vmems:
                pltpu.sync_copy(s_vmem, o_hbm.at[i_vmem.at[0]])

        pltpu.emit_pipeline(
            body, grid=(n // win,),
            in_specs=[pl.BlockSpec((win, w), lambda i: (i, 0))] + [idx_spec] * len(dests),
            out_specs=[], **split)(s_hbm, *d_hbms)

        def zero_body(z_vmem, i_vmem):
            pltpu.sync_copy(z_vmem, o_hbm.at[i_vmem.at[0]])

        pltpu.emit_pipeline(
            zero_body, grid=(pad_rows.shape[0] // win,),
            in_specs=[pl.BlockSpec((win, w), lambda i: (0, 0)), idx_spec],
            out_specs=[], **split)(z_hbm, p_hbm)

    zeros = jnp.zeros((win, w), src.dtype)
    return scatter(src, zeros, pad_rows.reshape(1, -1), *[dd.reshape(1, n) for dd in dests])


def _ffn_kernel(be_ref, bi_ref, *refs):
    xb_refs, (wg_ref, wu_ref, wd_ref) = refs[:ROW_PARTS], refs[ROW_PARTS:ROW_PARTS + 3]
    yb_refs, (wg_s, wu_s, wd_s) = refs[ROW_PARTS + 3:2 * ROW_PARTS + 3], refs[2 * ROW_PARTS + 3:]
    i = pl.program_id(0)
    changed = jnp.logical_or(i == 0, be_ref[i] != be_ref[jnp.maximum(i - 1, 0)])

    @pl.when(changed)
    def _():
        wg_s[...] = wg_ref[0, 0].astype(BF16)
        wu_s[...] = wu_ref[0, 0].astype(BF16)
        wd_s[...] = wd_ref[0, 0].astype(BF16)

    @pl.when(bi_ref[i] == i)
    def _():
        half = wg_s.shape[0] // 2
        gate = up = None
        for c in range(ROW_PARTS):
            hi, lo = _unpack_bf16_pairs(xb_refs[c][...])
            hi, lo = hi.astype(BF16), lo.astype(BF16)
            pw = hi.shape[1]
            hs, ls = slice(c * pw, (c + 1) * pw), slice(half + c * pw, half + (c + 1) * pw)
            g = _dot(hi, wg_s[hs, :]) + _dot(lo, wg_s[ls, :])
            u = _dot(hi, wu_s[hs, :]) + _dot(lo, wu_s[ls, :])
            gate, up = (g, u) if gate is None else (gate + g, up + u)
        act = (gate * _sigmoid(gate) * up).astype(BF16)
        packed = _pack_bf16_pairs(_dot(act, wd_s[...]))
        pw = packed.shape[1] // ROW_PARTS
        for c in range(ROW_PARTS):
            yb_refs[c][...] = packed[:, c * pw:(c + 1) * pw]


def _expert_ffn(block_e, block_i, xbs, wg, wu, wd, layer):
    n_rows, pw = xbs[0].shape
    d, ff = wg.shape[2], wg.shape[3]
    n_blk = n_rows // FFN_ROWS
    row_spec = pl.BlockSpec((FFN_ROWS, pw), lambda i, be, bi: (bi[i], 0))
    return pl.pallas_call(
        _ffn_kernel,
        grid_spec=pltpu.PrefetchScalarGridSpec(
            num_scalar_prefetch=2,
            grid=(n_blk,),
            in_specs=[row_spec] * ROW_PARTS + [
                pl.BlockSpec((1, 1, d, ff), lambda i, be, bi: (layer, be[i], 0, 0)),
                pl.BlockSpec((1, 1, d, ff), lambda i, be, bi: (layer, be[i], 0, 0)),
                pl.BlockSpec((1, 1, ff, d), lambda i, be, bi: (layer, be[i], 0, 0)),
            ],
            out_specs=[row_spec] * ROW_PARTS,
            scratch_shapes=[pltpu.VMEM((d, ff), BF16), pltpu.VMEM((d, ff), BF16), pltpu.VMEM((ff, d), BF16)],
        ),
        out_shape=[jax.ShapeDtypeStruct((n_rows, pw), U32)] * ROW_PARTS,
        compiler_params=_cparams(("arbitrary",)),
        name="moe_expert_ffn",
    )(block_e, block_i, *xbs, wg, wu, wd)


def _combine_kernel(x_ref, route_ref, *refs):
    y_refs, o_ref = refs[:-1], refs[-1]
    o_ref[...] = _add_expert_rows(x_ref[...], route_ref[...],
                                  [(y_refs[2 * c][...], y_refs[2 * c + 1][...]) for c in range(ROW_PARTS)])


def _combine(x2, route, ytoks, tc):
    n, d = x2.shape
    w = ytoks[0].shape[1]
    nsteps = n // tc
    y_specs, y_args = [], []
    for ytok in ytoks:
        y_specs += [pl.BlockSpec((tc, w), lambda i: (i, 0)), pl.BlockSpec((tc, w), lambda i: (i + nsteps, 0))]
        y_args += [ytok, ytok]
    return pl.pallas_call(
        _combine_kernel,
        grid=(nsteps,),
        in_specs=[pl.BlockSpec((tc, d), lambda i: (i, 0)), pl.BlockSpec((tc, ROUTE_LANES), lambda i: (i, 0))] + y_specs,
        out_specs=pl.BlockSpec((tc, d), lambda i: (i, 0)),
        out_shape=jax.ShapeDtypeStruct((n, d), F32),
        compiler_params=_cparams(("parallel",)),
        name="moe_combine",
    )(x2, route, *y_args)


def _moe(n, hfps, route_t, counts, wg, wu, wd, layer):
    cnt = counts[:, 0, :N_EXPERTS].astype(I32)
    total = jnp.sum(cnt, axis=0)
    padded = (total + FFN_ROWS - 1) // FFN_ROWS * FFN_ROWS
    pad_end = jnp.cumsum(padded)
    pad_start = pad_end - padded
    tile_base = pad_start[None, :] + jnp.cumsum(cnt, axis=0) - cnt
    expert_ids = jnp.arange(N_EXPERTS, dtype=I32)
    dests = []
    for k in range(TOP_K):
        ek = route_t[:, k, :].astype(I32)
        base = jnp.sum(jnp.where(ek[:, :, None] == expert_ids, tile_base[:, None, :], 0), axis=-1)
        dests.append((base + route_t[:, 4 + k, :].astype(I32)).reshape(n))
    dest_by_slot = jnp.concatenate(dests)
    n_blk = (n * TOP_K) // FFN_ROWS + N_EXPERTS
    n_rows = n_blk * FFN_ROWS
    used = pad_end[-1] // FFN_ROWS
    block_i = jnp.minimum(jnp.arange(n_blk, dtype=I32), used - 1).astype(I32)
    ended = (pad_end[None, :] <= (block_i * FFN_ROWS)[:, None]).astype(I32)
    block_e = jnp.minimum(jnp.sum(ended, axis=1), N_EXPERTS - 1).astype(I32)
    seg_len = jnp.concatenate([padded - total, (n_rows - pad_end[-1])[None]])
    seg_first = jnp.concatenate([pad_start + total, pad_end[-1:]])
    seg_end = jnp.cumsum(seg_len)
    jpad = jnp.arange(n_rows - n * TOP_K, dtype=I32)
    seg = jnp.sum((seg_end[None, :] <= jpad[:, None]).astype(I32), axis=1)
    pad_rows = (seg_first[seg] + jpad - (seg_end - seg_len)[seg]).astype(I32)

    xbs = [_sc_scatter_rows(part.reshape(n, part.shape[-1]), dests, pad_rows, n_rows) for part in hfps]
    yb = _expert_ffn(block_e, block_i, xbs, wg, wu, wd, layer)
    return [_sc_gather_rows(part, dest_by_slot) for part in yb]


def _apply_moe(x2, route, ytoks):
    b, s, d = x2.shape
    n = b * s
    out = _combine(x2.reshape(n, d), route.reshape(n, ROUTE_LANES), ytoks, min(COMBINE_TOKENS, n))
    return out.reshape(b, s, d)


def _rope_lane_freq():
    inv = ROPE_THETA ** (-jnp.arange(0, ROPE_DIM // 2, dtype=F32) * 2.0 / ROPE_DIM)
    idx = np.full((HEAD_LANES,), -1, np.int64)
    for r in range(ROPE_DIM):
        idx[_head_lane(NOPE_DIM + r)] = r % ROPE_HALF
    return _gather_cols(inv[None, :], idx)


FAST_SOFTMAX_MAX_LOG2 = 60.0


def _score_bound_log2(qg, kg):
    return 1.02 * LOG2E * QK_DIM ** 0.5 * jnp.max(jnp.abs(qg)) * jnp.max(jnp.abs(kg))


def _partner_lanes(idx):
    out = np.full_like(idx, -1)
    for base in range(0, idx.shape[0], HEAD_LANES):
        for r in range(ROPE_DIM):
            lane = _head_lane(NOPE_DIM + r)
            out[base + lane] = idx[base + (lane + HALF_LANES) % HEAD_LANES]
    return out


def kernel(x, mem, positions, ln_mix, w_in, q_lat_norm, w_uq, kv_lat_norm, w_ukv, q_norm, k_norm, conv_w, conv_b,
           dt_bias, a_log, d_skip, ssd_norm, w_out, pool_w, pool_b, pool_scale, ln_xq, ln_mem, xq_w, xkv_w, xq_norm,
           xk_norm, xo_w, ln_ffn, rg_w, rg_b, re_w, re_b, exp_w_gate, exp_w_up, exp_w_down):
    b, s, d = x.shape
    depth = ln_mix.shape[0]
    tm = min(1024, s)
    tq = min(512, s)
    assert d == D_MODEL and s % tm == 0 and s % CHUNK == 0 and tm >= POOL_CARRY

    pos = positions.astype(F32)[..., None]
    invf = _rope_lane_freq()
    hsum = jnp.asarray(np.kron(np.eye(X_HEADS), np.ones((X_HEAD_DIM, X_HEAD_DIM))), BF16)
    ltri = jnp.asarray(np.tril(np.ones((tm, tm)), -1), BF16)
    row2 = lambda v: v.reshape(1, -1)
    lane_pad = lambda v: jnp.pad(v, (0, LANES - v.shape[0])).reshape(1, LANES)

    pending = None
    for layer in range(depth):
        j = layer // 2
        kbd, vbd = _mem_kv(mem, row2(ln_mem[layer]), xkv_w[layer].astype(BF16),
                           row2(jnp.tile(xk_norm[layer], X_HEADS)), hsum)
        rw = jnp.pad(jnp.concatenate([rg_w[layer], re_w[layer]], axis=1),
                     ((0, 0), (0, ROUTE_LANES - MOE_GROUPS - N_EXPERTS)))
        rw_hi = rw.astype(BF16)
        rw_lo = (rw - rw_hi.astype(F32)).astype(BF16)
        rb = lane_pad(jnp.concatenate([rg_b[layer], re_b[layer]]))
        tail_args = [kbd, vbd, row2(ln_xq[layer]), xq_w[layer].astype(BF16), row2(jnp.tile(xq_norm[layer], X_HEADS)),
                     hsum, xo_w[layer].astype(BF16), row2(ln_ffn[layer]), rw_hi, rw_lo, rb, ltri]
        row = lambda i, jj: (i, jj, 0)
        fixed2 = lambda i, jj: (0, 0)
        if layer % 2 == 0:
            win = _gather_cols(w_in[j], _win_col_index()).astype(BF16)
            q_idx = _head_col_index(QK_DIM, 0, QK_DIM)
            wuq = _gather_cols(w_uq[j], q_idx).astype(BF16)
            wuq_p = _gather_cols(w_uq[j], _partner_lanes(q_idx)).astype(BF16)
            wuk = _gather_cols(w_ukv[j], _head_col_index(NOPE_DIM + V_DIM, 0, NOPE_DIM)).astype(BF16)
            v_idx = np.full((A_HEADS * HEAD_LANES,), -1, np.int64)
            for hd in range(A_HEADS):
                v_idx[hd * HEAD_LANES:hd * HEAD_LANES + V_DIM] = hd * (NOPE_DIM + V_DIM) + NOPE_DIM + np.arange(V_DIM)
            wuv = _gather_cols(w_ukv[j], v_idx).astype(BF16)
            bound = _score_bound_log2(q_norm[j], k_norm[j])
            koff = jnp.zeros((1, HEAD_LANES), F32).at[0, SCORE_PAD_LANE].set(-bound)
            gain_idx = _head_col_index(QK_DIM, 0, QK_DIM)[:HEAD_LANES]
            lane_consts = jnp.concatenate(
                [_gather_cols(g[None, :], idx) for g in (q_norm[j], k_norm[j])
                 for idx in (gain_idx, _partner_lanes(gain_idx))]
                + [koff, invf, jnp.zeros((2, HEAD_LANES), F32)], axis=0)
            q, k, v, z, xbc, misc = _front_even(
                x, pos, lane_consts, row2(ln_mix[layer]), win, row2(q_lat_norm[j]), wuq, wuq_p,
                row2(kv_lat_norm[j]), wuk, wuv, tm)
            attn = lax.cond(bound <= FAST_SOFTMAX_MAX_LOG2,
                            functools.partial(_attention, tq=tq, online=False),
                            functools.partial(_attention, tq=tq, online=True), q, k, v)
            y = _ssd(xbc, misc, z, conv_w[j], row2(conv_b[j]), lane_pad(dt_bias[j]), lane_pad(a_log[j]),
                     row2(jnp.repeat(d_skip[j], SSD_HEAD_DIM)), row2(ssd_norm[j]))
            half = A_HEADS * V_DIM
            front_args = [x, attn, y, w_out[j].astype(BF16)]
            front_specs = [pl.BlockSpec((1, tm, d), row), pl.BlockSpec((1, tm, half), row),
                           pl.BlockSpec((1, tm, D_INNER), row), pl.BlockSpec((half + D_INNER, d), fixed2)]
            x2, *hfps, route, route_t, counts = _post("even", front_args, front_specs, tail_args, b, s, tm, [])
        else:
            front_args, front_specs = [x], [pl.BlockSpec((1, tm, d), row)]
            if pending is not None:
                route_prev, ytoks = pending
                nt = s // tm
                pw = ytoks[0].shape[1]
                front_args += [route_prev]
                front_specs += [pl.BlockSpec((1, tm, ROUTE_LANES), row)]
                for ytok in ytoks:
                    front_args += [ytok, ytok]
                    front_specs += [pl.BlockSpec((tm, pw), lambda i, jj: (i * nt + jj, 0)),
                                    pl.BlockSpec((tm, pw), lambda i, jj: (i * nt + jj + b * nt, 0))]
            front_args += [row2(ln_mix[layer]), pool_w[j].astype(BF16), row2(pool_b[j]), row2(pool_scale[j])]
            front_specs += [pl.BlockSpec((1, d), fixed2),
                            pl.BlockSpec((len(POOL_WINDOWS), POOL_GROUP, POOL_GROUP), lambda i, jj: (0, 0, 0)),
                            pl.BlockSpec((1, d), fixed2), pl.BlockSpec((1, d), fixed2)]
            x2, *hfps, route, route_t, counts = _post(
                "pool" if pending is None else "pool_pending", front_args, front_specs, tail_args, b, s, tm,
                [pltpu.VMEM((POOL_CARRY, d), F32)])
        ytoks = _moe(b * s, hfps, route_t, counts, exp_w_gate, exp_w_up, exp_w_down, layer)
        x, pending = x2, (route, ytoks)
        if layer + 1 == depth or (layer + 1) % 2 == 0:
            x, pending = _apply_moe(x2, route, ytoks), None
    return x
```

```python
import functools

import numpy as np
import jax
import jax.numpy as jnp
from jax import lax
from jax.experimental import pallas as pl
from jax.experimental.pallas import tpu as pltpu
from jax.experimental.pallas import tpu_sc as plsc

F32 = jnp.float32
BF16 = jnp.bfloat16
U32 = jnp.uint32
I32 = jnp.int32

RMS_EPS = 1e-6
ROPE_THETA = 10000.0

D_MODEL = 1024
X_HEADS, X_HEAD_DIM = 4, 64
A_HEADS, NOPE_DIM, ROPE_DIM, V_DIM = 8, 64, 32, 64
QK_DIM = NOPE_DIM + ROPE_DIM
Q_LORA, KV_LORA = 256, 128
B_HEADS, SSD_HEAD_DIM, SSD_GROUPS, SSD_STATE, CONV_K, CHUNK = 8, 64, 2, 128, 4, 128
D_INNER = B_HEADS * SSD_HEAD_DIM
CONV_CH = D_INNER + 2 * SSD_GROUPS * SSD_STATE
POOL_WINDOWS = (2, 4, 8, 16)
POOL_GROUP = D_MODEL // 4
MOE_GROUPS, EXPERTS_PER_GROUP, TOP_K, EXPERT_FF = 4, 8, 2, 256
N_EXPERTS = MOE_GROUPS * EXPERTS_PER_GROUP

LANES = 128
HEAD_LANES = LANES
HALF_LANES = LANES // 2
ROPE_HALF = ROPE_DIM // 2
NOPE_HALF = NOPE_DIM // 2
POOL_CARRY = 16
CONV_CARRY = 8
VMEM_LIMIT = 56 * 1024 * 1024


def _cparams(sem):
    return pltpu.CompilerParams(dimension_semantics=sem, vmem_limit_bytes=VMEM_LIMIT)


def _rms(u, g):
    return u * lax.rsqrt(jnp.mean(u * u, axis=-1, keepdims=True) + RMS_EPS) * g


def _sigmoid(u):
    return 1.0 / (1.0 + jnp.exp(-u))


def _dot(a, b):
    return jnp.dot(a, b, preferred_element_type=F32)


def _dot_nt(a, b):
    return lax.dot_general(a, b, (((1,), (1,)), ((), ())), preferred_element_type=F32)


def _head_lane(d):
    if d < NOPE_HALF:
        return d
    if d < NOPE_DIM:
        return HALF_LANES + (d - NOPE_HALF)
    r = d - NOPE_DIM
    if r < ROPE_HALF:
        return NOPE_HALF + r
    return HALF_LANES + NOPE_HALF + (r - ROPE_HALF)


def _gather_cols(w, idx):
    w_ext = jnp.concatenate([w, jnp.zeros(w.shape[:-1] + (1,), w.dtype)], axis=-1)
    idx = np.where(idx < 0, w.shape[-1], idx)
    return jnp.take(w_ext, jnp.asarray(idx, dtype=jnp.int32), axis=-1)


IN_W = 2 * D_MODEL
_OFF_QLAT, _OFF_KVLAT, _OFF_MISC, _OFF_Z, _OFF_XBC = 0, 256, 384, 512, 1024


def _win_col_index():
    idx = np.full((IN_W,), -1, np.int64)
    idx[_OFF_QLAT:_OFF_QLAT + Q_LORA] = np.arange(Q_LORA)
    idx[_OFF_KVLAT:_OFF_KVLAT + KV_LORA] = Q_LORA + np.arange(KV_LORA)
    rope0 = Q_LORA + KV_LORA
    for r in range(ROPE_DIM):
        idx[_OFF_MISC + _head_lane(NOPE_DIM + r)] = rope0 + r
    z0 = rope0 + ROPE_DIM
    idx[_OFF_Z:_OFF_Z + D_INNER] = z0 + np.arange(D_INNER)
    xbc0 = z0 + D_INNER
    idx[_OFF_XBC:_OFF_XBC + CONV_CH] = xbc0 + np.arange(CONV_CH)
    dt0 = xbc0 + CONV_CH
    idx[_OFF_MISC:_OFF_MISC + B_HEADS] = dt0 + np.arange(B_HEADS)
    return idx


def _head_col_index(per_head, offset, count):
    idx = np.full((A_HEADS * HEAD_LANES,), -1, np.int64)
    for h in range(A_HEADS):
        for d in range(count):
            idx[h * HEAD_LANES + _head_lane(d)] = h * per_head + offset + d
    return idx


SCORE_PAD_LANE = NOPE_HALF + ROPE_HALF
ONES_LANE = V_DIM
LOG2E = 1.4426950408889634


def _front_even_kernel(x_ref, pos_ref, lc_ref, ln_ref, win_ref, qln_ref, wuq_ref, wuqr_ref, kvln_ref, wuk_ref, wuv_ref,
                       q_ref, k_ref, v_ref, z_ref, xbc_ref, misc_ref):
    x = x_ref[0]
    h = _rms(x, ln_ref[...]).astype(BF16)
    proj = _dot(h, win_ref[...])
    misc = proj[:, _OFF_MISC:_OFF_Z]
    z_ref[0] = proj[:, _OFF_Z:_OFF_XBC].astype(BF16)
    xbc_ref[0] = proj[:, _OFF_XBC:].astype(BF16)
    misc_ref[0] = misc
    ql = _rms(proj[:, _OFF_QLAT:_OFF_KVLAT], qln_ref[...]).astype(BF16)
    kvl = _rms(proj[:, _OFF_KVLAT:_OFF_MISC], kvln_ref[...]).astype(BF16)
    q = _dot(ql, wuq_ref[...])
    kn = _dot(kvl, wuk_ref[...])
    v = _dot(kvl, wuv_ref[...])
    lane = lax.broadcasted_iota(I32, (1, HEAD_LANES), 1)
    first_half = (lane >= NOPE_HALF) & (lane < NOPE_HALF + ROPE_HALF)
    second_half = (lane >= HALF_LANES + NOPE_HALF) & (lane < HALF_LANES + NOPE_HALF + ROPE_HALF)
    lc = lc_ref[...]
    qg, qg_p, kg, kg_p, k_off, invf = (lc[i:i + 1] for i in range(6))
    ones = jnp.ones((HEAD_LANES, HEAD_LANES), BF16)

    def lane_sumsq(u):
        return _dot((u * u).astype(BF16), ones)

    krope = jnp.where(first_half | second_half, misc, 0.0)
    kr_ss = lane_sumsq(krope)
    ang = pos_ref[0] * invf
    cos_t = jnp.cos(ang)
    sin_t = jnp.where(first_half, -jnp.sin(ang), jnp.sin(ang))
    q_one = (lane == SCORE_PAD_LANE).astype(F32)
    v_one = (lane == ONES_LANE).astype(F32)
    q_scale = QK_DIM ** -0.5 * LOG2E
    qa, qb = qg * cos_t * q_scale, qg_p * sin_t * q_scale
    ka, kc = kg * cos_t, pltpu.roll(krope, HALF_LANES, 1) * (kg_p * sin_t)
    q_p = _dot(ql, wuqr_ref[...])
    for hd in range(A_HEADS):
        sl = slice(hd * HEAD_LANES, (hd + 1) * HEAD_LANES)
        qs = q[:, sl]
        inv = lax.rsqrt(lane_sumsq(qs) * (1.0 / QK_DIM) + RMS_EPS)
        q_ref[0, hd] = ((qs * qa + q_p[:, sl] * qb) * inv + q_one).astype(BF16)
        kns = kn[:, sl]
        inv = lax.rsqrt((lane_sumsq(kns) + kr_ss) * (1.0 / QK_DIM) + RMS_EPS)
        k_ref[0, hd] = (((kns + krope) * ka + kc) * inv + k_off).astype(BF16)
        v_ref[0, hd] = (v[:, sl] + v_one).astype(BF16)


def _front_even(x, pos, lane_consts, ln, win, qln, wuq, wuq_p, kvln, wuk, wuv, tm):
    b, s, d = x.shape
    grid = (b, s // tm)
    row = lambda i, j: (i, j, 0)
    fixed2 = lambda i, j: (0, 0)
    head_row = lambda i, j: (i, 0, j, 0)
    hw = A_HEADS * HEAD_LANES
    return pl.pallas_call(
        _front_even_kernel,
        grid=grid,
        in_specs=[
            pl.BlockSpec((1, tm, d), row),
            pl.BlockSpec((1, tm, 1), row),
            pl.BlockSpec((8, HEAD_LANES), fixed2),
            pl.BlockSpec((1, d), fixed2),
            pl.BlockSpec((d, IN_W), fixed2),
            pl.BlockSpec((1, Q_LORA), fixed2),
            pl.BlockSpec((Q_LORA, hw), fixed2),
            pl.BlockSpec((Q_LORA, hw), fixed2),
            pl.BlockSpec((1, KV_LORA), fixed2),
            pl.BlockSpec((KV_LORA, hw), fixed2),
            pl.BlockSpec((KV_LORA, hw), fixed2),
        ],
        out_specs=[
            pl.BlockSpec((1, A_HEADS, tm, HEAD_LANES), head_row),
            pl.BlockSpec((1, A_HEADS, tm, HEAD_LANES), head_row),
            pl.BlockSpec((1, A_HEADS, tm, HEAD_LANES), head_row),
            pl.BlockSpec((1, tm, D_INNER), row),
            pl.BlockSpec((1, tm, CONV_CH), row),
            pl.BlockSpec((1, tm, HEAD_LANES), row),
        ],
        out_shape=[
            jax.ShapeDtypeStruct((b, A_HEADS, s, HEAD_LANES), BF16),
            jax.ShapeDtypeStruct((b, A_HEADS, s, HEAD_LANES), BF16),
            jax.ShapeDtypeStruct((b, A_HEADS, s, HEAD_LANES), BF16),
            jax.ShapeDtypeStruct((b, s, D_INNER), BF16),
            jax.ShapeDtypeStruct((b, s, CONV_CH), BF16),
            jax.ShapeDtypeStruct((b, s, HEAD_LANES), F32),
        ],
        compiler_params=_cparams(("parallel", "parallel")),
        name="front_even",
    )(x, pos, lane_consts, ln, win, qln, wuq, wuq_p, kvln, wuk, wuv)


HEADS_PER_STEP = 8


def _attn_kernel(q_ref, k_ref, v_ref, o_ref, *, tq, online):
    qi = pl.program_id(2)
    row = lax.broadcasted_iota(I32, (tq, tq), 0)
    col = lax.broadcasted_iota(I32, (tq, tq), 1)

    def head_step(hh, j, carry, masked):
        kj = k_ref[0, hh, pl.ds(j * tq, tq), :]
        vj = v_ref[0, hh, pl.ds(j * tq, tq), :]
        s = _dot_nt(q_ref[0, hh], kj)
        if masked:
            s = jnp.where(row >= col, s, -jnp.inf)
        if online:
            m, acc = carry
            m_new = jnp.maximum(m, jnp.max(s, axis=-1, keepdims=True))
            p = jnp.exp2(s - m_new).astype(BF16)
            return m_new, jnp.exp2(m - m_new) * acc + _dot(p, vj)
        return carry + _dot(jnp.exp2(s).astype(BF16), vj)

    def step(j, carries, masked):
        return tuple(head_step(hh, j, carries[hh], masked) for hh in range(HEADS_PER_STEP))

    acc0 = jnp.zeros((tq, HEAD_LANES), F32)
    init = (jnp.full((tq, 1), -jnp.inf, F32), acc0) if online else acc0
    carries = lax.fori_loop(0, qi, functools.partial(step, masked=False), (init,) * HEADS_PER_STEP)
    carries = step(qi, carries, True)
    outs = []
    for carry in carries:
        acc = carry[1] if online else carry
        outs.append(acc / acc[:, ONES_LANE:ONES_LANE + 1])
    lane = lax.broadcasted_iota(I32, (1, HEAD_LANES), 1)
    per_group = HEAD_LANES // V_DIM
    groups = []
    for g0 in range(0, HEADS_PER_STEP, per_group):
        out = outs[g0]
        for hh in range(1, per_group):
            out = jnp.where(lane >= hh * V_DIM, pltpu.roll(outs[g0 + hh], hh * V_DIM, 1), out)
        groups.append(out.astype(BF16))
    o_ref[0] = jnp.concatenate(groups, axis=1)


def _attention(q, k, v, tq, online):
    b, nh, s, _ = q.shape
    grid = (b, nh // HEADS_PER_STEP, s // tq)
    kv_spec = pl.BlockSpec((1, HEADS_PER_STEP, s, HEAD_LANES), lambda i, h, j: (i, h, 0, 0))
    return pl.pallas_call(
        functools.partial(_attn_kernel, tq=tq, online=online),
        grid=grid,
        in_specs=[
            pl.BlockSpec((1, HEADS_PER_STEP, tq, HEAD_LANES), lambda i, h, j: (i, h, j, 0)),
            kv_spec,
            kv_spec,
        ],
        out_specs=pl.BlockSpec((1, tq, HEADS_PER_STEP * V_DIM), lambda i, h, j: (i, j, h)),
        out_shape=jax.ShapeDtypeStruct((b, s, nh * V_DIM), BF16),
        compiler_params=_cparams(("parallel", "parallel", "parallel")),
        name="mla_attention_online" if online else "mla_attention",
    )(q, k, v)


def _ssd_kernel(xbc_ref, misc_ref, z_ref, cw_ref, cb_ref, dtb_ref, alog_ref, dskip_ref, gn_ref, y_ref,
                state_ref, carry_ref):
    c = pl.program_id(1)
    t = CHUNK
    rows = y_ref.shape[1]

    @pl.when(c == 0)
    def _():
        state_ref[...] = jnp.zeros_like(state_ref)
        carry_ref[...] = jnp.zeros_like(carry_ref)

    xr = xbc_ref[0].astype(F32)
    xcat = jnp.concatenate([carry_ref[...], xr], axis=0)
    carry_ref[...] = xr[rows - CONV_CARRY:, :]
    conv = jnp.zeros((rows, CONV_CH), F32) + cb_ref[...]
    for kk in range(CONV_K):
        sh = CONV_K - 1 - kk
        shifted = xcat if sh == 0 else pltpu.roll(xcat, sh, 0)
        conv = conv + cw_ref[kk:kk + 1, :] * shifted[CONV_CARRY:, :]
    xa = conv * _sigmoid(conv)
    xs = xa[:, :D_INNER]
    gw = SSD_GROUPS * SSD_STATE
    bmat = xa[:, D_INNER:D_INNER + gw]
    cmat = xa[:, D_INNER + gw:]

    u = misc_ref[0] + dtb_ref[...]
    dt = jnp.maximum(u, 0.0) + jnp.log(1.0 + jnp.exp(-jnp.abs(u)))
    a = -jnp.exp(alog_ref[...])
    lane = lax.broadcasted_iota(I32, (1, LANES), 1)
    adt_all = jnp.where(lane < B_HEADS, dt * a, 0.0)
    rowi = lax.broadcasted_iota(I32, (t, LANES), 0)
    tri = lax.broadcasted_iota(I32, (t, t), 0) >= lax.broadcasted_iota(I32, (t, t), 1)
    rep = B_HEADS // SSD_GROUPS
    y_chunks = []
    for ci in range(rows // t):
        sl = slice(ci * t, (ci + 1) * t)
        acs = adt_all[sl]
        sh = 1
        while sh < t:
            acs = acs + jnp.where(rowi >= sh, pltpu.roll(acs, sh, 0), 0.0)
            sh *= 2
        acs_t = acs.T
        ys = []
        for g in range(SSD_GROUPS):
            bg = bmat[sl, g * SSD_STATE:(g + 1) * SSD_STATE]
            cg = cmat[sl, g * SSD_STATE:(g + 1) * SSD_STATE]
            bg16, cg16 = bg.astype(BF16), cg.astype(BF16)
            cb = _dot_nt(cg16, bg16)
            bg_t = bg.T
            for r in range(rep):
                hd = g * rep + r
                col = acs[:, hd:hd + 1]
                rw = acs_t[hd:hd + 1, :]
                last = acs_t[hd:hd + 1, t - 1:t]
                decay = jnp.exp(jnp.where(tri, col - rw, -jnp.inf))
                xh = xs[sl, hd * SSD_HEAD_DIM:(hd + 1) * SSD_HEAD_DIM]
                xdt = (xh * dt[sl, hd:hd + 1]).astype(BF16)
                y_diag = _dot((cb * decay).astype(BF16), xdt)
                prev = state_ref[hd]
                y_off = _dot(cg16, prev.astype(BF16)) * jnp.exp(col)
                new_state = _dot((bg_t * jnp.exp(last - rw)).astype(BF16), xdt)
                state_ref[hd] = prev * jnp.exp(last) + new_state
                ys.append(y_diag + y_off)
        y_chunks.append(jnp.concatenate(ys, axis=1))
    y = jnp.concatenate(y_chunks, axis=0) + xs * dskip_ref[...]
    zf = z_ref[0].astype(F32)
    y = y * (zf * _sigmoid(zf))
    y_ref[0] = _rms(y, gn_ref[...]).astype(BF16)


SSD_CHUNKS_PER_STEP = 4


def _ssd(xbc, misc, z, cw, cb, dtb, alog, dskip, gn):
    b, s, _ = xbc.shape
    rows = CHUNK * SSD_CHUNKS_PER_STEP if s % (CHUNK * SSD_CHUNKS_PER_STEP) == 0 else CHUNK
    grid = (b, s // rows)
    row = lambda i, j: (i, j, 0)
    fixed2 = lambda i, j: (0, 0)
    return pl.pallas_call(
        _ssd_kernel,
        grid=grid,
        in_specs=[
            pl.BlockSpec((1, rows, CONV_CH), row),
            pl.BlockSpec((1, rows, LANES), row),
            pl.BlockSpec((1, rows, D_INNER), row),
            pl.BlockSpec((CONV_K, CONV_CH), fixed2),
            pl.BlockSpec((1, CONV_CH), fixed2),
            pl.BlockSpec((1, LANES), fixed2),
            pl.BlockSpec((1, LANES), fixed2),
            pl.BlockSpec((1, D_INNER), fixed2),
            pl.BlockSpec((1, D_INNER), fixed2),
        ],
        out_specs=pl.BlockSpec((1, rows, D_INNER), row),
        out_shape=jax.ShapeDtypeStruct((b, s, D_INNER), BF16),
        scratch_shapes=[
            pltpu.VMEM((B_HEADS, SSD_STATE, SSD_HEAD_DIM), F32),
            pltpu.VMEM((CONV_CARRY, CONV_CH), F32),
        ],
        compiler_params=_cparams(("parallel", "arbitrary")),
        name="ssd_scan",
    )(xbc, misc, z, cw, cb, dtb, alog, dskip, gn)


XW = X_HEADS * X_HEAD_DIM


def _mem_kv_kernel(mem_ref, ln_ref, wkv_ref, kg_ref, hsum_ref, kbd_ref, vbd_ref):
    m = mem_ref.shape[1]
    mn = _rms(mem_ref[0], ln_ref[...]).astype(BF16)
    kv = _dot(mn, wkv_ref[...])
    k, v = kv[:, :XW], kv[:, XW:]
    ss = _dot((k * k).astype(BF16), hsum_ref[...])
    kn = (k * lax.rsqrt(ss * (1.0 / X_HEAD_DIM) + RMS_EPS) * kg_ref[...]).astype(BF16)
    v16 = v.astype(BF16)
    head_of_lane = lax.shift_right_arithmetic(lax.broadcasted_iota(I32, (1, XW), 1), jnp.int32(_LOG2_XHD))
    for hd in range(X_HEADS):
        keep = head_of_lane == hd
        kbd_ref[0, hd * m:(hd + 1) * m, :] = jnp.where(keep, kn, jnp.zeros_like(kn))
        vbd_ref[0, hd * m:(hd + 1) * m, :] = jnp.where(keep, v16, jnp.zeros_like(v16))


def _mem_kv(mem, ln, wkv, kg, hsum):
    b, m, d = mem.shape
    fixed2 = lambda i: (0, 0)
    return pl.pallas_call(
        _mem_kv_kernel,
        grid=(b,),
        in_specs=[
            pl.BlockSpec((1, m, d), lambda i: (i, 0, 0)),
            pl.BlockSpec((1, d), fixed2),
            pl.BlockSpec((d, 2 * XW), fixed2),
            pl.BlockSpec((1, XW), fixed2),
            pl.BlockSpec((XW, XW), fixed2),
        ],
        out_specs=[
            pl.BlockSpec((1, X_HEADS * m, XW), lambda i: (i, 0, 0)),
            pl.BlockSpec((1, X_HEADS * m, XW), lambda i: (i, 0, 0)),
        ],
        out_shape=[
            jax.ShapeDtypeStruct((b, X_HEADS * m, XW), BF16),
            jax.ShapeDtypeStruct((b, X_HEADS * m, XW), BF16),
        ],
        compiler_params=_cparams(("parallel",)),
        name="mem_kv",
    )(mem, ln, wkv, kg, hsum)


ROUTE_LANES = LANES
ROW_PARTS = 2
_GROUP_LANE0 = 0
_EXPERT_LANE0 = MOE_GROUPS
_LOG2_EPG = EXPERTS_PER_GROUP.bit_length() - 1
_LOG2_XHD = X_HEAD_DIM.bit_length() - 1


def _pack_bf16_pairs(v):
    w = v.shape[1] // 2
    r = v.astype(BF16).astype(F32)
    hi = lax.bitcast_convert_type(r[:, :w], U32)
    lo = lax.bitcast_convert_type(r[:, w:], U32)
    return (hi & jnp.uint32(0xFFFF0000)) | (lo >> jnp.uint32(16))


def _unpack_bf16_pairs(u):
    hi = lax.bitcast_convert_type(u & jnp.uint32(0xFFFF0000), F32)
    lo = lax.bitcast_convert_type(u << jnp.uint32(16), F32)
    return hi, lo


TAIL_SPLIT = 1


def _tail_rows(x1, rows, kbd_ref, vbd_ref, lnq_ref, wq_ref, qg_ref, hsum_ref, wo_ref, lnf_ref, rwh_ref, rwl_ref,
               rb_ref, x2_ref, hfp_refs):
    tm = x1.shape[0]
    m = kbd_ref.shape[1] // X_HEADS
    hq = _rms(x1, lnq_ref[...]).astype(BF16)
    q = _dot(hq, wq_ref[...])
    ss = _dot((q * q).astype(BF16), hsum_ref[...])
    qn = (q * lax.rsqrt(ss * (1.0 / X_HEAD_DIM) + RMS_EPS) * qg_ref[...] * (X_HEAD_DIM ** -0.5)).astype(BF16)
    s = _dot_nt(qn, kbd_ref[0])
    ps = []
    for hd in range(X_HEADS):
        sh = s[:, hd * m:(hd + 1) * m]
        e = jnp.exp(sh - jnp.max(sh, axis=-1, keepdims=True))
        ps.append((e / jnp.sum(e, axis=-1, keepdims=True)).astype(BF16))
    o = _dot(jnp.concatenate(ps, axis=1), vbd_ref[0]).astype(BF16)
    x2 = x1 + _dot(o, wo_ref[...])
    x2_ref[0, rows, :] = x2

    hf = _rms(x2, lnf_ref[...])
    hf_hi = hf.astype(BF16)
    packed = _pack_bf16_pairs(hf)
    pw = packed.shape[1] // ROW_PARTS
    for c in range(ROW_PARTS):
        hfp_refs[c][0, rows, :] = packed[:, c * pw:(c + 1) * pw]
    hf_lo = (hf - hf_hi.astype(F32)).astype(BF16)
    logits = _dot(hf_hi, rwh_ref[...]) + _dot(hf_hi, rwl_ref[...]) + _dot(hf_lo, rwh_ref[...]) + rb_ref[...]

    lane_i = lax.broadcasted_iota(I32, (tm, ROUTE_LANES), 1)
    lane = lane_i.astype(F32)
    big = float(ROUTE_LANES)
    neg = -jnp.inf
    gl = jnp.where(lane_i < MOE_GROUPS, logits, neg)
    gmax = jnp.max(gl, axis=-1, keepdims=True)
    gsum = jnp.sum(jnp.exp(gl - gmax), axis=-1, keepdims=True)
    g_p = 1.0 / gsum
    g_idx = jnp.min(jnp.where(gl == gmax, lane, big), axis=-1, keepdims=True)
    e_lane = lane_i - _EXPERT_LANE0
    grp_of_lane = lax.shift_right_arithmetic(e_lane, jnp.int32(_LOG2_EPG)).astype(F32)
    in_grp = (e_lane >= 0) & (e_lane < N_EXPERTS) & (grp_of_lane == g_idx)
    el = jnp.where(in_grp, logits, neg)
    emax = jnp.max(el, axis=-1, keepdims=True)
    idx1 = jnp.min(jnp.where(el == emax, lane, big), axis=-1, keepdims=True)
    el2 = jnp.where(lane == idx1, neg, el)
    emax2 = jnp.max(el2, axis=-1, keepdims=True)
    idx2 = jnp.min(jnp.where(el2 == emax2, lane, big), axis=-1, keepdims=True)
    r2 = jnp.exp(emax2 - emax)
    gate1 = g_p / (1.0 + r2)
    gate2 = g_p * r2 / (1.0 + r2)
    e1 = idx1 - float(_EXPERT_LANE0)
    e2 = idx2 - float(_EXPERT_LANE0)

    route = jnp.where(lane == 0, e1, 0.0)
    route = jnp.where(lane == 1, e2, route)
    route = jnp.where(lane == 2, gate1, route)
    route = jnp.where(lane == 3, gate2, route)
    return route, (lane == e1).astype(F32), (lane == e2).astype(F32)


def _tail(x1, kbd_ref, vbd_ref, lnq_ref, wq_ref, qg_ref, hsum_ref, wo_ref, lnf_ref, rwh_ref, rwl_ref, rb_ref,
          ltri_ref, x2_ref, *out_refs):
    hfp_refs, (route_ref, route_t_ref, cnt_ref) = out_refs[:ROW_PARTS], out_refs[ROW_PARTS:]
    tm = x1.shape[0]
    tr = tm // TAIL_SPLIT
    parts = [_tail_rows(x1[r * tr:(r + 1) * tr], slice(r * tr, (r + 1) * tr), kbd_ref, vbd_ref, lnq_ref, wq_ref,
                        qg_ref, hsum_ref, wo_ref, lnf_ref, rwh_ref, rwl_ref, rb_ref, x2_ref, hfp_refs)
             for r in range(TAIL_SPLIT)]
    route, oh1, oh2 = (jnp.concatenate([p[i] for p in parts], axis=0) for i in range(3))
    both = oh1 + oh2
    before = _dot(ltri_ref[...], both.astype(BF16))
    rank1 = jnp.sum(before * oh1, axis=-1, keepdims=True)
    rank2 = jnp.sum(before * oh2, axis=-1, keepdims=True)
    cnt_ref[0] = jnp.broadcast_to(jnp.sum(both, axis=0, keepdims=True), cnt_ref.shape[1:])
    lane = lax.broadcasted_iota(I32, (tm, ROUTE_LANES), 1)
    route = jnp.where(lane == 4, rank1, route)
    route = jnp.where(lane == 5, rank2, route)
    route_ref[0] = route
    route_t_ref[0] = route.T[:route_t_ref.shape[1], :]


_TAIL_IN = 12


def _post_even_kernel(x_ref, a_ref, y_ref, wout_ref, *rest):
    tail_in, outs = rest[:_TAIL_IN], rest[_TAIL_IN:]
    half = wout_ref.shape[0] // 2
    x1 = x_ref[0] + _dot(a_ref[0], wout_ref[:half, :]) + _dot(y_ref[0], wout_ref[half:, :])
    _tail(x1, *tail_in, *outs)


def _add_expert_rows(x, route, ys):
    g1, g2 = route[:, 2:3], route[:, 3:4]
    his, los = [], []
    for y1, y2 in ys:
        h1, l1 = _unpack_bf16_pairs(y1)
        h2, l2 = _unpack_bf16_pairs(y2)
        his.append(h1 * g1 + h2 * g2)
        los.append(l1 * g1 + l2 * g2)
    return x + jnp.concatenate(his + los, axis=1)


N_PENDING = 1 + 2 * ROW_PARTS


def _post_pool_kernel(x_ref, *rest, pending):
    if pending:
        route_prev_ref, y_refs, rest = rest[0], rest[1:N_PENDING], rest[N_PENDING:]
    (ln_ref, pw_ref, pb_ref, ps_ref), rest = rest[:4], rest[4:]
    tail_in, outs, carry_ref = rest[:_TAIL_IN], rest[_TAIL_IN:-1], rest[-1]
    j = pl.program_id(1)
    tm = x_ref.shape[1]

    @pl.when(j == 0)
    def _():
        carry_ref[...] = jnp.zeros_like(carry_ref)

    x = x_ref[0]
    if pending:
        x = _add_expert_rows(x, route_prev_ref[0], [(y_refs[2 * c][...], y_refs[2 * c + 1][...])
                                                    for c in range(ROW_PARTS)])
    h = _rms(x, ln_ref[...])
    pos = (j * tm + 1 + lax.broadcasted_iota(I32, (tm, 1), 0)).astype(F32)
    mixed = []
    for g, w in enumerate(POOL_WINDOWS):
        sl = slice(g * POOL_GROUP, (g + 1) * POOL_GROUP)
        hg = h[:, sl]
        acc = jnp.concatenate([carry_ref[:, sl], hg], axis=0)
        sh = 1
        while sh < w:
            acc = acc + pltpu.roll(acc, sh, 0)
            sh *= 2
        win = acc[POOL_CARRY:, :]
        dlt = win / jnp.minimum(pos, float(w)) - hg
        mixed.append(_dot(dlt.astype(BF16), pw_ref[g]))
    carry_ref[...] = h[tm - POOL_CARRY:, :]
    y = (jnp.concatenate(mixed, axis=1) + pb_ref[...]) * ps_ref[...]
    _tail(x + y, *tail_in, *outs)


def _post(kind, front_args, front_specs, tail_args, b, s, tm, scratch, b0=0):
    d = D_MODEL
    m4 = tail_args[0].shape[1]
    row = lambda i, j: (i, j, 0)
    fixed2 = lambda i, j: (0, 0)
    per_b = lambda i, j: (i + b0, 0, 0)
    tail_specs = [
        pl.BlockSpec((1, m4, XW), per_b),
        pl.BlockSpec((1, m4, XW), per_b),
        pl.BlockSpec((1, d), fixed2),
        pl.BlockSpec((d, XW), fixed2),
        pl.BlockSpec((1, XW), fixed2),
        pl.BlockSpec((XW, XW), fixed2),
        pl.BlockSpec((XW, d), fixed2),
        pl.BlockSpec((1, d), fixed2),
        pl.BlockSpec((d, ROUTE_LANES), fixed2),
        pl.BlockSpec((d, ROUTE_LANES), fixed2),
        pl.BlockSpec((1, ROUTE_LANES), fixed2),
        pl.BlockSpec((tm, tm), fixed2),
    ]
    nt = s // tm
    pw = d // 2 // ROW_PARTS
    kernel = {"even": _post_even_kernel,
              "pool": functools.partial(_post_pool_kernel, pending=False),
              "pool_pending": functools.partial(_post_pool_kernel, pending=True)}[kind]
    return pl.pallas_call(
        kernel,
        grid=(b, nt),
        in_specs=front_specs + tail_specs,
        out_specs=[pl.BlockSpec((1, tm, d), row)]
        + [pl.BlockSpec((1, tm, pw), row)] * ROW_PARTS
        + [pl.BlockSpec((1, tm, ROUTE_LANES), row),
           pl.BlockSpec((1, 8, tm), lambda i, j: (i * nt + j, 0, 0)),
           pl.BlockSpec((1, 8, ROUTE_LANES), lambda i, j: (i * nt + j, 0, 0))],
        out_shape=[jax.ShapeDtypeStruct((b, s, d), F32)]
        + [jax.ShapeDtypeStruct((b, s, pw), U32)] * ROW_PARTS
        + [jax.ShapeDtypeStruct((b, s, ROUTE_LANES), F32),
           jax.ShapeDtypeStruct((b * nt, 8, tm), F32),
           jax.ShapeDtypeStruct((b * nt, 8, ROUTE_LANES), F32)],
        scratch_shapes=scratch,
        compiler_params=_cparams(("parallel", "arbitrary")),
        name="post_" + kind,
    )(*front_args, *tail_args)


FFN_ROWS = 1024
COMBINE_TOKENS = 512
MOE_STREAMS = 2
SC_GATHER_WINDOW = 128


def _sc_gather_rows(table, idx):
    m, w = idx.shape[0], table.shape[1]
    mesh = plsc.VectorSubcoreMesh(core_axis_name="core", subcore_axis_name="subcore")

    @pl.kernel(out_type=jax.ShapeDtypeStruct((m, w), table.dtype), mesh=mesh, name="moe_row_gather")
    def gather(t_hbm, i_hbm, o_hbm):
        def body(i_vmem, o_vmem):
            pltpu.sync_copy(t_hbm.at[i_vmem.at[0]], o_vmem)

        pltpu.emit_pipeline(
            body,
            grid=(m // SC_GATHER_WINDOW,),
            in_specs=[pl.BlockSpec((1, SC_GATHER_WINDOW), lambda i: (0, i))],
            out_specs=[pl.BlockSpec((SC_GATHER_WINDOW, w), lambda i: (i, 0))],
            core_axis_name=("core", "subcore"),
            dimension_semantics=(pltpu.PARALLEL,),
        )(i_hbm, o_hbm)

    return gather(table, idx.reshape(1, m))


def _sc_scatter_rows(src, dests, pad_rows, n_rows):
    n, w = src.shape
    win = SC_GATHER_WINDOW
    mesh = plsc.VectorSubcoreMesh(core_axis_name="core", subcore_axis_name="subcore")
    idx_spec = pl.BlockSpec((1, win), lambda i: (0, i))
    split = dict(core_axis_name=("core", "subcore"), dimension_semantics=(pltpu.PARALLEL,))

    @pl.kernel(out_type=jax.ShapeDtypeStruct((n_rows, w), src.dtype), mesh=mesh, name="moe_row_scatter")
    def scatter(s_hbm, z_hbm, p_hbm, *rest):
        d_hbms, o_hbm = rest[:-1], rest[-1]

        def body(s_vmem, *i_vmems):
            for i_vmem in i_vmems:
                pltpu.sync_copy(s_vmem, o_hbm.at[i_vmem.at[0]])

        pltpu.emit_pipeline(
            body, grid=(n // win,),
            in_specs=[pl.BlockSpec((win, w), lambda i: (i, 0))] + [idx_spec] * len(dests),
            out_specs=[], **split)(s_hbm, *d_hbms)

        def zero_body(z_vmem, i_vmem):
            pltpu.sync_copy(z_vmem, o_hbm.at[i_vmem.at[0]])

        pltpu.emit_pipeline(
            zero_body, grid=(pad_rows.shape[0] // win,),
            in_specs=[pl.BlockSpec((win, w), lambda i: (0, 0)), idx_spec],
            out_specs=[], **split)(z_hbm, p_hbm)

    zeros = jnp.zeros((win, w), src.dtype)
    return scatter(src, zeros, pad_rows.reshape(1, -1), *[dd.reshape(1, n) for dd in dests])


def _ffn_kernel(be_ref, bi_ref, *refs):
    xb_refs, (wg_ref, wu_ref, wd_ref) = refs[:ROW_PARTS], refs[ROW_PARTS:ROW_PARTS + 3]
    yb_refs, (wg_s, wu_s, wd_s) = refs[ROW_PARTS + 3:2 * ROW_PARTS + 3], refs[2 * ROW_PARTS + 3:]
    i = pl.program_id(0)
    changed = jnp.logical_or(i == 0, be_ref[i] != be_ref[jnp.maximum(i - 1, 0)])

    @pl.when(changed)
    def _():
        wg_s[...] = wg_ref[0, 0].astype(BF16)
        wu_s[...] = wu_ref[0, 0].astype(BF16)
        wd_s[...] = wd_ref[0, 0].astype(BF16)

    @pl.when(bi_ref[i] == i)
    def _():
        half = wg_s.shape[0] // 2
        gate = up = None
        for c in range(ROW_PARTS):
            hi, lo = _unpack_bf16_pairs(xb_refs[c][...])
            hi, lo = hi.astype(BF16), lo.astype(BF16)
            pw = hi.shape[1]
            hs, ls = slice(c * pw, (c + 1) * pw), slice(half + c * pw, half + (c + 1) * pw)
            g = _dot(hi, wg_s[hs, :]) + _dot(lo, wg_s[ls, :])
            u = _dot(hi, wu_s[hs, :]) + _dot(lo, wu_s[ls, :])
            gate, up = (g, u) if gate is None else (gate + g, up + u)
        act = (gate * _sigmoid(gate) * up).astype(BF16)
        packed = _pack_bf16_pairs(_dot(act, wd_s[...]))
        pw = packed.shape[1] // ROW_PARTS
        for c in range(ROW_PARTS):
            yb_refs[c][...] = packed[:, c * pw:(c + 1) * pw]


def _expert_ffn(block_e, block_i, xbs, wg, wu, wd, layer):
    n_rows, pw = xbs[0].shape
    d, ff = wg.shape[2], wg.shape[3]
    n_blk = n_rows // FFN_ROWS
    row_spec = pl.BlockSpec((FFN_ROWS, pw), lambda i, be, bi: (bi[i], 0))
    return pl.pallas_call(
        _ffn_kernel,
        grid_spec=pltpu.PrefetchScalarGridSpec(
            num_scalar_prefetch=2,
            grid=(n_blk,),
            in_specs=[row_spec] * ROW_PARTS + [
                pl.BlockSpec((1, 1, d, ff), lambda i, be, bi: (layer, be[i], 0, 0)),
                pl.BlockSpec((1, 1, d, ff), lambda i, be, bi: (layer, be[i], 0, 0)),
                pl.BlockSpec((1, 1, ff, d), lambda i, be, bi: (layer, be[i], 0, 0)),
            ],
            out_specs=[row_spec] * ROW_PARTS,
            scratch_shapes=[pltpu.VMEM((d, ff), BF16), pltpu.VMEM((d, ff), BF16), pltpu.VMEM((ff, d), BF16)],
        ),
        out_shape=[jax.ShapeDtypeStruct((n_rows, pw), U32)] * ROW_PARTS,
        compiler_params=_cparams(("arbitrary",)),
        name="moe_expert_ffn",
    )(block_e, block_i, *xbs, wg, wu, wd)


def _combine_kernel(x_ref, route_ref, *refs):
    y_refs, o_ref = refs[:2 * ROW_PARTS], refs[-1]
    o_ref[...] = _add_expert_rows(x_ref[...], route_ref[...],
                                  [(y_refs[2 * c][...], y_refs[2 * c + 1][...]) for c in range(ROW_PARTS)])


def _combine(x2, route, ytoks, tc, out_prev, row0, n_full):
    n, d = x2.shape
    w = ytoks[0].shape[1]
    nsteps = n // tc
    blk0 = row0 // tc
    specs = [pl.BlockSpec((tc, d), lambda i: (i, 0)), pl.BlockSpec((tc, ROUTE_LANES), lambda i: (i, 0))]
    args = [x2, route]
    for ytok in ytoks:
        specs += [pl.BlockSpec((tc, w), lambda i: (i, 0)), pl.BlockSpec((tc, w), lambda i: (i + nsteps, 0))]
        args += [ytok, ytok]
    aliases = {}
    if out_prev is not None:
        specs.append(pl.BlockSpec(memory_space=pl.ANY))
        args.append(out_prev)
        aliases = {len(args) - 1: 0}
    return pl.pallas_call(
        _combine_kernel,
        grid=(nsteps,),
        in_specs=specs,
        out_specs=pl.BlockSpec((tc, d), lambda i: (i + blk0, 0)),
        out_shape=jax.ShapeDtypeStruct((n_full, d), F32),
        input_output_aliases=aliases,
        compiler_params=_cparams(("parallel",)),
        name="moe_combine",
    )(*args)


def _moe(n, hfps, route_t, counts, wg, wu, wd, layer):
    cnt = counts[:, 0, :N_EXPERTS].astype(I32)
    total = jnp.sum(cnt, axis=0)
    padded = (total + FFN_ROWS - 1) // FFN_ROWS * FFN_ROWS
    pad_end = jnp.cumsum(padded)
    pad_start = pad_end - padded
    tile_base = pad_start[None, :] + jnp.cumsum(cnt, axis=0) - cnt
    expert_ids = jnp.arange(N_EXPERTS, dtype=I32)
    dests = []
    for k in range(TOP_K):
        ek = route_t[:, k, :].astype(I32)
        base = jnp.sum(jnp.where(ek[:, :, None] == expert_ids, tile_base[:, None, :], 0), axis=-1)
        dests.append((base + route_t[:, 4 + k, :].astype(I32)).reshape(n))
    dest_by_slot = jnp.concatenate(dests)
    n_blk = (n * TOP_K) // FFN_ROWS + N_EXPERTS
    n_rows = n_blk * FFN_ROWS
    used = pad_end[-1] // FFN_ROWS
    block_i = jnp.minimum(jnp.arange(n_blk, dtype=I32), used - 1).astype(I32)
    ended = (pad_end[None, :] <= (block_i * FFN_ROWS)[:, None]).astype(I32)
    block_e = jnp.minimum(jnp.sum(ended, axis=1), N_EXPERTS - 1).astype(I32)
    seg_len = jnp.concatenate([padded - total, (n_rows - pad_end[-1])[None]])
    seg_first = jnp.concatenate([pad_start + total, pad_end[-1:]])
    seg_end = jnp.cumsum(seg_len)
    jpad = jnp.arange(n_rows - n * TOP_K, dtype=I32)
    seg = jnp.sum((seg_end[None, :] <= jpad[:, None]).astype(I32), axis=1)
    pad_rows = (seg_first[seg] + jpad - (seg_end - seg_len)[seg]).astype(I32)

    xbs = [_sc_scatter_rows(part.reshape(n, part.shape[-1]), dests, pad_rows, n_rows) for part in hfps]
    yb = _expert_ffn(block_e, block_i, xbs, wg, wu, wd, layer)
    return [_sc_gather_rows(part, dest_by_slot) for part in yb]


def _apply_moe(x2, route, ytoks, out_prev, b0, b_full):
    b, s, d = x2.shape
    n = b * s
    prev = None if out_prev is None else out_prev.reshape(b_full * s, d)
    out = _combine(x2.reshape(n, d), route.reshape(n, ROUTE_LANES), ytoks, min(COMBINE_TOKENS, n), prev,
                   b0 * s, b_full * s)
    return out.reshape(b_full, s, d)


def _rope_lane_freq():
    inv = ROPE_THETA ** (-jnp.arange(0, ROPE_DIM // 2, dtype=F32) * 2.0 / ROPE_DIM)
    idx = np.full((HEAD_LANES,), -1, np.int64)
    for r in range(ROPE_DIM):
        idx[_head_lane(NOPE_DIM + r)] = r % ROPE_HALF
    return _gather_cols(inv[None, :], idx)


FAST_SOFTMAX_MAX_LOG2 = 60.0


def _score_bound_log2(qg, kg):
    return 1.02 * LOG2E * QK_DIM ** 0.5 * jnp.max(jnp.abs(qg)) * jnp.max(jnp.abs(kg))


def _partner_lanes(idx):
    out = np.full_like(idx, -1)
    for base in range(0, idx.shape[0], HEAD_LANES):
        for r in range(ROPE_DIM):
            lane = _head_lane(NOPE_DIM + r)
            out[base + lane] = idx[base + (lane + HALF_LANES) % HEAD_LANES]
    return out


def kernel(x, mem, positions, ln_mix, w_in, q_lat_norm, w_uq, kv_lat_norm, w_ukv, q_norm, k_norm, conv_w, conv_b,
           dt_bias, a_log, d_skip, ssd_norm, w_out, pool_w, pool_b, pool_scale, ln_xq, ln_mem, xq_w, xkv_w, xq_norm,
           xk_norm, xo_w, ln_ffn, rg_w, rg_b, re_w, re_b, exp_w_gate, exp_w_up, exp_w_down):
    b, s, d = x.shape
    depth = ln_mix.shape[0]
    tm = min(1024, s)
    tq = min(512, s)
    assert d == D_MODEL and s % tm == 0 and s % CHUNK == 0 and tm >= POOL_CARRY

    pos = positions.astype(F32)[..., None]
    invf = _rope_lane_freq()
    hsum = jnp.asarray(np.kron(np.eye(X_HEADS), np.ones((X_HEAD_DIM, X_HEAD_DIM))), BF16)
    ltri = jnp.asarray(np.tril(np.ones((tm, tm)), -1), BF16)
    row2 = lambda v: v.reshape(1, -1)
    lane_pad = lambda v: jnp.pad(v, (0, LANES - v.shape[0])).reshape(1, LANES)

    n_streams = MOE_STREAMS if b % MOE_STREAMS == 0 else 1
    nb = b // n_streams
    nt = s // tm
    streams = [dict(x=x, off=k * nb, pending=None) for k in range(n_streams)]
    for layer in range(depth):
        j = layer // 2
        kbd, vbd = _mem_kv(mem, row2(ln_mem[layer]), xkv_w[layer].astype(BF16),
                           row2(jnp.tile(xk_norm[layer], X_HEADS)), hsum)
        rw = jnp.pad(jnp.concatenate([rg_w[layer], re_w[layer]], axis=1),
                     ((0, 0), (0, ROUTE_LANES - MOE_GROUPS - N_EXPERTS)))
        rw_hi = rw.astype(BF16)
        rw_lo = (rw - rw_hi.astype(F32)).astype(BF16)
        rb = lane_pad(jnp.concatenate([rg_b[layer], re_b[layer]]))
        tail_args = [kbd, vbd, row2(ln_xq[layer]), xq_w[layer].astype(BF16), row2(jnp.tile(xq_norm[layer], X_HEADS)),
                     hsum, xo_w[layer].astype(BF16), row2(ln_ffn[layer]), rw_hi, rw_lo, rb, ltri]
        row = lambda i, jj: (i, jj, 0)
        fixed2 = lambda i, jj: (0, 0)
        posts = []
        if layer % 2 == 0:
            x = streams[0]["x"]
            assert all(st["x"] is x and st["pending"] is None for st in streams)
            win = _gather_cols(w_in[j], _win_col_index()).astype(BF16)
            q_idx = _head_col_index(QK_DIM, 0, QK_DIM)
            wuq = _gather_cols(w_uq[j], q_idx).astype(BF16)
            wuq_p = _gather_cols(w_uq[j], _partner_lanes(q_idx)).astype(BF16)
            wuk = _gather_cols(w_ukv[j], _head_col_index(NOPE_DIM + V_DIM, 0, NOPE_DIM)).astype(BF16)
            v_idx = np.full((A_HEADS * HEAD_LANES,), -1, np.int64)
            for hd in range(A_HEADS):
                v_idx[hd * HEAD_LANES:hd * HEAD_LANES + V_DIM] = hd * (NOPE_DIM + V_DIM) + NOPE_DIM + np.arange(V_DIM)
            wuv = _gather_cols(w_ukv[j], v_idx).astype(BF16)
            bound = _score_bound_log2(q_norm[j], k_norm[j])
            koff = jnp.zeros((1, HEAD_LANES), F32).at[0, SCORE_PAD_LANE].set(-bound)
            gain_idx = _head_col_index(QK_DIM, 0, QK_DIM)[:HEAD_LANES]
            lane_consts = jnp.concatenate(
                [_gather_cols(g[None, :], idx) for g in (q_norm[j], k_norm[j])
                 for idx in (gain_idx, _partner_lanes(gain_idx))]
                + [koff, invf, jnp.zeros((2, HEAD_LANES), F32)], axis=0)
            q, k, v, z, xbc, misc = _front_even(
                x, pos, lane_consts, row2(ln_mix[layer]), win, row2(q_lat_norm[j]), wuq, wuq_p,
                row2(kv_lat_norm[j]), wuk, wuv, tm)
            attn = lax.cond(bound <= FAST_SOFTMAX_MAX_LOG2,
                            functools.partial(_attention, tq=tq, online=False),
                            functools.partial(_attention, tq=tq, online=True), q, k, v)
            y = _ssd(xbc, misc, z, conv_w[j], row2(conv_b[j]), lane_pad(dt_bias[j]), lane_pad(a_log[j]),
                     row2(jnp.repeat(d_skip[j], SSD_HEAD_DIM)), row2(ssd_norm[j]))
            half = A_HEADS * V_DIM
            wout = w_out[j].astype(BF16)
            for k, st in enumerate(streams):
                row_k = lambda i, jj, off=st["off"]: (i + off, jj, 0)
                front_args = [x, attn, y, wout]
                front_specs = [pl.BlockSpec((1, tm, d), row_k), pl.BlockSpec((1, tm, half), row_k),
                               pl.BlockSpec((1, tm, D_INNER), row_k), pl.BlockSpec((half + D_INNER, d), fixed2)]
                posts.append(_post("even", front_args, front_specs, tail_args, nb, s, tm, [], b0=k * nb))
        else:
            for k, st in enumerate(streams):
                row_k = lambda i, jj, off=st["off"]: (i + off, jj, 0)
                front_args, front_specs = [st["x"]], [pl.BlockSpec((1, tm, d), row_k)]
                if st["pending"] is not None:
                    route_prev, ytoks = st["pending"]
                    pw = ytoks[0].shape[1]
                    front_args += [route_prev]
                    front_specs += [pl.BlockSpec((1, tm, ROUTE_LANES), row)]
                    for ytok in ytoks:
                        front_args += [ytok, ytok]
                        front_specs += [pl.BlockSpec((tm, pw), lambda i, jj: (i * nt + jj, 0)),
                                        pl.BlockSpec((tm, pw), lambda i, jj: (i * nt + jj + nb * nt, 0))]
                front_args += [row2(ln_mix[layer]), pool_w[j].astype(BF16), row2(pool_b[j]), row2(pool_scale[j])]
                front_specs += [pl.BlockSpec((1, d), fixed2),
                                pl.BlockSpec((len(POOL_WINDOWS), POOL_GROUP, POOL_GROUP), lambda i, jj: (0, 0, 0)),
                                pl.BlockSpec((1, d), fixed2), pl.BlockSpec((1, d), fixed2)]
                posts.append(_post("pool" if st["pending"] is None else "pool_pending", front_args, front_specs,
                                   tail_args, nb, s, tm, [pltpu.VMEM((POOL_CARRY, d), F32)], b0=k * nb))
        streams = []
        for x2, *hfps, route, route_t, counts in posts:
            ytoks = _moe(nb * s, hfps, route_t, counts, exp_w_gate, exp_w_up, exp_w_down, layer)
            streams.append(dict(x=x2, off=0, pending=(route, ytoks)))
        if layer + 1 == depth or (layer + 1) % 2 == 0:
            out = None
            for k, st in enumerate(streams):
                out = _apply_moe(st["x"], *st["pending"], out, k * nb, b)
            streams = [dict(x=out, off=k * nb, pending=None) for k in range(n_streams)]
    return streams[0]["x"]
```

```python
import functools

import numpy as np
import jax
import jax.numpy as jnp
from jax import lax
from jax.experimental import pallas as pl
from jax.experimental.pallas import tpu as pltpu
from jax.experimental.pallas import tpu_sc as plsc

F32 = jnp.float32
BF16 = jnp.bfloat16
U32 = jnp.uint32
I32 = jnp.int32

RMS_EPS = 1e-6
ROPE_THETA = 10000.0

D_MODEL = 1024
X_HEADS, X_HEAD_DIM = 4, 64
A_HEADS, NOPE_DIM, ROPE_DIM, V_DIM = 8, 64, 32, 64
QK_DIM = NOPE_DIM + ROPE_DIM
Q_LORA, KV_LORA = 256, 128
B_HEADS, SSD_HEAD_DIM, SSD_GROUPS, SSD_STATE, CONV_K, CHUNK = 8, 64, 2, 128, 4, 128
D_INNER = B_HEADS * SSD_HEAD_DIM
CONV_CH = D_INNER + 2 * SSD_GROUPS * SSD_STATE
POOL_WINDOWS = (2, 4, 8, 16)
POOL_GROUP = D_MODEL // 4
MOE_GROUPS, EXPERTS_PER_GROUP, TOP_K, EXPERT_FF = 4, 8, 2, 256
N_EXPERTS = MOE_GROUPS * EXPERTS_PER_GROUP

LANES = 128
HEAD_LANES = LANES
HALF_LANES = LANES // 2
ROPE_HALF = ROPE_DIM // 2
NOPE_HALF = NOPE_DIM // 2
POOL_CARRY = 16
CONV_CARRY = 8
VMEM_LIMIT = 56 * 1024 * 1024


def _cparams(sem):
    return pltpu.CompilerParams(dimension_semantics=sem, vmem_limit_bytes=VMEM_LIMIT)


def _rms(u, g):
    return u * lax.rsqrt(jnp.mean(u * u, axis=-1, keepdims=True) + RMS_EPS) * g


def _sigmoid(u):
    return 1.0 / (1.0 + jnp.exp(-u))


def _dot(a, b):
    return jnp.dot(a, b, preferred_element_type=F32)


def _dot_nt(a, b):
    return lax.dot_general(a, b, (((1,), (1,)), ((), ())), preferred_element_type=F32)


def _head_lane(d):
    if d < NOPE_HALF:
        return d
    if d < NOPE_DIM:
        return HALF_LANES + (d - NOPE_HALF)
    r = d - NOPE_DIM
    if r < ROPE_HALF:
        return NOPE_HALF + r
    return HALF_LANES + NOPE_HALF + (r - ROPE_HALF)


def _gather_cols(w, idx):
    w_ext = jnp.concatenate([w, jnp.zeros(w.shape[:-1] + (1,), w.dtype)], axis=-1)
    idx = np.where(idx < 0, w.shape[-1], idx)
    return jnp.take(w_ext, jnp.asarray(idx, dtype=jnp.int32), axis=-1)


IN_W = 2 * D_MODEL
_OFF_QLAT, _OFF_KVLAT, _OFF_MISC, _OFF_Z, _OFF_XBC = 0, 256, 384, 512, 1024


def _win_col_index():
    idx = np.full((IN_W,), -1, np.int64)
    idx[_OFF_QLAT:_OFF_QLAT + Q_LORA] = np.arange(Q_LORA)
    idx[_OFF_KVLAT:_OFF_KVLAT + KV_LORA] = Q_LORA + np.arange(KV_LORA)
    rope0 = Q_LORA + KV_LORA
    for r in range(ROPE_DIM):
        idx[_OFF_MISC + _head_lane(NOPE_DIM + r)] = rope0 + r
    z0 = rope0 + ROPE_DIM
    idx[_OFF_Z:_OFF_Z + D_INNER] = z0 + np.arange(D_INNER)
    xbc0 = z0 + D_INNER
    idx[_OFF_XBC:_OFF_XBC + CONV_CH] = xbc0 + np.arange(CONV_CH)
    dt0 = xbc0 + CONV_CH
    idx[_OFF_MISC:_OFF_MISC + B_HEADS] = dt0 + np.arange(B_HEADS)
    return idx


def _head_col_index(per_head, offset, count):
    idx = np.full((A_HEADS * HEAD_LANES,), -1, np.int64)
    for h in range(A_HEADS):
        for d in range(count):
            idx[h * HEAD_LANES + _head_lane(d)] = h * per_head + offset + d
    return idx


SCORE_PAD_LANE = NOPE_HALF + ROPE_HALF
ONES_LANE = V_DIM
LOG2E = 1.4426950408889634


def _front_even_kernel(x_ref, pos_ref, lc_ref, ln_ref, win_ref, qln_ref, wuq_ref, wuqr_ref, kvln_ref, wuk_ref, wuv_ref,
                       q_ref, k_ref, v_ref, z_ref, xbc_ref, misc_ref):
    x = x_ref[0]
    h = _rms(x, ln_ref[...]).astype(BF16)
    proj = _dot(h, win_ref[...])
    misc = proj[:, _OFF_MISC:_OFF_Z]
    z_ref[0] = proj[:, _OFF_Z:_OFF_XBC].astype(BF16)
    xbc_ref[0] = proj[:, _OFF_XBC:].astype(BF16)
    misc_ref[0] = misc
    ql = _rms(proj[:, _OFF_QLAT:_OFF_KVLAT], qln_ref[...]).astype(BF16)
    kvl = _rms(proj[:, _OFF_KVLAT:_OFF_MISC], kvln_ref[...]).astype(BF16)
    q = _dot(ql, wuq_ref[...])
    kn = _dot(kvl, wuk_ref[...])
    v = _dot(kvl, wuv_ref[...])
    lane = lax.broadcasted_iota(I32, (1, HEAD_LANES), 1)
    first_half = (lane >= NOPE_HALF) & (lane < NOPE_HALF + ROPE_HALF)
    second_half = (lane >= HALF_LANES + NOPE_HALF) & (lane < HALF_LANES + NOPE_HALF + ROPE_HALF)
    lc = lc_ref[...]
    qg, qg_p, kg, kg_p, k_off, invf = (lc[i:i + 1] for i in range(6))
    ones = jnp.ones((HEAD_LANES, HEAD_LANES), BF16)

    def lane_sumsq(u):
        return _dot((u * u).astype(BF16), ones)

    krope = jnp.where(first_half | second_half, misc, 0.0)
    kr_ss = lane_sumsq(krope)
    ang = pos_ref[0] * invf
    cos_t = jnp.cos(ang)
    sin_t = jnp.where(first_half, -jnp.sin(ang), jnp.sin(ang))
    q_one = (lane == SCORE_PAD_LANE).astype(F32)
    v_one = (lane == ONES_LANE).astype(F32)
    q_scale = QK_DIM ** -0.5 * LOG2E
    qa, qb = qg * cos_t * q_scale, qg_p * sin_t * q_scale
    ka, kc = kg * cos_t, pltpu.roll(krope, HALF_LANES, 1) * (kg_p * sin_t)
    q_p = _dot(ql, wuqr_ref[...])
    for hd in range(A_HEADS):
        sl = slice(hd * HEAD_LANES, (hd + 1) * HEAD_LANES)
        qs = q[:, sl]
        inv = lax.rsqrt(lane_sumsq(qs) * (1.0 / QK_DIM) + RMS_EPS)
        q_ref[0, hd] = ((qs * qa + q_p[:, sl] * qb) * inv + q_one).astype(BF16)
        kns = kn[:, sl]
        inv = lax.rsqrt((lane_sumsq(kns) + kr_ss) * (1.0 / QK_DIM) + RMS_EPS)
        k_ref[0, hd] = (((kns + krope) * ka + kc) * inv + k_off).astype(BF16)
        v_ref[0, hd] = (v[:, sl] + v_one).astype(BF16)


def _front_even(x, pos, lane_consts, ln, win, qln, wuq, wuq_p, kvln, wuk, wuv, tm):
    b, s, d = x.shape
    grid = (b, s // tm)
    row = lambda i, j: (i, j, 0)
    fixed2 = lambda i, j: (0, 0)
    head_row = lambda i, j: (i, 0, j, 0)
    hw = A_HEADS * HEAD_LANES
    return pl.pallas_call(
        _front_even_kernel,
        grid=grid,
        in_specs=[
            pl.BlockSpec((1, tm, d), row),
            pl.BlockSpec((1, tm, 1), row),
            pl.BlockSpec((8, HEAD_LANES), fixed2),
            pl.BlockSpec((1, d), fixed2),
            pl.BlockSpec((d, IN_W), fixed2),
            pl.BlockSpec((1, Q_LORA), fixed2),
            pl.BlockSpec((Q_LORA, hw), fixed2),
            pl.BlockSpec((Q_LORA, hw), fixed2),
            pl.BlockSpec((1, KV_LORA), fixed2),
            pl.BlockSpec((KV_LORA, hw), fixed2),
            pl.BlockSpec((KV_LORA, hw), fixed2),
        ],
        out_specs=[
            pl.BlockSpec((1, A_HEADS, tm, HEAD_LANES), head_row),
            pl.BlockSpec((1, A_HEADS, tm, HEAD_LANES), head_row),
            pl.BlockSpec((1, A_HEADS, tm, HEAD_LANES), head_row),
            pl.BlockSpec((1, tm, D_INNER), row),
            pl.BlockSpec((1, tm, CONV_CH), row),
            pl.BlockSpec((1, tm, HEAD_LANES), row),
        ],
        out_shape=[
            jax.ShapeDtypeStruct((b, A_HEADS, s, HEAD_LANES), BF16),
            jax.ShapeDtypeStruct((b, A_HEADS, s, HEAD_LANES), BF16),
            jax.ShapeDtypeStruct((b, A_HEADS, s, HEAD_LANES), BF16),
            jax.ShapeDtypeStruct((b, s, D_INNER), BF16),
            jax.ShapeDtypeStruct((b, s, CONV_CH), BF16),
            jax.ShapeDtypeStruct((b, s, HEAD_LANES), F32),
        ],
        compiler_params=_cparams(("parallel", "parallel")),
        name="front_even",
    )(x, pos, lane_consts, ln, win, qln, wuq, wuq_p, kvln, wuk, wuv)


HEADS_PER_STEP = 8


def _attn_kernel(q_ref, k_ref, v_ref, o_ref, *, tq, online):
    qi = pl.program_id(2)
    row = lax.broadcasted_iota(I32, (tq, tq), 0)
    col = lax.broadcasted_iota(I32, (tq, tq), 1)

    def head_step(hh, j, carry, masked):
        kj = k_ref[0, hh, pl.ds(j * tq, tq), :]
        vj = v_ref[0, hh, pl.ds(j * tq, tq), :]
        s = _dot_nt(q_ref[0, hh], kj)
        if masked:
            s = jnp.where(row >= col, s, -jnp.inf)
        if online:
            m, acc = carry
            m_new = jnp.maximum(m, jnp.max(s, axis=-1, keepdims=True))
            p = jnp.exp2(s - m_new).astype(BF16)
            return m_new, jnp.exp2(m - m_new) * acc + _dot(p, vj)
        return carry + _dot(jnp.exp2(s).astype(BF16), vj)

    def step(j, carries, masked):
        return tuple(head_step(hh, j, carries[hh], masked) for hh in range(HEADS_PER_STEP))

    acc0 = jnp.zeros((tq, HEAD_LANES), F32)
    init = (jnp.full((tq, 1), -jnp.inf, F32), acc0) if online else acc0
    carries = lax.fori_loop(0, qi, functools.partial(step, masked=False), (init,) * HEADS_PER_STEP)
    carries = step(qi, carries, True)
    outs = []
    for carry in carries:
        acc = carry[1] if online else carry
        outs.append(acc / acc[:, ONES_LANE:ONES_LANE + 1])
    lane = lax.broadcasted_iota(I32, (1, HEAD_LANES), 1)
    per_group = HEAD_LANES // V_DIM
    groups = []
    for g0 in range(0, HEADS_PER_STEP, per_group):
        out = outs[g0]
        for hh in range(1, per_group):
            out = jnp.where(lane >= hh * V_DIM, pltpu.roll(outs[g0 + hh], hh * V_DIM, 1), out)
        groups.append(out.astype(BF16))
    o_ref[0] = jnp.concatenate(groups, axis=1)


def _attention(q, k, v, tq, online):
    b, nh, s, _ = q.shape
    grid = (b, nh // HEADS_PER_STEP, s // tq)
    kv_spec = pl.BlockSpec((1, HEADS_PER_STEP, s, HEAD_LANES), lambda i, h, j: (i, h, 0, 0))
    return pl.pallas_call(
        functools.partial(_attn_kernel, tq=tq, online=online),
        grid=grid,
        in_specs=[
            pl.BlockSpec((1, HEADS_PER_STEP, tq, HEAD_LANES), lambda i, h, j: (i, h, j, 0)),
            kv_spec,
            kv_spec,
        ],
        out_specs=pl.BlockSpec((1, tq, HEADS_PER_STEP * V_DIM), lambda i, h, j: (i, j, h)),
        out_shape=jax.ShapeDtypeStruct((b, s, nh * V_DIM), BF16),
        compiler_params=_cparams(("parallel", "parallel", "parallel")),
        name="mla_attention_online" if online else "mla_attention",
    )(q, k, v)


def _ssd_kernel(xbc_ref, misc_ref, z_ref, cw_ref, cb_ref, dtb_ref, alog_ref, dskip_ref, gn_ref, y_ref,
                state_ref, carry_ref):
    c = pl.program_id(1)
    t = CHUNK
    rows = y_ref.shape[1]

    @pl.when(c == 0)
    def _():
        state_ref[...] = jnp.zeros_like(state_ref)
        carry_ref[...] = jnp.zeros_like(carry_ref)

    xr = xbc_ref[0].astype(F32)
    xcat = jnp.concatenate([carry_ref[...], xr], axis=0)
    carry_ref[...] = xr[rows - CONV_CARRY:, :]
    conv = jnp.zeros((rows, CONV_CH), F32) + cb_ref[...]
    for kk in range(CONV_K):
        sh = CONV_K - 1 - kk
        shifted = xcat if sh == 0 else pltpu.roll(xcat, sh, 0)
        conv = conv + cw_ref[kk:kk + 1, :] * shifted[CONV_CARRY:, :]
    xa = conv * _sigmoid(conv)
    xs = xa[:, :D_INNER]
    gw = SSD_GROUPS * SSD_STATE
    bmat = xa[:, D_INNER:D_INNER + gw]
    cmat = xa[:, D_INNER + gw:]

    u = misc_ref[0] + dtb_ref[...]
    dt = jnp.maximum(u, 0.0) + jnp.log(1.0 + jnp.exp(-jnp.abs(u)))
    a = -jnp.exp(alog_ref[...])
    lane = lax.broadcasted_iota(I32, (1, LANES), 1)
    adt_all = jnp.where(lane < B_HEADS, dt * a, 0.0)
    rowi = lax.broadcasted_iota(I32, (t, LANES), 0)
    tri = lax.broadcasted_iota(I32, (t, t), 0) >= lax.broadcasted_iota(I32, (t, t), 1)
    rep = B_HEADS // SSD_GROUPS
    y_chunks = []
    for ci in range(rows // t):
        sl = slice(ci * t, (ci + 1) * t)
        acs = adt_all[sl]
        sh = 1
        while sh < t:
            acs = acs + jnp.where(rowi >= sh, pltpu.roll(acs, sh, 0), 0.0)
            sh *= 2
        acs_t = acs.T
        ys = []
        for g in range(SSD_GROUPS):
            bg = bmat[sl, g * SSD_STATE:(g + 1) * SSD_STATE]
            cg = cmat[sl, g * SSD_STATE:(g + 1) * SSD_STATE]
            bg16, cg16 = bg.astype(BF16), cg.astype(BF16)
            cb = _dot_nt(cg16, bg16)
            bg_t = bg.T
            for r in range(rep):
                hd = g * rep + r
                col = acs[:, hd:hd + 1]
                rw = acs_t[hd:hd + 1, :]
                last = acs_t[hd:hd + 1, t - 1:t]
                decay = jnp.exp(jnp.where(tri, col - rw, -jnp.inf))
                xh = xs[sl, hd * SSD_HEAD_DIM:(hd + 1) * SSD_HEAD_DIM]
                xdt = (xh * dt[sl, hd:hd + 1]).astype(BF16)
                y_diag = _dot((cb * decay).astype(BF16), xdt)
                prev = state_ref[hd]
                y_off = _dot(cg16, prev.astype(BF16)) * jnp.exp(col)
                new_state = _dot((bg_t * jnp.exp(last - rw)).astype(BF16), xdt)
                state_ref[hd] = prev * jnp.exp(last) + new_state
                ys.append(y_diag + y_off)
        y_chunks.append(jnp.concatenate(ys, axis=1))
    y = jnp.concatenate(y_chunks, axis=0) + xs * dskip_ref[...]
    zf = z_ref[0].astype(F32)
    y = y * (zf * _sigmoid(zf))
    y_ref[0] = _rms(y, gn_ref[...]).astype(BF16)


SSD_CHUNKS_PER_STEP = 4


def _ssd(xbc, misc, z, cw, cb, dtb, alog, dskip, gn):
    b, s, _ = xbc.shape
    rows = CHUNK * SSD_CHUNKS_PER_STEP if s % (CHUNK * SSD_CHUNKS_PER_STEP) == 0 else CHUNK
    grid = (b, s // rows)
    row = lambda i, j: (i, j, 0)
    fixed2 = lambda i, j: (0, 0)
    return pl.pallas_call(
        _ssd_kernel,
        grid=grid,
        in_specs=[
            pl.BlockSpec((1, rows, CONV_CH), row),
            pl.BlockSpec((1, rows, LANES), row),
            pl.BlockSpec((1, rows, D_INNER), row),
            pl.BlockSpec((CONV_K, CONV_CH), fixed2),
            pl.BlockSpec((1, CONV_CH), fixed2),
            pl.BlockSpec((1, LANES), fixed2),
            pl.BlockSpec((1, LANES), fixed2),
            pl.BlockSpec((1, D_INNER), fixed2),
            pl.BlockSpec((1, D_INNER), fixed2),
        ],
        out_specs=pl.BlockSpec((1, rows, D_INNER), row),
        out_shape=jax.ShapeDtypeStruct((b, s, D_INNER), BF16),
        scratch_shapes=[
            pltpu.VMEM((B_HEADS, SSD_STATE, SSD_HEAD_DIM), F32),
            pltpu.VMEM((CONV_CARRY, CONV_CH), F32),
        ],
        compiler_params=_cparams(("parallel", "arbitrary")),
        name="ssd_scan",
    )(xbc, misc, z, cw, cb, dtb, alog, dskip, gn)


XW = X_HEADS * X_HEAD_DIM


def _mem_kv_kernel(mem_ref, ln_ref, wkv_ref, kg_ref, hsum_ref, kbd_ref, vbd_ref):
    m = mem_ref.shape[1]
    mn = _rms(mem_ref[0], ln_ref[...]).astype(BF16)
    kv = _dot(mn, wkv_ref[...])
    k, v = kv[:, :XW], kv[:, XW:]
    ss = _dot((k * k).astype(BF16), hsum_ref[...])
    kn = (k * lax.rsqrt(ss * (1.0 / X_HEAD_DIM) + RMS_EPS) * kg_ref[...]).astype(BF16)
    v16 = v.astype(BF16)
    head_of_lane = lax.shift_right_arithmetic(lax.broadcasted_iota(I32, (1, XW), 1), jnp.int32(_LOG2_XHD))
    for hd in range(X_HEADS):
        keep = head_of_lane == hd
        kbd_ref[0, hd * m:(hd + 1) * m, :] = jnp.where(keep, kn, jnp.zeros_like(kn))
        vbd_ref[0, hd * m:(hd + 1) * m, :] = jnp.where(keep, v16, jnp.zeros_like(v16))


def _mem_kv(mem, ln, wkv, kg, hsum):
    b, m, d = mem.shape
    fixed2 = lambda i: (0, 0)
    return pl.pallas_call(
        _mem_kv_kernel,
        grid=(b,),
        in_specs=[
            pl.BlockSpec((1, m, d), lambda i: (i, 0, 0)),
            pl.BlockSpec((1, d), fixed2),
            pl.BlockSpec((d, 2 * XW), fixed2),
            pl.BlockSpec((1, XW), fixed2),
            pl.BlockSpec((XW, XW), fixed2),
        ],
        out_specs=[
            pl.BlockSpec((1, X_HEADS * m, XW), lambda i: (i, 0, 0)),
            pl.BlockSpec((1, X_HEADS * m, XW), lambda i: (i, 0, 0)),
        ],
        out_shape=[
            jax.ShapeDtypeStruct((b, X_HEADS * m, XW), BF16),
            jax.ShapeDtypeStruct((b, X_HEADS * m, XW), BF16),
        ],
        compiler_params=_cparams(("parallel",)),
        name="mem_kv",
    )(mem, ln, wkv, kg, hsum)


ROUTE_LANES = LANES
ROW_PARTS = 2
_GROUP_LANE0 = 0
_EXPERT_LANE0 = MOE_GROUPS
_LOG2_EPG = EXPERTS_PER_GROUP.bit_length() - 1
_LOG2_XHD = X_HEAD_DIM.bit_length() - 1


def _pack_bf16_pairs(v):
    w = v.shape[1] // 2
    r = v.astype(BF16).astype(F32)
    hi = lax.bitcast_convert_type(r[:, :w], U32)
    lo = lax.bitcast_convert_type(r[:, w:], U32)
    return (hi & jnp.uint32(0xFFFF0000)) | (lo >> jnp.uint32(16))


def _unpack_bf16_pairs(u):
    hi = lax.bitcast_convert_type(u & jnp.uint32(0xFFFF0000), F32)
    lo = lax.bitcast_convert_type(u << jnp.uint32(16), F32)
    return hi, lo


TAIL_SPLIT = 1


def _tail_rows(x1, rows, kbd_ref, vbd_ref, lnq_ref, wq_ref, qg_ref, hsum_ref, wo_ref, lnf_ref, rwh_ref, rwl_ref,
               rb_ref, x2_ref, hfp_refs):
    tm = x1.shape[0]
    m = kbd_ref.shape[1] // X_HEADS
    hq = _rms(x1, lnq_ref[...]).astype(BF16)
    q = _dot(hq, wq_ref[...])
    ss = _dot((q * q).astype(BF16), hsum_ref[...])
    qn = (q * lax.rsqrt(ss * (1.0 / X_HEAD_DIM) + RMS_EPS) * qg_ref[...] * (X_HEAD_DIM ** -0.5)).astype(BF16)
    s = _dot_nt(qn, kbd_ref[0])
    ps = []
    for hd in range(X_HEADS):
        sh = s[:, hd * m:(hd + 1) * m]
        e = jnp.exp(sh - jnp.max(sh, axis=-1, keepdims=True))
        ps.append((e / jnp.sum(e, axis=-1, keepdims=True)).astype(BF16))
    o = _dot(jnp.concatenate(ps, axis=1), vbd_ref[0]).astype(BF16)
    x2 = x1 + _dot(o, wo_ref[...])
    x2_ref[0, rows, :] = x2

    hf = _rms(x2, lnf_ref[...])
    hf_hi = hf.astype(BF16)
    packed = _pack_bf16_pairs(hf)
    pw = packed.shape[1] // ROW_PARTS
    for c in range(ROW_PARTS):
        hfp_refs[c][0, rows, :] = packed[:, c * pw:(c + 1) * pw]
    hf_lo = (hf - hf_hi.astype(F32)).astype(BF16)
    logits = _dot(hf_hi, rwh_ref[...]) + _dot(hf_hi, rwl_ref[...]) + _dot(hf_lo, rwh_ref[...]) + rb_ref[...]

    lane_i = lax.broadcasted_iota(I32, (tm, ROUTE_LANES), 1)
    lane = lane_i.astype(F32)
    big = float(ROUTE_LANES)
    neg = -jnp.inf
    gl = jnp.where(lane_i < MOE_GROUPS, logits, neg)
    gmax = jnp.max(gl, axis=-1, keepdims=True)
    gsum = jnp.sum(jnp.exp(gl - gmax), axis=-1, keepdims=True)
    g_p = 1.0 / gsum
    g_idx = jnp.min(jnp.where(gl == gmax, lane, big), axis=-1, keepdims=True)
    e_lane = lane_i - _EXPERT_LANE0
    grp_of_lane = lax.shift_right_arithmetic(e_lane, jnp.int32(_LOG2_EPG)).astype(F32)
    in_grp = (e_lane >= 0) & (e_lane < N_EXPERTS) & (grp_of_lane == g_idx)
    el = jnp.where(in_grp, logits, neg)
    emax = jnp.max(el, axis=-1, keepdims=True)
    idx1 = jnp.min(jnp.where(el == emax, lane, big), axis=-1, keepdims=True)
    el2 = jnp.where(lane == idx1, neg, el)
    emax2 = jnp.max(el2, axis=-1, keepdims=True)
    idx2 = jnp.min(jnp.where(el2 == emax2, lane, big), axis=-1, keepdims=True)
    r2 = jnp.exp(emax2 - emax)
    gate1 = g_p / (1.0 + r2)
    gate2 = g_p * r2 / (1.0 + r2)
    e1 = idx1 - float(_EXPERT_LANE0)
    e2 = idx2 - float(_EXPERT_LANE0)

    route = jnp.where(lane == 0, e1, 0.0)
    route = jnp.where(lane == 1, e2, route)
    route = jnp.where(lane == 2, gate1, route)
    route = jnp.where(lane == 3, gate2, route)
    return route, (lane == e1).astype(F32), (lane == e2).astype(F32)


def _tail(x1, kbd_ref, vbd_ref, lnq_ref, wq_ref, qg_ref, hsum_ref, wo_ref, lnf_ref, rwh_ref, rwl_ref, rb_ref,
          ltri_ref, x2_ref, *out_refs):
    hfp_refs, (route_ref, route_t_ref, cnt_ref) = out_refs[:ROW_PARTS], out_refs[ROW_PARTS:]
    tm = x1.shape[0]
    tr = tm // TAIL_SPLIT
    parts = [_tail_rows(x1[r * tr:(r + 1) * tr], slice(r * tr, (r + 1) * tr), kbd_ref, vbd_ref, lnq_ref, wq_ref,
                        qg_ref, hsum_ref, wo_ref, lnf_ref, rwh_ref, rwl_ref, rb_ref, x2_ref, hfp_refs)
             for r in range(TAIL_SPLIT)]
    route, oh1, oh2 = (jnp.concatenate([p[i] for p in parts], axis=0) for i in range(3))
    both = oh1 + oh2
    before = _dot(ltri_ref[...], both.astype(BF16))
    rank1 = jnp.sum(before * oh1, axis=-1, keepdims=True)
    rank2 = jnp.sum(before * oh2, axis=-1, keepdims=True)
    cnt_ref[0] = jnp.broadcast_to(jnp.sum(both, axis=0, keepdims=True), cnt_ref.shape[1:])
    lane = lax.broadcasted_iota(I32, (tm, ROUTE_LANES), 1)
    route = jnp.where(lane == 4, rank1, route)
    route = jnp.where(lane == 5, rank2, route)
    route_ref[0] = route
    route_t_ref[0] = route.T[:route_t_ref.shape[1], :]


_TAIL_IN = 12


def _post_even_kernel(x_ref, a_ref, y_ref, wout_ref, *rest):
    tail_in, outs = rest[:_TAIL_IN], rest[_TAIL_IN:]
    half = wout_ref.shape[0] // 2
    x1 = x_ref[0] + _dot(a_ref[0], wout_ref[:half, :]) + _dot(y_ref[0], wout_ref[half:, :])
    _tail(x1, *tail_in, *outs)


def _add_expert_rows(x, route, ys):
    g1, g2 = route[:, 2:3], route[:, 3:4]
    his, los = [], []
    for y1, y2 in ys:
        h1, l1 = _unpack_bf16_pairs(y1)
        h2, l2 = _unpack_bf16_pairs(y2)
        his.append(h1 * g1 + h2 * g2)
        los.append(l1 * g1 + l2 * g2)
    return x + jnp.concatenate(his + los, axis=1)


N_PENDING = 1 + 2 * ROW_PARTS


def _post_pool_kernel(x_ref, *rest, pending):
    if pending:
        route_prev_ref, y_refs, rest = rest[0], rest[1:N_PENDING], rest[N_PENDING:]
    (ln_ref, pw_ref, pb_ref, ps_ref), rest = rest[:4], rest[4:]
    tail_in, outs, carry_ref = rest[:_TAIL_IN], rest[_TAIL_IN:-1], rest[-1]
    j = pl.program_id(1)
    tm = x_ref.shape[1]

    @pl.when(j == 0)
    def _():
        carry_ref[...] = jnp.zeros_like(carry_ref)

    x = x_ref[0]
    if pending:
        x = _add_expert_rows(x, route_prev_ref[0], [(y_refs[2 * c][...], y_refs[2 * c + 1][...])
                                                    for c in range(ROW_PARTS)])
    h = _rms(x, ln_ref[...])
    pos = (j * tm + 1 + lax.broadcasted_iota(I32, (tm, 1), 0)).astype(F32)
    mixed = []
    for g, w in enumerate(POOL_WINDOWS):
        sl = slice(g * POOL_GROUP, (g + 1) * POOL_GROUP)
        hg = h[:, sl]
        acc = jnp.concatenate([carry_ref[:, sl], hg], axis=0)
        sh = 1
        while sh < w:
            acc = acc + pltpu.roll(acc, sh, 0)
            sh *= 2
        win = acc[POOL_CARRY:, :]
        dlt = win / jnp.minimum(pos, float(w)) - hg
        mixed.append(_dot(dlt.astype(BF16), pw_ref[g]))
    carry_ref[...] = h[tm - POOL_CARRY:, :]
    y = (jnp.concatenate(mixed, axis=1) + pb_ref[...]) * ps_ref[...]
    _tail(x + y, *tail_in, *outs)


def _post(kind, front_args, front_specs, tail_args, b, s, tm, scratch, b0=0):
    d = D_MODEL
    m4 = tail_args[0].shape[1]
    row = lambda i, j: (i, j, 0)
    fixed2 = lambda i, j: (0, 0)
    per_b = lambda i, j: (i + b0, 0, 0)
    tail_specs = [
        pl.BlockSpec((1, m4, XW), per_b),
        pl.BlockSpec((1, m4, XW), per_b),
        pl.BlockSpec((1, d), fixed2),
        pl.BlockSpec((d, XW), fixed2),
        pl.BlockSpec((1, XW), fixed2),
        pl.BlockSpec((XW, XW), fixed2),
        pl.BlockSpec((XW, d), fixed2),
        pl.BlockSpec((1, d), fixed2),
        pl.BlockSpec((d, ROUTE_LANES), fixed2),
        pl.BlockSpec((d, ROUTE_LANES), fixed2),
        pl.BlockSpec((1, ROUTE_LANES), fixed2),
        pl.BlockSpec((tm, tm), fixed2),
    ]
    nt = s // tm
    pw = d // 2 // ROW_PARTS
    kernel = {"even": _post_even_kernel,
              "pool": functools.partial(_post_pool_kernel, pending=False),
              "pool_pending": functools.partial(_post_pool_kernel, pending=True)}[kind]
    return pl.pallas_call(
        kernel,
        grid=(b, nt),
        in_specs=front_specs + tail_specs,
        out_specs=[pl.BlockSpec((1, tm, d), row)]
        + [pl.BlockSpec((1, tm, pw), row)] * ROW_PARTS
        + [pl.BlockSpec((1, tm, ROUTE_LANES), row),
           pl.BlockSpec((1, 8, tm), lambda i, j: (i * nt + j, 0, 0)),
           pl.BlockSpec((1, 8, ROUTE_LANES), lambda i, j: (i * nt + j, 0, 0))],
        out_shape=[jax.ShapeDtypeStruct((b, s, d), F32)]
        + [jax.ShapeDtypeStruct((b, s, pw), U32)] * ROW_PARTS
        + [jax.ShapeDtypeStruct((b, s, ROUTE_LANES), F32),
           jax.ShapeDtypeStruct((b * nt, 8, tm), F32),
           jax.ShapeDtypeStruct((b * nt, 8, ROUTE_LANES), F32)],
        scratch_shapes=scratch,
        compiler_params=_cparams(("parallel", "arbitrary")),
        name="post_" + kind,
    )(*front_args, *tail_args)


FFN_ROWS = 512
COMBINE_TOKENS = 512
MOE_STREAMS = 2
SC_GATHER_WINDOW = 128


def _sc_gather_rows(table, idx):
    m, w = idx.shape[0], table.shape[1]
    mesh = plsc.VectorSubcoreMesh(core_axis_name="core", subcore_axis_name="subcore")

    @pl.kernel(out_type=jax.ShapeDtypeStruct((m, w), table.dtype), mesh=mesh, name="moe_row_gather")
    def gather(t_hbm, i_hbm, o_hbm):
        def body(i_vmem, o_vmem):
            pltpu.sync_copy(t_hbm.at[i_vmem.at[0]], o_vmem)

        pltpu.emit_pipeline(
            body,
            grid=(m // SC_GATHER_WINDOW,),
            in_specs=[pl.BlockSpec((1, SC_GATHER_WINDOW), lambda i: (0, i))],
            out_specs=[pl.BlockSpec((SC_GATHER_WINDOW, w), lambda i: (i, 0))],
            core_axis_name=("core", "subcore"),
            dimension_semantics=(pltpu.PARALLEL,),
        )(i_hbm, o_hbm)

    return gather(table, idx.reshape(1, m))


def _sc_scatter_rows(src, dests, pad_rows, n_rows):
    n, w = src.shape
    win = SC_GATHER_WINDOW
    mesh = plsc.VectorSubcoreMesh(core_axis_name="core", subcore_axis_name="subcore")
    idx_spec = pl.BlockSpec((1, win), lambda i: (0, i))
    split = dict(core_axis_name=("core", "subcore"), dimension_semantics=(pltpu.PARALLEL,))

    @pl.kernel(out_type=jax.ShapeDtypeStruct((n_rows, w), src.dtype), mesh=mesh, name="moe_row_scatter")
    def scatter(s_hbm, z_hbm, p_hbm, *rest):
        d_hbms, o_hbm = rest[:-1], rest[-1]

        def body(s_vmem, *i_vmems):
            for i_vmem in i_vmems:
                pltpu.sync_copy(s_vmem, o_hbm.at[i_vmem.at[0]])

        pltpu.emit_pipeline(
            body, grid=(n // win,),
            in_specs=[pl.BlockSpec((win, w), lambda i: (i, 0))] + [idx_spec] * len(dests),
            out_specs=[], **split)(s_hbm, *d_hbms)

        def zero_body(z_vmem, i_vmem):
            pltpu.sync_copy(z_vmem, o_hbm.at[i_vmem.at[0]])

        pltpu.emit_pipeline(
            zero_body, grid=(pad_rows.shape[0] // win,),
            in_specs=[pl.BlockSpec((win, w), lambda i: (0, 0)), idx_spec],
            out_specs=[], **split)(z_hbm, p_hbm)

    zeros = jnp.zeros((win, w), src.dtype)
    return scatter(src, zeros, pad_rows.reshape(1, -1), *[dd.reshape(1, n) for dd in dests])


def _ffn_kernel(be_ref, bi_ref, *refs):
    xb_refs, (wg_ref, wu_ref, wd_ref) = refs[:ROW_PARTS], refs[ROW_PARTS:ROW_PARTS + 3]
    yb_refs, (wg_s, wu_s, wd_s) = refs[ROW_PARTS + 3:2 * ROW_PARTS + 3], refs[2 * ROW_PARTS + 3:]
    i = pl.program_id(0)
    changed = jnp.logical_or(i == 0, be_ref[i] != be_ref[jnp.maximum(i - 1, 0)])

    @pl.when(changed)
    def _():
        wg_s[...] = wg_ref[0, 0].astype(BF16)
        wu_s[...] = wu_ref[0, 0].astype(BF16)
        wd_s[...] = wd_ref[0, 0].astype(BF16)

    @pl.when(bi_ref[i] == i)
    def _():
        half = wg_s.shape[0] // 2
        gate = up = None
        for c in range(ROW_PARTS):
            hi, lo = _unpack_bf16_pairs(xb_refs[c][...])
            hi, lo = hi.astype(BF16), lo.astype(BF16)
            pw = hi.shape[1]
            hs, ls = slice(c * pw, (c + 1) * pw), slice(half + c * pw, half + (c + 1) * pw)
            g = _dot(hi, wg_s[hs, :]) + _dot(lo, wg_s[ls, :])
            u = _dot(hi, wu_s[hs, :]) + _dot(lo, wu_s[ls, :])
            gate, up = (g, u) if gate is None else (gate + g, up + u)
        act = (gate * _sigmoid(gate) * up).astype(BF16)
        packed = _pack_bf16_pairs(_dot(act, wd_s[...]))
        pw = packed.shape[1] // ROW_PARTS
        for c in range(ROW_PARTS):
            yb_refs[c][...] = packed[:, c * pw:(c + 1) * pw]


def _expert_ffn(block_e, block_i, xbs, wg, wu, wd, layer):
    n_rows, pw = xbs[0].shape
    d, ff = wg.shape[2], wg.shape[3]
    n_blk = n_rows // FFN_ROWS
    row_spec = pl.BlockSpec((FFN_ROWS, pw), lambda i, be, bi: (bi[i], 0))
    return pl.pallas_call(
        _ffn_kernel,
        grid_spec=pltpu.PrefetchScalarGridSpec(
            num_scalar_prefetch=2,
            grid=(n_blk,),
            in_specs=[row_spec] * ROW_PARTS + [
                pl.BlockSpec((1, 1, d, ff), lambda i, be, bi: (layer, be[i], 0, 0)),
                pl.BlockSpec((1, 1, d, ff), lambda i, be, bi: (layer, be[i], 0, 0)),
                pl.BlockSpec((1, 1, ff, d), lambda i, be, bi: (layer, be[i], 0, 0)),
            ],
            out_specs=[row_spec] * ROW_PARTS,
            scratch_shapes=[pltpu.VMEM((d, ff), BF16), pltpu.VMEM((d, ff), BF16), pltpu.VMEM((ff, d), BF16)],
        ),
        out_shape=[jax.ShapeDtypeStruct((n_rows, pw), U32)] * ROW_PARTS,
        compiler_params=_cparams(("arbitrary",)),
        name="moe_expert_ffn",
    )(block_e, block_i, *xbs, wg, wu, wd)


def _combine_kernel(x_ref, route_ref, *refs):
    y_refs, o_ref = refs[:2 * ROW_PARTS], refs[-1]
    o_ref[...] = _add_expert_rows(x_ref[...], route_ref[...],
                                  [(y_refs[2 * c][...], y_refs[2 * c + 1][...]) for c in range(ROW_PARTS)])


def _combine(x2, route, ytoks, tc, out_prev, row0, n_full):
    n, d = x2.shape
    w = ytoks[0].shape[1]
    nsteps = n // tc
    blk0 = row0 // tc
    specs = [pl.BlockSpec((tc, d), lambda i: (i, 0)), pl.BlockSpec((tc, ROUTE_LANES), lambda i: (i, 0))]
    args = [x2, route]
    for ytok in ytoks:
        specs += [pl.BlockSpec((tc, w), lambda i: (i, 0)), pl.BlockSpec((tc, w), lambda i: (i + nsteps, 0))]
        args += [ytok, ytok]
    aliases = {}
    if out_prev is not None:
        specs.append(pl.BlockSpec(memory_space=pl.ANY))
        args.append(out_prev)
        aliases = {len(args) - 1: 0}
    return pl.pallas_call(
        _combine_kernel,
        grid=(nsteps,),
        in_specs=specs,
        out_specs=pl.BlockSpec((tc, d), lambda i: (i + blk0, 0)),
        out_shape=jax.ShapeDtypeStruct((n_full, d), F32),
        input_output_aliases=aliases,
        compiler_params=_cparams(("parallel",)),
        name="moe_combine",
    )(*args)


def _moe(n, hfps, route_t, counts, wg, wu, wd, layer):
    cnt = counts[:, 0, :N_EXPERTS].astype(I32)
    total = jnp.sum(cnt, axis=0)
    padded = (total + FFN_ROWS - 1) // FFN_ROWS * FFN_ROWS
    pad_end = jnp.cumsum(padded)
    pad_start = pad_end - padded
    tile_base = pad_start[None, :] + jnp.cumsum(cnt, axis=0) - cnt
    expert_ids = jnp.arange(N_EXPERTS, dtype=I32)
    dests = []
    for k in range(TOP_K):
        ek = route_t[:, k, :].astype(I32)
        base = jnp.sum(jnp.where(ek[:, :, None] == expert_ids, tile_base[:, None, :], 0), axis=-1)
        dests.append((base + route_t[:, 4 + k, :].astype(I32)).reshape(n))
    dest_by_slot = jnp.concatenate(dests)
    n_blk = (n * TOP_K) // FFN_ROWS + N_EXPERTS
    n_rows = n_blk * FFN_ROWS
    used = pad_end[-1] // FFN_ROWS
    block_i = jnp.minimum(jnp.arange(n_blk, dtype=I32), used - 1).astype(I32)
    ended = (pad_end[None, :] <= (block_i * FFN_ROWS)[:, None]).astype(I32)
    block_e = jnp.minimum(jnp.sum(ended, axis=1), N_EXPERTS - 1).astype(I32)
    seg_len = jnp.concatenate([padded - total, (n_rows - pad_end[-1])[None]])
    seg_first = jnp.concatenate([pad_start + total, pad_end[-1:]])
    seg_end = jnp.cumsum(seg_len)
    jpad = jnp.arange(n_rows - n * TOP_K, dtype=I32)
    seg = jnp.sum((seg_end[None, :] <= jpad[:, None]).astype(I32), axis=1)
    pad_rows = (seg_first[seg] + jpad - (seg_end - seg_len)[seg]).astype(I32)

    xbs = [_sc_scatter_rows(part.reshape(n, part.shape[-1]), dests, pad_rows, n_rows) for part in hfps]
    yb = _expert_ffn(block_e, block_i, xbs, wg, wu, wd, layer)
    return [_sc_gather_rows(part, dest_by_slot) for part in yb]


def _apply_moe(x2, route, ytoks, out_prev, b0, b_full):
    b, s, d = x2.shape
    n = b * s
    prev = None if out_prev is None else out_prev.reshape(b_full * s, d)
    out = _combine(x2.reshape(n, d), route.reshape(n, ROUTE_LANES), ytoks, min(COMBINE_TOKENS, n), prev,
                   b0 * s, b_full * s)
    return out.reshape(b_full, s, d)


def _rope_lane_freq():
    inv = ROPE_THETA ** (-jnp.arange(0, ROPE_DIM // 2, dtype=F32) * 2.0 / ROPE_DIM)
    idx = np.full((HEAD_LANES,), -1, np.int64)
    for r in range(ROPE_DIM):
        idx[_head_lane(NOPE_DIM + r)] = r % ROPE_HALF
    return _gather_cols(inv[None, :], idx)


FAST_SOFTMAX_MAX_LOG2 = 60.0


def _score_bound_log2(qg, kg):
    return 1.02 * LOG2E * QK_DIM ** 0.5 * jnp.max(jnp.abs(qg)) * jnp.max(jnp.abs(kg))


def _partner_lanes(idx):
    out = np.full_like(idx, -1)
    for base in range(0, idx.shape[0], HEAD_LANES):
        for r in range(ROPE_DIM):
            lane = _head_lane(NOPE_DIM + r)
            out[base + lane] = idx[base + (lane + HALF_LANES) % HEAD_LANES]
    return out


def kernel(x, mem, positions, ln_mix, w_in, q_lat_norm, w_uq, kv_lat_norm, w_ukv, q_norm, k_norm, conv_w, conv_b,
           dt_bias, a_log, d_skip, ssd_norm, w_out, pool_w, pool_b, pool_scale, ln_xq, ln_mem, xq_w, xkv_w, xq_norm,
           xk_norm, xo_w, ln_ffn, rg_w, rg_b, re_w, re_b, exp_w_gate, exp_w_up, exp_w_down):
    b, s, d = x.shape
    depth = ln_mix.shape[0]
    tm = min(1024, s)
    tq = min(512, s)
    assert d == D_MODEL and s % tm == 0 and s % CHUNK == 0 and tm >= POOL_CARRY

    pos = positions.astype(F32)[..., None]
    invf = _rope_lane_freq()
    hsum = jnp.asarray(np.kron(np.eye(X_HEADS), np.ones((X_HEAD_DIM, X_HEAD_DIM))), BF16)
    ltri = jnp.asarray(np.tril(np.ones((tm, tm)), -1), BF16)
    row2 = lambda v: v.reshape(1, -1)
    lane_pad = lambda v: jnp.pad(v, (0, LANES - v.shape[0])).reshape(1, LANES)

    n_streams = MOE_STREAMS if b % MOE_STREAMS == 0 else 1
    nb = b // n_streams
    nt = s // tm
    streams = [dict(x=x, off=k * nb, pending=None) for k in range(n_streams)]
    for layer in range(depth):
        j = layer // 2
        kbd, vbd = _mem_kv(mem, row2(ln_mem[layer]), xkv_w[layer].astype(BF16),
                           row2(jnp.tile(xk_norm[layer], X_HEADS)), hsum)
        rw = jnp.pad(jnp.concatenate([rg_w[layer], re_w[layer]], axis=1),
                     ((0, 0), (0, ROUTE_LANES - MOE_GROUPS - N_EXPERTS)))
        rw_hi = rw.astype(BF16)
        rw_lo = (rw - rw_hi.astype(F32)).astype(BF16)
        rb = lane_pad(jnp.concatenate([rg_b[layer], re_b[layer]]))
        tail_args = [kbd, vbd, row2(ln_xq[layer]), xq_w[layer].astype(BF16), row2(jnp.tile(xq_norm[layer], X_HEADS)),
                     hsum, xo_w[layer].astype(BF16), row2(ln_ffn[layer]), rw_hi, rw_lo, rb, ltri]
        row = lambda i, jj: (i, jj, 0)
        fixed2 = lambda i, jj: (0, 0)
        posts = []
        if layer % 2 == 0:
            x = streams[0]["x"]
            assert all(st["x"] is x and st["pending"] is None for st in streams)
            win = _gather_cols(w_in[j], _win_col_index()).astype(BF16)
            q_idx = _head_col_index(QK_DIM, 0, QK_DIM)
            wuq = _gather_cols(w_uq[j], q_idx).astype(BF16)
            wuq_p = _gather_cols(w_uq[j], _partner_lanes(q_idx)).astype(BF16)
            wuk = _gather_cols(w_ukv[j], _head_col_index(NOPE_DIM + V_DIM, 0, NOPE_DIM)).astype(BF16)
            v_idx = np.full((A_HEADS * HEAD_LANES,), -1, np.int64)
            for hd in range(A_HEADS):
                v_idx[hd * HEAD_LANES:hd * HEAD_LANES + V_DIM] = hd * (NOPE_DIM + V_DIM) + NOPE_DIM + np.arange(V_DIM)
            wuv = _gather_cols(w_ukv[j], v_idx).astype(BF16)
            bound = _score_bound_log2(q_norm[j], k_norm[j])
            koff = jnp.zeros((1, HEAD_LANES), F32).at[0, SCORE_PAD_LANE].set(-bound)
            gain_idx = _head_col_index(QK_DIM, 0, QK_DIM)[:HEAD_LANES]
            lane_consts = jnp.concatenate(
                [_gather_cols(g[None, :], idx) for g in (q_norm[j], k_norm[j])
                 for idx in (gain_idx, _partner_lanes(gain_idx))]
                + [koff, invf, jnp.zeros((2, HEAD_LANES), F32)], axis=0)
            q, k, v, z, xbc, misc = _front_even(
                x, pos, lane_consts, row2(ln_mix[layer]), win, row2(q_lat_norm[j]), wuq, wuq_p,
                row2(kv_lat_norm[j]), wuk, wuv, tm)
            attn = lax.cond(bound <= FAST_SOFTMAX_MAX_LOG2,
                            functools.partial(_attention, tq=tq, online=False),
                            functools.partial(_attention, tq=tq, online=True), q, k, v)
            y = _ssd(xbc, misc, z, conv_w[j], row2(conv_b[j]), lane_pad(dt_bias[j]), lane_pad(a_log[j]),
                     row2(jnp.repeat(d_skip[j], SSD_HEAD_DIM)), row2(ssd_norm[j]))
            half = A_HEADS * V_DIM
            wout = w_out[j].astype(BF16)
            for k, st in enumerate(streams):
                row_k = lambda i, jj, off=st["off"]: (i + off, jj, 0)
                front_args = [x, attn, y, wout]
                front_specs = [pl.BlockSpec((1, tm, d), row_k), pl.BlockSpec((1, tm, half), row_k),
                               pl.BlockSpec((1, tm, D_INNER), row_k), pl.BlockSpec((half + D_INNER, d), fixed2)]
                posts.append(_post("even", front_args, front_specs, tail_args, nb, s, tm, [], b0=k * nb))
        else:
            for k, st in enumerate(streams):
                row_k = lambda i, jj, off=st["off"]: (i + off, jj, 0)
                front_args, front_specs = [st["x"]], [pl.BlockSpec((1, tm, d), row_k)]
                if st["pending"] is not None:
                    route_prev, ytoks = st["pending"]
                    pw = ytoks[0].shape[1]
                    front_args += [route_prev]
                    front_specs += [pl.BlockSpec((1, tm, ROUTE_LANES), row)]
                    for ytok in ytoks:
                        front_args += [ytok, ytok]
                        front_specs += [pl.BlockSpec((tm, pw), lambda i, jj: (i * nt + jj, 0)),
                                        pl.BlockSpec((tm, pw), lambda i, jj: (i * nt + jj + nb * nt, 0))]
                front_args += [row2(ln_mix[layer]), pool_w[j].astype(BF16), row2(pool_b[j]), row2(pool_scale[j])]
                front_specs += [pl.BlockSpec((1, d), fixed2),
                                pl.BlockSpec((len(POOL_WINDOWS), POOL_GROUP, POOL_GROUP), lambda i, jj: (0, 0, 0)),
                                pl.BlockSpec((1, d), fixed2), pl.BlockSpec((1, d), fixed2)]
                posts.append(_post("pool" if st["pending"] is None else "pool_pending", front_args, front_specs,
                                   tail_args, nb, s, tm, [pltpu.VMEM((POOL_CARRY, d), F32)], b0=k * nb))
        streams = []
        for x2, *hfps, route, route_t, counts in posts:
            ytoks = _moe(nb * s, hfps, route_t, counts, exp_w_gate, exp_w_up, exp_w_down, layer)
            streams.append(dict(x=x2, off=0, pending=(route, ytoks)))
        if layer + 1 == depth or (layer + 1) % 2 == 0:
            out = None
            for k, st in enumerate(streams):
                out = _apply_moe(st["x"], *st["pending"], out, k * nb, b)
            streams = [dict(x=out, off=k * nb, pending=None) for k in range(n_streams)]
    return streams[0]["x"]
```

```python
import functools

import numpy as np
import jax
import jax.numpy as jnp
from jax import lax
from jax.experimental import pallas as pl
from jax.experimental.pallas import tpu as pltpu
from jax.experimental.pallas import tpu_sc as plsc

F32 = jnp.float32
BF16 = jnp.bfloat16
U32 = jnp.uint32
I32 = jnp.int32

RMS_EPS = 1e-6
ROPE_THETA = 10000.0

D_MODEL = 1024
X_HEADS, X_HEAD_DIM = 4, 64
A_HEADS, NOPE_DIM, ROPE_DIM, V_DIM = 8, 64, 32, 64
QK_DIM = NOPE_DIM + ROPE_DIM
Q_LORA, KV_LORA = 256, 128
B_HEADS, SSD_HEAD_DIM, SSD_GROUPS, SSD_STATE, CONV_K, CHUNK = 8, 64, 2, 128, 4, 128
D_INNER = B_HEADS * SSD_HEAD_DIM
CONV_CH = D_INNER + 2 * SSD_GROUPS * SSD_STATE
POOL_WINDOWS = (2, 4, 8, 16)
POOL_GROUP = D_MODEL // 4
MOE_GROUPS, EXPERTS_PER_GROUP, TOP_K, EXPERT_FF = 4, 8, 2, 256
N_EXPERTS = MOE_GROUPS * EXPERTS_PER_GROUP

LANES = 128
HEAD_LANES = LANES
HALF_LANES = LANES // 2
ROPE_HALF = ROPE_DIM // 2
NOPE_HALF = NOPE_DIM // 2
POOL_CARRY = 16
CONV_CARRY = 8
VMEM_LIMIT = 56 * 1024 * 1024


def _cparams(sem):
    return pltpu.CompilerParams(dimension_semantics=sem, vmem_limit_bytes=VMEM_LIMIT)


def _rms(u, g):
    return u * lax.rsqrt(jnp.mean(u * u, axis=-1, keepdims=True) + RMS_EPS) * g


def _sigmoid(u):
    return 1.0 / (1.0 + jnp.exp(-u))


def _dot(a, b):
    return jnp.dot(a, b, preferred_element_type=F32)


def _dot_nt(a, b):
    return lax.dot_general(a, b, (((1,), (1,)), ((), ())), preferred_element_type=F32)


def _head_lane(d):
    if d < NOPE_HALF:
        return d
    if d < NOPE_DIM:
        return HALF_LANES + (d - NOPE_HALF)
    r = d - NOPE_DIM
    if r < ROPE_HALF:
        return NOPE_HALF + r
    return HALF_LANES + NOPE_HALF + (r - ROPE_HALF)


def _gather_cols(w, idx):
    w_ext = jnp.concatenate([w, jnp.zeros(w.shape[:-1] + (1,), w.dtype)], axis=-1)
    idx = np.where(idx < 0, w.shape[-1], idx)
    return jnp.take(w_ext, jnp.asarray(idx, dtype=jnp.int32), axis=-1)


IN_W = 2 * D_MODEL
_OFF_QLAT, _OFF_KVLAT, _OFF_MISC, _OFF_Z, _OFF_XBC = 0, 256, 384, 512, 1024


def _win_col_index():
    idx = np.full((IN_W,), -1, np.int64)
    idx[_OFF_QLAT:_OFF_QLAT + Q_LORA] = np.arange(Q_LORA)
    idx[_OFF_KVLAT:_OFF_KVLAT + KV_LORA] = Q_LORA + np.arange(KV_LORA)
    rope0 = Q_LORA + KV_LORA
    for r in range(ROPE_DIM):
        idx[_OFF_MISC + _head_lane(NOPE_DIM + r)] = rope0 + r
    z0 = rope0 + ROPE_DIM
    idx[_OFF_Z:_OFF_Z + D_INNER] = z0 + np.arange(D_INNER)
    xbc0 = z0 + D_INNER
    idx[_OFF_XBC:_OFF_XBC + CONV_CH] = xbc0 + np.arange(CONV_CH)
    dt0 = xbc0 + CONV_CH
    idx[_OFF_MISC:_OFF_MISC + B_HEADS] = dt0 + np.arange(B_HEADS)
    return idx


def _head_col_index(per_head, offset, count):
    idx = np.full((A_HEADS * HEAD_LANES,), -1, np.int64)
    for h in range(A_HEADS):
        for d in range(count):
            idx[h * HEAD_LANES + _head_lane(d)] = h * per_head + offset + d
    return idx


SCORE_PAD_LANE = NOPE_HALF + ROPE_HALF
ONES_LANE = V_DIM
LOG2E = 1.4426950408889634


def _front_even_kernel(x_ref, pos_ref, lc_ref, ln_ref, win_ref, qln_ref, wuq_ref, wuqr_ref, kvln_ref, wuk_ref, wuv_ref,
                       q_ref, k_ref, v_ref, z_ref, xbc_ref, misc_ref):
    x = x_ref[0]
    h = _rms(x, ln_ref[...]).astype(BF16)
    proj = _dot(h, win_ref[...])
    misc = proj[:, _OFF_MISC:_OFF_Z]
    z_ref[0] = proj[:, _OFF_Z:_OFF_XBC].astype(BF16)
    xbc_ref[0] = proj[:, _OFF_XBC:].astype(BF16)
    misc_ref[0] = misc
    ql = _rms(proj[:, _OFF_QLAT:_OFF_KVLAT], qln_ref[...]).astype(BF16)
    kvl = _rms(proj[:, _OFF_KVLAT:_OFF_MISC], kvln_ref[...]).astype(BF16)
    q = _dot(ql, wuq_ref[...])
    kn = _dot(kvl, wuk_ref[...])
    v = _dot(kvl, wuv_ref[...])
    lane = lax.broadcasted_iota(I32, (1, HEAD_LANES), 1)
    first_half = (lane >= NOPE_HALF) & (lane < NOPE_HALF + ROPE_HALF)
    second_half = (lane >= HALF_LANES + NOPE_HALF) & (lane < HALF_LANES + NOPE_HALF + ROPE_HALF)
    lc = lc_ref[...]
    qg, qg_p, kg, kg_p, k_off, invf = (lc[i:i + 1] for i in range(6))
    ones = jnp.ones((HEAD_LANES, HEAD_LANES), BF16)

    def lane_sumsq(u):
        return _dot((u * u).astype(BF16), ones)

    krope = jnp.where(first_half | second_half, misc, 0.0)
    kr_ss = lane_sumsq(krope)
    ang = pos_ref[0] * invf
    cos_t = jnp.cos(ang)
    sin_t = jnp.where(first_half, -jnp.sin(ang), jnp.sin(ang))
    q_one = (lane == SCORE_PAD_LANE).astype(F32)
    v_one = (lane == ONES_LANE).astype(F32)
    q_scale = QK_DIM ** -0.5 * LOG2E
    qa, qb = qg * cos_t * q_scale, qg_p * sin_t * q_scale
    ka, kc = kg * cos_t, pltpu.roll(krope, HALF_LANES, 1) * (kg_p * sin_t)
    q_p = _dot(ql, wuqr_ref[...])
    for hd in range(A_HEADS):
        sl = slice(hd * HEAD_LANES, (hd + 1) * HEAD_LANES)
        qs = q[:, sl]
        inv = lax.rsqrt(lane_sumsq(qs) * (1.0 / QK_DIM) + RMS_EPS)
        q_ref[0, hd] = ((qs * qa + q_p[:, sl] * qb) * inv + q_one).astype(BF16)
        kns = kn[:, sl]
        inv = lax.rsqrt((lane_sumsq(kns) + kr_ss) * (1.0 / QK_DIM) + RMS_EPS)
        k_ref[0, hd] = (((kns + krope) * ka + kc) * inv + k_off).astype(BF16)
        v_ref[0, hd] = (v[:, sl] + v_one).astype(BF16)


def _front_even(x, pos, lane_consts, ln, win, qln, wuq, wuq_p, kvln, wuk, wuv, tm):
    b, s, d = x.shape
    grid = (b, s // tm)
    row = lambda i, j: (i, j, 0)
    fixed2 = lambda i, j: (0, 0)
    head_row = lambda i, j: (i, 0, j, 0)
    hw = A_HEADS * HEAD_LANES
    return pl.pallas_call(
        _front_even_kernel,
        grid=grid,
        in_specs=[
            pl.BlockSpec((1, tm, d), row),
            pl.BlockSpec((1, tm, 1), row),
            pl.BlockSpec((8, HEAD_LANES), fixed2),
            pl.BlockSpec((1, d), fixed2),
            pl.BlockSpec((d, IN_W), fixed2),
            pl.BlockSpec((1, Q_LORA), fixed2),
            pl.BlockSpec((Q_LORA, hw), fixed2),
            pl.BlockSpec((Q_LORA, hw), fixed2),
            pl.BlockSpec((1, KV_LORA), fixed2),
            pl.BlockSpec((KV_LORA, hw), fixed2),
            pl.BlockSpec((KV_LORA, hw), fixed2),
        ],
        out_specs=[
            pl.BlockSpec((1, A_HEADS, tm, HEAD_LANES), head_row),
            pl.BlockSpec((1, A_HEADS, tm, HEAD_LANES), head_row),
            pl.BlockSpec((1, A_HEADS, tm, HEAD_LANES), head_row),
            pl.BlockSpec((1, tm, D_INNER), row),
            pl.BlockSpec((1, tm, CONV_CH), row),
            pl.BlockSpec((1, tm, HEAD_LANES), row),
        ],
        out_shape=[
            jax.ShapeDtypeStruct((b, A_HEADS, s, HEAD_LANES), BF16),
            jax.ShapeDtypeStruct((b, A_HEADS, s, HEAD_LANES), BF16),
            jax.ShapeDtypeStruct((b, A_HEADS, s, HEAD_LANES), BF16),
            jax.ShapeDtypeStruct((b, s, D_INNER), BF16),
            jax.ShapeDtypeStruct((b, s, CONV_CH), BF16),
            jax.ShapeDtypeStruct((b, s, HEAD_LANES), F32),
        ],
        compiler_params=_cparams(("parallel", "parallel")),
        name="front_even",
    )(x, pos, lane_consts, ln, win, qln, wuq, wuq_p, kvln, wuk, wuv)


HEADS_PER_STEP = 8


def _attn_kernel(q_ref, k_ref, v_ref, o_ref, *, tq, online):
    qi = pl.program_id(2)
    row = lax.broadcasted_iota(I32, (tq, tq), 0)
    col = lax.broadcasted_iota(I32, (tq, tq), 1)

    def head_step(hh, j, carry, masked):
        kj = k_ref[0, hh, pl.ds(j * tq, tq), :]
        vj = v_ref[0, hh, pl.ds(j * tq, tq), :]
        s = _dot_nt(q_ref[0, hh], kj)
        if masked:
            s = jnp.where(row >= col, s, -jnp.inf)
        if online:
            m, acc = carry
            m_new = jnp.maximum(m, jnp.max(s, axis=-1, keepdims=True))
            p = jnp.exp2(s - m_new).astype(BF16)
            return m_new, jnp.exp2(m - m_new) * acc + _dot(p, vj)
        return carry + _dot(jnp.exp2(s).astype(BF16), vj)

    def step(j, carries, masked):
        return tuple(head_step(hh, j, carries[hh], masked) for hh in range(HEADS_PER_STEP))

    acc0 = jnp.zeros((tq, HEAD_LANES), F32)
    init = (jnp.full((tq, 1), -jnp.inf, F32), acc0) if online else acc0
    carries = lax.fori_loop(0, qi, functools.partial(step, masked=False), (init,) * HEADS_PER_STEP)
    carries = step(qi, carries, True)
    outs = []
    for carry in carries:
        acc = carry[1] if online else carry
        outs.append(acc / acc[:, ONES_LANE:ONES_LANE + 1])
    lane = lax.broadcasted_iota(I32, (1, HEAD_LANES), 1)
    per_group = HEAD_LANES // V_DIM
    groups = []
    for g0 in range(0, HEADS_PER_STEP, per_group):
        out = outs[g0]
        for hh in range(1, per_group):
            out = jnp.where(lane >= hh * V_DIM, pltpu.roll(outs[g0 + hh], hh * V_DIM, 1), out)
        groups.append(out.astype(BF16))
    o_ref[0] = jnp.concatenate(groups, axis=1)


def _attention(q, k, v, tq, online):
    b, nh, s, _ = q.shape
    grid = (b, nh // HEADS_PER_STEP, s // tq)
    kv_spec = pl.BlockSpec((1, HEADS_PER_STEP, s, HEAD_LANES), lambda i, h, j: (i, h, 0, 0))
    return pl.pallas_call(
        functools.partial(_attn_kernel, tq=tq, online=online),
        grid=grid,
        in_specs=[
            pl.BlockSpec((1, HEADS_PER_STEP, tq, HEAD_LANES), lambda i, h, j: (i, h, j, 0)),
            kv_spec,
            kv_spec,
        ],
        out_specs=pl.BlockSpec((1, tq, HEADS_PER_STEP * V_DIM), lambda i, h, j: (i, j, h)),
        out_shape=jax.ShapeDtypeStruct((b, s, nh * V_DIM), BF16),
        compiler_params=_cparams(("parallel", "parallel", "parallel")),
        name="mla_attention_online" if online else "mla_attention",
    )(q, k, v)


def _ssd_kernel(xbc_ref, misc_ref, z_ref, cw_ref, cb_ref, dtb_ref, alog_ref, dskip_ref, gn_ref, y_ref,
                state_ref, carry_ref):
    c = pl.program_id(1)
    t = CHUNK
    rows = y_ref.shape[1]

    @pl.when(c == 0)
    def _():
        state_ref[...] = jnp.zeros_like(state_ref)
        carry_ref[...] = jnp.zeros_like(carry_ref)

    xr = xbc_ref[0].astype(F32)
    xcat = jnp.concatenate([carry_ref[...], xr], axis=0)
    carry_ref[...] = xr[rows - CONV_CARRY:, :]
    conv = jnp.zeros((rows, CONV_CH), F32) + cb_ref[...]
    for kk in range(CONV_K):
        sh = CONV_K - 1 - kk
        shifted = xcat if sh == 0 else pltpu.roll(xcat, sh, 0)
        conv = conv + cw_ref[kk:kk + 1, :] * shifted[CONV_CARRY:, :]
    xa = conv * _sigmoid(conv)
    xs = xa[:, :D_INNER]
    gw = SSD_GROUPS * SSD_STATE
    bmat = xa[:, D_INNER:D_INNER + gw]
    cmat = xa[:, D_INNER + gw:]

    u = misc_ref[0] + dtb_ref[...]
    dt = jnp.maximum(u, 0.0) + jnp.log(1.0 + jnp.exp(-jnp.abs(u)))
    a = -jnp.exp(alog_ref[...])
    lane = lax.broadcasted_iota(I32, (1, LANES), 1)
    adt_all = jnp.where(lane < B_HEADS, dt * a, 0.0)
    rowi = lax.broadcasted_iota(I32, (t, LANES), 0)
    tri = lax.broadcasted_iota(I32, (t, t), 0) >= lax.broadcasted_iota(I32, (t, t), 1)
    rep = B_HEADS // SSD_GROUPS
    y_chunks = []
    for ci in range(rows // t):
        sl = slice(ci * t, (ci + 1) * t)
        acs = adt_all[sl]
        sh = 1
        while sh < t:
            acs = acs + jnp.where(rowi >= sh, pltpu.roll(acs, sh, 0), 0.0)
            sh *= 2
        acs_t = acs.T
        ys = []
        for g in range(SSD_GROUPS):
            bg = bmat[sl, g * SSD_STATE:(g + 1) * SSD_STATE]
            cg = cmat[sl, g * SSD_STATE:(g + 1) * SSD_STATE]
            bg16, cg16 = bg.astype(BF16), cg.astype(BF16)
            cb = _dot_nt(cg16, bg16)
            bg_t = bg.T
            for r in range(rep):
                hd = g * rep + r
                col = acs[:, hd:hd + 1]
                rw = acs_t[hd:hd + 1, :]
                last = acs_t[hd:hd + 1, t - 1:t]
                decay = jnp.exp(jnp.where(tri, col - rw, -jnp.inf))
                xh = xs[sl, hd * SSD_HEAD_DIM:(hd + 1) * SSD_HEAD_DIM]
                xdt = (xh * dt[sl, hd:hd + 1]).astype(BF16)
                y_diag = _dot((cb * decay).astype(BF16), xdt)
                prev = state_ref[hd]
                y_off = _dot(cg16, prev.astype(BF16)) * jnp.exp(col)
                new_state = _dot((bg_t * jnp.exp(last - rw)).astype(BF16), xdt)
                state_ref[hd] = prev * jnp.exp(last) + new_state
                ys.append(y_diag + y_off)
        y_chunks.append(jnp.concatenate(ys, axis=1))
    y = jnp.concatenate(y_chunks, axis=0) + xs * dskip_ref[...]
    zf = z_ref[0].astype(F32)
    y = y * (zf * _sigmoid(zf))
    y_ref[0] = _rms(y, gn_ref[...]).astype(BF16)


SSD_CHUNKS_PER_STEP = 4


def _ssd(xbc, misc, z, cw, cb, dtb, alog, dskip, gn):
    b, s, _ = xbc.shape
    rows = CHUNK * SSD_CHUNKS_PER_STEP if s % (CHUNK * SSD_CHUNKS_PER_STEP) == 0 else CHUNK
    grid = (b, s // rows)
    row = lambda i, j: (i, j, 0)
    fixed2 = lambda i, j: (0, 0)
    return pl.pallas_call(
        _ssd_kernel,
        grid=grid,
        in_specs=[
            pl.BlockSpec((1, rows, CONV_CH), row),
            pl.BlockSpec((1, rows, LANES), row),
            pl.BlockSpec((1, rows, D_INNER), row),
            pl.BlockSpec((CONV_K, CONV_CH), fixed2),
            pl.BlockSpec((1, CONV_CH), fixed2),
            pl.BlockSpec((1, LANES), fixed2),
            pl.BlockSpec((1, LANES), fixed2),
            pl.BlockSpec((1, D_INNER), fixed2),
            pl.BlockSpec((1, D_INNER), fixed2),
        ],
        out_specs=pl.BlockSpec((1, rows, D_INNER), row),
        out_shape=jax.ShapeDtypeStruct((b, s, D_INNER), BF16),
        scratch_shapes=[
            pltpu.VMEM((B_HEADS, SSD_STATE, SSD_HEAD_DIM), F32),
            pltpu.VMEM((CONV_CARRY, CONV_CH), F32),
        ],
        compiler_params=_cparams(("parallel", "arbitrary")),
        name="ssd_scan",
    )(xbc, misc, z, cw, cb, dtb, alog, dskip, gn)


XW = X_HEADS * X_HEAD_DIM


def _mem_kv_kernel(mem_ref, ln_ref, wkv_ref, kg_ref, hsum_ref, kbd_ref, vbd_ref):
    m = mem_ref.shape[1]
    mn = _rms(mem_ref[0], ln_ref[...]).astype(BF16)
    kv = _dot(mn, wkv_ref[...])
    k, v = kv[:, :XW], kv[:, XW:]
    ss = _dot((k * k).astype(BF16), hsum_ref[...])
    kn = (k * lax.rsqrt(ss * (1.0 / X_HEAD_DIM) + RMS_EPS) * kg_ref[...]).astype(BF16)
    v16 = v.astype(BF16)
    head_of_lane = lax.shift_right_arithmetic(lax.broadcasted_iota(I32, (1, XW), 1), jnp.int32(_LOG2_XHD))
    for hd in range(X_HEADS):
        keep = head_of_lane == hd
        kbd_ref[0, hd * m:(hd + 1) * m, :] = jnp.where(keep, kn, jnp.zeros_like(kn))
        vbd_ref[0, hd * m:(hd + 1) * m, :] = jnp.where(keep, v16, jnp.zeros_like(v16))


def _mem_kv(mem, ln, wkv, kg, hsum):
    b, m, d = mem.shape
    fixed2 = lambda i: (0, 0)
    return pl.pallas_call(
        _mem_kv_kernel,
        grid=(b,),
        in_specs=[
            pl.BlockSpec((1, m, d), lambda i: (i, 0, 0)),
            pl.BlockSpec((1, d), fixed2),
            pl.BlockSpec((d, 2 * XW), fixed2),
            pl.BlockSpec((1, XW), fixed2),
            pl.BlockSpec((XW, XW), fixed2),
        ],
        out_specs=[
            pl.BlockSpec((1, X_HEADS * m, XW), lambda i: (i, 0, 0)),
            pl.BlockSpec((1, X_HEADS * m, XW), lambda i: (i, 0, 0)),
        ],
        out_shape=[
            jax.ShapeDtypeStruct((b, X_HEADS * m, XW), BF16),
            jax.ShapeDtypeStruct((b, X_HEADS * m, XW), BF16),
        ],
        compiler_params=_cparams(("parallel",)),
        name="mem_kv",
    )(mem, ln, wkv, kg, hsum)


ROUTE_LANES = LANES
ROW_PARTS = 2
_GROUP_LANE0 = 0
_EXPERT_LANE0 = MOE_GROUPS
_LOG2_EPG = EXPERTS_PER_GROUP.bit_length() - 1
_LOG2_XHD = X_HEAD_DIM.bit_length() - 1


def _pack_bf16_pairs(v):
    w = v.shape[1] // 2
    r = v.astype(BF16).astype(F32)
    hi = lax.bitcast_convert_type(r[:, :w], U32)
    lo = lax.bitcast_convert_type(r[:, w:], U32)
    return (hi & jnp.uint32(0xFFFF0000)) | (lo >> jnp.uint32(16))


def _unpack_bf16_pairs(u):
    hi = lax.bitcast_convert_type(u & jnp.uint32(0xFFFF0000), F32)
    lo = lax.bitcast_convert_type(u << jnp.uint32(16), F32)
    return hi, lo


TAIL_SPLIT = 1


def _tail_rows(x1, rows, kbd_ref, vbd_ref, lnq_ref, wq_ref, qg_ref, hsum_ref, wo_ref, lnf_ref, rwh_ref, rwl_ref,
               rb_ref, x2_ref, hfp_refs):
    tm = x1.shape[0]
    m = kbd_ref.shape[1] // X_HEADS
    hq = _rms(x1, lnq_ref[...]).astype(BF16)
    q = _dot(hq, wq_ref[...])
    ss = _dot((q * q).astype(BF16), hsum_ref[...])
    qn = (q * lax.rsqrt(ss * (1.0 / X_HEAD_DIM) + RMS_EPS) * qg_ref[...] * (X_HEAD_DIM ** -0.5)).astype(BF16)
    s = _dot_nt(qn, kbd_ref[0])
    ps = []
    for hd in range(X_HEADS):
        sh = s[:, hd * m:(hd + 1) * m]
        e = jnp.exp(sh - jnp.max(sh, axis=-1, keepdims=True))
        ps.append((e / jnp.sum(e, axis=-1, keepdims=True)).astype(BF16))
    o = _dot(jnp.concatenate(ps, axis=1), vbd_ref[0]).astype(BF16)
    x2 = x1 + _dot(o, wo_ref[...])
    x2_ref[0, rows, :] = x2

    hf = _rms(x2, lnf_ref[...])
    hf_hi = hf.astype(BF16)
    packed = _pack_bf16_pairs(hf)
    pw = packed.shape[1] // ROW_PARTS
    for c in range(ROW_PARTS):
        hfp_refs[c][0, rows, :] = packed[:, c * pw:(c + 1) * pw]
    hf_lo = (hf - hf_hi.astype(F32)).astype(BF16)
    logits = _dot(hf_hi, rwh_ref[...]) + _dot(hf_hi, rwl_ref[...]) + _dot(hf_lo, rwh_ref[...]) + rb_ref[...]

    lane_i = lax.broadcasted_iota(I32, (tm, ROUTE_LANES), 1)
    lane = lane_i.astype(F32)
    big = float(ROUTE_LANES)
    neg = -jnp.inf
    gl = jnp.where(lane_i < MOE_GROUPS, logits, neg)
    gmax = jnp.max(gl, axis=-1, keepdims=True)
    gsum = jnp.sum(jnp.exp(gl - gmax), axis=-1, keepdims=True)
    g_p = 1.0 / gsum
    g_idx = jnp.min(jnp.where(gl == gmax, lane, big), axis=-1, keepdims=True)
    e_lane = lane_i - _EXPERT_LANE0
    grp_of_lane = lax.shift_right_arithmetic(e_lane, jnp.int32(_LOG2_EPG)).astype(F32)
    in_grp = (e_lane >= 0) & (e_lane < N_EXPERTS) & (grp_of_lane == g_idx)
    el = jnp.where(in_grp, logits, neg)
    emax = jnp.max(el, axis=-1, keepdims=True)
    idx1 = jnp.min(jnp.where(el == emax, lane, big), axis=-1, keepdims=True)
    el2 = jnp.where(lane == idx1, neg, el)
    emax2 = jnp.max(el2, axis=-1, keepdims=True)
    idx2 = jnp.min(jnp.where(el2 == emax2, lane, big), axis=-1, keepdims=True)
    r2 = jnp.exp(emax2 - emax)
    gate1 = g_p / (1.0 + r2)
    gate2 = g_p * r2 / (1.0 + r2)
    e1 = idx1 - float(_EXPERT_LANE0)
    e2 = idx2 - float(_EXPERT_LANE0)

    route = jnp.where(lane == 0, e1, 0.0)
    route = jnp.where(lane == 1, e2, route)
    route = jnp.where(lane == 2, gate1, route)
    route = jnp.where(lane == 3, gate2, route)
    return route, (lane == e1).astype(F32), (lane == e2).astype(F32)


def _tail(x1, kbd_ref, vbd_ref, lnq_ref, wq_ref, qg_ref, hsum_ref, wo_ref, lnf_ref, rwh_ref, rwl_ref, rb_ref,
          ltri_ref, x2_ref, *out_refs):
    hfp_refs, (route_ref, route_t_ref, cnt_ref) = out_refs[:ROW_PARTS], out_refs[ROW_PARTS:]
    tm = x1.shape[0]
    tr = tm // TAIL_SPLIT
    parts = [_tail_rows(x1[r * tr:(r + 1) * tr], slice(r * tr, (r + 1) * tr), kbd_ref, vbd_ref, lnq_ref, wq_ref,
                        qg_ref, hsum_ref, wo_ref, lnf_ref, rwh_ref, rwl_ref, rb_ref, x2_ref, hfp_refs)
             for r in range(TAIL_SPLIT)]
    route, oh1, oh2 = (jnp.concatenate([p[i] for p in parts], axis=0) for i in range(3))
    both = oh1 + oh2
    before = _dot(ltri_ref[...], both.astype(BF16))
    rank1 = jnp.sum(before * oh1, axis=-1, keepdims=True)
    rank2 = jnp.sum(before * oh2, axis=-1, keepdims=True)
    cnt_ref[0] = jnp.broadcast_to(jnp.sum(both, axis=0, keepdims=True), cnt_ref.shape[1:])
    lane = lax.broadcasted_iota(I32, (tm, ROUTE_LANES), 1)
    route = jnp.where(lane == 4, rank1, route)
    route = jnp.where(lane == 5, rank2, route)
    route_ref[0] = route
    route_t_ref[0] = route.T[:route_t_ref.shape[1], :]


_TAIL_IN = 12


def _post_even_kernel(x_ref, a_ref, y_ref, wout_ref, *rest):
    tail_in, outs = rest[:_TAIL_IN], rest[_TAIL_IN:]
    half = wout_ref.shape[0] // 2
    x1 = x_ref[0] + _dot(a_ref[0], wout_ref[:half, :]) + _dot(y_ref[0], wout_ref[half:, :])
    _tail(x1, *tail_in, *outs)


def _add_expert_rows(x, route, ys):
    g1, g2 = route[:, 2:3], route[:, 3:4]
    his, los = [], []
    for y1, y2 in ys:
        h1, l1 = _unpack_bf16_pairs(y1)
        h2, l2 = _unpack_bf16_pairs(y2)
        his.append(h1 * g1 + h2 * g2)
        los.append(l1 * g1 + l2 * g2)
    return x + jnp.concatenate(his + los, axis=1)


N_PENDING = 1 + 2 * ROW_PARTS


def _post_pool_kernel(x_ref, *rest, pending):
    if pending:
        route_prev_ref, y_refs, rest = rest[0], rest[1:N_PENDING], rest[N_PENDING:]
    (ln_ref, pw_ref, pb_ref, ps_ref), rest = rest[:4], rest[4:]
    tail_in, outs, carry_ref = rest[:_TAIL_IN], rest[_TAIL_IN:-1], rest[-1]
    j = pl.program_id(1)
    tm = x_ref.shape[1]

    @pl.when(j == 0)
    def _():
        carry_ref[...] = jnp.zeros_like(carry_ref)

    x = x_ref[0]
    if pending:
        x = _add_expert_rows(x, route_prev_ref[0], [(y_refs[2 * c][...], y_refs[2 * c + 1][...])
                                                    for c in range(ROW_PARTS)])
    h = _rms(x, ln_ref[...])
    pos = (j * tm + 1 + lax.broadcasted_iota(I32, (tm, 1), 0)).astype(F32)
    mixed = []
    for g, w in enumerate(POOL_WINDOWS):
        sl = slice(g * POOL_GROUP, (g + 1) * POOL_GROUP)
        hg = h[:, sl]
        acc = jnp.concatenate([carry_ref[:, sl], hg], axis=0)
        sh = 1
        while sh < w:
            acc = acc + pltpu.roll(acc, sh, 0)
            sh *= 2
        win = acc[POOL_CARRY:, :]
        dlt = win / jnp.minimum(pos, float(w)) - hg
        mixed.append(_dot(dlt.astype(BF16), pw_ref[g]))
    carry_ref[...] = h[tm - POOL_CARRY:, :]
    y = (jnp.concatenate(mixed, axis=1) + pb_ref[...]) * ps_ref[...]
    _tail(x + y, *tail_in, *outs)


def _post(kind, front_args, front_specs, tail_args, b, s, tm, scratch, b0=0):
    d = D_MODEL
    m4 = tail_args[0].shape[1]
    row = lambda i, j: (i, j, 0)
    fixed2 = lambda i, j: (0, 0)
    per_b = lambda i, j: (i + b0, 0, 0)
    tail_specs = [
        pl.BlockSpec((1, m4, XW), per_b),
        pl.BlockSpec((1, m4, XW), per_b),
        pl.BlockSpec((1, d), fixed2),
        pl.BlockSpec((d, XW), fixed2),
        pl.BlockSpec((1, XW), fixed2),
        pl.BlockSpec((XW, XW), fixed2),
        pl.BlockSpec((XW, d), fixed2),
        pl.BlockSpec((1, d), fixed2),
        pl.BlockSpec((d, ROUTE_LANES), fixed2),
        pl.BlockSpec((d, ROUTE_LANES), fixed2),
        pl.BlockSpec((1, ROUTE_LANES), fixed2),
        pl.BlockSpec((tm, tm), fixed2),
    ]
    nt = s // tm
    pw = d // 2 // ROW_PARTS
    kernel = {"even": _post_even_kernel,
              "pool": functools.partial(_post_pool_kernel, pending=False),
              "pool_pending": functools.partial(_post_pool_kernel, pending=True)}[kind]
    return pl.pallas_call(
        kernel,
        grid=(b, nt),
        in_specs=front_specs + tail_specs,
        out_specs=[pl.BlockSpec((1, tm, d), row)]
        + [pl.BlockSpec((1, tm, pw), row)] * ROW_PARTS
        + [pl.BlockSpec((1, tm, ROUTE_LANES), row),
           pl.BlockSpec((1, 8, tm), lambda i, j: (i * nt + j, 0, 0)),
           pl.BlockSpec((1, 8, ROUTE_LANES), lambda i, j: (i * nt + j, 0, 0))],
        out_shape=[jax.ShapeDtypeStruct((b, s, d), F32)]
        + [jax.ShapeDtypeStruct((b, s, pw), U32)] * ROW_PARTS
        + [jax.ShapeDtypeStruct((b, s, ROUTE_LANES), F32),
           jax.ShapeDtypeStruct((b * nt, 8, tm), F32),
           jax.ShapeDtypeStruct((b * nt, 8, ROUTE_LANES), F32)],
        scratch_shapes=scratch,
        compiler_params=_cparams(("parallel", "arbitrary")),
        name="post_" + kind,
    )(*front_args, *tail_args)


FFN_ROWS = 1024
COMBINE_TOKENS = 1024
MOE_STREAMS = 2
SC_GATHER_WINDOW = 128


def _sc_gather_rows(table, idx):
    m, w = idx.shape[0], table.shape[1]
    mesh = plsc.VectorSubcoreMesh(core_axis_name="core", subcore_axis_name="subcore")

    @pl.kernel(out_type=jax.ShapeDtypeStruct((m, w), table.dtype), mesh=mesh, name="moe_row_gather")
    def gather(t_hbm, i_hbm, o_hbm):
        def body(i_vmem, o_vmem):
            pltpu.sync_copy(t_hbm.at[i_vmem.at[0]], o_vmem)

        pltpu.emit_pipeline(
            body,
            grid=(m // SC_GATHER_WINDOW,),
            in_specs=[pl.BlockSpec((1, SC_GATHER_WINDOW), lambda i: (0, i))],
            out_specs=[pl.BlockSpec((SC_GATHER_WINDOW, w), lambda i: (i, 0))],
            core_axis_name=("core", "subcore"),
            dimension_semantics=(pltpu.PARALLEL,),
        )(i_hbm, o_hbm)

    return gather(table, idx.reshape(1, m))


def _sc_scatter_rows(src, dests, pad_rows, n_rows):
    n, w = src.shape
    win = SC_GATHER_WINDOW
    mesh = plsc.VectorSubcoreMesh(core_axis_name="core", subcore_axis_name="subcore")
    idx_spec = pl.BlockSpec((1, win), lambda i: (0, i))
    split = dict(core_axis_name=("core", "subcore"), dimension_semantics=(pltpu.PARALLEL,))

    @pl.kernel(out_type=jax.ShapeDtypeStruct((n_rows, w), src.dtype), mesh=mesh, name="moe_row_scatter")
    def scatter(s_hbm, z_hbm, p_hbm, *rest):
        d_hbms, o_hbm = rest[:-1], rest[-1]

        def body(s_vmem, *i_vmems):
            for i_vmem in i_vmems:
                pltpu.sync_copy(s_vmem, o_hbm.at[i_vmem.at[0]])

        pltpu.emit_pipeline(
            body, grid=(n // win,),
            in_specs=[pl.BlockSpec((win, w), lambda i: (i, 0))] + [idx_spec] * len(dests),
            out_specs=[], **split)(s_hbm, *d_hbms)

        def zero_body(z_vmem, i_vmem):
            pltpu.sync_copy(z_vmem, o_hbm.at[i_vmem.at[0]])

        pltpu.emit_pipeline(
            zero_body, grid=(pad_rows.shape[0] // win,),
            in_specs=[pl.BlockSpec((win, w), lambda i: (0, 0)), idx_spec],
            out_specs=[], **split)(z_hbm, p_hbm)

    zeros = jnp.zeros((win, w), src.dtype)
    return scatter(src, zeros, pad_rows.reshape(1, -1), *[dd.reshape(1, n) for dd in dests])


def _ffn_kernel(be_ref, bi_ref, *refs):
    xb_refs, (wg_ref, wu_ref, wd_ref) = refs[:ROW_PARTS], refs[ROW_PARTS:ROW_PARTS + 3]
    yb_refs, (wg_s, wu_s, wd_s) = refs[ROW_PARTS + 3:2 * ROW_PARTS + 3], refs[2 * ROW_PARTS + 3:]
    i = pl.program_id(0)
    changed = jnp.logical_or(i == 0, be_ref[i] != be_ref[jnp.maximum(i - 1, 0)])

    @pl.when(changed)
    def _():
        wg_s[...] = wg_ref[0, 0].astype(BF16)
        wu_s[...] = wu_ref[0, 0].astype(BF16)
        wd_s[...] = wd_ref[0, 0].astype(BF16)

    @pl.when(bi_ref[i] == i)
    def _():
        half = wg_s.shape[0] // 2
        gate = up = None
        for c in range(ROW_PARTS):
            hi, lo = _unpack_bf16_pairs(xb_refs[c][...])
            hi, lo = hi.astype(BF16), lo.astype(BF16)
            pw = hi.shape[1]
            hs, ls = slice(c * pw, (c + 1) * pw), slice(half + c * pw, half + (c + 1) * pw)
            g = _dot(hi, wg_s[hs, :]) + _dot(lo, wg_s[ls, :])
            u = _dot(hi, wu_s[hs, :]) + _dot(lo, wu_s[ls, :])
            gate, up = (g, u) if gate is None else (gate + g, up + u)
        act = (gate * _sigmoid(gate) * up).astype(BF16)
        packed = _pack_bf16_pairs(_dot(act, wd_s[...]))
        pw = packed.shape[1] // ROW_PARTS
        for c in range(ROW_PARTS):
            yb_refs[c][...] = packed[:, c * pw:(c + 1) * pw]


def _expert_ffn(block_e, block_i, xbs, wg, wu, wd, layer):
    n_rows, pw = xbs[0].shape
    d, ff = wg.shape[2], wg.shape[3]
    n_blk = n_rows // FFN_ROWS
    row_spec = pl.BlockSpec((FFN_ROWS, pw), lambda i, be, bi: (bi[i], 0))
    return pl.pallas_call(
        _ffn_kernel,
        grid_spec=pltpu.PrefetchScalarGridSpec(
            num_scalar_prefetch=2,
            grid=(n_blk,),
            in_specs=[row_spec] * ROW_PARTS + [
                pl.BlockSpec((1, 1, d, ff), lambda i, be, bi: (layer, be[i], 0, 0)),
                pl.BlockSpec((1, 1, d, ff), lambda i, be, bi: (layer, be[i], 0, 0)),
                pl.BlockSpec((1, 1, ff, d), lambda i, be, bi: (layer, be[i], 0, 0)),
            ],
            out_specs=[row_spec] * ROW_PARTS,
            scratch_shapes=[pltpu.VMEM((d, ff), BF16), pltpu.VMEM((d, ff), BF16), pltpu.VMEM((ff, d), BF16)],
        ),
        out_shape=[jax.ShapeDtypeStruct((n_rows, pw), U32)] * ROW_PARTS,
        compiler_params=_cparams(("arbitrary",)),
        name="moe_expert_ffn",
    )(block_e, block_i, *xbs, wg, wu, wd)


def _combine_kernel(x_ref, route_ref, *refs):
    y_refs, o_ref = refs[:2 * ROW_PARTS], refs[-1]
    o_ref[...] = _add_expert_rows(x_ref[...], route_ref[...],
                                  [(y_refs[2 * c][...], y_refs[2 * c + 1][...]) for c in range(ROW_PARTS)])


def _combine(x2, route, ytoks, tc, out_prev, row0, n_full):
    n, d = x2.shape
    w = ytoks[0].shape[1]
    nsteps = n // tc
    blk0 = row0 // tc
    specs = [pl.BlockSpec((tc, d), lambda i: (i, 0)), pl.BlockSpec((tc, ROUTE_LANES), lambda i: (i, 0))]
    args = [x2, route]
    for ytok in ytoks:
        specs += [pl.BlockSpec((tc, w), lambda i: (i, 0)), pl.BlockSpec((tc, w), lambda i: (i + nsteps, 0))]
        args += [ytok, ytok]
    aliases = {}
    if out_prev is not None:
        specs.append(pl.BlockSpec(memory_space=pl.ANY))
        args.append(out_prev)
        aliases = {len(args) - 1: 0}
    return pl.pallas_call(
        _combine_kernel,
        grid=(nsteps,),
        in_specs=specs,
        out_specs=pl.BlockSpec((tc, d), lambda i: (i + blk0, 0)),
        out_shape=jax.ShapeDtypeStruct((n_full, d), F32),
        input_output_aliases=aliases,
        compiler_params=_cparams(("parallel",)),
        name="moe_combine",
    )(*args)


def _moe(n, hfps, route_t, counts, wg, wu, wd, layer):
    cnt = counts[:, 0, :N_EXPERTS].astype(I32)
    total = jnp.sum(cnt, axis=0)
    padded = (total + FFN_ROWS - 1) // FFN_ROWS * FFN_ROWS
    pad_end = jnp.cumsum(padded)
    pad_start = pad_end - padded
    tile_base = pad_start[None, :] + jnp.cumsum(cnt, axis=0) - cnt
    dests = []
    for k in range(TOP_K):
        ek = route_t[:, k, :].astype(I32)
        base = jnp.zeros_like(ek)
        for e in range(N_EXPERTS):
            base = jnp.where(ek == e, tile_base[:, e:e + 1], base)
        dests.append((base + route_t[:, 4 + k, :].astype(I32)).reshape(n))
    dest_by_slot = jnp.concatenate(dests)
    n_blk = (n * TOP_K) // FFN_ROWS + N_EXPERTS
    n_rows = n_blk * FFN_ROWS
    used = pad_end[-1] // FFN_ROWS
    block_i = jnp.minimum(jnp.arange(n_blk, dtype=I32), used - 1).astype(I32)
    ended = (pad_end[None, :] <= (block_i * FFN_ROWS)[:, None]).astype(I32)
    block_e = jnp.minimum(jnp.sum(ended, axis=1), N_EXPERTS - 1).astype(I32)
    seg_len = jnp.concatenate([padded - total, (n_rows - pad_end[-1])[None]])
    seg_first = jnp.concatenate([pad_start + total, pad_end[-1:]])
    seg_end = jnp.cumsum(seg_len)
    jpad = jnp.arange(n_rows - n * TOP_K, dtype=I32)
    shift = (seg_first - (seg_end - seg_len)).astype(I32)
    pad_rows = jpad + shift[0]
    for sg in range(1, N_EXPERTS + 1):
        pad_rows = jnp.where(jpad >= seg_end[sg - 1], jpad + shift[sg], pad_rows)

    xbs = [_sc_scatter_rows(part.reshape(n, part.shape[-1]), dests, pad_rows, n_rows) for part in hfps]
    yb = _expert_ffn(block_e, block_i, xbs, wg, wu, wd, layer)
    return [_sc_gather_rows(part, dest_by_slot) for part in yb]


def _apply_moe(x2, route, ytoks, out_prev, b0, b_full):
    b, s, d = x2.shape
    n = b * s
    prev = None if out_prev is None else out_prev.reshape(b_full * s, d)
    out = _combine(x2.reshape(n, d), route.reshape(n, ROUTE_LANES), ytoks, min(COMBINE_TOKENS, n), prev,
                   b0 * s, b_full * s)
    return out.reshape(b_full, s, d)


def _rope_lane_freq():
    inv = ROPE_THETA ** (-jnp.arange(0, ROPE_DIM // 2, dtype=F32) * 2.0 / ROPE_DIM)
    idx = np.full((HEAD_LANES,), -1, np.int64)
    for r in range(ROPE_DIM):
        idx[_head_lane(NOPE_DIM + r)] = r % ROPE_HALF
    return _gather_cols(inv[None, :], idx)


FAST_SOFTMAX_MAX_LOG2 = 60.0


def _score_bound_log2(qg, kg):
    return 1.02 * LOG2E * QK_DIM ** 0.5 * jnp.max(jnp.abs(qg)) * jnp.max(jnp.abs(kg))


def _partner_lanes(idx):
    out = np.full_like(idx, -1)
    for base in range(0, idx.shape[0], HEAD_LANES):
        for r in range(ROPE_DIM):
            lane = _head_lane(NOPE_DIM + r)
            out[base + lane] = idx[base + (lane + HALF_LANES) % HEAD_LANES]
    return out


def kernel(x, mem, positions, ln_mix, w_in, q_lat_norm, w_uq, kv_lat_norm, w_ukv, q_norm, k_norm, conv_w, conv_b,
           dt_bias, a_log, d_skip, ssd_norm, w_out, pool_w, pool_b, pool_scale, ln_xq, ln_mem, xq_w, xkv_w, xq_norm,
           xk_norm, xo_w, ln_ffn, rg_w, rg_b, re_w, re_b, exp_w_gate, exp_w_up, exp_w_down):
    b, s, d = x.shape
    depth = ln_mix.shape[0]
    tm = min(1024, s)
    tq = min(512, s)
    assert d == D_MODEL and s % tm == 0 and s % CHUNK == 0 and tm >= POOL_CARRY

    pos = positions.astype(F32)[..., None]
    invf = _rope_lane_freq()
    hsum = jnp.asarray(np.kron(np.eye(X_HEADS), np.ones((X_HEAD_DIM, X_HEAD_DIM))), BF16)
    ltri = jnp.asarray(np.tril(np.ones((tm, tm)), -1), BF16)
    row2 = lambda v: v.reshape(1, -1)
    lane_pad = lambda v: jnp.pad(v, (0, LANES - v.shape[0])).reshape(1, LANES)

    n_streams = MOE_STREAMS if b % MOE_STREAMS == 0 else 1
    nb = b // n_streams
    nt = s // tm
    streams = [dict(x=x, off=k * nb, pending=None) for k in range(n_streams)]
    for layer in range(depth):
        j = layer // 2
        kbd, vbd = _mem_kv(mem, row2(ln_mem[layer]), xkv_w[layer].astype(BF16),
                           row2(jnp.tile(xk_norm[layer], X_HEADS)), hsum)
        rw = jnp.pad(jnp.concatenate([rg_w[layer], re_w[layer]], axis=1),
                     ((0, 0), (0, ROUTE_LANES - MOE_GROUPS - N_EXPERTS)))
        rw_hi = rw.astype(BF16)
        rw_lo = (rw - rw_hi.astype(F32)).astype(BF16)
        rb = lane_pad(jnp.concatenate([rg_b[layer], re_b[layer]]))
        tail_args = [kbd, vbd, row2(ln_xq[layer]), xq_w[layer].astype(BF16), row2(jnp.tile(xq_norm[layer], X_HEADS)),
                     hsum, xo_w[layer].astype(BF16), row2(ln_ffn[layer]), rw_hi, rw_lo, rb, ltri]
        row = lambda i, jj: (i, jj, 0)
        fixed2 = lambda i, jj: (0, 0)
        posts = []
        if layer % 2 == 0:
            x = streams[0]["x"]
            assert all(st["x"] is x and st["pending"] is None for st in streams)
            win = _gather_cols(w_in[j], _win_col_index()).astype(BF16)
            q_idx = _head_col_index(QK_DIM, 0, QK_DIM)
            wuq = _gather_cols(w_uq[j], q_idx).astype(BF16)
            wuq_p = _gather_cols(w_uq[j], _partner_lanes(q_idx)).astype(BF16)
            wuk = _gather_cols(w_ukv[j], _head_col_index(NOPE_DIM + V_DIM, 0, NOPE_DIM)).astype(BF16)
            v_idx = np.full((A_HEADS * HEAD_LANES,), -1, np.int64)
            for hd in range(A_HEADS):
                v_idx[hd * HEAD_LANES:hd * HEAD_LANES + V_DIM] = hd * (NOPE_DIM + V_DIM) + NOPE_DIM + np.arange(V_DIM)
            wuv = _gather_cols(w_ukv[j], v_idx).astype(BF16)
            bound = _score_bound_log2(q_norm[j], k_norm[j])
            koff = jnp.zeros((1, HEAD_LANES), F32).at[0, SCORE_PAD_LANE].set(-bound)
            gain_idx = _head_col_index(QK_DIM, 0, QK_DIM)[:HEAD_LANES]
            lane_consts = jnp.concatenate(
                [_gather_cols(g[None, :], idx) for g in (q_norm[j], k_norm[j])
                 for idx in (gain_idx, _partner_lanes(gain_idx))]
                + [koff, invf, jnp.zeros((2, HEAD_LANES), F32)], axis=0)
            q, k, v, z, xbc, misc = _front_even(
                x, pos, lane_consts, row2(ln_mix[layer]), win, row2(q_lat_norm[j]), wuq, wuq_p,
                row2(kv_lat_norm[j]), wuk, wuv, tm)
            attn = lax.cond(bound <= FAST_SOFTMAX_MAX_LOG2,
                            functools.partial(_attention, tq=tq, online=False),
                            functools.partial(_attention, tq=tq, online=True), q, k, v)
            y = _ssd(xbc, misc, z, conv_w[j], row2(conv_b[j]), lane_pad(dt_bias[j]), lane_pad(a_log[j]),
                     row2(jnp.repeat(d_skip[j], SSD_HEAD_DIM)), row2(ssd_norm[j]))
            half = A_HEADS * V_DIM
            wout = w_out[j].astype(BF16)
            for k, st in enumerate(streams):
                row_k = lambda i, jj, off=st["off"]: (i + off, jj, 0)
                front_args = [x, attn, y, wout]
                front_specs = [pl.BlockSpec((1, tm, d), row_k), pl.BlockSpec((1, tm, half), row_k),
                               pl.BlockSpec((1, tm, D_INNER), row_k), pl.BlockSpec((half + D_INNER, d), fixed2)]
                posts.append(_post("even", front_args, front_specs, tail_args, nb, s, tm, [], b0=k * nb))
        else:
            for k, st in enumerate(streams):
                row_k = lambda i, jj, off=st["off"]: (i + off, jj, 0)
                front_args, front_specs = [st["x"]], [pl.BlockSpec((1, tm, d), row_k)]
                if st["pending"] is not None:
                    route_prev, ytoks = st["pending"]
                    pw = ytoks[0].shape[1]
                    front_args += [route_prev]
                    front_specs += [pl.BlockSpec((1, tm, ROUTE_LANES), row)]
                    for ytok in ytoks:
                        front_args += [ytok, ytok]
                        front_specs += [pl.BlockSpec((tm, pw), lambda i, jj: (i * nt + jj, 0)),
                                        pl.BlockSpec((tm, pw), lambda i, jj: (i * nt + jj + nb * nt, 0))]
                front_args += [row2(ln_mix[layer]), pool_w[j].astype(BF16), row2(pool_b[j]), row2(pool_scale[j])]
                front_specs += [pl.BlockSpec((1, d), fixed2),
                                pl.BlockSpec((len(POOL_WINDOWS), POOL_GROUP, POOL_GROUP), lambda i, jj: (0, 0, 0)),
                                pl.BlockSpec((1, d), fixed2), pl.BlockSpec((1, d), fixed2)]
                posts.append(_post("pool" if st["pending"] is None else "pool_pending", front_args, front_specs,
                                   tail_args, nb, s, tm, [pltpu.VMEM((POOL_CARRY, d), F32)], b0=k * nb))
        streams = []
        for x2, *hfps, route, route_t, counts in posts:
            ytoks = _moe(nb * s, hfps, route_t, counts, exp_w_gate, exp_w_up, exp_w_down, layer)
            streams.append(dict(x=x2, off=0, pending=(route, ytoks)))
        if layer + 1 == depth or (layer + 1) % 2 == 0:
            out = None
            for k, st in enumerate(streams):
                out = _apply_moe(st["x"], *st["pending"], out, k * nb, b)
            streams = [dict(x=out, off=k * nb, pending=None) for k in range(n_streams)]
    return streams[0]["x"]
```

```python
import functools

import numpy as np
import jax
import jax.numpy as jnp
from jax import lax
from jax.experimental import pallas as pl
from jax.experimental.pallas import tpu as pltpu
from jax.experimental.pallas import tpu_sc as plsc

F32 = jnp.float32
BF16 = jnp.bfloat16
U32 = jnp.uint32
I32 = jnp.int32

RMS_EPS = 1e-6
ROPE_THETA = 10000.0

D_MODEL = 1024
X_HEADS, X_HEAD_DIM = 4, 64
A_HEADS, NOPE_DIM, ROPE_DIM, V_DIM = 8, 64, 32, 64
QK_DIM = NOPE_DIM + ROPE_DIM
Q_LORA, KV_LORA = 256, 128
B_HEADS, SSD_HEAD_DIM, SSD_GROUPS, SSD_STATE, CONV_K, CHUNK = 8, 64, 2, 128, 4, 128
D_INNER = B_HEADS * SSD_HEAD_DIM
CONV_CH = D_INNER + 2 * SSD_GROUPS * SSD_STATE
POOL_WINDOWS = (2, 4, 8, 16)
POOL_GROUP = D_MODEL // 4
MOE_GROUPS, EXPERTS_PER_GROUP, TOP_K, EXPERT_FF = 4, 8, 2, 256
N_EXPERTS = MOE_GROUPS * EXPERTS_PER_GROUP

LANES = 128
HEAD_LANES = LANES
HALF_LANES = LANES // 2
ROPE_HALF = ROPE_DIM // 2
NOPE_HALF = NOPE_DIM // 2
POOL_CARRY = 16
CONV_CARRY = 8
VMEM_LIMIT = 56 * 1024 * 1024


def _cparams(sem):
    return pltpu.CompilerParams(dimension_semantics=sem, vmem_limit_bytes=VMEM_LIMIT)


def _rms(u, g):
    return u * lax.rsqrt(jnp.mean(u * u, axis=-1, keepdims=True) + RMS_EPS) * g


def _sigmoid(u):
    return 1.0 / (1.0 + jnp.exp(-u))


def _dot(a, b):
    return jnp.dot(a, b, preferred_element_type=F32)


def _dot_nt(a, b):
    return lax.dot_general(a, b, (((1,), (1,)), ((), ())), preferred_element_type=F32)


def _head_lane(d):
    if d < NOPE_HALF:
        return d
    if d < NOPE_DIM:
        return HALF_LANES + (d - NOPE_HALF)
    r = d - NOPE_DIM
    if r < ROPE_HALF:
        return NOPE_HALF + r
    return HALF_LANES + NOPE_HALF + (r - ROPE_HALF)


def _gather_cols(w, idx):
    w_ext = jnp.concatenate([w, jnp.zeros(w.shape[:-1] + (1,), w.dtype)], axis=-1)
    idx = np.where(idx < 0, w.shape[-1], idx)
    return jnp.take(w_ext, jnp.asarray(idx, dtype=jnp.int32), axis=-1)


IN_W = 2 * D_MODEL
_OFF_QLAT, _OFF_KVLAT, _OFF_MISC, _OFF_Z, _OFF_XBC = 0, 256, 384, 512, 1024


def _win_col_index():
    idx = np.full((IN_W,), -1, np.int64)
    idx[_OFF_QLAT:_OFF_QLAT + Q_LORA] = np.arange(Q_LORA)
    idx[_OFF_KVLAT:_OFF_KVLAT + KV_LORA] = Q_LORA + np.arange(KV_LORA)
    rope0 = Q_LORA + KV_LORA
    for r in range(ROPE_DIM):
        idx[_OFF_MISC + _head_lane(NOPE_DIM + r)] = rope0 + r
    z0 = rope0 + ROPE_DIM
    idx[_OFF_Z:_OFF_Z + D_INNER] = z0 + np.arange(D_INNER)
    xbc0 = z0 + D_INNER
    idx[_OFF_XBC:_OFF_XBC + CONV_CH] = xbc0 + np.arange(CONV_CH)
    dt0 = xbc0 + CONV_CH
    idx[_OFF_MISC:_OFF_MISC + B_HEADS] = dt0 + np.arange(B_HEADS)
    return idx


def _head_col_index(per_head, offset, count):
    idx = np.full((A_HEADS * HEAD_LANES,), -1, np.int64)
    for h in range(A_HEADS):
        for d in range(count):
            idx[h * HEAD_LANES + _head_lane(d)] = h * per_head + offset + d
    return idx


SCORE_PAD_LANE = NOPE_HALF + ROPE_HALF
ONES_LANE = V_DIM
LOG2E = 1.4426950408889634


def _front_even_kernel(x_ref, pos_ref, lc_ref, ln_ref, win_ref, qln_ref, wuq_ref, wuqr_ref, kvln_ref, wuk_ref, wuv_ref,
                       q_ref, k_ref, v_ref, z_ref, xbc_ref, misc_ref):
    x = x_ref[0]
    h = _rms(x, ln_ref[...]).astype(BF16)
    proj = _dot(h, win_ref[...])
    misc = proj[:, _OFF_MISC:_OFF_Z]
    z_ref[0] = proj[:, _OFF_Z:_OFF_XBC].astype(BF16)
    xbc_ref[0] = proj[:, _OFF_XBC:].astype(BF16)
    misc_ref[0] = misc
    ql = _rms(proj[:, _OFF_QLAT:_OFF_KVLAT], qln_ref[...]).astype(BF16)
    kvl = _rms(proj[:, _OFF_KVLAT:_OFF_MISC], kvln_ref[...]).astype(BF16)
    q = _dot(ql, wuq_ref[...])
    kn = _dot(kvl, wuk_ref[...])
    v = _dot(kvl, wuv_ref[...])
    lane = lax.broadcasted_iota(I32, (1, HEAD_LANES), 1)
    first_half = (lane >= NOPE_HALF) & (lane < NOPE_HALF + ROPE_HALF)
    second_half = (lane >= HALF_LANES + NOPE_HALF) & (lane < HALF_LANES + NOPE_HALF + ROPE_HALF)
    lc = lc_ref[...]
    qg, qg_p, kg, kg_p, k_off, invf = (lc[i:i + 1] for i in range(6))
    ones = jnp.ones((HEAD_LANES, HEAD_LANES), BF16)

    def lane_sumsq(u):
        return _dot((u * u).astype(BF16), ones)

    krope = jnp.where(first_half | second_half, misc, 0.0)
    kr_ss = lane_sumsq(krope)
    ang = pos_ref[0] * invf
    cos_t = jnp.cos(ang)
    sin_t = jnp.where(first_half, -jnp.sin(ang), jnp.sin(ang))
    q_one = (lane == SCORE_PAD_LANE).astype(F32)
    v_one = (lane == ONES_LANE).astype(F32)
    q_scale = QK_DIM ** -0.5 * LOG2E
    qa, qb = qg * cos_t * q_scale, qg_p * sin_t * q_scale
    ka, kc = kg * cos_t, pltpu.roll(krope, HALF_LANES, 1) * (kg_p * sin_t)
    q_p = _dot(ql, wuqr_ref[...])
    for hd in range(A_HEADS):
        sl = slice(hd * HEAD_LANES, (hd + 1) * HEAD_LANES)
        qs = q[:, sl]
        inv = lax.rsqrt(lane_sumsq(qs) * (1.0 / QK_DIM) + RMS_EPS)
        q_ref[0, hd] = ((qs * qa + q_p[:, sl] * qb) * inv + q_one).astype(BF16)
        kns = kn[:, sl]
        inv = lax.rsqrt((lane_sumsq(kns) + kr_ss) * (1.0 / QK_DIM) + RMS_EPS)
        k_ref[0, hd] = (((kns + krope) * ka + kc) * inv + k_off).astype(BF16)
        v_ref[0, hd] = (v[:, sl] + v_one).astype(BF16)


def _front_even(x, pos, lane_consts, ln, win, qln, wuq, wuq_p, kvln, wuk, wuv, tm):
    b, s, d = x.shape
    grid = (b, s // tm)
    row = lambda i, j: (i, j, 0)
    fixed2 = lambda i, j: (0, 0)
    head_row = lambda i, j: (i, 0, j, 0)
    hw = A_HEADS * HEAD_LANES
    return pl.pallas_call(
        _front_even_kernel,
        grid=grid,
        in_specs=[
            pl.BlockSpec((1, tm, d), row),
            pl.BlockSpec((1, tm, 1), row),
            pl.BlockSpec((8, HEAD_LANES), fixed2),
            pl.BlockSpec((1, d), fixed2),
            pl.BlockSpec((d, IN_W), fixed2),
            pl.BlockSpec((1, Q_LORA), fixed2),
            pl.BlockSpec((Q_LORA, hw), fixed2),
            pl.BlockSpec((Q_LORA, hw), fixed2),
            pl.BlockSpec((1, KV_LORA), fixed2),
            pl.BlockSpec((KV_LORA, hw), fixed2),
            pl.BlockSpec((KV_LORA, hw), fixed2),
        ],
        out_specs=[
            pl.BlockSpec((1, A_HEADS, tm, HEAD_LANES), head_row),
            pl.BlockSpec((1, A_HEADS, tm, HEAD_LANES), head_row),
            pl.BlockSpec((1, A_HEADS, tm, HEAD_LANES), head_row),
            pl.BlockSpec((1, tm, D_INNER), row),
            pl.BlockSpec((1, tm, CONV_CH), row),
            pl.BlockSpec((1, tm, HEAD_LANES), row),
        ],
        out_shape=[
            jax.ShapeDtypeStruct((b, A_HEADS, s, HEAD_LANES), BF16),
            jax.ShapeDtypeStruct((b, A_HEADS, s, HEAD_LANES), BF16),
            jax.ShapeDtypeStruct((b, A_HEADS, s, HEAD_LANES), BF16),
            jax.ShapeDtypeStruct((b, s, D_INNER), BF16),
            jax.ShapeDtypeStruct((b, s, CONV_CH), BF16),
            jax.ShapeDtypeStruct((b, s, HEAD_LANES), F32),
        ],
        compiler_params=_cparams(("parallel", "parallel")),
        name="front_even",
    )(x, pos, lane_consts, ln, win, qln, wuq, wuq_p, kvln, wuk, wuv)


HEADS_PER_STEP = 8


def _attn_kernel(q_ref, k_ref, v_ref, o_ref, *, tq, online):
    qi = pl.program_id(2)
    row = lax.broadcasted_iota(I32, (tq, tq), 0)
    col = lax.broadcasted_iota(I32, (tq, tq), 1)

    def head_step(hh, j, carry, masked):
        kj = k_ref[0, hh, pl.ds(j * tq, tq), :]
        vj = v_ref[0, hh, pl.ds(j * tq, tq), :]
        s = _dot_nt(q_ref[0, hh], kj)
        if masked:
            s = jnp.where(row >= col, s, -jnp.inf)
        if online:
            m, acc = carry
            m_new = jnp.maximum(m, jnp.max(s, axis=-1, keepdims=True))
            p = jnp.exp2(s - m_new).astype(BF16)
            return m_new, jnp.exp2(m - m_new) * acc + _dot(p, vj)
        return carry + _dot(jnp.exp2(s).astype(BF16), vj)

    def step(j, carries, masked):
        return tuple(head_step(hh, j, carries[hh], masked) for hh in range(HEADS_PER_STEP))

    acc0 = jnp.zeros((tq, HEAD_LANES), F32)
    init = (jnp.full((tq, 1), -jnp.inf, F32), acc0) if online else acc0
    carries = lax.fori_loop(0, qi, functools.partial(step, masked=False), (init,) * HEADS_PER_STEP)
    carries = step(qi, carries, True)
    outs = []
    for carry in carries:
        acc = carry[1] if online else carry
        outs.append(acc / acc[:, ONES_LANE:ONES_LANE + 1])
    lane = lax.broadcasted_iota(I32, (1, HEAD_LANES), 1)
    per_group = HEAD_LANES // V_DIM
    groups = []
    for g0 in range(0, HEADS_PER_STEP, per_group):
        out = outs[g0]
        for hh in range(1, per_group):
            out = jnp.where(lane >= hh * V_DIM, pltpu.roll(outs[g0 + hh], hh * V_DIM, 1), out)
        groups.append(out.astype(BF16))
    o_ref[0] = jnp.concatenate(groups, axis=1)


def _attention(q, k, v, tq, online):
    b, nh, s, _ = q.shape
    grid = (b, nh // HEADS_PER_STEP, s // tq)
    kv_spec = pl.BlockSpec((1, HEADS_PER_STEP, s, HEAD_LANES), lambda i, h, j: (i, h, 0, 0))
    return pl.pallas_call(
        functools.partial(_attn_kernel, tq=tq, online=online),
        grid=grid,
        in_specs=[
            pl.BlockSpec((1, HEADS_PER_STEP, tq, HEAD_LANES), lambda i, h, j: (i, h, j, 0)),
            kv_spec,
            kv_spec,
        ],
        out_specs=pl.BlockSpec((1, tq, HEADS_PER_STEP * V_DIM), lambda i, h, j: (i, j, h)),
        out_shape=jax.ShapeDtypeStruct((b, s, nh * V_DIM), BF16),
        compiler_params=_cparams(("parallel", "parallel", "parallel")),
        name="mla_attention_online" if online else "mla_attention",
    )(q, k, v)


def _ssd_kernel(xbc_ref, misc_ref, z_ref, cw_ref, cb_ref, dtb_ref, alog_ref, dskip_ref, gn_ref, y_ref,
                state_ref, carry_ref):
    c = pl.program_id(1)
    t = CHUNK
    rows = y_ref.shape[1]

    @pl.when(c == 0)
    def _():
        state_ref[...] = jnp.zeros_like(state_ref)
        carry_ref[...] = jnp.zeros_like(carry_ref)

    xr = xbc_ref[0].astype(F32)
    xcat = jnp.concatenate([carry_ref[...], xr], axis=0)
    carry_ref[...] = xr[rows - CONV_CARRY:, :]
    conv = jnp.zeros((rows, CONV_CH), F32) + cb_ref[...]
    for kk in range(CONV_K):
        sh = CONV_K - 1 - kk
        shifted = xcat if sh == 0 else pltpu.roll(xcat, sh, 0)
        conv = conv + cw_ref[kk:kk + 1, :] * shifted[CONV_CARRY:, :]
    xa = conv * _sigmoid(conv)
    xs = xa[:, :D_INNER]
    gw = SSD_GROUPS * SSD_STATE
    bmat = xa[:, D_INNER:D_INNER + gw]
    cmat = xa[:, D_INNER + gw:]

    u = misc_ref[0] + dtb_ref[...]
    dt = jnp.maximum(u, 0.0) + jnp.log(1.0 + jnp.exp(-jnp.abs(u)))
    a = -jnp.exp(alog_ref[...])
    lane = lax.broadcasted_iota(I32, (1, LANES), 1)
    adt_all = jnp.where(lane < B_HEADS, dt * a, 0.0)
    rowi = lax.broadcasted_iota(I32, (t, LANES), 0)
    tri = lax.broadcasted_iota(I32, (t, t), 0) >= lax.broadcasted_iota(I32, (t, t), 1)
    rep = B_HEADS // SSD_GROUPS
    y_chunks = []
    for ci in range(rows // t):
        sl = slice(ci * t, (ci + 1) * t)
        acs = adt_all[sl]
        sh = 1
        while sh < t:
            acs = acs + jnp.where(rowi >= sh, pltpu.roll(acs, sh, 0), 0.0)
            sh *= 2
        acs_t = acs.T
        ys = []
        for g in range(SSD_GROUPS):
            bg = bmat[sl, g * SSD_STATE:(g + 1) * SSD_STATE]
            cg = cmat[sl, g * SSD_STATE:(g + 1) * SSD_STATE]
            bg16, cg16 = bg.astype(BF16), cg.astype(BF16)
            cb = _dot_nt(cg16, bg16)
            bg_t = bg.T
            for r in range(rep):
                hd = g * rep + r
                col = acs[:, hd:hd + 1]
                rw = acs_t[hd:hd + 1, :]
                last = acs_t[hd:hd + 1, t - 1:t]
                decay = jnp.exp(jnp.where(tri, col - rw, -jnp.inf))
                xh = xs[sl, hd * SSD_HEAD_DIM:(hd + 1) * SSD_HEAD_DIM]
                xdt = (xh * dt[sl, hd:hd + 1]).astype(BF16)
                y_diag = _dot((cb * decay).astype(BF16), xdt)
                prev = state_ref[hd]
                y_off = _dot(cg16, prev.astype(BF16)) * jnp.exp(col)
                new_state = _dot((bg_t * jnp.exp(last - rw)).astype(BF16), xdt)
                state_ref[hd] = prev * jnp.exp(last) + new_state
                ys.append(y_diag + y_off)
        y_chunks.append(jnp.concatenate(ys, axis=1))
    y = jnp.concatenate(y_chunks, axis=0) + xs * dskip_ref[...]
    zf = z_ref[0].astype(F32)
    y = y * (zf * _sigmoid(zf))
    y_ref[0] = _rms(y, gn_ref[...]).astype(BF16)


SSD_CHUNKS_PER_STEP = 4


def _ssd(xbc, misc, z, cw, cb, dtb, alog, dskip, gn):
    b, s, _ = xbc.shape
    rows = CHUNK * SSD_CHUNKS_PER_STEP if s % (CHUNK * SSD_CHUNKS_PER_STEP) == 0 else CHUNK
    grid = (b, s // rows)
    row = lambda i, j: (i, j, 0)
    fixed2 = lambda i, j: (0, 0)
    return pl.pallas_call(
        _ssd_kernel,
        grid=grid,
        in_specs=[
            pl.BlockSpec((1, rows, CONV_CH), row),
            pl.BlockSpec((1, rows, LANES), row),
            pl.BlockSpec((1, rows, D_INNER), row),
            pl.BlockSpec((CONV_K, CONV_CH), fixed2),
            pl.BlockSpec((1, CONV_CH), fixed2),
            pl.BlockSpec((1, LANES), fixed2),
            pl.BlockSpec((1, LANES), fixed2),
            pl.BlockSpec((1, D_INNER), fixed2),
            pl.BlockSpec((1, D_INNER), fixed2),
        ],
        out_specs=pl.BlockSpec((1, rows, D_INNER), row),
        out_shape=jax.ShapeDtypeStruct((b, s, D_INNER), BF16),
        scratch_shapes=[
            pltpu.VMEM((B_HEADS, SSD_STATE, SSD_HEAD_DIM), F32),
            pltpu.VMEM((CONV_CARRY, CONV_CH), F32),
        ],
        compiler_params=_cparams(("parallel", "arbitrary")),
        name="ssd_scan",
    )(xbc, misc, z, cw, cb, dtb, alog, dskip, gn)


XW = X_HEADS * X_HEAD_DIM


def _mem_kv_kernel(mem_ref, ln_ref, wkv_ref, kg_ref, hsum_ref, kbd_ref, vbd_ref):
    m = mem_ref.shape[1]
    mn = _rms(mem_ref[0], ln_ref[...]).astype(BF16)
    kv = _dot(mn, wkv_ref[...])
    k, v = kv[:, :XW], kv[:, XW:]
    ss = _dot((k * k).astype(BF16), hsum_ref[...])
    kn = (k * lax.rsqrt(ss * (1.0 / X_HEAD_DIM) + RMS_EPS) * kg_ref[...]).astype(BF16)
    v16 = v.astype(BF16)
    head_of_lane = lax.shift_right_arithmetic(lax.broadcasted_iota(I32, (1, XW), 1), jnp.int32(_LOG2_XHD))
    for hd in range(X_HEADS):
        keep = head_of_lane == hd
        kbd_ref[0, hd * m:(hd + 1) * m, :] = jnp.where(keep, kn, jnp.zeros_like(kn))
        vbd_ref[0, hd * m:(hd + 1) * m, :] = jnp.where(keep, v16, jnp.zeros_like(v16))


def _mem_kv(mem, ln, wkv, kg, hsum):
    b, m, d = mem.shape
    fixed2 = lambda i: (0, 0)
    return pl.pallas_call(
        _mem_kv_kernel,
        grid=(b,),
        in_specs=[
            pl.BlockSpec((1, m, d), lambda i: (i, 0, 0)),
            pl.BlockSpec((1, d), fixed2),
            pl.BlockSpec((d, 2 * XW), fixed2),
            pl.BlockSpec((1, XW), fixed2),
            pl.BlockSpec((XW, XW), fixed2),
        ],
        out_specs=[
            pl.BlockSpec((1, X_HEADS * m, XW), lambda i: (i, 0, 0)),
            pl.BlockSpec((1, X_HEADS * m, XW), lambda i: (i, 0, 0)),
        ],
        out_shape=[
            jax.ShapeDtypeStruct((b, X_HEADS * m, XW), BF16),
            jax.ShapeDtypeStruct((b, X_HEADS * m, XW), BF16),
        ],
        compiler_params=_cparams(("parallel",)),
        name="mem_kv",
    )(mem, ln, wkv, kg, hsum)


ROUTE_LANES = LANES
ROW_PARTS = 2
_GROUP_LANE0 = 0
_EXPERT_LANE0 = MOE_GROUPS
_LOG2_EPG = EXPERTS_PER_GROUP.bit_length() - 1
_LOG2_XHD = X_HEAD_DIM.bit_length() - 1


def _pack_bf16_pairs(v):
    w = v.shape[1] // 2
    r = v.astype(BF16).astype(F32)
    hi = lax.bitcast_convert_type(r[:, :w], U32)
    lo = lax.bitcast_convert_type(r[:, w:], U32)
    return (hi & jnp.uint32(0xFFFF0000)) | (lo >> jnp.uint32(16))


def _unpack_bf16_pairs(u):
    hi = lax.bitcast_convert_type(u & jnp.uint32(0xFFFF0000), F32)
    lo = lax.bitcast_convert_type(u << jnp.uint32(16), F32)
    return hi, lo


TAIL_SPLIT = 1


def _tail_rows(x1, rows, kbd_ref, vbd_ref, lnq_ref, wq_ref, qg_ref, hsum_ref, wo_ref, lnf_ref, rwh_ref, rwl_ref,
               rb_ref, x2_ref, hfp_refs):
    tm = x1.shape[0]
    m = kbd_ref.shape[1] // X_HEADS
    hq = _rms(x1, lnq_ref[...]).astype(BF16)
    q = _dot(hq, wq_ref[...])
    ss = _dot((q * q).astype(BF16), hsum_ref[...])
    qn = (q * lax.rsqrt(ss * (1.0 / X_HEAD_DIM) + RMS_EPS) * qg_ref[...] * (X_HEAD_DIM ** -0.5)).astype(BF16)
    s = _dot_nt(qn, kbd_ref[0])
    ps = []
    for hd in range(X_HEADS):
        sh = s[:, hd * m:(hd + 1) * m]
        e = jnp.exp(sh - jnp.max(sh, axis=-1, keepdims=True))
        ps.append((e / jnp.sum(e, axis=-1, keepdims=True)).astype(BF16))
    o = _dot(jnp.concatenate(ps, axis=1), vbd_ref[0]).astype(BF16)
    x2 = x1 + _dot(o, wo_ref[...])
    x2_ref[0, rows, :] = x2

    hf = _rms(x2, lnf_ref[...])
    hf_hi = hf.astype(BF16)
    packed = _pack_bf16_pairs(hf)
    pw = packed.shape[1] // ROW_PARTS
    for c in range(ROW_PARTS):
        hfp_refs[c][0, rows, :] = packed[:, c * pw:(c + 1) * pw]
    hf_lo = (hf - hf_hi.astype(F32)).astype(BF16)
    logits = _dot(hf_hi, rwh_ref[...]) + _dot(hf_hi, rwl_ref[...]) + _dot(hf_lo, rwh_ref[...]) + rb_ref[...]

    lane_i = lax.broadcasted_iota(I32, (tm, ROUTE_LANES), 1)
    lane = lane_i.astype(F32)
    big = float(ROUTE_LANES)
    neg = -jnp.inf
    gl = jnp.where(lane_i < MOE_GROUPS, logits, neg)
    gmax = jnp.max(gl, axis=-1, keepdims=True)
    gsum = jnp.sum(jnp.exp(gl - gmax), axis=-1, keepdims=True)
    g_p = 1.0 / gsum
    g_idx = jnp.min(jnp.where(gl == gmax, lane, big), axis=-1, keepdims=True)
    e_lane = lane_i - _EXPERT_LANE0
    grp_of_lane = lax.shift_right_arithmetic(e_lane, jnp.int32(_LOG2_EPG)).astype(F32)
    in_grp = (e_lane >= 0) & (e_lane < N_EXPERTS) & (grp_of_lane == g_idx)
    el = jnp.where(in_grp, logits, neg)
    emax = jnp.max(el, axis=-1, keepdims=True)
    idx1 = jnp.min(jnp.where(el == emax, lane, big), axis=-1, keepdims=True)
    el2 = jnp.where(lane == idx1, neg, el)
    emax2 = jnp.max(el2, axis=-1, keepdims=True)
    idx2 = jnp.min(jnp.where(el2 == emax2, lane, big), axis=-1, keepdims=True)
    r2 = jnp.exp(emax2 - emax)
    gate1 = g_p / (1.0 + r2)
    gate2 = g_p * r2 / (1.0 + r2)
    e1 = idx1 - float(_EXPERT_LANE0)
    e2 = idx2 - float(_EXPERT_LANE0)

    route = jnp.where(lane == 0, e1, 0.0)
    route = jnp.where(lane == 1, e2, route)
    route = jnp.where(lane == 2, gate1, route)
    route = jnp.where(lane == 3, gate2, route)
    return route, (lane == e1).astype(F32), (lane == e2).astype(F32)


def _tail(x1, kbd_ref, vbd_ref, lnq_ref, wq_ref, qg_ref, hsum_ref, wo_ref, lnf_ref, rwh_ref, rwl_ref, rb_ref,
          ltri_ref, x2_ref, *out_refs):
    hfp_refs, (route_ref, route_t_ref, cnt_ref) = out_refs[:ROW_PARTS], out_refs[ROW_PARTS:]
    tm = x1.shape[0]
    tr = tm // TAIL_SPLIT
    parts = [_tail_rows(x1[r * tr:(r + 1) * tr], slice(r * tr, (r + 1) * tr), kbd_ref, vbd_ref, lnq_ref, wq_ref,
                        qg_ref, hsum_ref, wo_ref, lnf_ref, rwh_ref, rwl_ref, rb_ref, x2_ref, hfp_refs)
             for r in range(TAIL_SPLIT)]
    route, oh1, oh2 = (jnp.concatenate([p[i] for p in parts], axis=0) for i in range(3))
    both = oh1 + oh2
    before = _dot(ltri_ref[...], both.astype(BF16))
    rank1 = jnp.sum(before * oh1, axis=-1, keepdims=True)
    rank2 = jnp.sum(before * oh2, axis=-1, keepdims=True)
    cnt_ref[0] = jnp.broadcast_to(jnp.sum(both, axis=0, keepdims=True), cnt_ref.shape[1:])
    lane = lax.broadcasted_iota(I32, (tm, ROUTE_LANES), 1)
    route = jnp.where(lane == 4, rank1, route)
    route = jnp.where(lane == 5, rank2, route)
    route_ref[0] = route
    route_t_ref[0] = route.T[:route_t_ref.shape[1], :]


_TAIL_IN = 12


def _post_even_kernel(x_ref, a_ref, y_ref, wout_ref, *rest):
    tail_in, outs = rest[:_TAIL_IN], rest[_TAIL_IN:]
    half = wout_ref.shape[0] // 2
    x1 = x_ref[0] + _dot(a_ref[0], wout_ref[:half, :]) + _dot(y_ref[0], wout_ref[half:, :])
    _tail(x1, *tail_in, *outs)


def _add_expert_rows(x, route, ys):
    g1, g2 = route[:, 2:3], route[:, 3:4]
    his, los = [], []
    for y1, y2 in ys:
        h1, l1 = _unpack_bf16_pairs(y1)
        h2, l2 = _unpack_bf16_pairs(y2)
        his.append(h1 * g1 + h2 * g2)
        los.append(l1 * g1 + l2 * g2)
    return x + jnp.concatenate(his + los, axis=1)


N_PENDING = 1 + 2 * ROW_PARTS


def _post_pool_kernel(x_ref, *rest, pending):
    if pending:
        route_prev_ref, y_refs, rest = rest[0], rest[1:N_PENDING], rest[N_PENDING:]
    (ln_ref, pw_ref, pb_ref, ps_ref), rest = rest[:4], rest[4:]
    tail_in, outs, carry_ref = rest[:_TAIL_IN], rest[_TAIL_IN:-1], rest[-1]
    j = pl.program_id(1)
    tm = x_ref.shape[1]

    @pl.when(j == 0)
    def _():
        carry_ref[...] = jnp.zeros_like(carry_ref)

    x = x_ref[0]
    if pending:
        x = _add_expert_rows(x, route_prev_ref[0], [(y_refs[2 * c][...], y_refs[2 * c + 1][...])
                                                    for c in range(ROW_PARTS)])
    h = _rms(x, ln_ref[...])
    pos = (j * tm + 1 + lax.broadcasted_iota(I32, (tm, 1), 0)).astype(F32)
    mixed = []
    for g, w in enumerate(POOL_WINDOWS):
        sl = slice(g * POOL_GROUP, (g + 1) * POOL_GROUP)
        hg = h[:, sl]
        acc = jnp.concatenate([carry_ref[:, sl], hg], axis=0)
        sh = 1
        while sh < w:
            acc = acc + pltpu.roll(acc, sh, 0)
            sh *= 2
        win = acc[POOL_CARRY:, :]
        dlt = win / jnp.minimum(pos, float(w)) - hg
        mixed.append(_dot(dlt.astype(BF16), pw_ref[g]))
    carry_ref[...] = h[tm - POOL_CARRY:, :]
    y = (jnp.concatenate(mixed, axis=1) + pb_ref[...]) * ps_ref[...]
    _tail(x + y, *tail_in, *outs)


def _post(kind, front_args, front_specs, tail_args, b, s, tm, scratch, b0=0):
    d = D_MODEL
    m4 = tail_args[0].shape[1]
    row = lambda i, j: (i, j, 0)
    fixed2 = lambda i, j: (0, 0)
    per_b = lambda i, j: (i + b0, 0, 0)
    tail_specs = [
        pl.BlockSpec((1, m4, XW), per_b),
        pl.BlockSpec((1, m4, XW), per_b),
        pl.BlockSpec((1, d), fixed2),
        pl.BlockSpec((d, XW), fixed2),
        pl.BlockSpec((1, XW), fixed2),
        pl.BlockSpec((XW, XW), fixed2),
        pl.BlockSpec((XW, d), fixed2),
        pl.BlockSpec((1, d), fixed2),
        pl.BlockSpec((d, ROUTE_LANES), fixed2),
        pl.BlockSpec((d, ROUTE_LANES), fixed2),
        pl.BlockSpec((1, ROUTE_LANES), fixed2),
        pl.BlockSpec((tm, tm), fixed2),
    ]
    nt = s // tm
    pw = d // 2 // ROW_PARTS
    kernel = {"even": _post_even_kernel,
              "pool": functools.partial(_post_pool_kernel, pending=False),
              "pool_pending": functools.partial(_post_pool_kernel, pending=True)}[kind]
    return pl.pallas_call(
        kernel,
        grid=(b, nt),
        in_specs=front_specs + tail_specs,
        out_specs=[pl.BlockSpec((1, tm, d), row)]
        + [pl.BlockSpec((1, tm, pw), row)] * ROW_PARTS
        + [pl.BlockSpec((1, tm, ROUTE_LANES), row),
           pl.BlockSpec((1, 8, tm), lambda i, j: (i * nt + j, 0, 0)),
           pl.BlockSpec((1, 8, ROUTE_LANES), lambda i, j: (i * nt + j, 0, 0))],
        out_shape=[jax.ShapeDtypeStruct((b, s, d), F32)]
        + [jax.ShapeDtypeStruct((b, s, pw), U32)] * ROW_PARTS
        + [jax.ShapeDtypeStruct((b, s, ROUTE_LANES), F32),
           jax.ShapeDtypeStruct((b * nt, 8, tm), F32),
           jax.ShapeDtypeStruct((b * nt, 8, ROUTE_LANES), F32)],
        scratch_shapes=scratch,
        compiler_params=_cparams(("parallel", "arbitrary")),
        name="post_" + kind,
    )(*front_args, *tail_args)


FFN_ROWS = 1024
COMBINE_TOKENS = 512
MOE_STREAMS = 2
SC_GATHER_WINDOW = 128


def _sc_gather_rows(table, idx):
    m, w = idx.shape[0], table.shape[1]
    mesh = plsc.VectorSubcoreMesh(core_axis_name="core", subcore_axis_name="subcore")

    @pl.kernel(out_type=jax.ShapeDtypeStruct((m, w), table.dtype), mesh=mesh, name="moe_row_gather")
    def gather(t_hbm, i_hbm, o_hbm):
        def body(i_vmem, o_vmem):
            pltpu.sync_copy(t_hbm.at[i_vmem.at[0]], o_vmem)

        pltpu.emit_pipeline(
            body,
            grid=(m // SC_GATHER_WINDOW,),
            in_specs=[pl.BlockSpec((1, SC_GATHER_WINDOW), lambda i: (0, i))],
            out_specs=[pl.BlockSpec((SC_GATHER_WINDOW, w), lambda i: (i, 0))],
            core_axis_name=("core", "subcore"),
            dimension_semantics=(pltpu.PARALLEL,),
        )(i_hbm, o_hbm)

    return gather(table, idx.reshape(1, m))


def _sc_scatter_rows(src, dests, pad_rows, n_rows):
    n, w = src.shape
    win = SC_GATHER_WINDOW
    mesh = plsc.VectorSubcoreMesh(core_axis_name="core", subcore_axis_name="subcore")
    idx_spec = pl.BlockSpec((1, win), lambda i: (0, i))
    split = dict(core_axis_name=("core", "subcore"), dimension_semantics=(pltpu.PARALLEL,))

    @pl.kernel(out_type=jax.ShapeDtypeStruct((n_rows, w), src.dtype), mesh=mesh, name="moe_row_scatter")
    def scatter(s_hbm, z_hbm, p_hbm, *rest):
        d_hbms, o_hbm = rest[:-1], rest[-1]

        def body(s_vmem, *i_vmems):
            for i_vmem in i_vmems:
                pltpu.sync_copy(s_vmem, o_hbm.at[i_vmem.at[0]])

        pltpu.emit_pipeline(
            body, grid=(n // win,),
            in_specs=[pl.BlockSpec((win, w), lambda i: (i, 0))] + [idx_spec] * len(dests),
            out_specs=[], **split)(s_hbm, *d_hbms)

        def zero_body(z_vmem, i_vmem):
            pltpu.sync_copy(z_vmem, o_hbm.at[i_vmem.at[0]])

        pltpu.emit_pipeline(
            zero_body, grid=(pad_rows.shape[0] // win,),
            in_specs=[pl.BlockSpec((win, w), lambda i: (0, 0)), idx_spec],
            out_specs=[], **split)(z_hbm, p_hbm)

    zeros = jnp.zeros((win, w), src.dtype)
    return scatter(src, zeros, pad_rows.reshape(1, -1), *[dd.reshape(1, n) for dd in dests])


def _ffn_kernel(be_ref, bi_ref, *refs):
    xb_refs, (wg_ref, wu_ref, wd_ref) = refs[:ROW_PARTS], refs[ROW_PARTS:ROW_PARTS + 3]
    yb_refs, (wg_s, wu_s, wd_s) = refs[ROW_PARTS + 3:2 * ROW_PARTS + 3], refs[2 * ROW_PARTS + 3:]
    i = pl.program_id(0)
    changed = jnp.logical_or(i == 0, be_ref[i] != be_ref[jnp.maximum(i - 1, 0)])

    @pl.when(changed)
    def _():
        wg_s[...] = wg_ref[0, 0].astype(BF16)
        wu_s[...] = wu_ref[0, 0].astype(BF16)
        wd_s[...] = wd_ref[0, 0].astype(BF16)

    @pl.when(bi_ref[i] == i)
    def _():
        half = wg_s.shape[0] // 2
        gate = up = None
        for c in range(ROW_PARTS):
            hi, lo = _unpack_bf16_pairs(xb_refs[c][...])
            hi, lo = hi.astype(BF16), lo.astype(BF16)
            pw = hi.shape[1]
            hs, ls = slice(c * pw, (c + 1) * pw), slice(half + c * pw, half + (c + 1) * pw)
            g = _dot(hi, wg_s[hs, :]) + _dot(lo, wg_s[ls, :])
            u = _dot(hi, wu_s[hs, :]) + _dot(lo, wu_s[ls, :])
            gate, up = (g, u) if gate is None else (gate + g, up + u)
        act = (gate * _sigmoid(gate) * up).astype(BF16)
        packed = _pack_bf16_pairs(_dot(act, wd_s[...]))
        pw = packed.shape[1] // ROW_PARTS
        for c in range(ROW_PARTS):
            yb_refs[c][...] = packed[:, c * pw:(c + 1) * pw]


def _expert_ffn(block_e, block_i, xbs, wg, wu, wd, layer):
    n_rows, pw = xbs[0].shape
    d, ff = wg.shape[2], wg.shape[3]
    n_blk = n_rows // FFN_ROWS
    row_spec = pl.BlockSpec((FFN_ROWS, pw), lambda i, be, bi: (bi[i], 0))
    return pl.pallas_call(
        _ffn_kernel,
        grid_spec=pltpu.PrefetchScalarGridSpec(
            num_scalar_prefetch=2,
            grid=(n_blk,),
            in_specs=[row_spec] * ROW_PARTS + [
                pl.BlockSpec((1, 1, d, ff), lambda i, be, bi: (layer, be[i], 0, 0)),
                pl.BlockSpec((1, 1, d, ff), lambda i, be, bi: (layer, be[i], 0, 0)),
                pl.BlockSpec((1, 1, ff, d), lambda i, be, bi: (layer, be[i], 0, 0)),
            ],
            out_specs=[row_spec] * ROW_PARTS,
            scratch_shapes=[pltpu.VMEM((d, ff), BF16), pltpu.VMEM((d, ff), BF16), pltpu.VMEM((ff, d), BF16)],
        ),
        out_shape=[jax.ShapeDtypeStruct((n_rows, pw), U32)] * ROW_PARTS,
        compiler_params=_cparams(("arbitrary",)),
        name="moe_expert_ffn",
    )(block_e, block_i, *xbs, wg, wu, wd)


def _combine_kernel(x_ref, route_ref, *refs):
    y_refs, o_ref = refs[:2 * ROW_PARTS], refs[-1]
    o_ref[...] = _add_expert_rows(x_ref[...], route_ref[...],
                                  [(y_refs[2 * c][...], y_refs[2 * c + 1][...]) for c in range(ROW_PARTS)])


def _combine(x2, route, ytoks, tc, out_prev, row0, n_full):
    n, d = x2.shape
    w = ytoks[0].shape[1]
    nsteps = n // tc
    blk0 = row0 // tc
    specs = [pl.BlockSpec((tc, d), lambda i: (i, 0)), pl.BlockSpec((tc, ROUTE_LANES), lambda i: (i, 0))]
    args = [x2, route]
    for ytok in ytoks:
        specs += [pl.BlockSpec((tc, w), lambda i: (i, 0)), pl.BlockSpec((tc, w), lambda i: (i + nsteps, 0))]
        args += [ytok, ytok]
    aliases = {}
    if out_prev is not None:
        specs.append(pl.BlockSpec(memory_space=pl.ANY))
        args.append(out_prev)
        aliases = {len(args) - 1: 0}
    return pl.pallas_call(
        _combine_kernel,
        grid=(nsteps,),
        in_specs=specs,
        out_specs=pl.BlockSpec((tc, d), lambda i: (i + blk0, 0)),
        out_shape=jax.ShapeDtypeStruct((n_full, d), F32),
        input_output_aliases=aliases,
        compiler_params=_cparams(("parallel",)),
        name="moe_combine",
    )(*args)


def _moe(n, hfps, route_t, counts, wg, wu, wd, layer):
    cnt = counts[:, 0, :N_EXPERTS].astype(I32)
    total = jnp.sum(cnt, axis=0)
    padded = (total + FFN_ROWS - 1) // FFN_ROWS * FFN_ROWS
    pad_end = jnp.cumsum(padded)
    pad_start = pad_end - padded
    tile_base = pad_start[None, :] + jnp.cumsum(cnt, axis=0) - cnt
    expert_ids = jnp.arange(N_EXPERTS, dtype=I32)[:, None]
    dests = []
    for k in range(TOP_K):
        ek = route_t[:, k, :].astype(I32)
        base = jnp.sum(jnp.where(ek[:, None, :] == expert_ids, tile_base[:, :, None], 0), axis=1)
        dests.append((base + route_t[:, 4 + k, :].astype(I32)).reshape(n))
    dest_by_slot = jnp.concatenate(dests)
    n_blk = (n * TOP_K) // FFN_ROWS + N_EXPERTS
    n_rows = n_blk * FFN_ROWS
    used = pad_end[-1] // FFN_ROWS
    block_i = jnp.minimum(jnp.arange(n_blk, dtype=I32), used - 1).astype(I32)
    ended = (pad_end[None, :] <= (block_i * FFN_ROWS)[:, None]).astype(I32)
    block_e = jnp.minimum(jnp.sum(ended, axis=1), N_EXPERTS - 1).astype(I32)
    seg_len = jnp.concatenate([padded - total, (n_rows - pad_end[-1])[None]])
    seg_first = jnp.concatenate([pad_start + total, pad_end[-1:]])
    seg_end = jnp.cumsum(seg_len)
    jpad = jnp.arange(n_rows - n * TOP_K, dtype=I32)
    shift = seg_first - (seg_end - seg_len)
    step = (shift[1:] - shift[:-1])[:, None]
    pad_rows = (jpad + shift[0] + jnp.sum(jnp.where(jpad[None, :] >= seg_end[:-1, None], step, 0), axis=0)).astype(I32)

    xbs = [_sc_scatter_rows(part.reshape(n, part.shape[-1]), dests, pad_rows, n_rows) for part in hfps]
    yb = _expert_ffn(block_e, block_i, xbs, wg, wu, wd, layer)
    return [_sc_gather_rows(part, dest_by_slot) for part in yb]


def _apply_moe(x2, route, ytoks, out_prev, b0, b_full):
    b, s, d = x2.shape
    n = b * s
    prev = None if out_prev is None else out_prev.reshape(b_full * s, d)
    out = _combine(x2.reshape(n, d), route.reshape(n, ROUTE_LANES), ytoks, min(COMBINE_TOKENS, n), prev,
                   b0 * s, b_full * s)
    return out.reshape(b_full, s, d)


def _rope_lane_freq():
    inv = ROPE_THETA ** (-jnp.arange(0, ROPE_DIM // 2, dtype=F32) * 2.0 / ROPE_DIM)
    idx = np.full((HEAD_LANES,), -1, np.int64)
    for r in range(ROPE_DIM):
        idx[_head_lane(NOPE_DIM + r)] = r % ROPE_HALF
    return _gather_cols(inv[None, :], idx)


FAST_SOFTMAX_MAX_LOG2 = 60.0


def _score_bound_log2(qg, kg):
    return 1.02 * LOG2E * QK_DIM ** 0.5 * jnp.max(jnp.abs(qg)) * jnp.max(jnp.abs(kg))


def _partner_lanes(idx):
    out = np.full_like(idx, -1)
    for base in range(0, idx.shape[0], HEAD_LANES):
        for r in range(ROPE_DIM):
            lane = _head_lane(NOPE_DIM + r)
            out[base + lane] = idx[base + (lane + HALF_LANES) % HEAD_LANES]
    return out


def kernel(x, mem, positions, ln_mix, w_in, q_lat_norm, w_uq, kv_lat_norm, w_ukv, q_norm, k_norm, conv_w, conv_b,
           dt_bias, a_log, d_skip, ssd_norm, w_out, pool_w, pool_b, pool_scale, ln_xq, ln_mem, xq_w, xkv_w, xq_norm,
           xk_norm, xo_w, ln_ffn, rg_w, rg_b, re_w, re_b, exp_w_gate, exp_w_up, exp_w_down):
    b, s, d = x.shape
    depth = ln_mix.shape[0]
    tm = min(1024, s)
    tq = min(512, s)
    assert d == D_MODEL and s % tm == 0 and s % CHUNK == 0 and tm >= POOL_CARRY

    pos = positions.astype(F32)[..., None]
    invf = _rope_lane_freq()
    hsum = jnp.asarray(np.kron(np.eye(X_HEADS), np.ones((X_HEAD_DIM, X_HEAD_DIM))), BF16)
    ltri = jnp.asarray(np.tril(np.ones((tm, tm)), -1), BF16)
    row2 = lambda v: v.reshape(1, -1)
    lane_pad = lambda v: jnp.pad(v, (0, LANES - v.shape[0])).reshape(1, LANES)

    n_streams = MOE_STREAMS if b % MOE_STREAMS == 0 else 1
    nb = b // n_streams
    nt = s // tm
    streams = [dict(x=x, off=k * nb, pending=None) for k in range(n_streams)]
    for layer in range(depth):
        j = layer // 2
        kbd, vbd = _mem_kv(mem, row2(ln_mem[layer]), xkv_w[layer].astype(BF16),
                           row2(jnp.tile(xk_norm[layer], X_HEADS)), hsum)
        rw = jnp.pad(jnp.concatenate([rg_w[layer], re_w[layer]], axis=1),
                     ((0, 0), (0, ROUTE_LANES - MOE_GROUPS - N_EXPERTS)))
        rw_hi = rw.astype(BF16)
        rw_lo = (rw - rw_hi.astype(F32)).astype(BF16)
        rb = lane_pad(jnp.concatenate([rg_b[layer], re_b[layer]]))
        tail_args = [kbd, vbd, row2(ln_xq[layer]), xq_w[layer].astype(BF16), row2(jnp.tile(xq_norm[layer], X_HEADS)),
                     hsum, xo_w[layer].astype(BF16), row2(ln_ffn[layer]), rw_hi, rw_lo, rb, ltri]
        row = lambda i, jj: (i, jj, 0)
        fixed2 = lambda i, jj: (0, 0)
        posts = []
        if layer % 2 == 0:
            x = streams[0]["x"]
            assert all(st["x"] is x and st["pending"] is None for st in streams)
            win = _gather_cols(w_in[j], _win_col_index()).astype(BF16)
            q_idx = _head_col_index(QK_DIM, 0, QK_DIM)
            wuq = _gather_cols(w_uq[j], q_idx).astype(BF16)
            wuq_p = _gather_cols(w_uq[j], _partner_lanes(q_idx)).astype(BF16)
            wuk = _gather_cols(w_ukv[j], _head_col_index(NOPE_DIM + V_DIM, 0, NOPE_DIM)).astype(BF16)
            v_idx = np.full((A_HEADS * HEAD_LANES,), -1, np.int64)
            for hd in range(A_HEADS):
                v_idx[hd * HEAD_LANES:hd * HEAD_LANES + V_DIM] = hd * (NOPE_DIM + V_DIM) + NOPE_DIM + np.arange(V_DIM)
            wuv = _gather_cols(w_ukv[j], v_idx).astype(BF16)
            bound = _score_bound_log2(q_norm[j], k_norm[j])
            koff = jnp.zeros((1, HEAD_LANES), F32).at[0, SCORE_PAD_LANE].set(-bound)
            gain_idx = _head_col_index(QK_DIM, 0, QK_DIM)[:HEAD_LANES]
            lane_consts = jnp.concatenate(
                [_gather_cols(g[None, :], idx) for g in (q_norm[j], k_norm[j])
                 for idx in (gain_idx, _partner_lanes(gain_idx))]
                + [koff, invf, jnp.zeros((2, HEAD_LANES), F32)], axis=0)
            q, k, v, z, xbc, misc = _front_even(
                x, pos, lane_consts, row2(ln_mix[layer]), win, row2(q_lat_norm[j]), wuq, wuq_p,
                row2(kv_lat_norm[j]), wuk, wuv, tm)
            attn = lax.cond(bound <= FAST_SOFTMAX_MAX_LOG2,
                            functools.partial(_attention, tq=tq, online=False),
                            functools.partial(_attention, tq=tq, online=True), q, k, v)
            y = _ssd(xbc, misc, z, conv_w[j], row2(conv_b[j]), lane_pad(dt_bias[j]), lane_pad(a_log[j]),
                     row2(jnp.repeat(d_skip[j], SSD_HEAD_DIM)), row2(ssd_norm[j]))
            half = A_HEADS * V_DIM
            wout = w_out[j].astype(BF16)
            for k, st in enumerate(streams):
                row_k = lambda i, jj, off=st["off"]: (i + off, jj, 0)
                front_args = [x, attn, y, wout]
                front_specs = [pl.BlockSpec((1, tm, d), row_k), pl.BlockSpec((1, tm, half), row_k),
                               pl.BlockSpec((1, tm, D_INNER), row_k), pl.BlockSpec((half + D_INNER, d), fixed2)]
                posts.append(_post("even", front_args, front_specs, tail_args, nb, s, tm, [], b0=k * nb))
        else:
            for k, st in enumerate(streams):
                row_k = lambda i, jj, off=st["off"]: (i + off, jj, 0)
                front_args, front_specs = [st["x"]], [pl.BlockSpec((1, tm, d), row_k)]
                if st["pending"] is not None:
                    route_prev, ytoks = st["pending"]
                    pw = ytoks[0].shape[1]
                    front_args += [route_prev]
                    front_specs += [pl.BlockSpec((1, tm, ROUTE_LANES), row)]
                    for ytok in ytoks:
                        front_args += [ytok, ytok]
                        front_specs += [pl.BlockSpec((tm, pw), lambda i, jj: (i * nt + jj, 0)),
                                        pl.BlockSpec((tm, pw), lambda i, jj: (i * nt + jj + nb * nt, 0))]
                front_args += [row2(ln_mix[layer]), pool_w[j].astype(BF16), row2(pool_b[j]), row2(pool_scale[j])]
                front_specs += [pl.BlockSpec((1, d), fixed2),
                                pl.BlockSpec((len(POOL_WINDOWS), POOL_GROUP, POOL_GROUP), lambda i, jj: (0, 0, 0)),
                                pl.BlockSpec((1, d), fixed2), pl.BlockSpec((1, d), fixed2)]
                posts.append(_post("pool" if st["pending"] is None else "pool_pending", front_args, front_specs,
                                   tail_args, nb, s, tm, [pltpu.VMEM((POOL_CARRY, d), F32)], b0=k * nb))
        streams = []
        for x2, *hfps, route, route_t, counts in posts:
            ytoks = _moe(nb * s, hfps, route_t, counts, exp_w_gate, exp_w_up, exp_w_down, layer)
            streams.append(dict(x=x2, off=0, pending=(route, ytoks)))
        if layer + 1 == depth or (layer + 1) % 2 == 0:
            out = None
            for k, st in enumerate(streams):
                out = _apply_moe(st["x"], *st["pending"], out, k * nb, b)
            streams = [dict(x=out, off=k * nb, pending=None) for k in range(n_streams)]
    return streams[0]["x"]
```

```python
import functools

import numpy as np
import jax
import jax.numpy as jnp
from jax import lax
from jax.experimental import pallas as pl
from jax.experimental.pallas import tpu as pltpu
from jax.experimental.pallas import tpu_sc as plsc

F32 = jnp.float32
BF16 = jnp.bfloat16
U32 = jnp.uint32
I32 = jnp.int32

RMS_EPS = 1e-6
ROPE_THETA = 10000.0

D_MODEL = 1024
X_HEADS, X_HEAD_DIM = 4, 64
A_HEADS, NOPE_DIM, ROPE_DIM, V_DIM = 8, 64, 32, 64
QK_DIM = NOPE_DIM + ROPE_DIM
Q_LORA, KV_LORA = 256, 128
B_HEADS, SSD_HEAD_DIM, SSD_GROUPS, SSD_STATE, CONV_K, CHUNK = 8, 64, 2, 128, 4, 128
D_INNER = B_HEADS * SSD_HEAD_DIM
CONV_CH = D_INNER + 2 * SSD_GROUPS * SSD_STATE
POOL_WINDOWS = (2, 4, 8, 16)
POOL_GROUP = D_MODEL // 4
MOE_GROUPS, EXPERTS_PER_GROUP, TOP_K, EXPERT_FF = 4, 8, 2, 256
N_EXPERTS = MOE_GROUPS * EXPERTS_PER_GROUP

LANES = 128
HEAD_LANES = LANES
HALF_LANES = LANES // 2
ROPE_HALF = ROPE_DIM // 2
NOPE_HALF = NOPE_DIM // 2
POOL_CARRY = 16
CONV_CARRY = 8
VMEM_LIMIT = 56 * 1024 * 1024


def _cparams(sem):
    return pltpu.CompilerParams(dimension_semantics=sem, vmem_limit_bytes=VMEM_LIMIT)


def _rms(u, g):
    return u * lax.rsqrt(jnp.mean(u * u, axis=-1, keepdims=True) + RMS_EPS) * g


def _sigmoid(u):
    return 1.0 / (1.0 + jnp.exp(-u))


def _dot(a, b):
    return jnp.dot(a, b, preferred_element_type=F32)


def _dot_nt(a, b):
    return lax.dot_general(a, b, (((1,), (1,)), ((), ())), preferred_element_type=F32)


def _head_lane(d):
    if d < NOPE_HALF:
        return d
    if d < NOPE_DIM:
        return HALF_LANES + (d - NOPE_HALF)
    r = d - NOPE_DIM
    if r < ROPE_HALF:
        return NOPE_HALF + r
    return HALF_LANES + NOPE_HALF + (r - ROPE_HALF)


def _gather_cols(w, idx):
    w_ext = jnp.concatenate([w, jnp.zeros(w.shape[:-1] + (1,), w.dtype)], axis=-1)
    idx = np.where(idx < 0, w.shape[-1], idx)
    return jnp.take(w_ext, jnp.asarray(idx, dtype=jnp.int32), axis=-1)


_OFF_QLAT = 0
_OFF_KVLAT = _OFF_QLAT + Q_LORA
_OFF_MISC = _OFF_KVLAT + KV_LORA
_OFF_Z = _OFF_MISC + LANES
_OFF_XBC = _OFF_Z + D_INNER
IN_W = _OFF_XBC + CONV_CH


def _win_col_index():
    idx = np.full((IN_W,), -1, np.int64)
    idx[_OFF_QLAT:_OFF_QLAT + Q_LORA] = np.arange(Q_LORA)
    idx[_OFF_KVLAT:_OFF_KVLAT + KV_LORA] = Q_LORA + np.arange(KV_LORA)
    rope0 = Q_LORA + KV_LORA
    for r in range(ROPE_DIM):
        idx[_OFF_MISC + _head_lane(NOPE_DIM + r)] = rope0 + r
    z0 = rope0 + ROPE_DIM
    idx[_OFF_Z:_OFF_Z + D_INNER] = z0 + np.arange(D_INNER)
    xbc0 = z0 + D_INNER
    idx[_OFF_XBC:_OFF_XBC + CONV_CH] = xbc0 + np.arange(CONV_CH)
    dt0 = xbc0 + CONV_CH
    idx[_OFF_MISC:_OFF_MISC + B_HEADS] = dt0 + np.arange(B_HEADS)
    return idx


def _head_col_index(per_head, offset, count):
    idx = np.full((A_HEADS * HEAD_LANES,), -1, np.int64)
    for h in range(A_HEADS):
        for d in range(count):
            idx[h * HEAD_LANES + _head_lane(d)] = h * per_head + offset + d
    return idx


SCORE_PAD_LANE = NOPE_HALF + ROPE_HALF
ONES_LANE = V_DIM
LOG2E = 1.4426950408889634


def _front_even_kernel(x_ref, pos_ref, lc_ref, ln_ref, win_ref, qln_ref, wuq_ref, wuqr_ref, kvln_ref, wuk_ref, wuv_ref,
                       q_ref, k_ref, v_ref, z_ref, xbc_ref, misc_ref):
    x = x_ref[0]
    h = _rms(x, ln_ref[...]).astype(BF16)
    proj = _dot(h, win_ref[...])
    misc = proj[:, _OFF_MISC:_OFF_Z]
    z_ref[0] = proj[:, _OFF_Z:_OFF_XBC].astype(BF16)
    xbc_ref[0] = proj[:, _OFF_XBC:].astype(BF16)
    misc_ref[0] = misc
    ql = _rms(proj[:, _OFF_QLAT:_OFF_KVLAT], qln_ref[...]).astype(BF16)
    kvl = _rms(proj[:, _OFF_KVLAT:_OFF_MISC], kvln_ref[...]).astype(BF16)
    q = _dot(ql, wuq_ref[...])
    kn = _dot(kvl, wuk_ref[...])
    v = _dot(kvl, wuv_ref[...])
    lane = lax.broadcasted_iota(I32, (1, HEAD_LANES), 1)
    first_half = (lane >= NOPE_HALF) & (lane < NOPE_HALF + ROPE_HALF)
    second_half = (lane >= HALF_LANES + NOPE_HALF) & (lane < HALF_LANES + NOPE_HALF + ROPE_HALF)
    lc = lc_ref[...]
    qg, qg_p, kg, kg_p, k_off, invf = (lc[i:i + 1] for i in range(6))
    ones = jnp.ones((HEAD_LANES, HEAD_LANES), BF16)

    def lane_sumsq(u):
        return _dot((u * u).astype(BF16), ones)

    krope = jnp.where(first_half | second_half, misc, 0.0)
    kr_ss = lane_sumsq(krope)
    ang = pos_ref[0] * invf
    cos_t = jnp.cos(ang)
    sin_t = jnp.where(first_half, -jnp.sin(ang), jnp.sin(ang))
    q_one = (lane == SCORE_PAD_LANE).astype(F32)
    v_one = (lane == ONES_LANE).astype(F32)
    q_scale = QK_DIM ** -0.5 * LOG2E
    qa, qb = qg * cos_t * q_scale, qg_p * sin_t * q_scale
    ka, kc = kg * cos_t, pltpu.roll(krope, HALF_LANES, 1) * (kg_p * sin_t)
    q_p = _dot(ql, wuqr_ref[...])
    for hd in range(A_HEADS):
        sl = slice(hd * HEAD_LANES, (hd + 1) * HEAD_LANES)
        qs = q[:, sl]
        inv = lax.rsqrt(lane_sumsq(qs) * (1.0 / QK_DIM) + RMS_EPS)
        q_ref[0, hd] = ((qs * qa + q_p[:, sl] * qb) * inv + q_one).astype(BF16)
        kns = kn[:, sl]
        inv = lax.rsqrt((lane_sumsq(kns) + kr_ss) * (1.0 / QK_DIM) + RMS_EPS)
        k_ref[0, hd] = (((kns + krope) * ka + kc) * inv + k_off).astype(BF16)
        v_ref[0, hd] = (v[:, sl] + v_one).astype(BF16)


def _front_even(x, pos, lane_consts, ln, win, qln, wuq, wuq_p, kvln, wuk, wuv, tm):
    b, s, d = x.shape
    grid = (b, s // tm)
    row = lambda i, j: (i, j, 0)
    fixed2 = lambda i, j: (0, 0)
    head_row = lambda i, j: (i, 0, j, 0)
    hw = A_HEADS * HEAD_LANES
    return pl.pallas_call(
        _front_even_kernel,
        grid=grid,
        in_specs=[
            pl.BlockSpec((1, tm, d), row),
            pl.BlockSpec((1, tm, 1), row),
            pl.BlockSpec((8, HEAD_LANES), fixed2),
            pl.BlockSpec((1, d), fixed2),
            pl.BlockSpec((d, IN_W), fixed2),
            pl.BlockSpec((1, Q_LORA), fixed2),
            pl.BlockSpec((Q_LORA, hw), fixed2),
            pl.BlockSpec((Q_LORA, hw), fixed2),
            pl.BlockSpec((1, KV_LORA), fixed2),
            pl.BlockSpec((KV_LORA, hw), fixed2),
            pl.BlockSpec((KV_LORA, hw), fixed2),
        ],
        out_specs=[
            pl.BlockSpec((1, A_HEADS, tm, HEAD_LANES), head_row),
            pl.BlockSpec((1, A_HEADS, tm, HEAD_LANES), head_row),
            pl.BlockSpec((1, A_HEADS, tm, HEAD_LANES), head_row),
            pl.BlockSpec((1, tm, D_INNER), row),
            pl.BlockSpec((1, tm, CONV_CH), row),
            pl.BlockSpec((1, tm, HEAD_LANES), row),
        ],
        out_shape=[
            jax.ShapeDtypeStruct((b, A_HEADS, s, HEAD_LANES), BF16),
            jax.ShapeDtypeStruct((b, A_HEADS, s, HEAD_LANES), BF16),
            jax.ShapeDtypeStruct((b, A_HEADS, s, HEAD_LANES), BF16),
            jax.ShapeDtypeStruct((b, s, D_INNER), BF16),
            jax.ShapeDtypeStruct((b, s, CONV_CH), BF16),
            jax.ShapeDtypeStruct((b, s, HEAD_LANES), F32),
        ],
        compiler_params=_cparams(("parallel", "parallel")),
        name="front_even",
    )(x, pos, lane_consts, ln, win, qln, wuq, wuq_p, kvln, wuk, wuv)


HEADS_PER_STEP = 8


def _attn_kernel(q_ref, k_ref, v_ref, o_ref, *, tq, online):
    qi = pl.program_id(2)
    row = lax.broadcasted_iota(I32, (tq, tq), 0)
    col = lax.broadcasted_iota(I32, (tq, tq), 1)

    def head_step(hh, j, carry, masked):
        kj = k_ref[0, hh, pl.ds(j * tq, tq), :]
        vj = v_ref[0, hh, pl.ds(j * tq, tq), :]
        s = _dot_nt(q_ref[0, hh], kj)
        if masked:
            s = jnp.where(row >= col, s, -jnp.inf)
        if online:
            m, acc = carry
            m_new = jnp.maximum(m, jnp.max(s, axis=-1, keepdims=True))
            p = jnp.exp2(s - m_new).astype(BF16)
            return m_new, jnp.exp2(m - m_new) * acc + _dot(p, vj)
        return carry + _dot(jnp.exp2(s).astype(BF16), vj)

    def step(j, carries, masked):
        return tuple(head_step(hh, j, carries[hh], masked) for hh in range(HEADS_PER_STEP))

    acc0 = jnp.zeros((tq, HEAD_LANES), F32)
    init = (jnp.full((tq, 1), -jnp.inf, F32), acc0) if online else acc0
    carries = lax.fori_loop(0, qi, functools.partial(step, masked=False), (init,) * HEADS_PER_STEP)
    carries = step(qi, carries, True)
    outs = []
    for carry in carries:
        acc = carry[1] if online else carry
        outs.append(acc / acc[:, ONES_LANE:ONES_LANE + 1])
    lane = lax.broadcasted_iota(I32, (1, HEAD_LANES), 1)
    per_group = HEAD_LANES // V_DIM
    groups = []
    for g0 in range(0, HEADS_PER_STEP, per_group):
        out = outs[g0]
        for hh in range(1, per_group):
            out = jnp.where(lane >= hh * V_DIM, pltpu.roll(outs[g0 + hh], hh * V_DIM, 1), out)
        groups.append(out.astype(BF16))
    o_ref[0] = jnp.concatenate(groups, axis=1)


def _attention(q, k, v, tq, online):
    b, nh, s, _ = q.shape
    grid = (b, nh // HEADS_PER_STEP, s // tq)
    kv_spec = pl.BlockSpec((1, HEADS_PER_STEP, s, HEAD_LANES), lambda i, h, j: (i, h, 0, 0))
    return pl.pallas_call(
        functools.partial(_attn_kernel, tq=tq, online=online),
        grid=grid,
        in_specs=[
            pl.BlockSpec((1, HEADS_PER_STEP, tq, HEAD_LANES), lambda i, h, j: (i, h, j, 0)),
            kv_spec,
            kv_spec,
        ],
        out_specs=pl.BlockSpec((1, tq, HEADS_PER_STEP * V_DIM), lambda i, h, j: (i, j, h)),
        out_shape=jax.ShapeDtypeStruct((b, s, nh * V_DIM), BF16),
        compiler_params=_cparams(("parallel", "parallel", "parallel")),
        name="mla_attention_online" if online else "mla_attention",
    )(q, k, v)


def _ssd_kernel(xbc_ref, misc_ref, z_ref, cw_ref, cb_ref, dtb_ref, alog_ref, dskip_ref, gn_ref, y_ref,
                state_ref, carry_ref):
    c = pl.program_id(1)
    t = CHUNK
    rows = y_ref.shape[1]

    @pl.when(c == 0)
    def _():
        state_ref[...] = jnp.zeros_like(state_ref)
        carry_ref[...] = jnp.zeros_like(carry_ref)

    xr = xbc_ref[0].astype(F32)
    xcat = jnp.concatenate([carry_ref[...], xr], axis=0)
    carry_ref[...] = xr[rows - CONV_CARRY:, :]
    conv = jnp.zeros((rows, CONV_CH), F32) + cb_ref[...]
    for kk in range(CONV_K):
        sh = CONV_K - 1 - kk
        shifted = xcat if sh == 0 else pltpu.roll(xcat, sh, 0)
        conv = conv + cw_ref[kk:kk + 1, :] * shifted[CONV_CARRY:, :]
    xa = conv * _sigmoid(conv)
    xs = xa[:, :D_INNER]
    gw = SSD_GROUPS * SSD_STATE
    bmat = xa[:, D_INNER:D_INNER + gw]
    cmat = xa[:, D_INNER + gw:]

    u = misc_ref[0] + dtb_ref[...]
    dt = jnp.maximum(u, 0.0) + jnp.log(1.0 + jnp.exp(-jnp.abs(u)))
    a = -jnp.exp(alog_ref[...])
    lane = lax.broadcasted_iota(I32, (1, LANES), 1)
    adt_all = jnp.where(lane < B_HEADS, dt * a, 0.0)
    rowi = lax.broadcasted_iota(I32, (t, LANES), 0)
    tri = lax.broadcasted_iota(I32, (t, t), 0) >= lax.broadcasted_iota(I32, (t, t), 1)
    rep = B_HEADS // SSD_GROUPS
    y_chunks = []
    for ci in range(rows // t):
        sl = slice(ci * t, (ci + 1) * t)
        acs = adt_all[sl]
        sh = 1
        while sh < t:
            acs = acs + jnp.where(rowi >= sh, pltpu.roll(acs, sh, 0), 0.0)
            sh *= 2
        acs_t = acs.T
        ys = []
        for g in range(SSD_GROUPS):
            bg = bmat[sl, g * SSD_STATE:(g + 1) * SSD_STATE]
            cg = cmat[sl, g * SSD_STATE:(g + 1) * SSD_STATE]
            bg16, cg16 = bg.astype(BF16), cg.astype(BF16)
            cb = _dot_nt(cg16, bg16)
            bg_t = bg.T
            for r in range(rep):
                hd = g * rep + r
                col = acs[:, hd:hd + 1]
                rw = acs_t[hd:hd + 1, :]
                last = acs_t[hd:hd + 1, t - 1:t]
                decay = jnp.exp(jnp.where(tri, col - rw, -jnp.inf))
                xh = xs[sl, hd * SSD_HEAD_DIM:(hd + 1) * SSD_HEAD_DIM]
                xdt = (xh * dt[sl, hd:hd + 1]).astype(BF16)
                y_diag = _dot((cb * decay).astype(BF16), xdt)
                prev = state_ref[hd]
                y_off = _dot(cg16, prev.astype(BF16)) * jnp.exp(col)
                new_state = _dot((bg_t * jnp.exp(last - rw)).astype(BF16), xdt)
                state_ref[hd] = prev * jnp.exp(last) + new_state
                ys.append(y_diag + y_off)
        y_chunks.append(jnp.concatenate(ys, axis=1))
    y = jnp.concatenate(y_chunks, axis=0) + xs * dskip_ref[...]
    zf = z_ref[0].astype(F32)
    y = y * (zf * _sigmoid(zf))
    y_ref[0] = _rms(y, gn_ref[...]).astype(BF16)


SSD_CHUNKS_PER_STEP = 4


def _ssd(xbc, misc, z, cw, cb, dtb, alog, dskip, gn):
    b, s, _ = xbc.shape
    rows = CHUNK * SSD_CHUNKS_PER_STEP if s % (CHUNK * SSD_CHUNKS_PER_STEP) == 0 else CHUNK
    grid = (b, s // rows)
    row = lambda i, j: (i, j, 0)
    fixed2 = lambda i, j: (0, 0)
    return pl.pallas_call(
        _ssd_kernel,
        grid=grid,
        in_specs=[
            pl.BlockSpec((1, rows, CONV_CH), row),
            pl.BlockSpec((1, rows, LANES), row),
            pl.BlockSpec((1, rows, D_INNER), row),
            pl.BlockSpec((CONV_K, CONV_CH), fixed2),
            pl.BlockSpec((1, CONV_CH), fixed2),
            pl.BlockSpec((1, LANES), fixed2),
            pl.BlockSpec((1, LANES), fixed2),
            pl.BlockSpec((1, D_INNER), fixed2),
            pl.BlockSpec((1, D_INNER), fixed2),
        ],
        out_specs=pl.BlockSpec((1, rows, D_INNER), row),
        out_shape=jax.ShapeDtypeStruct((b, s, D_INNER), BF16),
        scratch_shapes=[
            pltpu.VMEM((B_HEADS, SSD_STATE, SSD_HEAD_DIM), F32),
            pltpu.VMEM((CONV_CARRY, CONV_CH), F32),
        ],
        compiler_params=_cparams(("parallel", "arbitrary")),
        name="ssd_scan",
    )(xbc, misc, z, cw, cb, dtb, alog, dskip, gn)


XW = X_HEADS * X_HEAD_DIM


def _mem_kv_kernel(mem_ref, ln_ref, wkv_ref, kg_ref, hsum_ref, kbd_ref, vbd_ref):
    m = mem_ref.shape[1]
    mn = _rms(mem_ref[0], ln_ref[...]).astype(BF16)
    kv = _dot(mn, wkv_ref[...])
    k, v = kv[:, :XW], kv[:, XW:]
    ss = _dot((k * k).astype(BF16), hsum_ref[...])
    kn = (k * lax.rsqrt(ss * (1.0 / X_HEAD_DIM) + RMS_EPS) * kg_ref[...]).astype(BF16)
    v16 = v.astype(BF16)
    head_of_lane = lax.shift_right_arithmetic(lax.broadcasted_iota(I32, (1, XW), 1), jnp.int32(_LOG2_XHD))
    for hd in range(X_HEADS):
        keep = head_of_lane == hd
        kbd_ref[0, hd * m:(hd + 1) * m, :] = jnp.where(keep, kn, jnp.zeros_like(kn))
        vbd_ref[0, hd * m:(hd + 1) * m, :] = jnp.where(keep, v16, jnp.zeros_like(v16))


def _mem_kv(mem, ln, wkv, kg, hsum):
    b, m, d = mem.shape
    fixed2 = lambda i: (0, 0)
    return pl.pallas_call(
        _mem_kv_kernel,
        grid=(b,),
        in_specs=[
            pl.BlockSpec((1, m, d), lambda i: (i, 0, 0)),
            pl.BlockSpec((1, d), fixed2),
            pl.BlockSpec((d, 2 * XW), fixed2),
            pl.BlockSpec((1, XW), fixed2),
            pl.BlockSpec((XW, XW), fixed2),
        ],
        out_specs=[
            pl.BlockSpec((1, X_HEADS * m, XW), lambda i: (i, 0, 0)),
            pl.BlockSpec((1, X_HEADS * m, XW), lambda i: (i, 0, 0)),
        ],
        out_shape=[
            jax.ShapeDtypeStruct((b, X_HEADS * m, XW), BF16),
            jax.ShapeDtypeStruct((b, X_HEADS * m, XW), BF16),
        ],
        compiler_params=_cparams(("parallel",)),
        name="mem_kv",
    )(mem, ln, wkv, kg, hsum)


ROUTE_LANES = LANES
ROW_PARTS = 2
_GROUP_LANE0 = 0
_EXPERT_LANE0 = MOE_GROUPS
_LOG2_EPG = EXPERTS_PER_GROUP.bit_length() - 1
_LOG2_XHD = X_HEAD_DIM.bit_length() - 1


def _pack_bf16_pairs(v):
    w = v.shape[1] // 2
    r = v.astype(BF16).astype(F32)
    hi = lax.bitcast_convert_type(r[:, :w], U32)
    lo = lax.bitcast_convert_type(r[:, w:], U32)
    return (hi & jnp.uint32(0xFFFF0000)) | (lo >> jnp.uint32(16))


def _unpack_bf16_pairs(u):
    hi = lax.bitcast_convert_type(u & jnp.uint32(0xFFFF0000), F32)
    lo = lax.bitcast_convert_type(u << jnp.uint32(16), F32)
    return hi, lo


def _tail_rows(x1, kbd_ref, vbd_ref, lnq_ref, wq_ref, qg_ref, hsum_ref, wo_ref, lnf_ref, rwh_ref, rwl_ref,
               rb_ref, x2_ref, hfp_refs):
    tm = x1.shape[0]
    m = kbd_ref.shape[1] // X_HEADS
    hq = _rms(x1, lnq_ref[...]).astype(BF16)
    q = _dot(hq, wq_ref[...])
    ss = _dot((q * q).astype(BF16), hsum_ref[...])
    qn = (q * lax.rsqrt(ss * (1.0 / X_HEAD_DIM) + RMS_EPS) * qg_ref[...] * (X_HEAD_DIM ** -0.5)).astype(BF16)
    s = _dot_nt(qn, kbd_ref[0])
    ps = []
    for hd in range(X_HEADS):
        sh = s[:, hd * m:(hd + 1) * m]
        e = jnp.exp(sh - jnp.max(sh, axis=-1, keepdims=True))
        ps.append((e / jnp.sum(e, axis=-1, keepdims=True)).astype(BF16))
    o = _dot(jnp.concatenate(ps, axis=1), vbd_ref[0]).astype(BF16)
    x2 = x1 + _dot(o, wo_ref[...])
    x2_ref[0] = x2

    hf = _rms(x2, lnf_ref[...])
    hf_hi = hf.astype(BF16)
    packed = _pack_bf16_pairs(hf)
    pw = packed.shape[1] // ROW_PARTS
    for c in range(ROW_PARTS):
        hfp_refs[c][0] = packed[:, c * pw:(c + 1) * pw]
    hf_lo = (hf - hf_hi.astype(F32)).astype(BF16)
    logits = _dot(hf_hi, rwh_ref[...]) + _dot(hf_hi, rwl_ref[...]) + _dot(hf_lo, rwh_ref[...]) + rb_ref[...]

    lane_i = lax.broadcasted_iota(I32, (tm, ROUTE_LANES), 1)
    lane = lane_i.astype(F32)
    big = float(ROUTE_LANES)
    neg = -jnp.inf
    gl = jnp.where(lane_i < MOE_GROUPS, logits, neg)
    gmax = jnp.max(gl, axis=-1, keepdims=True)
    gsum = jnp.sum(jnp.exp(gl - gmax), axis=-1, keepdims=True)
    g_p = 1.0 / gsum
    g_idx = jnp.min(jnp.where(gl == gmax, lane, big), axis=-1, keepdims=True)
    e_lane = lane_i - _EXPERT_LANE0
    grp_of_lane = lax.shift_right_arithmetic(e_lane, jnp.int32(_LOG2_EPG)).astype(F32)
    in_grp = (e_lane >= 0) & (e_lane < N_EXPERTS) & (grp_of_lane == g_idx)
    el = jnp.where(in_grp, logits, neg)
    emax = jnp.max(el, axis=-1, keepdims=True)
    idx1 = jnp.min(jnp.where(el == emax, lane, big), axis=-1, keepdims=True)
    el2 = jnp.where(lane == idx1, neg, el)
    emax2 = jnp.max(el2, axis=-1, keepdims=True)
    idx2 = jnp.min(jnp.where(el2 == emax2, lane, big), axis=-1, keepdims=True)
    r2 = jnp.exp(emax2 - emax)
    gate1 = g_p / (1.0 + r2)
    gate2 = g_p * r2 / (1.0 + r2)
    e1 = idx1 - float(_EXPERT_LANE0)
    e2 = idx2 - float(_EXPERT_LANE0)

    route = jnp.where(lane == 0, e1, 0.0)
    route = jnp.where(lane == 1, e2, route)
    route = jnp.where(lane == 2, gate1, route)
    route = jnp.where(lane == 3, gate2, route)
    return route, (lane == e1).astype(F32), (lane == e2).astype(F32)


def _tail(x1, kbd_ref, vbd_ref, lnq_ref, wq_ref, qg_ref, hsum_ref, wo_ref, lnf_ref, rwh_ref, rwl_ref, rb_ref,
          ltri_ref, x2_ref, *out_refs):
    hfp_refs, (route_ref, route_t_ref, cnt_ref) = out_refs[:ROW_PARTS], out_refs[ROW_PARTS:]
    tm = x1.shape[0]
    route, oh1, oh2 = _tail_rows(x1, kbd_ref, vbd_ref, lnq_ref, wq_ref, qg_ref, hsum_ref, wo_ref, lnf_ref, rwh_ref,
                                 rwl_ref, rb_ref, x2_ref, hfp_refs)
    both = oh1 + oh2
    before = _dot(ltri_ref[...], both.astype(BF16))
    rank1 = jnp.sum(before * oh1, axis=-1, keepdims=True)
    rank2 = jnp.sum(before * oh2, axis=-1, keepdims=True)
    cnt_ref[0] = jnp.broadcast_to(jnp.sum(both, axis=0, keepdims=True), cnt_ref.shape[1:])
    lane = lax.broadcasted_iota(I32, (tm, ROUTE_LANES), 1)
    route = jnp.where(lane == 4, rank1, route)
    route = jnp.where(lane == 5, rank2, route)
    route_ref[0] = route
    route_t_ref[0] = route.T[:route_t_ref.shape[1], :]


_TAIL_IN = 12


def _post_even_kernel(x_ref, a_ref, y_ref, wout_ref, *rest):
    tail_in, outs = rest[:_TAIL_IN], rest[_TAIL_IN:]
    half = wout_ref.shape[0] // 2
    x1 = x_ref[0] + _dot(a_ref[0], wout_ref[:half, :]) + _dot(y_ref[0], wout_ref[half:, :])
    _tail(x1, *tail_in, *outs)


def _add_expert_rows(x, route, ys):
    g1, g2 = route[:, 2:3], route[:, 3:4]
    his, los = [], []
    for y1, y2 in ys:
        h1, l1 = _unpack_bf16_pairs(y1)
        h2, l2 = _unpack_bf16_pairs(y2)
        his.append(h1 * g1 + h2 * g2)
        los.append(l1 * g1 + l2 * g2)
    return x + jnp.concatenate(his + los, axis=1)


N_PENDING = 1 + 2 * ROW_PARTS


def _post_pool_kernel(x_ref, *rest, pending):
    if pending:
        route_prev_ref, y_refs, rest = rest[0], rest[1:N_PENDING], rest[N_PENDING:]
    (ln_ref, pw_ref, pb_ref, ps_ref), rest = rest[:4], rest[4:]
    tail_in, outs, carry_ref = rest[:_TAIL_IN], rest[_TAIL_IN:-1], rest[-1]
    j = pl.program_id(1)
    tm = x_ref.shape[1]

    @pl.when(j == 0)
    def _():
        carry_ref[...] = jnp.zeros_like(carry_ref)

    x = x_ref[0]
    if pending:
        x = _add_expert_rows(x, route_prev_ref[0], [(y_refs[2 * c][...], y_refs[2 * c + 1][...])
                                                    for c in range(ROW_PARTS)])
    h = _rms(x, ln_ref[...])
    pos = (j * tm + 1 + lax.broadcasted_iota(I32, (tm, 1), 0)).astype(F32)
    mixed = []
    for g, w in enumerate(POOL_WINDOWS):
        sl = slice(g * POOL_GROUP, (g + 1) * POOL_GROUP)
        hg = h[:, sl]
        acc = jnp.concatenate([carry_ref[:, sl], hg], axis=0)
        sh = 1
        while sh < w:
            acc = acc + pltpu.roll(acc, sh, 0)
            sh *= 2
        win = acc[POOL_CARRY:, :]
        dlt = win / jnp.minimum(pos, float(w)) - hg
        mixed.append(_dot(dlt.astype(BF16), pw_ref[g]))
    carry_ref[...] = h[tm - POOL_CARRY:, :]
    y = (jnp.concatenate(mixed, axis=1) + pb_ref[...]) * ps_ref[...]
    _tail(x + y, *tail_in, *outs)


def _post(kind, front_args, front_specs, tail_args, b, s, tm, scratch, b0=0):
    d = D_MODEL
    m4 = tail_args[0].shape[1]
    row = lambda i, j: (i, j, 0)
    fixed2 = lambda i, j: (0, 0)
    per_b = lambda i, j: (i + b0, 0, 0)
    tail_specs = [
        pl.BlockSpec((1, m4, XW), per_b),
        pl.BlockSpec((1, m4, XW), per_b),
        pl.BlockSpec((1, d), fixed2),
        pl.BlockSpec((d, XW), fixed2),
        pl.BlockSpec((1, XW), fixed2),
        pl.BlockSpec((XW, XW), fixed2),
        pl.BlockSpec((XW, d), fixed2),
        pl.BlockSpec((1, d), fixed2),
        pl.BlockSpec((d, ROUTE_LANES), fixed2),
        pl.BlockSpec((d, ROUTE_LANES), fixed2),
        pl.BlockSpec((1, ROUTE_LANES), fixed2),
        pl.BlockSpec((tm, tm), fixed2),
    ]
    nt = s // tm
    pw = d // 2 // ROW_PARTS
    kernel = {"even": _post_even_kernel,
              "pool": functools.partial(_post_pool_kernel, pending=False),
              "pool_pending": functools.partial(_post_pool_kernel, pending=True)}[kind]
    return pl.pallas_call(
        kernel,
        grid=(b, nt),
        in_specs=front_specs + tail_specs,
        out_specs=[pl.BlockSpec((1, tm, d), row)]
        + [pl.BlockSpec((1, tm, pw), row)] * ROW_PARTS
        + [pl.BlockSpec((1, tm, ROUTE_LANES), row),
           pl.BlockSpec((1, 8, tm), lambda i, j: (i * nt + j, 0, 0)),
           pl.BlockSpec((1, 8, ROUTE_LANES), lambda i, j: (i * nt + j, 0, 0))],
        out_shape=[jax.ShapeDtypeStruct((b, s, d), F32)]
        + [jax.ShapeDtypeStruct((b, s, pw), U32)] * ROW_PARTS
        + [jax.ShapeDtypeStruct((b, s, ROUTE_LANES), F32),
           jax.ShapeDtypeStruct((b * nt, 8, tm), F32),
           jax.ShapeDtypeStruct((b * nt, 8, ROUTE_LANES), F32)],
        scratch_shapes=scratch,
        compiler_params=_cparams(("parallel", "arbitrary")),
        name="post_" + kind,
    )(*front_args, *tail_args)


FFN_ROWS = 1024
COMBINE_TOKENS = 1024
MOE_STREAMS = 2
SC_GATHER_WINDOW = 128


def _sc_gather_rows(table, idx):
    m, w = idx.shape[0], table.shape[1]
    mesh = plsc.VectorSubcoreMesh(core_axis_name="core", subcore_axis_name="subcore")

    @pl.kernel(out_type=jax.ShapeDtypeStruct((m, w), table.dtype), mesh=mesh, name="moe_row_gather")
    def gather(t_hbm, i_hbm, o_hbm):
        def body(i_vmem, o_vmem):
            pltpu.sync_copy(t_hbm.at[i_vmem.at[0]], o_vmem)

        pltpu.emit_pipeline(
            body,
            grid=(m // SC_GATHER_WINDOW,),
            in_specs=[pl.BlockSpec((1, SC_GATHER_WINDOW), lambda i: (0, i))],
            out_specs=[pl.BlockSpec((SC_GATHER_WINDOW, w), lambda i: (i, 0))],
            core_axis_name=("core", "subcore"),
            dimension_semantics=(pltpu.PARALLEL,),
        )(i_hbm, o_hbm)

    return gather(table, idx.reshape(1, m))


def _sc_scatter_rows(src, dests, pad_rows, n_rows):
    n, w = src.shape
    win = SC_GATHER_WINDOW
    mesh = plsc.VectorSubcoreMesh(core_axis_name="core", subcore_axis_name="subcore")
    idx_spec = pl.BlockSpec((1, win), lambda i: (0, i))
    split = dict(core_axis_name=("core", "subcore"), dimension_semantics=(pltpu.PARALLEL,))

    @pl.kernel(out_type=jax.ShapeDtypeStruct((n_rows, w), src.dtype), mesh=mesh, name="moe_row_scatter")
    def scatter(s_hbm, z_hbm, p_hbm, *rest):
        d_hbms, o_hbm = rest[:-1], rest[-1]

        def body(s_vmem, *i_vmems):
            for i_vmem in i_vmems:
                pltpu.sync_copy(s_vmem, o_hbm.at[i_vmem.at[0]])

        pltpu.emit_pipeline(
            body, grid=(n // win,),
            in_specs=[pl.BlockSpec((win, w), lambda i: (i, 0))] + [idx_spec] * len(dests),
            out_specs=[], **split)(s_hbm, *d_hbms)

        def zero_body(z_vmem, i_vmem):
            pltpu.sync_copy(z_vmem, o_hbm.at[i_vmem.at[0]])

        pltpu.emit_pipeline(
            zero_body, grid=(pad_rows.shape[0] // win,),
            in_specs=[pl.BlockSpec((win, w), lambda i: (0, 0)), idx_spec],
            out_specs=[], **split)(z_hbm, p_hbm)

    zeros = jnp.zeros((win, w), src.dtype)
    return scatter(src, zeros, pad_rows.reshape(1, -1), *[dd.reshape(1, n) for dd in dests])


def _ffn_kernel(be_ref, bi_ref, *refs):
    xb_refs, (wg_ref, wu_ref, wd_ref) = refs[:ROW_PARTS], refs[ROW_PARTS:ROW_PARTS + 3]
    yb_refs, (wg_s, wu_s, wd_s) = refs[ROW_PARTS + 3:2 * ROW_PARTS + 3], refs[2 * ROW_PARTS + 3:]
    i = pl.program_id(0)
    changed = jnp.logical_or(i == 0, be_ref[i] != be_ref[jnp.maximum(i - 1, 0)])

    @pl.when(changed)
    def _():
        wg_s[...] = wg_ref[0, 0].astype(BF16)
        wu_s[...] = wu_ref[0, 0].astype(BF16)
        wd_s[...] = wd_ref[0, 0].astype(BF16)

    @pl.when(bi_ref[i] == i)
    def _():
        half = wg_s.shape[0] // 2
        gate = up = None
        for c in range(ROW_PARTS):
            hi, lo = _unpack_bf16_pairs(xb_refs[c][...])
            hi, lo = hi.astype(BF16), lo.astype(BF16)
            pw = hi.shape[1]
            hs, ls = slice(c * pw, (c + 1) * pw), slice(half + c * pw, half + (c + 1) * pw)
            g = _dot(hi, wg_s[hs, :]) + _dot(lo, wg_s[ls, :])
            u = _dot(hi, wu_s[hs, :]) + _dot(lo, wu_s[ls, :])
            gate, up = (g, u) if gate is None else (gate + g, up + u)
        act = (gate * _sigmoid(gate) * up).astype(BF16)
        packed = _pack_bf16_pairs(_dot(act, wd_s[...]))
        pw = packed.shape[1] // ROW_PARTS
        for c in range(ROW_PARTS):
            yb_refs[c][...] = packed[:, c * pw:(c + 1) * pw]


def _expert_ffn(block_e, block_i, xbs, wg, wu, wd, layer):
    n_rows, pw = xbs[0].shape
    d, ff = wg.shape[2], wg.shape[3]
    n_blk = n_rows // FFN_ROWS
    row_spec = pl.BlockSpec((FFN_ROWS, pw), lambda i, be, bi: (bi[i], 0))
    return pl.pallas_call(
        _ffn_kernel,
        grid_spec=pltpu.PrefetchScalarGridSpec(
            num_scalar_prefetch=2,
            grid=(n_blk,),
            in_specs=[row_spec] * ROW_PARTS + [
                pl.BlockSpec((1, 1, d, ff), lambda i, be, bi: (layer, be[i], 0, 0)),
                pl.BlockSpec((1, 1, d, ff), lambda i, be, bi: (layer, be[i], 0, 0)),
                pl.BlockSpec((1, 1, ff, d), lambda i, be, bi: (layer, be[i], 0, 0)),
            ],
            out_specs=[row_spec] * ROW_PARTS,
            scratch_shapes=[pltpu.VMEM((d, ff), BF16), pltpu.VMEM((d, ff), BF16), pltpu.VMEM((ff, d), BF16)],
        ),
        out_shape=[jax.ShapeDtypeStruct((n_rows, pw), U32)] * ROW_PARTS,
        compiler_params=_cparams(("arbitrary",)),
        name="moe_expert_ffn",
    )(block_e, block_i, *xbs, wg, wu, wd)


def _combine_kernel(x_ref, route_ref, *refs):
    y_refs, o_ref = refs[:2 * ROW_PARTS], refs[-1]
    o_ref[...] = _add_expert_rows(x_ref[...], route_ref[...],
                                  [(y_refs[2 * c][...], y_refs[2 * c + 1][...]) for c in range(ROW_PARTS)])


def _combine(x2, route, ytoks, tc, out_prev, row0, n_full):
    n, d = x2.shape
    w = ytoks[0].shape[1]
    nsteps = n // tc
    blk0 = row0 // tc
    specs = [pl.BlockSpec((tc, d), lambda i: (i, 0)), pl.BlockSpec((tc, ROUTE_LANES), lambda i: (i, 0))]
    args = [x2, route]
    for ytok in ytoks:
        specs += [pl.BlockSpec((tc, w), lambda i: (i, 0)), pl.BlockSpec((tc, w), lambda i: (i + nsteps, 0))]
        args += [ytok, ytok]
    aliases = {}
    if out_prev is not None:
        specs.append(pl.BlockSpec(memory_space=pl.ANY))
        args.append(out_prev)
        aliases = {len(args) - 1: 0}
    return pl.pallas_call(
        _combine_kernel,
        grid=(nsteps,),
        in_specs=specs,
        out_specs=pl.BlockSpec((tc, d), lambda i: (i + blk0, 0)),
        out_shape=jax.ShapeDtypeStruct((n_full, d), F32),
        input_output_aliases=aliases,
        compiler_params=_cparams(("parallel",)),
        name="moe_combine",
    )(*args)


def _moe(n, hfps, route_t, counts, wg, wu, wd, layer):
    cnt = counts[:, 0, :N_EXPERTS].astype(I32)
    total = jnp.sum(cnt, axis=0)
    padded = (total + FFN_ROWS - 1) // FFN_ROWS * FFN_ROWS
    pad_end = jnp.cumsum(padded)
    pad_start = pad_end - padded
    tile_base = pad_start[None, :] + jnp.cumsum(cnt, axis=0) - cnt
    expert_ids = jnp.arange(N_EXPERTS, dtype=I32)[:, None]
    dests = []
    for k in range(TOP_K):
        ek = route_t[:, k, :].astype(I32)
        base = jnp.sum(jnp.where(ek[:, None, :] == expert_ids, tile_base[:, :, None], 0), axis=1)
        dests.append((base + route_t[:, 4 + k, :].astype(I32)).reshape(n))
    dest_by_slot = jnp.concatenate(dests)
    n_blk = (n * TOP_K) // FFN_ROWS + N_EXPERTS
    n_rows = n_blk * FFN_ROWS
    used = pad_end[-1] // FFN_ROWS
    block_i = jnp.minimum(jnp.arange(n_blk, dtype=I32), used - 1).astype(I32)
    ended = (pad_end[None, :] <= (block_i * FFN_ROWS)[:, None]).astype(I32)
    block_e = jnp.minimum(jnp.sum(ended, axis=1), N_EXPERTS - 1).astype(I32)
    seg_len = jnp.concatenate([padded - total, (n_rows - pad_end[-1])[None]])
    seg_first = jnp.concatenate([pad_start + total, pad_end[-1:]])
    seg_end = jnp.cumsum(seg_len)
    jpad = jnp.arange(n_rows - n * TOP_K, dtype=I32)
    shift = seg_first - (seg_end - seg_len)
    step = (shift[1:] - shift[:-1])[:, None]
    pad_rows = (jpad + shift[0] + jnp.sum(jnp.where(jpad[None, :] >= seg_end[:-1, None], step, 0), axis=0)).astype(I32)

    xbs = [_sc_scatter_rows(part.reshape(n, part.shape[-1]), dests, pad_rows, n_rows) for part in hfps]
    yb = _expert_ffn(block_e, block_i, xbs, wg, wu, wd, layer)
    return [_sc_gather_rows(part, dest_by_slot) for part in yb]


def _apply_moe(x2, route, ytoks, out_prev, b0, b_full):
    b, s, d = x2.shape
    n = b * s
    prev = None if out_prev is None else out_prev.reshape(b_full * s, d)
    out = _combine(x2.reshape(n, d), route.reshape(n, ROUTE_LANES), ytoks, min(COMBINE_TOKENS, n), prev,
                   b0 * s, b_full * s)
    return out.reshape(b_full, s, d)


def _rope_lane_freq():
    inv = ROPE_THETA ** (-jnp.arange(0, ROPE_DIM // 2, dtype=F32) * 2.0 / ROPE_DIM)
    idx = np.full((HEAD_LANES,), -1, np.int64)
    for r in range(ROPE_DIM):
        idx[_head_lane(NOPE_DIM + r)] = r % ROPE_HALF
    return _gather_cols(inv[None, :], idx)


FAST_SOFTMAX_MAX_LOG2 = 60.0


def _score_bound_log2(qg, kg):
    return 1.02 * LOG2E * QK_DIM ** 0.5 * jnp.max(jnp.abs(qg)) * jnp.max(jnp.abs(kg))


def _partner_lanes(idx):
    out = np.full_like(idx, -1)
    for base in range(0, idx.shape[0], HEAD_LANES):
        for r in range(ROPE_DIM):
            lane = _head_lane(NOPE_DIM + r)
            out[base + lane] = idx[base + (lane + HALF_LANES) % HEAD_LANES]
    return out


def kernel(x, mem, positions, ln_mix, w_in, q_lat_norm, w_uq, kv_lat_norm, w_ukv, q_norm, k_norm, conv_w, conv_b,
           dt_bias, a_log, d_skip, ssd_norm, w_out, pool_w, pool_b, pool_scale, ln_xq, ln_mem, xq_w, xkv_w, xq_norm,
           xk_norm, xo_w, ln_ffn, rg_w, rg_b, re_w, re_b, exp_w_gate, exp_w_up, exp_w_down):
    b, s, d = x.shape
    depth = ln_mix.shape[0]
    tm = min(1024, s)
    tq = min(512, s)
    assert d == D_MODEL and s % tm == 0 and s % CHUNK == 0 and tm >= POOL_CARRY

    pos = positions.astype(F32)[..., None]
    invf = _rope_lane_freq()
    hsum = jnp.asarray(np.kron(np.eye(X_HEADS), np.ones((X_HEAD_DIM, X_HEAD_DIM))), BF16)
    ltri = jnp.asarray(np.tril(np.ones((tm, tm)), -1), BF16)
    row2 = lambda v: v.reshape(1, -1)
    lane_pad = lambda v: jnp.pad(v, (0, LANES - v.shape[0])).reshape(1, LANES)

    n_streams = MOE_STREAMS if b % MOE_STREAMS == 0 else 1
    nb = b // n_streams
    nt = s // tm
    streams = [dict(x=x, off=k * nb, pending=None) for k in range(n_streams)]
    for layer in range(depth):
        j = layer // 2
        kbd, vbd = _mem_kv(mem, row2(ln_mem[layer]), xkv_w[layer].astype(BF16),
                           row2(jnp.tile(xk_norm[layer], X_HEADS)), hsum)
        rw = jnp.pad(jnp.concatenate([rg_w[layer], re_w[layer]], axis=1),
                     ((0, 0), (0, ROUTE_LANES - MOE_GROUPS - N_EXPERTS)))
        rw_hi = rw.astype(BF16)
        rw_lo = (rw - rw_hi.astype(F32)).astype(BF16)
        rb = lane_pad(jnp.concatenate([rg_b[layer], re_b[layer]]))
        tail_args = [kbd, vbd, row2(ln_xq[layer]), xq_w[layer].astype(BF16), row2(jnp.tile(xq_norm[layer], X_HEADS)),
                     hsum, xo_w[layer].astype(BF16), row2(ln_ffn[layer]), rw_hi, rw_lo, rb, ltri]
        row = lambda i, jj: (i, jj, 0)
        fixed2 = lambda i, jj: (0, 0)
        posts = []
        if layer % 2 == 0:
            x = streams[0]["x"]
            assert all(st["x"] is x and st["pending"] is None for st in streams)
            win = _gather_cols(w_in[j], _win_col_index()).astype(BF16)
            q_idx = _head_col_index(QK_DIM, 0, QK_DIM)
            wuq = _gather_cols(w_uq[j], q_idx).astype(BF16)
            wuq_p = _gather_cols(w_uq[j], _partner_lanes(q_idx)).astype(BF16)
            wuk = _gather_cols(w_ukv[j], _head_col_index(NOPE_DIM + V_DIM, 0, NOPE_DIM)).astype(BF16)
            v_idx = np.full((A_HEADS * HEAD_LANES,), -1, np.int64)
            for hd in range(A_HEADS):
                v_idx[hd * HEAD_LANES:hd * HEAD_LANES + V_DIM] = hd * (NOPE_DIM + V_DIM) + NOPE_DIM + np.arange(V_DIM)
            wuv = _gather_cols(w_ukv[j], v_idx).astype(BF16)
            bound = _score_bound_log2(q_norm[j], k_norm[j])
            koff = jnp.zeros((1, HEAD_LANES), F32).at[0, SCORE_PAD_LANE].set(-bound)
            gain_idx = _head_col_index(QK_DIM, 0, QK_DIM)[:HEAD_LANES]
            lane_consts = jnp.concatenate(
                [_gather_cols(g[None, :], idx) for g in (q_norm[j], k_norm[j])
                 for idx in (gain_idx, _partner_lanes(gain_idx))]
                + [koff, invf, jnp.zeros((2, HEAD_LANES), F32)], axis=0)
            q, k, v, z, xbc, misc = _front_even(
                x, pos, lane_consts, row2(ln_mix[layer]), win, row2(q_lat_norm[j]), wuq, wuq_p,
                row2(kv_lat_norm[j]), wuk, wuv, tm)
            attn = lax.cond(bound <= FAST_SOFTMAX_MAX_LOG2,
                            functools.partial(_attention, tq=tq, online=False),
                            functools.partial(_attention, tq=tq, online=True), q, k, v)
            y = _ssd(xbc, misc, z, conv_w[j], row2(conv_b[j]), lane_pad(dt_bias[j]), lane_pad(a_log[j]),
                     row2(jnp.repeat(d_skip[j], SSD_HEAD_DIM)), row2(ssd_norm[j]))
            half = A_HEADS * V_DIM
            wout = w_out[j].astype(BF16)
            for k, st in enumerate(streams):
                row_k = lambda i, jj, off=st["off"]: (i + off, jj, 0)
                front_args = [x, attn, y, wout]
                front_specs = [pl.BlockSpec((1, tm, d), row_k), pl.BlockSpec((1, tm, half), row_k),
                               pl.BlockSpec((1, tm, D_INNER), row_k), pl.BlockSpec((half + D_INNER, d), fixed2)]
                posts.append(_post("even", front_args, front_specs, tail_args, nb, s, tm, [], b0=k * nb))
        else:
            for k, st in enumerate(streams):
                row_k = lambda i, jj, off=st["off"]: (i + off, jj, 0)
                front_args, front_specs = [st["x"]], [pl.BlockSpec((1, tm, d), row_k)]
                if st["pending"] is not None:
                    route_prev, ytoks = st["pending"]
                    pw = ytoks[0].shape[1]
                    front_args += [route_prev]
                    front_specs += [pl.BlockSpec((1, tm, ROUTE_LANES), row)]
                    for ytok in ytoks:
                        front_args += [ytok, ytok]
                        front_specs += [pl.BlockSpec((tm, pw), lambda i, jj: (i * nt + jj, 0)),
                                        pl.BlockSpec((tm, pw), lambda i, jj: (i * nt + jj + nb * nt, 0))]
                front_args += [row2(ln_mix[layer]), pool_w[j].astype(BF16), row2(pool_b[j]), row2(pool_scale[j])]
                front_specs += [pl.BlockSpec((1, d), fixed2),
                                pl.BlockSpec((len(POOL_WINDOWS), POOL_GROUP, POOL_GROUP), lambda i, jj: (0, 0, 0)),
                                pl.BlockSpec((1, d), fixed2), pl.BlockSpec((1, d), fixed2)]
                posts.append(_post("pool" if st["pending"] is None else "pool_pending", front_args, front_specs,
                                   tail_args, nb, s, tm, [pltpu.VMEM((POOL_CARRY, d), F32)], b0=k * nb))
        streams = []
        for x2, *hfps, route, route_t, counts in posts:
            ytoks = _moe(nb * s, hfps, route_t, counts, exp_w_gate, exp_w_up, exp_w_down, layer)
            streams.append(dict(x=x2, off=0, pending=(route, ytoks)))
        if layer + 1 == depth or (layer + 1) % 2 == 0:
            out = None
            for k, st in enumerate(streams):
                out = _apply_moe(st["x"], *st["pending"], out, k * nb, b)
            streams = [dict(x=out, off=k * nb, pending=None) for k in range(n_streams)]
    return streams[0]["x"]
```

```python
import functools

import numpy as np
import jax
import jax.numpy as jnp
from jax import lax
from jax.experimental import pallas as pl
from jax.experimental.pallas import tpu as pltpu
from jax.experimental.pallas import tpu_sc as plsc

F32 = jnp.float32
BF16 = jnp.bfloat16
U32 = jnp.uint32
I32 = jnp.int32

RMS_EPS = 1e-6
ROPE_THETA = 10000.0

D_MODEL = 1024
X_HEADS, X_HEAD_DIM = 4, 64
A_HEADS, NOPE_DIM, ROPE_DIM, V_DIM = 8, 64, 32, 64
QK_DIM = NOPE_DIM + ROPE_DIM
Q_LORA, KV_LORA = 256, 128
B_HEADS, SSD_HEAD_DIM, SSD_GROUPS, SSD_STATE, CONV_K, CHUNK = 8, 64, 2, 128, 4, 128
D_INNER = B_HEADS * SSD_HEAD_DIM
CONV_CH = D_INNER + 2 * SSD_GROUPS * SSD_STATE
POOL_WINDOWS = (2, 4, 8, 16)
POOL_GROUP = D_MODEL // 4
MOE_GROUPS, EXPERTS_PER_GROUP, TOP_K, EXPERT_FF = 4, 8, 2, 256
N_EXPERTS = MOE_GROUPS * EXPERTS_PER_GROUP

LANES = 128
HEAD_LANES = LANES
HALF_LANES = LANES // 2
ROPE_HALF = ROPE_DIM // 2
NOPE_HALF = NOPE_DIM // 2
POOL_CARRY = 16
CONV_CARRY = 8
VMEM_LIMIT = 56 * 1024 * 1024


def _cparams(sem):
    return pltpu.CompilerParams(dimension_semantics=sem, vmem_limit_bytes=VMEM_LIMIT)


def _rms(u, g):
    return u * lax.rsqrt(jnp.mean(u * u, axis=-1, keepdims=True) + RMS_EPS) * g


def _sigmoid(u):
    return 1.0 / (1.0 + jnp.exp(-u))


def _dot(a, b):
    return jnp.dot(a, b, preferred_element_type=F32)


def _dot_nt(a, b):
    return lax.dot_general(a, b, (((1,), (1,)), ((), ())), preferred_element_type=F32)


def _head_lane(d):
    if d < NOPE_HALF:
        return d
    if d < NOPE_DIM:
        return HALF_LANES + (d - NOPE_HALF)
    r = d - NOPE_DIM
    if r < ROPE_HALF:
        return NOPE_HALF + r
    return HALF_LANES + NOPE_HALF + (r - ROPE_HALF)


def _gather_cols(w, idx):
    w_ext = jnp.concatenate([w, jnp.zeros(w.shape[:-1] + (1,), w.dtype)], axis=-1)
    idx = np.where(idx < 0, w.shape[-1], idx)
    return jnp.take(w_ext, jnp.asarray(idx, dtype=jnp.int32), axis=-1)


_OFF_QLAT = 0
_OFF_KVLAT = _OFF_QLAT + Q_LORA
_OFF_MISC = _OFF_KVLAT + KV_LORA
_OFF_Z = _OFF_MISC + LANES
_OFF_XBC = _OFF_Z + D_INNER
IN_W = _OFF_XBC + CONV_CH


def _win_col_index():
    idx = np.full((IN_W,), -1, np.int64)
    idx[_OFF_QLAT:_OFF_QLAT + Q_LORA] = np.arange(Q_LORA)
    idx[_OFF_KVLAT:_OFF_KVLAT + KV_LORA] = Q_LORA + np.arange(KV_LORA)
    rope0 = Q_LORA + KV_LORA
    for r in range(ROPE_DIM):
        idx[_OFF_MISC + _head_lane(NOPE_DIM + r)] = rope0 + r
    z0 = rope0 + ROPE_DIM
    idx[_OFF_Z:_OFF_Z + D_INNER] = z0 + np.arange(D_INNER)
    xbc0 = z0 + D_INNER
    idx[_OFF_XBC:_OFF_XBC + CONV_CH] = xbc0 + np.arange(CONV_CH)
    dt0 = xbc0 + CONV_CH
    idx[_OFF_MISC:_OFF_MISC + B_HEADS] = dt0 + np.arange(B_HEADS)
    return idx


def _head_col_index(per_head, offset, count):
    idx = np.full((A_HEADS * HEAD_LANES,), -1, np.int64)
    for h in range(A_HEADS):
        for d in range(count):
            idx[h * HEAD_LANES + _head_lane(d)] = h * per_head + offset + d
    return idx


SCORE_PAD_LANE = NOPE_HALF + ROPE_HALF
ONES_LANE = V_DIM
LOG2E = 1.4426950408889634


def _front_even_kernel(x_ref, pos_ref, lc_ref, ln_ref, win_ref, qln_ref, wuq_ref, wuqr_ref, kvln_ref, wuk_ref, wuv_ref,
                       q_ref, k_ref, v_ref, z_ref, xbc_ref, misc_ref):
    x = x_ref[0]
    h = _rms(x, ln_ref[...]).astype(BF16)
    proj = _dot(h, win_ref[...])
    misc = proj[:, _OFF_MISC:_OFF_Z]
    z_ref[0] = proj[:, _OFF_Z:_OFF_XBC].astype(BF16)
    xbc_ref[0] = proj[:, _OFF_XBC:].astype(BF16)
    misc_ref[0] = misc
    ql = _rms(proj[:, _OFF_QLAT:_OFF_KVLAT], qln_ref[...]).astype(BF16)
    kvl = _rms(proj[:, _OFF_KVLAT:_OFF_MISC], kvln_ref[...]).astype(BF16)
    q = _dot(ql, wuq_ref[...])
    kn = _dot(kvl, wuk_ref[...])
    v = _dot(kvl, wuv_ref[...])
    lane = lax.broadcasted_iota(I32, (1, HEAD_LANES), 1)
    first_half = (lane >= NOPE_HALF) & (lane < NOPE_HALF + ROPE_HALF)
    second_half = (lane >= HALF_LANES + NOPE_HALF) & (lane < HALF_LANES + NOPE_HALF + ROPE_HALF)
    lc = lc_ref[...]
    qg, qg_p, kg, kg_p, k_off, invf = (lc[i:i + 1] for i in range(6))
    ones = jnp.ones((HEAD_LANES, HEAD_LANES), BF16)

    def lane_sumsq(u):
        return _dot((u * u).astype(BF16), ones)

    krope = jnp.where(first_half | second_half, misc, 0.0)
    kr_ss = lane_sumsq(krope)
    ang = pos_ref[0] * invf
    cos_t = jnp.cos(ang)
    sin_t = jnp.where(first_half, -jnp.sin(ang), jnp.sin(ang))
    q_one = (lane == SCORE_PAD_LANE).astype(F32)
    v_one = (lane == ONES_LANE).astype(F32)
    q_scale = QK_DIM ** -0.5 * LOG2E
    qa, qb = qg * cos_t * q_scale, qg_p * sin_t * q_scale
    ka, kc = kg * cos_t, pltpu.roll(krope, HALF_LANES, 1) * (kg_p * sin_t)
    q_p = _dot(ql, wuqr_ref[...])
    for hd in range(A_HEADS):
        sl = slice(hd * HEAD_LANES, (hd + 1) * HEAD_LANES)
        qs = q[:, sl]
        inv = lax.rsqrt(lane_sumsq(qs) * (1.0 / QK_DIM) + RMS_EPS)
        q_ref[0, hd] = ((qs * qa + q_p[:, sl] * qb) * inv + q_one).astype(BF16)
        kns = kn[:, sl]
        inv = lax.rsqrt((lane_sumsq(kns) + kr_ss) * (1.0 / QK_DIM) + RMS_EPS)
        k_ref[0, hd] = (((kns + krope) * ka + kc) * inv + k_off).astype(BF16)
        v_ref[0, hd] = (v[:, sl] + v_one).astype(BF16)


def _front_even(x, pos, lane_consts, ln, win, qln, wuq, wuq_p, kvln, wuk, wuv, tm):
    b, s, d = x.shape
    grid = (b, s // tm)
    row = lambda i, j: (i, j, 0)
    fixed2 = lambda i, j: (0, 0)
    head_row = lambda i, j: (i, 0, j, 0)
    hw = A_HEADS * HEAD_LANES
    return pl.pallas_call(
        _front_even_kernel,
        grid=grid,
        in_specs=[
            pl.BlockSpec((1, tm, d), row),
            pl.BlockSpec((1, tm, 1), row),
            pl.BlockSpec((8, HEAD_LANES), fixed2),
            pl.BlockSpec((1, d), fixed2),
            pl.BlockSpec((d, IN_W), fixed2),
            pl.BlockSpec((1, Q_LORA), fixed2),
            pl.BlockSpec((Q_LORA, hw), fixed2),
            pl.BlockSpec((Q_LORA, hw), fixed2),
            pl.BlockSpec((1, KV_LORA), fixed2),
            pl.BlockSpec((KV_LORA, hw), fixed2),
            pl.BlockSpec((KV_LORA, hw), fixed2),
        ],
        out_specs=[
            pl.BlockSpec((1, A_HEADS, tm, HEAD_LANES), head_row),
            pl.BlockSpec((1, A_HEADS, tm, HEAD_LANES), head_row),
            pl.BlockSpec((1, A_HEADS, tm, HEAD_LANES), head_row),
            pl.BlockSpec((1, tm, D_INNER), row),
            pl.BlockSpec((1, tm, CONV_CH), row),
            pl.BlockSpec((1, tm, HEAD_LANES), row),
        ],
        out_shape=[
            jax.ShapeDtypeStruct((b, A_HEADS, s, HEAD_LANES), BF16),
            jax.ShapeDtypeStruct((b, A_HEADS, s, HEAD_LANES), BF16),
            jax.ShapeDtypeStruct((b, A_HEADS, s, HEAD_LANES), BF16),
            jax.ShapeDtypeStruct((b, s, D_INNER), BF16),
            jax.ShapeDtypeStruct((b, s, CONV_CH), BF16),
            jax.ShapeDtypeStruct((b, s, HEAD_LANES), F32),
        ],
        compiler_params=_cparams(("parallel", "parallel")),
        name="front_even",
    )(x, pos, lane_consts, ln, win, qln, wuq, wuq_p, kvln, wuk, wuv)


HEADS_PER_STEP = 8


def _attn_kernel(q_ref, k_ref, v_ref, o_ref, *, tq, online):
    qi = pl.program_id(2)
    row = lax.broadcasted_iota(I32, (tq, tq), 0)
    col = lax.broadcasted_iota(I32, (tq, tq), 1)

    def head_step(hh, j, carry, masked):
        kj = k_ref[0, hh, pl.ds(j * tq, tq), :]
        vj = v_ref[0, hh, pl.ds(j * tq, tq), :]
        s = _dot_nt(q_ref[0, hh], kj)
        if masked:
            s = jnp.where(row >= col, s, -jnp.inf)
        if online:
            m, acc = carry
            m_new = jnp.maximum(m, jnp.max(s, axis=-1, keepdims=True))
            p = jnp.exp2(s - m_new).astype(BF16)
            return m_new, jnp.exp2(m - m_new) * acc + _dot(p, vj)
        return carry + _dot(jnp.exp2(s).astype(BF16), vj)

    def step(j, carries, masked):
        return tuple(head_step(hh, j, carries[hh], masked) for hh in range(HEADS_PER_STEP))

    acc0 = jnp.zeros((tq, HEAD_LANES), F32)
    init = (jnp.full((tq, 1), -jnp.inf, F32), acc0) if online else acc0
    carries = lax.fori_loop(0, qi, functools.partial(step, masked=False), (init,) * HEADS_PER_STEP)
    carries = step(qi, carries, True)
    outs = []
    for carry in carries:
        acc = carry[1] if online else carry
        outs.append(acc / acc[:, ONES_LANE:ONES_LANE + 1])
    lane = lax.broadcasted_iota(I32, (1, HEAD_LANES), 1)
    per_group = HEAD_LANES // V_DIM
    groups = []
    for g0 in range(0, HEADS_PER_STEP, per_group):
        out = outs[g0]
        for hh in range(1, per_group):
            out = jnp.where(lane >= hh * V_DIM, pltpu.roll(outs[g0 + hh], hh * V_DIM, 1), out)
        groups.append(out.astype(BF16))
    o_ref[0] = jnp.concatenate(groups, axis=1)


def _attention(q, k, v, tq, online):
    b, nh, s, _ = q.shape
    grid = (b, nh // HEADS_PER_STEP, s // tq)
    kv_spec = pl.BlockSpec((1, HEADS_PER_STEP, s, HEAD_LANES), lambda i, h, j: (i, h, 0, 0))
    return pl.pallas_call(
        functools.partial(_attn_kernel, tq=tq, online=online),
        grid=grid,
        in_specs=[
            pl.BlockSpec((1, HEADS_PER_STEP, tq, HEAD_LANES), lambda i, h, j: (i, h, j, 0)),
            kv_spec,
            kv_spec,
        ],
        out_specs=pl.BlockSpec((1, tq, HEADS_PER_STEP * V_DIM), lambda i, h, j: (i, j, h)),
        out_shape=jax.ShapeDtypeStruct((b, s, nh * V_DIM), BF16),
        compiler_params=_cparams(("parallel", "parallel", "parallel")),
        name="mla_attention_online" if online else "mla_attention",
    )(q, k, v)


def _ssd_kernel(xbc_ref, misc_ref, z_ref, cw_ref, cb_ref, dtb_ref, alog_ref, dskip_ref, gn_ref, y_ref,
                state_ref, carry_ref):
    c = pl.program_id(1)
    t = CHUNK
    rows = y_ref.shape[1]

    @pl.when(c == 0)
    def _():
        state_ref[...] = jnp.zeros_like(state_ref)
        carry_ref[...] = jnp.zeros_like(carry_ref)

    xr = xbc_ref[0].astype(F32)
    xcat = jnp.concatenate([carry_ref[...], xr], axis=0)
    carry_ref[...] = xr[rows - CONV_CARRY:, :]
    conv = jnp.zeros((rows, CONV_CH), F32) + cb_ref[...]
    for kk in range(CONV_K):
        sh = CONV_K - 1 - kk
        shifted = xcat if sh == 0 else pltpu.roll(xcat, sh, 0)
        conv = conv + cw_ref[kk:kk + 1, :] * shifted[CONV_CARRY:, :]
    xa = conv * _sigmoid(conv)
    xs = xa[:, :D_INNER]
    gw = SSD_GROUPS * SSD_STATE
    bmat = xa[:, D_INNER:D_INNER + gw]
    cmat = xa[:, D_INNER + gw:]

    u = misc_ref[0] + dtb_ref[...]
    dt = jnp.maximum(u, 0.0) + jnp.log(1.0 + jnp.exp(-jnp.abs(u)))
    a = -jnp.exp(alog_ref[...])
    lane = lax.broadcasted_iota(I32, (1, LANES), 1)
    adt_all = jnp.where(lane < B_HEADS, dt * a, 0.0)
    rowi = lax.broadcasted_iota(I32, (t, LANES), 0)
    tri = lax.broadcasted_iota(I32, (t, t), 0) >= lax.broadcasted_iota(I32, (t, t), 1)
    rep = B_HEADS // SSD_GROUPS
    y_chunks = []
    for ci in range(rows // t):
        sl = slice(ci * t, (ci + 1) * t)
        acs = adt_all[sl]
        sh = 1
        while sh < t:
            acs = acs + jnp.where(rowi >= sh, pltpu.roll(acs, sh, 0), 0.0)
            sh *= 2
        acs_t = acs.T
        ys = []
        for g in range(SSD_GROUPS):
            bg = bmat[sl, g * SSD_STATE:(g + 1) * SSD_STATE]
            cg = cmat[sl, g * SSD_STATE:(g + 1) * SSD_STATE]
            bg16, cg16 = bg.astype(BF16), cg.astype(BF16)
            cb = _dot_nt(cg16, bg16)
            bg_t = bg.T
            for r in range(rep):
                hd = g * rep + r
                col = acs[:, hd:hd + 1]
                rw = acs_t[hd:hd + 1, :]
                last = acs_t[hd:hd + 1, t - 1:t]
                decay = jnp.exp(jnp.where(tri, col - rw, -jnp.inf))
                xh = xs[sl, hd * SSD_HEAD_DIM:(hd + 1) * SSD_HEAD_DIM]
                xdt = (xh * dt[sl, hd:hd + 1]).astype(BF16)
                y_diag = _dot((cb * decay).astype(BF16), xdt)
                prev = state_ref[hd]
                y_off = _dot(cg16, prev.astype(BF16)) * jnp.exp(col)
                new_state = _dot((bg_t * jnp.exp(last - rw)).astype(BF16), xdt)
                state_ref[hd] = prev * jnp.exp(last) + new_state
                ys.append(y_diag + y_off)
        y_chunks.append(jnp.concatenate(ys, axis=1))
    y = jnp.concatenate(y_chunks, axis=0) + xs * dskip_ref[...]
    zf = z_ref[0].astype(F32)
    y = y * (zf * _sigmoid(zf))
    y_ref[0] = _rms(y, gn_ref[...]).astype(BF16)


SSD_CHUNKS_PER_STEP = 4


def _ssd(xbc, misc, z, cw, cb, dtb, alog, dskip, gn):
    b, s, _ = xbc.shape
    rows = CHUNK * SSD_CHUNKS_PER_STEP if s % (CHUNK * SSD_CHUNKS_PER_STEP) == 0 else CHUNK
    grid = (b, s // rows)
    row = lambda i, j: (i, j, 0)
    fixed2 = lambda i, j: (0, 0)
    return pl.pallas_call(
        _ssd_kernel,
        grid=grid,
        in_specs=[
            pl.BlockSpec((1, rows, CONV_CH), row),
            pl.BlockSpec((1, rows, LANES), row),
            pl.BlockSpec((1, rows, D_INNER), row),
            pl.BlockSpec((CONV_K, CONV_CH), fixed2),
            pl.BlockSpec((1, CONV_CH), fixed2),
            pl.BlockSpec((1, LANES), fixed2),
            pl.BlockSpec((1, LANES), fixed2),
            pl.BlockSpec((1, D_INNER), fixed2),
            pl.BlockSpec((1, D_INNER), fixed2),
        ],
        out_specs=pl.BlockSpec((1, rows, D_INNER), row),
        out_shape=jax.ShapeDtypeStruct((b, s, D_INNER), BF16),
        scratch_shapes=[
            pltpu.VMEM((B_HEADS, SSD_STATE, SSD_HEAD_DIM), F32),
            pltpu.VMEM((CONV_CARRY, CONV_CH), F32),
        ],
        compiler_params=_cparams(("parallel", "arbitrary")),
        name="ssd_scan",
    )(xbc, misc, z, cw, cb, dtb, alog, dskip, gn)


XW = X_HEADS * X_HEAD_DIM


def _mem_kv_kernel(mem_ref, ln_ref, wkv_ref, kg_ref, hsum_ref, kbd_ref, vbd_ref):
    m = mem_ref.shape[1]
    mn = _rms(mem_ref[0], ln_ref[...]).astype(BF16)
    kv = _dot(mn, wkv_ref[...])
    k, v = kv[:, :XW], kv[:, XW:]
    ss = _dot((k * k).astype(BF16), hsum_ref[...])
    kn = (k * lax.rsqrt(ss * (1.0 / X_HEAD_DIM) + RMS_EPS) * kg_ref[...]).astype(BF16)
    v16 = v.astype(BF16)
    head_of_lane = lax.shift_right_arithmetic(lax.broadcasted_iota(I32, (1, XW), 1), jnp.int32(_LOG2_XHD))
    for hd in range(X_HEADS):
        keep = head_of_lane == hd
        kbd_ref[0, hd * m:(hd + 1) * m, :] = jnp.where(keep, kn, jnp.zeros_like(kn))
        vbd_ref[0, hd * m:(hd + 1) * m, :] = jnp.where(keep, v16, jnp.zeros_like(v16))


def _mem_kv(mem, ln, wkv, kg, hsum):
    b, m, d = mem.shape
    fixed2 = lambda i: (0, 0)
    return pl.pallas_call(
        _mem_kv_kernel,
        grid=(b,),
        in_specs=[
            pl.BlockSpec((1, m, d), lambda i: (i, 0, 0)),
            pl.BlockSpec((1, d), fixed2),
            pl.BlockSpec((d, 2 * XW), fixed2),
            pl.BlockSpec((1, XW), fixed2),
            pl.BlockSpec((XW, XW), fixed2),
        ],
        out_specs=[
            pl.BlockSpec((1, X_HEADS * m, XW), lambda i: (i, 0, 0)),
            pl.BlockSpec((1, X_HEADS * m, XW), lambda i: (i, 0, 0)),
        ],
        out_shape=[
            jax.ShapeDtypeStruct((b, X_HEADS * m, XW), BF16),
            jax.ShapeDtypeStruct((b, X_HEADS * m, XW), BF16),
        ],
        compiler_params=_cparams(("parallel",)),
        name="mem_kv",
    )(mem, ln, wkv, kg, hsum)


ROUTE_LANES = LANES
ROW_PARTS = 2
_GROUP_LANE0 = 0
_EXPERT_LANE0 = MOE_GROUPS
_LOG2_EPG = EXPERTS_PER_GROUP.bit_length() - 1
_LOG2_XHD = X_HEAD_DIM.bit_length() - 1


def _pack_bf16_pairs(v):
    w = v.shape[1] // 2
    r = v.astype(BF16).astype(F32)
    hi = lax.bitcast_convert_type(r[:, :w], U32)
    lo = lax.bitcast_convert_type(r[:, w:], U32)
    return (hi & jnp.uint32(0xFFFF0000)) | (lo >> jnp.uint32(16))


def _unpack_bf16_pairs(u):
    hi = lax.bitcast_convert_type(u & jnp.uint32(0xFFFF0000), F32)
    lo = lax.bitcast_convert_type(u << jnp.uint32(16), F32)
    return hi, lo


def _tail_rows(x1, kbd_ref, vbd_ref, lnq_ref, wq_ref, qg_ref, hsum_ref, wo_ref, lnf_ref, rwh_ref, rwl_ref,
               rb_ref, x2_ref, hfp_refs):
    tm = x1.shape[0]
    m = kbd_ref.shape[1] // X_HEADS
    hq = _rms(x1, lnq_ref[...]).astype(BF16)
    q = _dot(hq, wq_ref[...])
    ss = _dot((q * q).astype(BF16), hsum_ref[...])
    qn = (q * lax.rsqrt(ss * (1.0 / X_HEAD_DIM) + RMS_EPS) * qg_ref[...] * (X_HEAD_DIM ** -0.5)).astype(BF16)
    s = _dot_nt(qn, kbd_ref[0])
    ps = []
    for hd in range(X_HEADS):
        sh = s[:, hd * m:(hd + 1) * m]
        e = jnp.exp(sh - jnp.max(sh, axis=-1, keepdims=True))
        ps.append((e / jnp.sum(e, axis=-1, keepdims=True)).astype(BF16))
    o = _dot(jnp.concatenate(ps, axis=1), vbd_ref[0]).astype(BF16)
    x2 = x1 + _dot(o, wo_ref[...])
    x2_ref[0] = x2

    hf = _rms(x2, lnf_ref[...])
    hf_hi = hf.astype(BF16)
    packed = _pack_bf16_pairs(hf)
    pw = packed.shape[1] // ROW_PARTS
    for c in range(ROW_PARTS):
        hfp_refs[c][0] = packed[:, c * pw:(c + 1) * pw]
    hf_lo = (hf - hf_hi.astype(F32)).astype(BF16)
    logits = _dot(hf_hi, rwh_ref[...]) + _dot(hf_hi, rwl_ref[...]) + _dot(hf_lo, rwh_ref[...]) + rb_ref[...]

    lane_i = lax.broadcasted_iota(I32, (tm, ROUTE_LANES), 1)
    lane = lane_i.astype(F32)
    big = float(ROUTE_LANES)
    neg = -jnp.inf
    gl = jnp.where(lane_i < MOE_GROUPS, logits, neg)
    gmax = jnp.max(gl, axis=-1, keepdims=True)
    gsum = jnp.sum(jnp.exp(gl - gmax), axis=-1, keepdims=True)
    g_p = 1.0 / gsum
    g_idx = jnp.min(jnp.where(gl == gmax, lane, big), axis=-1, keepdims=True)
    e_lane = lane_i - _EXPERT_LANE0
    grp_of_lane = lax.shift_right_arithmetic(e_lane, jnp.int32(_LOG2_EPG)).astype(F32)
    in_grp = (e_lane >= 0) & (e_lane < N_EXPERTS) & (grp_of_lane == g_idx)
    el = jnp.where(in_grp, logits, neg)
    emax = jnp.max(el, axis=-1, keepdims=True)
    idx1 = jnp.min(jnp.where(el == emax, lane, big), axis=-1, keepdims=True)
    el2 = jnp.where(lane == idx1, neg, el)
    emax2 = jnp.max(el2, axis=-1, keepdims=True)
    idx2 = jnp.min(jnp.where(el2 == emax2, lane, big), axis=-1, keepdims=True)
    r2 = jnp.exp(emax2 - emax)
    gate1 = g_p / (1.0 + r2)
    gate2 = g_p * r2 / (1.0 + r2)
    e1 = idx1 - float(_EXPERT_LANE0)
    e2 = idx2 - float(_EXPERT_LANE0)

    route = jnp.where(lane == 0, e1, 0.0)
    route = jnp.where(lane == 1, e2, route)
    route = jnp.where(lane == 2, gate1, route)
    route = jnp.where(lane == 3, gate2, route)
    return route, (lane == e1).astype(F32), (lane == e2).astype(F32)


def _tail(x1, kbd_ref, vbd_ref, lnq_ref, wq_ref, qg_ref, hsum_ref, wo_ref, lnf_ref, rwh_ref, rwl_ref, rb_ref,
          ltri_ref, x2_ref, *out_refs):
    hfp_refs, (route_ref, route_t_ref, cnt_ref) = out_refs[:ROW_PARTS], out_refs[ROW_PARTS:]
    tm = x1.shape[0]
    route, oh1, oh2 = _tail_rows(x1, kbd_ref, vbd_ref, lnq_ref, wq_ref, qg_ref, hsum_ref, wo_ref, lnf_ref, rwh_ref,
                                 rwl_ref, rb_ref, x2_ref, hfp_refs)
    both = oh1 + oh2
    before = _dot(ltri_ref[...], both.astype(BF16))
    rank1 = jnp.sum(before * oh1, axis=-1, keepdims=True)
    rank2 = jnp.sum(before * oh2, axis=-1, keepdims=True)
    cnt_ref[0] = jnp.broadcast_to(jnp.sum(both, axis=0, keepdims=True), cnt_ref.shape[1:])
    lane = lax.broadcasted_iota(I32, (tm, ROUTE_LANES), 1)
    route = jnp.where(lane == 4, rank1, route)
    route = jnp.where(lane == 5, rank2, route)
    route_ref[0] = route
    route_t_ref[0] = route.T[:route_t_ref.shape[1], :]


_TAIL_IN = 12


def _post_even_kernel(x_ref, a_ref, y_ref, wout_ref, *rest):
    tail_in, outs = rest[:_TAIL_IN], rest[_TAIL_IN:]
    half = wout_ref.shape[0] // 2
    x1 = x_ref[0] + _dot(a_ref[0], wout_ref[:half, :]) + _dot(y_ref[0], wout_ref[half:, :])
    _tail(x1, *tail_in, *outs)


def _add_expert_rows(x, route, ys):
    g1, g2 = route[:, 2:3], route[:, 3:4]
    his, los = [], []
    for y1, y2 in ys:
        h1, l1 = _unpack_bf16_pairs(y1)
        h2, l2 = _unpack_bf16_pairs(y2)
        his.append(h1 * g1 + h2 * g2)
        los.append(l1 * g1 + l2 * g2)
    return x + jnp.concatenate(his + los, axis=1)


N_PENDING = 1 + 2 * ROW_PARTS


def _post_pool_kernel(x_ref, *rest, pending):
    if pending:
        route_prev_ref, y_refs, rest = rest[0], rest[1:N_PENDING], rest[N_PENDING:]
    (ln_ref, pw_ref, pb_ref, ps_ref), rest = rest[:4], rest[4:]
    tail_in, outs, carry_ref = rest[:_TAIL_IN], rest[_TAIL_IN:-1], rest[-1]
    j = pl.program_id(1)
    tm = x_ref.shape[1]

    @pl.when(j == 0)
    def _():
        carry_ref[...] = jnp.zeros_like(carry_ref)

    x = x_ref[0]
    if pending:
        x = _add_expert_rows(x, route_prev_ref[0], [(y_refs[2 * c][...], y_refs[2 * c + 1][...])
                                                    for c in range(ROW_PARTS)])
    h = _rms(x, ln_ref[...])
    pos = (j * tm + 1 + lax.broadcasted_iota(I32, (tm, 1), 0)).astype(F32)
    mixed = []
    for g, w in enumerate(POOL_WINDOWS):
        sl = slice(g * POOL_GROUP, (g + 1) * POOL_GROUP)
        hg = h[:, sl]
        acc = jnp.concatenate([carry_ref[:, sl], hg], axis=0)
        sh = 1
        while sh < w:
            acc = acc + pltpu.roll(acc, sh, 0)
            sh *= 2
        win = acc[POOL_CARRY:, :]
        dlt = win / jnp.minimum(pos, float(w)) - hg
        mixed.append(_dot(dlt.astype(BF16), pw_ref[g]))
    carry_ref[...] = h[tm - POOL_CARRY:, :]
    y = (jnp.concatenate(mixed, axis=1) + pb_ref[...]) * ps_ref[...]
    _tail(x + y, *tail_in, *outs)


def _post(kind, front_args, front_specs, tail_args, b, s, tm, scratch, b0=0):
    d = D_MODEL
    m4 = tail_args[0].shape[1]
    row = lambda i, j: (i, j, 0)
    fixed2 = lambda i, j: (0, 0)
    per_b = lambda i, j: (i + b0, 0, 0)
    tail_specs = [
        pl.BlockSpec((1, m4, XW), per_b),
        pl.BlockSpec((1, m4, XW), per_b),
        pl.BlockSpec((1, d), fixed2),
        pl.BlockSpec((d, XW), fixed2),
        pl.BlockSpec((1, XW), fixed2),
        pl.BlockSpec((XW, XW), fixed2),
        pl.BlockSpec((XW, d), fixed2),
        pl.BlockSpec((1, d), fixed2),
        pl.BlockSpec((d, ROUTE_LANES), fixed2),
        pl.BlockSpec((d, ROUTE_LANES), fixed2),
        pl.BlockSpec((1, ROUTE_LANES), fixed2),
        pl.BlockSpec((tm, tm), fixed2),
    ]
    nt = s // tm
    pw = d // 2 // ROW_PARTS
    kernel = {"even": _post_even_kernel,
              "pool": functools.partial(_post_pool_kernel, pending=False),
              "pool_pending": functools.partial(_post_pool_kernel, pending=True)}[kind]
    return pl.pallas_call(
        kernel,
        grid=(b, nt),
        in_specs=front_specs + tail_specs,
        out_specs=[pl.BlockSpec((1, tm, d), row)]
        + [pl.BlockSpec((1, tm, pw), row)] * ROW_PARTS
        + [pl.BlockSpec((1, tm, ROUTE_LANES), row),
           pl.BlockSpec((1, 8, tm), lambda i, j: (i * nt + j, 0, 0)),
           pl.BlockSpec((1, 8, ROUTE_LANES), lambda i, j: (i * nt + j, 0, 0))],
        out_shape=[jax.ShapeDtypeStruct((b, s, d), F32)]
        + [jax.ShapeDtypeStruct((b, s, pw), U32)] * ROW_PARTS
        + [jax.ShapeDtypeStruct((b, s, ROUTE_LANES), F32),
           jax.ShapeDtypeStruct((b * nt, 8, tm), F32),
           jax.ShapeDtypeStruct((b * nt, 8, ROUTE_LANES), F32)],
        scratch_shapes=scratch,
        compiler_params=_cparams(("parallel", "arbitrary")),
        name="post_" + kind,
    )(*front_args, *tail_args)


FFN_ROWS = 1024
COMBINE_TOKENS = 512
MOE_STREAMS = 2
SC_GATHER_WINDOW = 128


def _sc_gather_rows(table, idx):
    m, w = idx.shape[0], table.shape[1]
    mesh = plsc.VectorSubcoreMesh(core_axis_name="core", subcore_axis_name="subcore")

    @pl.kernel(out_type=jax.ShapeDtypeStruct((m, w), table.dtype), mesh=mesh, name="moe_row_gather")
    def gather(t_hbm, i_hbm, o_hbm):
        def body(i_vmem, o_vmem):
            pltpu.sync_copy(t_hbm.at[i_vmem.at[0]], o_vmem)

        pltpu.emit_pipeline(
            body,
            grid=(m // SC_GATHER_WINDOW,),
            in_specs=[pl.BlockSpec((1, SC_GATHER_WINDOW), lambda i: (0, i))],
            out_specs=[pl.BlockSpec((SC_GATHER_WINDOW, w), lambda i: (i, 0))],
            core_axis_name=("core", "subcore"),
            dimension_semantics=(pltpu.PARALLEL,),
        )(i_hbm, o_hbm)

    return gather(table, idx.reshape(1, m))


def _sc_scatter_rows(src, dests, pad_rows, n_rows):
    n, w = src.shape
    win = SC_GATHER_WINDOW
    mesh = plsc.VectorSubcoreMesh(core_axis_name="core", subcore_axis_name="subcore")
    idx_spec = pl.BlockSpec((1, win), lambda i: (0, i))
    split = dict(core_axis_name=("core", "subcore"), dimension_semantics=(pltpu.PARALLEL,))

    @pl.kernel(out_type=jax.ShapeDtypeStruct((n_rows, w), src.dtype), mesh=mesh, name="moe_row_scatter")
    def scatter(s_hbm, z_hbm, p_hbm, *rest):
        d_hbms, o_hbm = rest[:-1], rest[-1]

        def body(s_vmem, *i_vmems):
            for i_vmem in i_vmems:
                pltpu.sync_copy(s_vmem, o_hbm.at[i_vmem.at[0]])

        pltpu.emit_pipeline(
            body, grid=(n // win,),
            in_specs=[pl.BlockSpec((win, w), lambda i: (i, 0))] + [idx_spec] * len(dests),
            out_specs=[], **split)(s_hbm, *d_hbms)

        def zero_body(z_vmem, i_vmem):
            pltpu.sync_copy(z_vmem, o_hbm.at[i_vmem.at[0]])

        pltpu.emit_pipeline(
            zero_body, grid=(pad_rows.shape[0] // win,),
            in_specs=[pl.BlockSpec((win, w), lambda i: (0, 0)), idx_spec],
            out_specs=[], **split)(z_hbm, p_hbm)

    zeros = jnp.zeros((win, w), src.dtype)
    return scatter(src, zeros, pad_rows.reshape(1, -1), *[dd.reshape(1, n) for dd in dests])


def _ffn_kernel(be_ref, bi_ref, *refs):
    xb_refs, (wg_ref, wu_ref, wd_ref) = refs[:ROW_PARTS], refs[ROW_PARTS:ROW_PARTS + 3]
    yb_refs, (wg_s, wu_s, wd_s) = refs[ROW_PARTS + 3:2 * ROW_PARTS + 3], refs[2 * ROW_PARTS + 3:]
    i = pl.program_id(0)
    changed = jnp.logical_or(i == 0, be_ref[i] != be_ref[jnp.maximum(i - 1, 0)])

    @pl.when(changed)
    def _():
        wg_s[...] = wg_ref[0, 0].astype(BF16)
        wu_s[...] = wu_ref[0, 0].astype(BF16)
        wd_s[...] = wd_ref[0, 0].astype(BF16)

    @pl.when(bi_ref[i] == i)
    def _():
        half = wg_s.shape[0] // 2
        gate = up = None
        for c in range(ROW_PARTS):
            hi, lo = _unpack_bf16_pairs(xb_refs[c][...])
            hi, lo = hi.astype(BF16), lo.astype(BF16)
            pw = hi.shape[1]
            hs, ls = slice(c * pw, (c + 1) * pw), slice(half + c * pw, half + (c + 1) * pw)
            g = _dot(hi, wg_s[hs, :]) + _dot(lo, wg_s[ls, :])
            u = _dot(hi, wu_s[hs, :]) + _dot(lo, wu_s[ls, :])
            gate, up = (g, u) if gate is None else (gate + g, up + u)
        act = (gate * _sigmoid(gate) * up).astype(BF16)
        packed = _pack_bf16_pairs(_dot(act, wd_s[...]))
        pw = packed.shape[1] // ROW_PARTS
        for c in range(ROW_PARTS):
            yb_refs[c][...] = packed[:, c * pw:(c + 1) * pw]


def _expert_ffn(block_e, block_i, xbs, wg, wu, wd, layer):
    n_rows, pw = xbs[0].shape
    d, ff = wg.shape[2], wg.shape[3]
    n_blk = n_rows // FFN_ROWS
    row_spec = pl.BlockSpec((FFN_ROWS, pw), lambda i, be, bi: (bi[i], 0))
    return pl.pallas_call(
        _ffn_kernel,
        grid_spec=pltpu.PrefetchScalarGridSpec(
            num_scalar_prefetch=2,
            grid=(n_blk,),
            in_specs=[row_spec] * ROW_PARTS + [
                pl.BlockSpec((1, 1, d, ff), lambda i, be, bi: (layer, be[i], 0, 0)),
                pl.BlockSpec((1, 1, d, ff), lambda i, be, bi: (layer, be[i], 0, 0)),
                pl.BlockSpec((1, 1, ff, d), lambda i, be, bi: (layer, be[i], 0, 0)),
            ],
            out_specs=[row_spec] * ROW_PARTS,
            scratch_shapes=[pltpu.VMEM((d, ff), BF16), pltpu.VMEM((d, ff), BF16), pltpu.VMEM((ff, d), BF16)],
        ),
        out_shape=[jax.ShapeDtypeStruct((n_rows, pw), U32)] * ROW_PARTS,
        compiler_params=_cparams(("arbitrary",)),
        name="moe_expert_ffn",
    )(block_e, block_i, *xbs, wg, wu, wd)


def _combine_kernel(x_ref, route_ref, *refs):
    y_refs, o_ref = refs[:2 * ROW_PARTS], refs[-1]
    o_ref[...] = _add_expert_rows(x_ref[...], route_ref[...],
                                  [(y_refs[2 * c][...], y_refs[2 * c + 1][...]) for c in range(ROW_PARTS)])


def _combine(x2, route, ytoks, tc, out_prev, row0, n_full):
    n, d = x2.shape
    w = ytoks[0].shape[1]
    nsteps = n // tc
    blk0 = row0 // tc
    specs = [pl.BlockSpec((tc, d), lambda i: (i, 0)), pl.BlockSpec((tc, ROUTE_LANES), lambda i: (i, 0))]
    args = [x2, route]
    for ytok in ytoks:
        specs += [pl.BlockSpec((tc, w), lambda i: (i, 0)), pl.BlockSpec((tc, w), lambda i: (i + nsteps, 0))]
        args += [ytok, ytok]
    aliases = {}
    if out_prev is not None:
        specs.append(pl.BlockSpec(memory_space=pl.ANY))
        args.append(out_prev)
        aliases = {len(args) - 1: 0}
    return pl.pallas_call(
        _combine_kernel,
        grid=(nsteps,),
        in_specs=specs,
        out_specs=pl.BlockSpec((tc, d), lambda i: (i + blk0, 0)),
        out_shape=jax.ShapeDtypeStruct((n_full, d), F32),
        input_output_aliases=aliases,
        compiler_params=_cparams(("parallel",)),
        name="moe_combine",
    )(*args)


def _moe(n, hfps, route_t, counts, wg, wu, wd, layer):
    cnt = counts[:, 0, :N_EXPERTS].astype(I32)
    total = jnp.sum(cnt, axis=0)
    padded = (total + FFN_ROWS - 1) // FFN_ROWS * FFN_ROWS
    pad_end = jnp.cumsum(padded)
    pad_start = pad_end - padded
    tile_base = pad_start[None, :] + jnp.cumsum(cnt, axis=0) - cnt
    expert_ids = jnp.arange(N_EXPERTS, dtype=I32)[:, None]
    dests = []
    for k in range(TOP_K):
        ek = route_t[:, k, :].astype(I32)
        base = jnp.sum(jnp.where(ek[:, None, :] == expert_ids, tile_base[:, :, None], 0), axis=1)
        dests.append((base + route_t[:, 4 + k, :].astype(I32)).reshape(n))
    dest_by_slot = jnp.concatenate(dests)
    n_blk = (n * TOP_K) // FFN_ROWS + N_EXPERTS
    n_rows = n_blk * FFN_ROWS
    used = pad_end[-1] // FFN_ROWS
    block_i = jnp.minimum(jnp.arange(n_blk, dtype=I32), used - 1).astype(I32)
    ended = (pad_end[None, :] <= (block_i * FFN_ROWS)[:, None]).astype(I32)
    block_e = jnp.minimum(jnp.sum(ended, axis=1), N_EXPERTS - 1).astype(I32)
    seg_len = jnp.concatenate([padded - total, (n_rows - pad_end[-1])[None]])
    seg_first = jnp.concatenate([pad_start + total, pad_end[-1:]])
    seg_end = jnp.cumsum(seg_len)
    jpad = jnp.arange(n_rows - n * TOP_K, dtype=I32)
    shift = seg_first - (seg_end - seg_len)
    step = (shift[1:] - shift[:-1])[:, None]
    pad_rows = (jpad + shift[0] + jnp.sum(jnp.where(jpad[None, :] >= seg_end[:-1, None], step, 0), axis=0)).astype(I32)

    xbs = [_sc_scatter_rows(part.reshape(n, part.shape[-1]), dests, pad_rows, n_rows) for part in hfps]
    yb = _expert_ffn(block_e, block_i, xbs, wg, wu, wd, layer)
    return [_sc_gather_rows(part, dest_by_slot) for part in yb]


def _apply_moe(x2, route, ytoks, out_prev, b0, b_full):
    b, s, d = x2.shape
    n = b * s
    prev = None if out_prev is None else out_prev.reshape(b_full * s, d)
    out = _combine(x2.reshape(n, d), route.reshape(n, ROUTE_LANES), ytoks, min(COMBINE_TOKENS, n), prev,
                   b0 * s, b_full * s)
    return out.reshape(b_full, s, d)


def _rope_lane_freq():
    inv = ROPE_THETA ** (-jnp.arange(0, ROPE_DIM // 2, dtype=F32) * 2.0 / ROPE_DIM)
    idx = np.full((HEAD_LANES,), -1, np.int64)
    for r in range(ROPE_DIM):
        idx[_head_lane(NOPE_DIM + r)] = r % ROPE_HALF
    return _gather_cols(inv[None, :], idx)


FAST_SOFTMAX_MAX_LOG2 = 60.0


def _score_bound_log2(qg, kg):
    return 1.02 * LOG2E * QK_DIM ** 0.5 * jnp.max(jnp.abs(qg)) * jnp.max(jnp.abs(kg))


def _partner_lanes(idx):
    out = np.full_like(idx, -1)
    for base in range(0, idx.shape[0], HEAD_LANES):
        for r in range(ROPE_DIM):
            lane = _head_lane(NOPE_DIM + r)
            out[base + lane] = idx[base + (lane + HALF_LANES) % HEAD_LANES]
    return out


def kernel(x, mem, positions, ln_mix, w_in, q_lat_norm, w_uq, kv_lat_norm, w_ukv, q_norm, k_norm, conv_w, conv_b,
           dt_bias, a_log, d_skip, ssd_norm, w_out, pool_w, pool_b, pool_scale, ln_xq, ln_mem, xq_w, xkv_w, xq_norm,
           xk_norm, xo_w, ln_ffn, rg_w, rg_b, re_w, re_b, exp_w_gate, exp_w_up, exp_w_down):
    b, s, d = x.shape
    depth = ln_mix.shape[0]
    tm = min(1024, s)
    tq = min(512, s)
    assert d == D_MODEL and s % tm == 0 and s % CHUNK == 0 and tm >= POOL_CARRY

    pos = positions.astype(F32)[..., None]
    invf = _rope_lane_freq()
    hsum = jnp.asarray(np.kron(np.eye(X_HEADS), np.ones((X_HEAD_DIM, X_HEAD_DIM))), BF16)
    ltri = jnp.asarray(np.tril(np.ones((tm, tm)), -1), BF16)
    row2 = lambda v: v.reshape(1, -1)
    lane_pad = lambda v: jnp.pad(v, (0, LANES - v.shape[0])).reshape(1, LANES)

    n_streams = MOE_STREAMS if b % MOE_STREAMS == 0 else 1
    nb = b // n_streams
    nt = s // tm
    streams = [dict(x=x, off=k * nb, pending=None) for k in range(n_streams)]
    for layer in range(depth):
        j = layer // 2
        kbd, vbd = _mem_kv(mem, row2(ln_mem[layer]), xkv_w[layer].astype(BF16),
                           row2(jnp.tile(xk_norm[layer], X_HEADS)), hsum)
        rw = jnp.pad(jnp.concatenate([rg_w[layer], re_w[layer]], axis=1),
                     ((0, 0), (0, ROUTE_LANES - MOE_GROUPS - N_EXPERTS)))
        rw_hi = rw.astype(BF16)
        rw_lo = (rw - rw_hi.astype(F32)).astype(BF16)
        rb = lane_pad(jnp.concatenate([rg_b[layer], re_b[layer]]))
        tail_args = [kbd, vbd, row2(ln_xq[layer]), xq_w[layer].astype(BF16), row2(jnp.tile(xq_norm[layer], X_HEADS)),
                     hsum, xo_w[layer].astype(BF16), row2(ln_ffn[layer]), rw_hi, rw_lo, rb, ltri]
        row = lambda i, jj: (i, jj, 0)
        fixed2 = lambda i, jj: (0, 0)
        posts = []
        if layer % 2 == 0:
            x = streams[0]["x"]
            assert all(st["x"] is x and st["pending"] is None for st in streams)
            win = _gather_cols(w_in[j], _win_col_index()).astype(BF16)
            q_idx = _head_col_index(QK_DIM, 0, QK_DIM)
            wuq = _gather_cols(w_uq[j], q_idx).astype(BF16)
            wuq_p = _gather_cols(w_uq[j], _partner_lanes(q_idx)).astype(BF16)
            wuk = _gather_cols(w_ukv[j], _head_col_index(NOPE_DIM + V_DIM, 0, NOPE_DIM)).astype(BF16)
            v_idx = np.full((A_HEADS * HEAD_LANES,), -1, np.int64)
            for hd in range(A_HEADS):
                v_idx[hd * HEAD_LANES:hd * HEAD_LANES + V_DIM] = hd * (NOPE_DIM + V_DIM) + NOPE_DIM + np.arange(V_DIM)
            wuv = _gather_cols(w_ukv[j], v_idx).astype(BF16)
            bound = _score_bound_log2(q_norm[j], k_norm[j])
            koff = jnp.zeros((1, HEAD_LANES), F32).at[0, SCORE_PAD_LANE].set(-bound)
            gain_idx = _head_col_index(QK_DIM, 0, QK_DIM)[:HEAD_LANES]
            lane_consts = jnp.concatenate(
                [_gather_cols(g[None, :], idx) for g in (q_norm[j], k_norm[j])
                 for idx in (gain_idx, _partner_lanes(gain_idx))]
                + [koff, invf, jnp.zeros((2, HEAD_LANES), F32)], axis=0)
            q, k, v, z, xbc, misc = _front_even(
                x, pos, lane_consts, row2(ln_mix[layer]), win, row2(q_lat_norm[j]), wuq, wuq_p,
                row2(kv_lat_norm[j]), wuk, wuv, tm)
            attn = lax.cond(bound <= FAST_SOFTMAX_MAX_LOG2,
                            functools.partial(_attention, tq=tq, online=False),
                            functools.partial(_attention, tq=tq, online=True), q, k, v)
            y = _ssd(xbc, misc, z, conv_w[j], row2(conv_b[j]), lane_pad(dt_bias[j]), lane_pad(a_log[j]),
                     row2(jnp.repeat(d_skip[j], SSD_HEAD_DIM)), row2(ssd_norm[j]))
            half = A_HEADS * V_DIM
            wout = w_out[j].astype(BF16)
            for k, st in enumerate(streams):
                row_k = lambda i, jj, off=st["off"]: (i + off, jj, 0)
                front_args = [x, attn, y, wout]
                front_specs = [pl.BlockSpec((1, tm, d), row_k), pl.BlockSpec((1, tm, half), row_k),
                               pl.BlockSpec((1, tm, D_INNER), row_k), pl.BlockSpec((half + D_INNER, d), fixed2)]
                posts.append(_post("even", front_args, front_specs, tail_args, nb, s, tm, [], b0=k * nb))
        else:
            for k, st in enumerate(streams):
                row_k = lambda i, jj, off=st["off"]: (i + off, jj, 0)
                front_args, front_specs = [st["x"]], [pl.BlockSpec((1, tm, d), row_k)]
                if st["pending"] is not None:
                    route_prev, ytoks = st["pending"]
                    pw = ytoks[0].shape[1]
                    front_args += [route_prev]
                    front_specs += [pl.BlockSpec((1, tm, ROUTE_LANES), row)]
                    for ytok in ytoks:
                        front_args += [ytok, ytok]
                        front_specs += [pl.BlockSpec((tm, pw), lambda i, jj: (i * nt + jj, 0)),
                                        pl.BlockSpec((tm, pw), lambda i, jj: (i * nt + jj + nb * nt, 0))]
                front_args += [row2(ln_mix[layer]), pool_w[j].astype(BF16), row2(pool_b[j]), row2(pool_scale[j])]
                front_specs += [pl.BlockSpec((1, d), fixed2),
                                pl.BlockSpec((len(POOL_WINDOWS), POOL_GROUP, POOL_GROUP), lambda i, jj: (0, 0, 0)),
                                pl.BlockSpec((1, d), fixed2), pl.BlockSpec((1, d), fixed2)]
                posts.append(_post("pool" if st["pending"] is None else "pool_pending", front_args, front_specs,
                                   tail_args, nb, s, tm, [pltpu.VMEM((POOL_CARRY, d), F32)], b0=k * nb))
        streams = []
        for x2, *hfps, route, route_t, counts in posts:
            ytoks = _moe(nb * s, hfps, route_t, counts, exp_w_gate, exp_w_up, exp_w_down, layer)
            streams.append(dict(x=x2, off=0, pending=(route, ytoks)))
        if layer + 1 == depth or (layer + 1) % 2 == 0:
            out = None
            for k, st in enumerate(streams):
                out = _apply_moe(st["x"], *st["pending"], out, k * nb, b)
            streams = [dict(x=out, off=k * nb, pending=None) for k in range(n_streams)]
    return streams[0]["x"]
```

```python
import functools

import numpy as np
import jax
import jax.numpy as jnp
from jax import lax
from jax.experimental import pallas as pl
from jax.experimental.pallas import tpu as pltpu
from jax.experimental.pallas import tpu_sc as plsc

F32 = jnp.float32
BF16 = jnp.bfloat16
U32 = jnp.uint32
I32 = jnp.int32

RMS_EPS = 1e-6
ROPE_THETA = 10000.0

D_MODEL = 1024
X_HEADS, X_HEAD_DIM = 4, 64
A_HEADS, NOPE_DIM, ROPE_DIM, V_DIM = 8, 64, 32, 64
QK_DIM = NOPE_DIM + ROPE_DIM
Q_LORA, KV_LORA = 256, 128
B_HEADS, SSD_HEAD_DIM, SSD_GROUPS, SSD_STATE, CONV_K, CHUNK = 8, 64, 2, 128, 4, 128
D_INNER = B_HEADS * SSD_HEAD_DIM
CONV_CH = D_INNER + 2 * SSD_GROUPS * SSD_STATE
POOL_WINDOWS = (2, 4, 8, 16)
POOL_GROUP = D_MODEL // 4
MOE_GROUPS, EXPERTS_PER_GROUP, TOP_K, EXPERT_FF = 4, 8, 2, 256
N_EXPERTS = MOE_GROUPS * EXPERTS_PER_GROUP

LANES = 128
HEAD_LANES = LANES
HALF_LANES = LANES // 2
ROPE_HALF = ROPE_DIM // 2
NOPE_HALF = NOPE_DIM // 2
POOL_CARRY = 16
CONV_CARRY = 8
VMEM_LIMIT = 56 * 1024 * 1024


def _cparams(sem):
    return pltpu.CompilerParams(dimension_semantics=sem, vmem_limit_bytes=VMEM_LIMIT)


def _rms(u, g):
    return u * lax.rsqrt(jnp.mean(u * u, axis=-1, keepdims=True) + RMS_EPS) * g


def _sigmoid(u):
    return 1.0 / (1.0 + jnp.exp(-u))


def _dot(a, b):
    return jnp.dot(a, b, preferred_element_type=F32)


def _dot_nt(a, b):
    return lax.dot_general(a, b, (((1,), (1,)), ((), ())), preferred_element_type=F32)


def _head_lane(d):
    if d < NOPE_HALF:
        return d
    if d < NOPE_DIM:
        return HALF_LANES + (d - NOPE_HALF)
    r = d - NOPE_DIM
    if r < ROPE_HALF:
        return NOPE_HALF + r
    return HALF_LANES + NOPE_HALF + (r - ROPE_HALF)


def _gather_cols(w, idx):
    w_ext = jnp.concatenate([w, jnp.zeros(w.shape[:-1] + (1,), w.dtype)], axis=-1)
    idx = np.where(idx < 0, w.shape[-1], idx)
    return jnp.take(w_ext, jnp.asarray(idx, dtype=jnp.int32), axis=-1)


_OFF_QLAT = 0
_OFF_KVLAT = _OFF_QLAT + Q_LORA
_OFF_MISC = _OFF_KVLAT + KV_LORA
_OFF_Z = _OFF_MISC + LANES
_OFF_XBC = _OFF_Z + D_INNER
IN_W = _OFF_XBC + CONV_CH


def _win_col_index():
    idx = np.full((IN_W,), -1, np.int64)
    idx[_OFF_QLAT:_OFF_QLAT + Q_LORA] = np.arange(Q_LORA)
    idx[_OFF_KVLAT:_OFF_KVLAT + KV_LORA] = Q_LORA + np.arange(KV_LORA)
    rope0 = Q_LORA + KV_LORA
    for r in range(ROPE_DIM):
        idx[_OFF_MISC + _head_lane(NOPE_DIM + r)] = rope0 + r
    z0 = rope0 + ROPE_DIM
    idx[_OFF_Z:_OFF_Z + D_INNER] = z0 + np.arange(D_INNER)
    xbc0 = z0 + D_INNER
    idx[_OFF_XBC:_OFF_XBC + CONV_CH] = xbc0 + np.arange(CONV_CH)
    dt0 = xbc0 + CONV_CH
    idx[_OFF_MISC:_OFF_MISC + B_HEADS] = dt0 + np.arange(B_HEADS)
    return idx


def _head_col_index(per_head, offset, count):
    idx = np.full((A_HEADS * HEAD_LANES,), -1, np.int64)
    for h in range(A_HEADS):
        for d in range(count):
            idx[h * HEAD_LANES + _head_lane(d)] = h * per_head + offset + d
    return idx


SCORE_PAD_LANE = NOPE_HALF + ROPE_HALF
ONES_LANE = V_DIM
LOG2E = 1.4426950408889634


def _front_even_kernel(x_ref, pos_ref, lc_ref, ln_ref, win_ref, qln_ref, wuq_ref, wuqr_ref, kvln_ref, wuk_ref, wuv_ref,
                       q_ref, k_ref, v_ref, z_ref, xbc_ref, misc_ref):
    x = x_ref[0]
    h = _rms(x, ln_ref[...]).astype(BF16)
    proj = _dot(h, win_ref[...])
    misc = proj[:, _OFF_MISC:_OFF_Z]
    z_ref[0] = proj[:, _OFF_Z:_OFF_XBC].astype(BF16)
    xbc_ref[0] = proj[:, _OFF_XBC:].astype(BF16)
    misc_ref[0] = misc
    ql = _rms(proj[:, _OFF_QLAT:_OFF_KVLAT], qln_ref[...]).astype(BF16)
    kvl = _rms(proj[:, _OFF_KVLAT:_OFF_MISC], kvln_ref[...]).astype(BF16)
    q = _dot(ql, wuq_ref[...])
    kn = _dot(kvl, wuk_ref[...])
    v = _dot(kvl, wuv_ref[...])
    lane = lax.broadcasted_iota(I32, (1, HEAD_LANES), 1)
    first_half = (lane >= NOPE_HALF) & (lane < NOPE_HALF + ROPE_HALF)
    second_half = (lane >= HALF_LANES + NOPE_HALF) & (lane < HALF_LANES + NOPE_HALF + ROPE_HALF)
    lc = lc_ref[...]
    qg, qg_p, kg, kg_p, k_off, invf = (lc[i:i + 1] for i in range(6))
    ones = jnp.ones((HEAD_LANES, HEAD_LANES), BF16)

    def lane_sumsq(u):
        return _dot((u * u).astype(BF16), ones)

    krope = jnp.where(first_half | second_half, misc, 0.0)
    kr_ss = lane_sumsq(krope)
    ang = pos_ref[0] * invf
    cos_t = jnp.cos(ang)
    sin_t = jnp.where(first_half, -jnp.sin(ang), jnp.sin(ang))
    q_one = (lane == SCORE_PAD_LANE).astype(F32)
    v_one = (lane == ONES_LANE).astype(F32)
    q_scale = QK_DIM ** -0.5 * LOG2E
    qa, qb = qg * cos_t * q_scale, qg_p * sin_t * q_scale
    ka, kc = kg * cos_t, pltpu.roll(krope, HALF_LANES, 1) * (kg_p * sin_t)
    q_p = _dot(ql, wuqr_ref[...])
    for hd in range(A_HEADS):
        sl = slice(hd * HEAD_LANES, (hd + 1) * HEAD_LANES)
        qs = q[:, sl]
        inv = lax.rsqrt(lane_sumsq(qs) * (1.0 / QK_DIM) + RMS_EPS)
        q_ref[0, hd] = ((qs * qa + q_p[:, sl] * qb) * inv + q_one).astype(BF16)
        kns = kn[:, sl]
        inv = lax.rsqrt((lane_sumsq(kns) + kr_ss) * (1.0 / QK_DIM) + RMS_EPS)
        k_ref[0, hd] = (((kns + krope) * ka + kc) * inv + k_off).astype(BF16)
        v_ref[0, hd] = (v[:, sl] + v_one).astype(BF16)


def _front_even(x, pos, lane_consts, ln, win, qln, wuq, wuq_p, kvln, wuk, wuv, tm):
    b, s, d = x.shape
    grid = (b, s // tm)
    row = lambda i, j: (i, j, 0)
    fixed2 = lambda i, j: (0, 0)
    head_row = lambda i, j: (i, 0, j, 0)
    hw = A_HEADS * HEAD_LANES
    return pl.pallas_call(
        _front_even_kernel,
        grid=grid,
        in_specs=[
            pl.BlockSpec((1, tm, d), row),
            pl.BlockSpec((1, tm, 1), row),
            pl.BlockSpec((8, HEAD_LANES), fixed2),
            pl.BlockSpec((1, d), fixed2),
            pl.BlockSpec((d, IN_W), fixed2),
            pl.BlockSpec((1, Q_LORA), fixed2),
            pl.BlockSpec((Q_LORA, hw), fixed2),
            pl.BlockSpec((Q_LORA, hw), fixed2),
            pl.BlockSpec((1, KV_LORA), fixed2),
            pl.BlockSpec((KV_LORA, hw), fixed2),
            pl.BlockSpec((KV_LORA, hw), fixed2),
        ],
        out_specs=[
            pl.BlockSpec((1, A_HEADS, tm, HEAD_LANES), head_row),
            pl.BlockSpec((1, A_HEADS, tm, HEAD_LANES), head_row),
            pl.BlockSpec((1, A_HEADS, tm, HEAD_LANES), head_row),
            pl.BlockSpec((1, tm, D_INNER), row),
            pl.BlockSpec((1, tm, CONV_CH), row),
            pl.BlockSpec((1, tm, HEAD_LANES), row),
        ],
        out_shape=[
            jax.ShapeDtypeStruct((b, A_HEADS, s, HEAD_LANES), BF16),
            jax.ShapeDtypeStruct((b, A_HEADS, s, HEAD_LANES), BF16),
            jax.ShapeDtypeStruct((b, A_HEADS, s, HEAD_LANES), BF16),
            jax.ShapeDtypeStruct((b, s, D_INNER), BF16),
            jax.ShapeDtypeStruct((b, s, CONV_CH), BF16),
            jax.ShapeDtypeStruct((b, s, HEAD_LANES), F32),
        ],
        compiler_params=_cparams(("parallel", "parallel")),
        name="front_even",
    )(x, pos, lane_consts, ln, win, qln, wuq, wuq_p, kvln, wuk, wuv)


HEADS_PER_STEP = 8


def _attn_kernel(q_ref, k_ref, v_ref, o_ref, *, tq, online):
    qi = pl.program_id(2)
    row = lax.broadcasted_iota(I32, (tq, tq), 0)
    col = lax.broadcasted_iota(I32, (tq, tq), 1)

    def head_step(hh, j, carry, masked):
        kj = k_ref[0, hh, pl.ds(j * tq, tq), :]
        vj = v_ref[0, hh, pl.ds(j * tq, tq), :]
        s = _dot_nt(q_ref[0, hh], kj)
        if masked:
            s = jnp.where(row >= col, s, -jnp.inf)
        if online:
            m, acc = carry
            m_new = jnp.maximum(m, jnp.max(s, axis=-1, keepdims=True))
            p = jnp.exp2(s - m_new).astype(BF16)
            return m_new, jnp.exp2(m - m_new) * acc + _dot(p, vj)
        return carry + _dot(jnp.exp2(s).astype(BF16), vj)

    def step(j, carries, masked):
        return tuple(head_step(hh, j, carries[hh], masked) for hh in range(HEADS_PER_STEP))

    acc0 = jnp.zeros((tq, HEAD_LANES), F32)
    init = (jnp.full((tq, 1), -jnp.inf, F32), acc0) if online else acc0
    carries = lax.fori_loop(0, qi, functools.partial(step, masked=False), (init,) * HEADS_PER_STEP)
    carries = step(qi, carries, True)
    outs = []
    for carry in carries:
        acc = carry[1] if online else carry
        outs.append(acc / acc[:, ONES_LANE:ONES_LANE + 1])
    lane = lax.broadcasted_iota(I32, (1, HEAD_LANES), 1)
    per_group = HEAD_LANES // V_DIM
    groups = []
    for g0 in range(0, HEADS_PER_STEP, per_group):
        out = outs[g0]
        for hh in range(1, per_group):
            out = jnp.where(lane >= hh * V_DIM, pltpu.roll(outs[g0 + hh], hh * V_DIM, 1), out)
        groups.append(out.astype(BF16))
    o_ref[0] = jnp.concatenate(groups, axis=1)


def _attention(q, k, v, tq, online):
    b, nh, s, _ = q.shape
    grid = (b, nh // HEADS_PER_STEP, s // tq)
    kv_spec = pl.BlockSpec((1, HEADS_PER_STEP, s, HEAD_LANES), lambda i, h, j: (i, h, 0, 0))
    return pl.pallas_call(
        functools.partial(_attn_kernel, tq=tq, online=online),
        grid=grid,
        in_specs=[
            pl.BlockSpec((1, HEADS_PER_STEP, tq, HEAD_LANES), lambda i, h, j: (i, h, j, 0)),
            kv_spec,
            kv_spec,
        ],
        out_specs=pl.BlockSpec((1, tq, HEADS_PER_STEP * V_DIM), lambda i, h, j: (i, j, h)),
        out_shape=jax.ShapeDtypeStruct((b, s, nh * V_DIM), BF16),
        compiler_params=_cparams(("parallel", "parallel", "parallel")),
        name="mla_attention_online" if online else "mla_attention",
    )(q, k, v)


def _ssd_kernel(xbc_ref, misc_ref, z_ref, cw_ref, cb_ref, dtb_ref, alog_ref, dskip_ref, gn_ref, y_ref,
                state_ref, carry_ref):
    c = pl.program_id(1)
    t = CHUNK
    rows = y_ref.shape[1]

    @pl.when(c == 0)
    def _():
        state_ref[...] = jnp.zeros_like(state_ref)
        carry_ref[...] = jnp.zeros_like(carry_ref)

    xr = xbc_ref[0].astype(F32)
    xcat = jnp.concatenate([carry_ref[...], xr], axis=0)
    carry_ref[...] = xr[rows - CONV_CARRY:, :]
    conv = jnp.zeros((rows, CONV_CH), F32) + cb_ref[...]
    for kk in range(CONV_K):
        sh = CONV_K - 1 - kk
        shifted = xcat if sh == 0 else pltpu.roll(xcat, sh, 0)
        conv = conv + cw_ref[kk:kk + 1, :] * shifted[CONV_CARRY:, :]
    xa = conv * _sigmoid(conv)
    xs = xa[:, :D_INNER]
    gw = SSD_GROUPS * SSD_STATE
    bmat = xa[:, D_INNER:D_INNER + gw]
    cmat = xa[:, D_INNER + gw:]

    u = misc_ref[0] + dtb_ref[...]
    dt = jnp.maximum(u, 0.0) + jnp.log(1.0 + jnp.exp(-jnp.abs(u)))
    a = -jnp.exp(alog_ref[...])
    lane = lax.broadcasted_iota(I32, (1, LANES), 1)
    adt_all = jnp.where(lane < B_HEADS, dt * a, 0.0)
    rowi = lax.broadcasted_iota(I32, (t, LANES), 0)
    tri = lax.broadcasted_iota(I32, (t, t), 0) >= lax.broadcasted_iota(I32, (t, t), 1)
    rep = B_HEADS // SSD_GROUPS
    y_chunks = []
    for ci in range(rows // t):
        sl = slice(ci * t, (ci + 1) * t)
        acs = adt_all[sl]
        sh = 1
        while sh < t:
            acs = acs + jnp.where(rowi >= sh, pltpu.roll(acs, sh, 0), 0.0)
            sh *= 2
        acs_t = acs.T
        ys = []
        for g in range(SSD_GROUPS):
            bg = bmat[sl, g * SSD_STATE:(g + 1) * SSD_STATE]
            cg = cmat[sl, g * SSD_STATE:(g + 1) * SSD_STATE]
            bg16, cg16 = bg.astype(BF16), cg.astype(BF16)
            cb = _dot_nt(cg16, bg16)
            bg_t = bg.T
            for r in range(rep):
                hd = g * rep + r
                col = acs[:, hd:hd + 1]
                rw = acs_t[hd:hd + 1, :]
                last = acs_t[hd:hd + 1, t - 1:t]
                decay = jnp.exp(jnp.where(tri, col - rw, -jnp.inf))
                xh = xs[sl, hd * SSD_HEAD_DIM:(hd + 1) * SSD_HEAD_DIM]
                xdt = (xh * dt[sl, hd:hd + 1]).astype(BF16)
                y_diag = _dot((cb * decay).astype(BF16), xdt)
                prev = state_ref[hd]
                y_off = _dot(cg16, prev.astype(BF16)) * jnp.exp(col)
                new_state = _dot((bg_t * jnp.exp(last - rw)).astype(BF16), xdt)
                state_ref[hd] = prev * jnp.exp(last) + new_state
                ys.append(y_diag + y_off)
        y_chunks.append(jnp.concatenate(ys, axis=1))
    y = jnp.concatenate(y_chunks, axis=0) + xs * dskip_ref[...]
    zf = z_ref[0].astype(F32)
    y = y * (zf * _sigmoid(zf))
    y_ref[0] = _rms(y, gn_ref[...]).astype(BF16)


SSD_CHUNKS_PER_STEP = 4


def _ssd(xbc, misc, z, cw, cb, dtb, alog, dskip, gn):
    b, s, _ = xbc.shape
    rows = CHUNK * SSD_CHUNKS_PER_STEP if s % (CHUNK * SSD_CHUNKS_PER_STEP) == 0 else CHUNK
    grid = (b, s // rows)
    row = lambda i, j: (i, j, 0)
    fixed2 = lambda i, j: (0, 0)
    return pl.pallas_call(
        _ssd_kernel,
        grid=grid,
        in_specs=[
            pl.BlockSpec((1, rows, CONV_CH), row),
            pl.BlockSpec((1, rows, LANES), row),
            pl.BlockSpec((1, rows, D_INNER), row),
            pl.BlockSpec((CONV_K, CONV_CH), fixed2),
            pl.BlockSpec((1, CONV_CH), fixed2),
            pl.BlockSpec((1, LANES), fixed2),
            pl.BlockSpec((1, LANES), fixed2),
            pl.BlockSpec((1, D_INNER), fixed2),
            pl.BlockSpec((1, D_INNER), fixed2),
        ],
        out_specs=pl.BlockSpec((1, rows, D_INNER), row),
        out_shape=jax.ShapeDtypeStruct((b, s, D_INNER), BF16),
        scratch_shapes=[
            pltpu.VMEM((B_HEADS, SSD_STATE, SSD_HEAD_DIM), F32),
            pltpu.VMEM((CONV_CARRY, CONV_CH), F32),
        ],
        compiler_params=_cparams(("parallel", "arbitrary")),
        name="ssd_scan",
    )(xbc, misc, z, cw, cb, dtb, alog, dskip, gn)


XW = X_HEADS * X_HEAD_DIM


def _mem_kv_kernel(mem_ref, ln_ref, wkv_ref, kg_ref, hsum_ref, kbd_ref, vbd_ref):
    m = mem_ref.shape[1]
    mn = _rms(mem_ref[0], ln_ref[...]).astype(BF16)
    kv = _dot(mn, wkv_ref[...])
    k, v = kv[:, :XW], kv[:, XW:]
    ss = _dot((k * k).astype(BF16), hsum_ref[...])
    kn_t = (k * lax.rsqrt(ss * (1.0 / X_HEAD_DIM) + RMS_EPS) * kg_ref[...]).T.astype(BF16)
    v16 = v.astype(BF16)
    head_of_lane = lax.shift_right_arithmetic(lax.broadcasted_iota(I32, (1, XW), 1), jnp.int32(_LOG2_XHD))
    head_of_row = lax.shift_right_arithmetic(lax.broadcasted_iota(I32, (XW, 1), 0), jnp.int32(_LOG2_XHD))
    for hd in range(X_HEADS):
        kbd_ref[0, :, hd * m:(hd + 1) * m] = jnp.where(head_of_row == hd, kn_t, jnp.zeros_like(kn_t))
        vbd_ref[0, hd * m:(hd + 1) * m, :] = jnp.where(head_of_lane == hd, v16, jnp.zeros_like(v16))


def _mem_kv(mem, ln, wkv, kg, hsum):
    b, m, d = mem.shape
    fixed2 = lambda i: (0, 0)
    return pl.pallas_call(
        _mem_kv_kernel,
        grid=(b,),
        in_specs=[
            pl.BlockSpec((1, m, d), lambda i: (i, 0, 0)),
            pl.BlockSpec((1, d), fixed2),
            pl.BlockSpec((d, 2 * XW), fixed2),
            pl.BlockSpec((1, XW), fixed2),
            pl.BlockSpec((XW, XW), fixed2),
        ],
        out_specs=[
            pl.BlockSpec((1, XW, X_HEADS * m), lambda i: (i, 0, 0)),
            pl.BlockSpec((1, X_HEADS * m, XW), lambda i: (i, 0, 0)),
        ],
        out_shape=[
            jax.ShapeDtypeStruct((b, XW, X_HEADS * m), BF16),
            jax.ShapeDtypeStruct((b, X_HEADS * m, XW), BF16),
        ],
        compiler_params=_cparams(("parallel",)),
        name="mem_kv",
    )(mem, ln, wkv, kg, hsum)


ROUTE_LANES = LANES
ROW_PARTS = 2
_GROUP_LANE0 = 0
_EXPERT_LANE0 = MOE_GROUPS
_LOG2_EPG = EXPERTS_PER_GROUP.bit_length() - 1
_LOG2_XHD = X_HEAD_DIM.bit_length() - 1


def _pack_bf16_pairs(v):
    w = v.shape[1] // 2
    r = v.astype(BF16).astype(F32)
    hi = lax.bitcast_convert_type(r[:, :w], U32)
    lo = lax.bitcast_convert_type(r[:, w:], U32)
    return (hi & jnp.uint32(0xFFFF0000)) | (lo >> jnp.uint32(16))


def _unpack_bf16_pairs(u):
    hi = lax.bitcast_convert_type(u & jnp.uint32(0xFFFF0000), F32)
    lo = lax.bitcast_convert_type(u << jnp.uint32(16), F32)
    return hi, lo


def _tail_rows(x1, kbd_ref, vbd_ref, lnq_ref, wq_ref, qg_ref, hsum_ref, wo_ref, lnf_ref, rwh_ref, rwl_ref,
               rb_ref, x2_ref, hfp_refs):
    tm = x1.shape[0]
    m = vbd_ref.shape[1] // X_HEADS
    hq = _rms(x1, lnq_ref[...]).astype(BF16)
    q = _dot(hq, wq_ref[...])
    ss = _dot((q * q).astype(BF16), hsum_ref[...])
    qn = (q * lax.rsqrt(ss * (1.0 / X_HEAD_DIM) + RMS_EPS) * qg_ref[...] * (X_HEAD_DIM ** -0.5)).astype(BF16)
    s = _dot(qn, kbd_ref[0])
    ps = []
    for hd in range(X_HEADS):
        sh = s[:, hd * m:(hd + 1) * m]
        e = jnp.exp(sh - jnp.max(sh, axis=-1, keepdims=True))
        ps.append((e / jnp.sum(e, axis=-1, keepdims=True)).astype(BF16))
    o = _dot(jnp.concatenate(ps, axis=1), vbd_ref[0]).astype(BF16)
    x2 = x1 + _dot(o, wo_ref[...])
    x2_ref[0] = x2

    hf = _rms(x2, lnf_ref[...])
    hf_hi = hf.astype(BF16)
    packed = _pack_bf16_pairs(hf)
    pw = packed.shape[1] // ROW_PARTS
    for c in range(ROW_PARTS):
        hfp_refs[c][0] = packed[:, c * pw:(c + 1) * pw]
    hf_lo = (hf - hf_hi.astype(F32)).astype(BF16)
    hi_terms = _dot(hf_hi, rwl_ref[...])
    logits = (hi_terms[:, :ROUTE_LANES] + hi_terms[:, ROUTE_LANES:] + _dot(hf_lo, rwh_ref[...]) + rb_ref[...])

    lane_i = lax.broadcasted_iota(I32, (tm, ROUTE_LANES), 1)
    lane = lane_i.astype(F32)
    big = float(ROUTE_LANES)
    neg = -jnp.inf
    gl = jnp.where(lane_i < MOE_GROUPS, logits, neg)
    gmax = jnp.max(gl, axis=-1, keepdims=True)
    gsum = jnp.sum(jnp.exp(gl - gmax), axis=-1, keepdims=True)
    g_p = 1.0 / gsum
    g_idx = jnp.min(jnp.where(gl == gmax, lane, big), axis=-1, keepdims=True)
    e_lane = lane_i - _EXPERT_LANE0
    grp_of_lane = lax.shift_right_arithmetic(e_lane, jnp.int32(_LOG2_EPG)).astype(F32)
    in_grp = (e_lane >= 0) & (e_lane < N_EXPERTS) & (grp_of_lane == g_idx)
    el = jnp.where(in_grp, logits, neg)
    emax = jnp.max(el, axis=-1, keepdims=True)
    idx1 = jnp.min(jnp.where(el == emax, lane, big), axis=-1, keepdims=True)
    el2 = jnp.where(lane == idx1, neg, el)
    emax2 = jnp.max(el2, axis=-1, keepdims=True)
    idx2 = jnp.min(jnp.where(el2 == emax2, lane, big), axis=-1, keepdims=True)
    r2 = jnp.exp(emax2 - emax)
    gate1 = g_p / (1.0 + r2)
    gate2 = g_p * r2 / (1.0 + r2)
    e1 = idx1 - float(_EXPERT_LANE0)
    e2 = idx2 - float(_EXPERT_LANE0)

    route = jnp.where(lane == 0, e1, 0.0)
    route = jnp.where(lane == 1, e2, route)
    route = jnp.where(lane == 2, gate1, route)
    route = jnp.where(lane == 3, gate2, route)
    return route, (lane == e1).astype(F32), (lane == e2).astype(F32)


def _tail(x1, kbd_ref, vbd_ref, lnq_ref, wq_ref, qg_ref, hsum_ref, wo_ref, lnf_ref, rwh_ref, rwl_ref, rb_ref,
          ltri_ref, x2_ref, *out_refs):
    hfp_refs, (route_ref, route_t_ref, cnt_ref) = out_refs[:ROW_PARTS], out_refs[ROW_PARTS:]
    tm = x1.shape[0]
    route, oh1, oh2 = _tail_rows(x1, kbd_ref, vbd_ref, lnq_ref, wq_ref, qg_ref, hsum_ref, wo_ref, lnf_ref, rwh_ref,
                                 rwl_ref, rb_ref, x2_ref, hfp_refs)
    both = oh1 + oh2
    before = _dot(ltri_ref[...], both.astype(BF16))
    rank1 = jnp.sum(before * oh1, axis=-1, keepdims=True)
    rank2 = jnp.sum(before * oh2, axis=-1, keepdims=True)
    cnt_ref[0] = jnp.broadcast_to(jnp.sum(both, axis=0, keepdims=True), cnt_ref.shape[1:])
    lane = lax.broadcasted_iota(I32, (tm, ROUTE_LANES), 1)
    route = jnp.where(lane == 4, rank1, route)
    route = jnp.where(lane == 5, rank2, route)
    route_ref[0] = route
    route_t_ref[0] = route.T[:route_t_ref.shape[1], :]


_TAIL_IN = 12


def _post_even_kernel(x_ref, a_ref, y_ref, wout_ref, *rest):
    tail_in, outs = rest[:_TAIL_IN], rest[_TAIL_IN:]
    half = wout_ref.shape[0] // 2
    x1 = x_ref[0] + _dot(a_ref[0], wout_ref[:half, :]) + _dot(y_ref[0], wout_ref[half:, :])
    _tail(x1, *tail_in, *outs)


def _add_expert_rows(x, route, ys):
    g1, g2 = route[:, 2:3], route[:, 3:4]
    his, los = [], []
    for y1, y2 in ys:
        h1, l1 = _unpack_bf16_pairs(y1)
        h2, l2 = _unpack_bf16_pairs(y2)
        his.append(h1 * g1 + h2 * g2)
        los.append(l1 * g1 + l2 * g2)
    return x + jnp.concatenate(his + los, axis=1)


N_PENDING = 1 + 2 * ROW_PARTS


def _post_pool_kernel(x_ref, *rest, pending):
    if pending:
        route_prev_ref, y_refs, rest = rest[0], rest[1:N_PENDING], rest[N_PENDING:]
    (ln_ref, pw_ref, pb_ref, ps_ref), rest = rest[:4], rest[4:]
    tail_in, outs, carry_ref = rest[:_TAIL_IN], rest[_TAIL_IN:-1], rest[-1]
    j = pl.program_id(1)
    tm = x_ref.shape[1]

    @pl.when(j == 0)
    def _():
        carry_ref[...] = jnp.zeros_like(carry_ref)

    x = x_ref[0]
    if pending:
        x = _add_expert_rows(x, route_prev_ref[0], [(y_refs[2 * c][...], y_refs[2 * c + 1][...])
                                                    for c in range(ROW_PARTS)])
    h = _rms(x, ln_ref[...])
    pos = (j * tm + 1 + lax.broadcasted_iota(I32, (tm, 1), 0)).astype(F32)
    mixed = []
    for g, w in enumerate(POOL_WINDOWS):
        sl = slice(g * POOL_GROUP, (g + 1) * POOL_GROUP)
        hg = h[:, sl]
        acc = jnp.concatenate([carry_ref[:, sl], hg], axis=0)
        sh = 1
        while sh < w:
            acc = acc + pltpu.roll(acc, sh, 0)
            sh *= 2
        win = acc[POOL_CARRY:, :]
        dlt = win / jnp.minimum(pos, float(w)) - hg
        mixed.append(_dot(dlt.astype(BF16), pw_ref[g]))
    carry_ref[...] = h[tm - POOL_CARRY:, :]
    y = (jnp.concatenate(mixed, axis=1) + pb_ref[...]) * ps_ref[...]
    _tail(x + y, *tail_in, *outs)


def _post(kind, front_args, front_specs, tail_args, b, s, tm, scratch, b0=0):
    d = D_MODEL
    m4 = tail_args[1].shape[1]
    row = lambda i, j: (i, j, 0)
    fixed2 = lambda i, j: (0, 0)
    per_b = lambda i, j: (i + b0, 0, 0)
    tail_specs = [
        pl.BlockSpec((1, XW, m4), per_b),
        pl.BlockSpec((1, m4, XW), per_b),
        pl.BlockSpec((1, d), fixed2),
        pl.BlockSpec((d, XW), fixed2),
        pl.BlockSpec((1, XW), fixed2),
        pl.BlockSpec((XW, XW), fixed2),
        pl.BlockSpec((XW, d), fixed2),
        pl.BlockSpec((1, d), fixed2),
        pl.BlockSpec((d, ROUTE_LANES), fixed2),
        pl.BlockSpec((d, 2 * ROUTE_LANES), fixed2),
        pl.BlockSpec((1, ROUTE_LANES), fixed2),
        pl.BlockSpec((tm, tm), fixed2),
    ]
    nt = s // tm
    pw = d // 2 // ROW_PARTS
    kernel = {"even": _post_even_kernel,
              "pool": functools.partial(_post_pool_kernel, pending=False),
              "pool_pending": functools.partial(_post_pool_kernel, pending=True)}[kind]
    return pl.pallas_call(
        kernel,
        grid=(b, nt),
        in_specs=front_specs + tail_specs,
        out_specs=[pl.BlockSpec((1, tm, d), row)]
        + [pl.BlockSpec((1, tm, pw), row)] * ROW_PARTS
        + [pl.BlockSpec((1, tm, ROUTE_LANES), row),
           pl.BlockSpec((1, 8, tm), lambda i, j: (i * nt + j, 0, 0)),
           pl.BlockSpec((1, 8, ROUTE_LANES), lambda i, j: (i * nt + j, 0, 0))],
        out_shape=[jax.ShapeDtypeStruct((b, s, d), F32)]
        + [jax.ShapeDtypeStruct((b, s, pw), U32)] * ROW_PARTS
        + [jax.ShapeDtypeStruct((b, s, ROUTE_LANES), F32),
           jax.ShapeDtypeStruct((b * nt, 8, tm), F32),
           jax.ShapeDtypeStruct((b * nt, 8, ROUTE_LANES), F32)],
        scratch_shapes=scratch,
        compiler_params=_cparams(("parallel", "arbitrary")),
        name="post_" + kind,
    )(*front_args, *tail_args)


FFN_ROWS = 1024
COMBINE_TOKENS = 512
MOE_STREAMS = 2
SC_GATHER_WINDOW = 128


def _sc_gather_rows(table, idx):
    m, w = idx.shape[0], table.shape[1]
    mesh = plsc.VectorSubcoreMesh(core_axis_name="core", subcore_axis_name="subcore")

    @pl.kernel(out_type=jax.ShapeDtypeStruct((m, w), table.dtype), mesh=mesh, name="moe_row_gather")
    def gather(t_hbm, i_hbm, o_hbm):
        def body(i_vmem, o_vmem):
            pltpu.sync_copy(t_hbm.at[i_vmem.at[0]], o_vmem)

        pltpu.emit_pipeline(
            body,
            grid=(m // SC_GATHER_WINDOW,),
            in_specs=[pl.BlockSpec((1, SC_GATHER_WINDOW), lambda i: (0, i))],
            out_specs=[pl.BlockSpec((SC_GATHER_WINDOW, w), lambda i: (i, 0))],
            core_axis_name=("core", "subcore"),
            dimension_semantics=(pltpu.PARALLEL,),
        )(i_hbm, o_hbm)

    return gather(table, idx.reshape(1, m))


def _sc_scatter_rows(src, dests, pad_rows, n_rows):
    n, w = src.shape
    win = SC_GATHER_WINDOW
    mesh = plsc.VectorSubcoreMesh(core_axis_name="core", subcore_axis_name="subcore")
    idx_spec = pl.BlockSpec((1, win), lambda i: (0, i))
    split = dict(core_axis_name=("core", "subcore"), dimension_semantics=(pltpu.PARALLEL,))

    @pl.kernel(out_type=jax.ShapeDtypeStruct((n_rows, w), src.dtype), mesh=mesh, name="moe_row_scatter")
    def scatter(s_hbm, z_hbm, p_hbm, *rest):
        d_hbms, o_hbm = rest[:-1], rest[-1]

        def body(s_vmem, *i_vmems):
            for i_vmem in i_vmems:
                pltpu.sync_copy(s_vmem, o_hbm.at[i_vmem.at[0]])

        pltpu.emit_pipeline(
            body, grid=(n // win,),
            in_specs=[pl.BlockSpec((win, w), lambda i: (i, 0))] + [idx_spec] * len(dests),
            out_specs=[], **split)(s_hbm, *d_hbms)

        def zero_body(z_vmem, i_vmem):
            pltpu.sync_copy(z_vmem, o_hbm.at[i_vmem.at[0]])

        pltpu.emit_pipeline(
            zero_body, grid=(pad_rows.shape[0] // win,),
            in_specs=[pl.BlockSpec((win, w), lambda i: (0, 0)), idx_spec],
            out_specs=[], **split)(z_hbm, p_hbm)

    zeros = jnp.zeros((win, w), src.dtype)
    return scatter(src, zeros, pad_rows.reshape(1, -1), *[dd.reshape(1, n) for dd in dests])


def _ffn_kernel(be_ref, bi_ref, *refs):
    xb_refs, (wg_ref, wu_ref, wd_ref) = refs[:ROW_PARTS], refs[ROW_PARTS:ROW_PARTS + 3]
    yb_refs, (wg_s, wu_s, wd_s) = refs[ROW_PARTS + 3:2 * ROW_PARTS + 3], refs[2 * ROW_PARTS + 3:]
    i = pl.program_id(0)
    changed = jnp.logical_or(i == 0, be_ref[i] != be_ref[jnp.maximum(i - 1, 0)])

    @pl.when(changed)
    def _():
        wg_s[...] = wg_ref[0, 0].astype(BF16)
        wu_s[...] = wu_ref[0, 0].astype(BF16)
        wd_s[...] = wd_ref[0, 0].astype(BF16)

    @pl.when(bi_ref[i] == i)
    def _():
        half = wg_s.shape[0] // 2
        gate = up = None
        for c in range(ROW_PARTS):
            hi, lo = _unpack_bf16_pairs(xb_refs[c][...])
            hi, lo = hi.astype(BF16), lo.astype(BF16)
            pw = hi.shape[1]
            hs, ls = slice(c * pw, (c + 1) * pw), slice(half + c * pw, half + (c + 1) * pw)
            g = _dot(hi, wg_s[hs, :]) + _dot(lo, wg_s[ls, :])
            u = _dot(hi, wu_s[hs, :]) + _dot(lo, wu_s[ls, :])
            gate, up = (g, u) if gate is None else (gate + g, up + u)
        act = (gate * _sigmoid(gate) * up).astype(BF16)
        packed = _pack_bf16_pairs(_dot(act, wd_s[...]))
        pw = packed.shape[1] // ROW_PARTS
        for c in range(ROW_PARTS):
            yb_refs[c][...] = packed[:, c * pw:(c + 1) * pw]


def _expert_ffn(block_e, block_i, xbs, wg, wu, wd, layer):
    n_rows, pw = xbs[0].shape
    d, ff = wg.shape[2], wg.shape[3]
    n_blk = n_rows // FFN_ROWS
    row_spec = pl.BlockSpec((FFN_ROWS, pw), lambda i, be, bi: (bi[i], 0))
    return pl.pallas_call(
        _ffn_kernel,
        grid_spec=pltpu.PrefetchScalarGridSpec(
            num_scalar_prefetch=2,
            grid=(n_blk,),
            in_specs=[row_spec] * ROW_PARTS + [
                pl.BlockSpec((1, 1, d, ff), lambda i, be, bi: (layer, be[i], 0, 0)),
                pl.BlockSpec((1, 1, d, ff), lambda i, be, bi: (layer, be[i], 0, 0)),
                pl.BlockSpec((1, 1, ff, d), lambda i, be, bi: (layer, be[i], 0, 0)),
            ],
            out_specs=[row_spec] * ROW_PARTS,
            scratch_shapes=[pltpu.VMEM((d, ff), BF16), pltpu.VMEM((d, ff), BF16), pltpu.VMEM((ff, d), BF16)],
        ),
        out_shape=[jax.ShapeDtypeStruct((n_rows, pw), U32)] * ROW_PARTS,
        compiler_params=_cparams(("arbitrary",)),
        name="moe_expert_ffn",
    )(block_e, block_i, *xbs, wg, wu, wd)


def _combine_kernel(x_ref, route_ref, *refs):
    y_refs, o_ref = refs[:2 * ROW_PARTS], refs[-1]
    o_ref[...] = _add_expert_rows(x_ref[...], route_ref[...],
                                  [(y_refs[2 * c][...], y_refs[2 * c + 1][...]) for c in range(ROW_PARTS)])


def _combine(x2, route, ytoks, tc, out_prev, row0, n_full):
    n, d = x2.shape
    w = ytoks[0].shape[1]
    nsteps = n // tc
    blk0 = row0 // tc
    specs = [pl.BlockSpec((tc, d), lambda i: (i, 0)), pl.BlockSpec((tc, ROUTE_LANES), lambda i: (i, 0))]
    args = [x2, route]
    for ytok in ytoks:
        specs += [pl.BlockSpec((tc, w), lambda i: (i, 0)), pl.BlockSpec((tc, w), lambda i: (i + nsteps, 0))]
        args += [ytok, ytok]
    aliases = {}
    if out_prev is not None:
        specs.append(pl.BlockSpec(memory_space=pl.ANY))
        args.append(out_prev)
        aliases = {len(args) - 1: 0}
    return pl.pallas_call(
        _combine_kernel,
        grid=(nsteps,),
        in_specs=specs,
        out_specs=pl.BlockSpec((tc, d), lambda i: (i + blk0, 0)),
        out_shape=jax.ShapeDtypeStruct((n_full, d), F32),
        input_output_aliases=aliases,
        compiler_params=_cparams(("parallel",)),
        name="moe_combine",
    )(*args)


def _moe(n, hfps, route_t, counts, wg, wu, wd, layer):
    cnt = counts[:, 0, :N_EXPERTS].astype(I32)
    total = jnp.sum(cnt, axis=0)
    padded = (total + FFN_ROWS - 1) // FFN_ROWS * FFN_ROWS
    pad_end = jnp.cumsum(padded)
    pad_start = pad_end - padded
    tile_base = pad_start[None, :] + jnp.cumsum(cnt, axis=0) - cnt
    expert_ids = jnp.arange(N_EXPERTS, dtype=I32)[:, None]
    dests = []
    for k in range(TOP_K):
        ek = route_t[:, k, :].astype(I32)
        base = jnp.sum(jnp.where(ek[:, None, :] == expert_ids, tile_base[:, :, None], 0), axis=1)
        dests.append((base + route_t[:, 4 + k, :].astype(I32)).reshape(n))
    dest_by_slot = jnp.concatenate(dests)
    n_blk = (n * TOP_K) // FFN_ROWS + N_EXPERTS
    n_rows = n_blk * FFN_ROWS
    used = pad_end[-1] // FFN_ROWS
    block_i = jnp.minimum(jnp.arange(n_blk, dtype=I32), used - 1).astype(I32)
    ended = (pad_end[None, :] <= (block_i * FFN_ROWS)[:, None]).astype(I32)
    block_e = jnp.minimum(jnp.sum(ended, axis=1), N_EXPERTS - 1).astype(I32)
    seg_len = jnp.concatenate([padded - total, (n_rows - pad_end[-1])[None]])
    seg_first = jnp.concatenate([pad_start + total, pad_end[-1:]])
    seg_end = jnp.cumsum(seg_len)
    jpad = jnp.arange(n_rows - n * TOP_K, dtype=I32)
    shift = seg_first - (seg_end - seg_len)
    step = (shift[1:] - shift[:-1])[:, None]
    pad_rows = (jpad + shift[0] + jnp.sum(jnp.where(jpad[None, :] >= seg_end[:-1, None], step, 0), axis=0)).astype(I32)

    xbs = [_sc_scatter_rows(part.reshape(n, part.shape[-1]), dests, pad_rows, n_rows) for part in hfps]
    yb = _expert_ffn(block_e, block_i, xbs, wg, wu, wd, layer)
    return [_sc_gather_rows(part, dest_by_slot) for part in yb]


def _apply_moe(x2, route, ytoks, out_prev, b0, b_full):
    b, s, d = x2.shape
    n = b * s
    prev = None if out_prev is None else out_prev.reshape(b_full * s, d)
    out = _combine(x2.reshape(n, d), route.reshape(n, ROUTE_LANES), ytoks, min(COMBINE_TOKENS, n), prev,
                   b0 * s, b_full * s)
    return out.reshape(b_full, s, d)


def _rope_lane_freq():
    inv = ROPE_THETA ** (-jnp.arange(0, ROPE_DIM // 2, dtype=F32) * 2.0 / ROPE_DIM)
    idx = np.full((HEAD_LANES,), -1, np.int64)
    for r in range(ROPE_DIM):
        idx[_head_lane(NOPE_DIM + r)] = r % ROPE_HALF
    return _gather_cols(inv[None, :], idx)


FAST_SOFTMAX_MAX_LOG2 = 60.0


def _score_bound_log2(qg, kg):
    return 1.02 * LOG2E * QK_DIM ** 0.5 * jnp.max(jnp.abs(qg)) * jnp.max(jnp.abs(kg))


def _partner_lanes(idx):
    out = np.full_like(idx, -1)
    for base in range(0, idx.shape[0], HEAD_LANES):
        for r in range(ROPE_DIM):
            lane = _head_lane(NOPE_DIM + r)
            out[base + lane] = idx[base + (lane + HALF_LANES) % HEAD_LANES]
    return out


def kernel(x, mem, positions, ln_mix, w_in, q_lat_norm, w_uq, kv_lat_norm, w_ukv, q_norm, k_norm, conv_w, conv_b,
           dt_bias, a_log, d_skip, ssd_norm, w_out, pool_w, pool_b, pool_scale, ln_xq, ln_mem, xq_w, xkv_w, xq_norm,
           xk_norm, xo_w, ln_ffn, rg_w, rg_b, re_w, re_b, exp_w_gate, exp_w_up, exp_w_down):
    b, s, d = x.shape
    depth = ln_mix.shape[0]
    tm = min(1024, s)
    tq = min(512, s)
    assert d == D_MODEL and s % tm == 0 and s % CHUNK == 0 and tm >= POOL_CARRY

    pos = positions.astype(F32)[..., None]
    invf = _rope_lane_freq()
    hsum = jnp.asarray(np.kron(np.eye(X_HEADS), np.ones((X_HEAD_DIM, X_HEAD_DIM))), BF16)
    ltri = jnp.asarray(np.tril(np.ones((tm, tm)), -1), BF16)
    row2 = lambda v: v.reshape(1, -1)
    lane_pad = lambda v: jnp.pad(v, (0, LANES - v.shape[0])).reshape(1, LANES)

    n_streams = MOE_STREAMS if b % MOE_STREAMS == 0 else 1
    nb = b // n_streams
    nt = s // tm
    streams = [dict(x=x, off=k * nb, pending=None) for k in range(n_streams)]
    for layer in range(depth):
        j = layer // 2
        kbd, vbd = _mem_kv(mem, row2(ln_mem[layer]), xkv_w[layer].astype(BF16),
                           row2(jnp.tile(xk_norm[layer], X_HEADS)), hsum)
        rw = jnp.pad(jnp.concatenate([rg_w[layer], re_w[layer]], axis=1),
                     ((0, 0), (0, ROUTE_LANES - MOE_GROUPS - N_EXPERTS)))
        rw_hi = rw.astype(BF16)
        rw_lo = (rw - rw_hi.astype(F32)).astype(BF16)
        rb = lane_pad(jnp.concatenate([rg_b[layer], re_b[layer]]))
        tail_args = [kbd, vbd, row2(ln_xq[layer]), xq_w[layer].astype(BF16), row2(jnp.tile(xq_norm[layer], X_HEADS)),
                     hsum, xo_w[layer].astype(BF16), row2(ln_ffn[layer]), rw_hi,
                     jnp.concatenate([rw_hi, rw_lo], axis=1), rb, ltri]
        row = lambda i, jj: (i, jj, 0)
        fixed2 = lambda i, jj: (0, 0)
        posts = []
        if layer % 2 == 0:
            x = streams[0]["x"]
            assert all(st["x"] is x and st["pending"] is None for st in streams)
            win = _gather_cols(w_in[j], _win_col_index()).astype(BF16)
            q_idx = _head_col_index(QK_DIM, 0, QK_DIM)
            wuq = _gather_cols(w_uq[j], q_idx).astype(BF16)
            wuq_p = _gather_cols(w_uq[j], _partner_lanes(q_idx)).astype(BF16)
            wuk = _gather_cols(w_ukv[j], _head_col_index(NOPE_DIM + V_DIM, 0, NOPE_DIM)).astype(BF16)
            v_idx = np.full((A_HEADS * HEAD_LANES,), -1, np.int64)
            for hd in range(A_HEADS):
                v_idx[hd * HEAD_LANES:hd * HEAD_LANES + V_DIM] = hd * (NOPE_DIM + V_DIM) + NOPE_DIM + np.arange(V_DIM)
            wuv = _gather_cols(w_ukv[j], v_idx).astype(BF16)
            bound = _score_bound_log2(q_norm[j], k_norm[j])
            koff = jnp.zeros((1, HEAD_LANES), F32).at[0, SCORE_PAD_LANE].set(-bound)
            gain_idx = _head_col_index(QK_DIM, 0, QK_DIM)[:HEAD_LANES]
            lane_consts = jnp.concatenate(
                [_gather_cols(g[None, :], idx) for g in (q_norm[j], k_norm[j])
                 for idx in (gain_idx, _partner_lanes(gain_idx))]
                + [koff, invf, jnp.zeros((2, HEAD_LANES), F32)], axis=0)
            q, k, v, z, xbc, misc = _front_even(
                x, pos, lane_consts, row2(ln_mix[layer]), win, row2(q_lat_norm[j]), wuq, wuq_p,
                row2(kv_lat_norm[j]), wuk, wuv, tm)
            attn = lax.cond(bound <= FAST_SOFTMAX_MAX_LOG2,
                            functools.partial(_attention, tq=tq, online=False),
                            functools.partial(_attention, tq=tq, online=True), q, k, v)
            y = _ssd(xbc, misc, z, conv_w[j], row2(conv_b[j]), lane_pad(dt_bias[j]), lane_pad(a_log[j]),
                     row2(jnp.repeat(d_skip[j], SSD_HEAD_DIM)), row2(ssd_norm[j]))
            half = A_HEADS * V_DIM
            wout = w_out[j].astype(BF16)
            for k, st in enumerate(streams):
                row_k = lambda i, jj, off=st["off"]: (i + off, jj, 0)
                front_args = [x, attn, y, wout]
                front_specs = [pl.BlockSpec((1, tm, d), row_k), pl.BlockSpec((1, tm, half), row_k),
                               pl.BlockSpec((1, tm, D_INNER), row_k), pl.BlockSpec((half + D_INNER, d), fixed2)]
                posts.append(_post("even", front_args, front_specs, tail_args, nb, s, tm, [], b0=k * nb))
        else:
            for k, st in enumerate(streams):
                row_k = lambda i, jj, off=st["off"]: (i + off, jj, 0)
                front_args, front_specs = [st["x"]], [pl.BlockSpec((1, tm, d), row_k)]
                if st["pending"] is not None:
                    route_prev, ytoks = st["pending"]
                    pw = ytoks[0].shape[1]
                    front_args += [route_prev]
                    front_specs += [pl.BlockSpec((1, tm, ROUTE_LANES), row)]
                    for ytok in ytoks:
                        front_args += [ytok, ytok]
                        front_specs += [pl.BlockSpec((tm, pw), lambda i, jj: (i * nt + jj, 0)),
                                        pl.BlockSpec((tm, pw), lambda i, jj: (i * nt + jj + nb * nt, 0))]
                front_args += [row2(ln_mix[layer]), pool_w[j].astype(BF16), row2(pool_b[j]), row2(pool_scale[j])]
                front_specs += [pl.BlockSpec((1, d), fixed2),
                                pl.BlockSpec((len(POOL_WINDOWS), POOL_GROUP, POOL_GROUP), lambda i, jj: (0, 0, 0)),
                                pl.BlockSpec((1, d), fixed2), pl.BlockSpec((1, d), fixed2)]
                posts.append(_post("pool" if st["pending"] is None else "pool_pending", front_args, front_specs,
                                   tail_args, nb, s, tm, [pltpu.VMEM((POOL_CARRY, d), F32)], b0=k * nb))
        streams = []
        for x2, *hfps, route, route_t, counts in posts:
            ytoks = _moe(nb * s, hfps, route_t, counts, exp_w_gate, exp_w_up, exp_w_down, layer)
            streams.append(dict(x=x2, off=0, pending=(route, ytoks)))
        if layer + 1 == depth or (layer + 1) % 2 == 0:
            out = None
            for k, st in enumerate(streams):
                out = _apply_moe(st["x"], *st["pending"], out, k * nb, b)
            streams = [dict(x=out, off=k * nb, pending=None) for k in range(n_streams)]
    return streams[0]["x"]
```

```python
import functools

import numpy as np
import jax
import jax.numpy as jnp
from jax import lax
from jax.experimental import pallas as pl
from jax.experimental.pallas import tpu as pltpu
from jax.experimental.pallas import tpu_sc as plsc

F32 = jnp.float32
BF16 = jnp.bfloat16
U32 = jnp.uint32
I32 = jnp.int32

RMS_EPS = 1e-6
ROPE_THETA = 10000.0

D_MODEL = 1024
X_HEADS, X_HEAD_DIM = 4, 64
A_HEADS, NOPE_DIM, ROPE_DIM, V_DIM = 8, 64, 32, 64
QK_DIM = NOPE_DIM + ROPE_DIM
Q_LORA, KV_LORA = 256, 128
B_HEADS, SSD_HEAD_DIM, SSD_GROUPS, SSD_STATE, CONV_K, CHUNK = 8, 64, 2, 128, 4, 128
D_INNER = B_HEADS * SSD_HEAD_DIM
CONV_CH = D_INNER + 2 * SSD_GROUPS * SSD_STATE
POOL_WINDOWS = (2, 4, 8, 16)
POOL_GROUP = D_MODEL // 4
MOE_GROUPS, EXPERTS_PER_GROUP, TOP_K, EXPERT_FF = 4, 8, 2, 256
N_EXPERTS = MOE_GROUPS * EXPERTS_PER_GROUP

LANES = 128
HEAD_LANES = LANES
HALF_LANES = LANES // 2
ROPE_HALF = ROPE_DIM // 2
NOPE_HALF = NOPE_DIM // 2
POOL_CARRY = 16
CONV_CARRY = 16
VMEM_LIMIT = 56 * 1024 * 1024


def _cparams(sem):
    return pltpu.CompilerParams(dimension_semantics=sem, vmem_limit_bytes=VMEM_LIMIT)


def _rms(u, g):
    return u * lax.rsqrt(jnp.mean(u * u, axis=-1, keepdims=True) + RMS_EPS) * g


def _sigmoid(u):
    return 1.0 / (1.0 + jnp.exp(-u))


def _dot(a, b):
    return jnp.dot(a, b, preferred_element_type=F32)


def _dot_nt(a, b):
    return lax.dot_general(a, b, (((1,), (1,)), ((), ())), preferred_element_type=F32)


def _head_lane(d):
    if d < NOPE_HALF:
        return d
    if d < NOPE_DIM:
        return HALF_LANES + (d - NOPE_HALF)
    r = d - NOPE_DIM
    if r < ROPE_HALF:
        return NOPE_HALF + r
    return HALF_LANES + NOPE_HALF + (r - ROPE_HALF)


def _gather_cols(w, idx):
    w_ext = jnp.concatenate([w, jnp.zeros(w.shape[:-1] + (1,), w.dtype)], axis=-1)
    idx = np.where(idx < 0, w.shape[-1], idx)
    return jnp.take(w_ext, jnp.asarray(idx, dtype=jnp.int32), axis=-1)


_OFF_QLAT = 0
_OFF_KVLAT = _OFF_QLAT + Q_LORA
_OFF_MISC = _OFF_KVLAT + KV_LORA
_OFF_Z = _OFF_MISC + LANES
_OFF_XBC = _OFF_Z + D_INNER
IN_W = _OFF_XBC + CONV_CH


def _win_col_index():
    idx = np.full((IN_W,), -1, np.int64)
    idx[_OFF_QLAT:_OFF_QLAT + Q_LORA] = np.arange(Q_LORA)
    idx[_OFF_KVLAT:_OFF_KVLAT + KV_LORA] = Q_LORA + np.arange(KV_LORA)
    rope0 = Q_LORA + KV_LORA
    for r in range(ROPE_DIM):
        idx[_OFF_MISC + _head_lane(NOPE_DIM + r)] = rope0 + r
    z0 = rope0 + ROPE_DIM
    idx[_OFF_Z:_OFF_Z + D_INNER] = z0 + np.arange(D_INNER)
    xbc0 = z0 + D_INNER
    idx[_OFF_XBC:_OFF_XBC + CONV_CH] = xbc0 + np.arange(CONV_CH)
    dt0 = xbc0 + CONV_CH
    idx[_OFF_MISC:_OFF_MISC + B_HEADS] = dt0 + np.arange(B_HEADS)
    return idx


def _head_col_index(per_head, offset, count):
    idx = np.full((A_HEADS * HEAD_LANES,), -1, np.int64)
    for h in range(A_HEADS):
        for d in range(count):
            idx[h * HEAD_LANES + _head_lane(d)] = h * per_head + offset + d
    return idx


SCORE_PAD_LANE = NOPE_HALF + ROPE_HALF
ONES_LANE = V_DIM
LOG2E = 1.4426950408889634


def _front_even_kernel(x_ref, pos_ref, lc_ref, ln_ref, win_ref, qln_ref, wuq_ref, wuqr_ref, kvln_ref, wuk_ref, wuv_ref,
                       q_ref, k_ref, v_ref, z_ref, xbc_ref, misc_ref):
    x = x_ref[0]
    h = _rms(x, ln_ref[...]).astype(BF16)
    proj = _dot(h, win_ref[...])
    misc = proj[:, _OFF_MISC:_OFF_Z]
    z_ref[0] = proj[:, _OFF_Z:_OFF_XBC].astype(BF16)
    xbc_ref[0] = proj[:, _OFF_XBC:].astype(BF16)
    misc_ref[0] = misc
    ql = _rms(proj[:, _OFF_QLAT:_OFF_KVLAT], qln_ref[...]).astype(BF16)
    kvl = _rms(proj[:, _OFF_KVLAT:_OFF_MISC], kvln_ref[...]).astype(BF16)
    q = _dot(ql, wuq_ref[...])
    kn = _dot(kvl, wuk_ref[...])
    v = _dot(kvl, wuv_ref[...])
    lane = lax.broadcasted_iota(I32, (1, HEAD_LANES), 1)
    first_half = (lane >= NOPE_HALF) & (lane < NOPE_HALF + ROPE_HALF)
    second_half = (lane >= HALF_LANES + NOPE_HALF) & (lane < HALF_LANES + NOPE_HALF + ROPE_HALF)
    lc = lc_ref[...]
    qg, qg_p, kg, kg_p, k_off, invf = (lc[i:i + 1] for i in range(6))
    ones = jnp.ones((HEAD_LANES, HEAD_LANES), BF16)

    def lane_sumsq(u):
        return _dot((u * u).astype(BF16), ones)

    krope = jnp.where(first_half | second_half, misc, 0.0)
    kr_ss = lane_sumsq(krope)
    ang = pos_ref[0] * invf
    cos_t = jnp.cos(ang)
    sin_t = jnp.where(first_half, -jnp.sin(ang), jnp.sin(ang))
    q_one = (lane == SCORE_PAD_LANE).astype(F32)
    v_one = (lane == ONES_LANE).astype(F32)
    q_scale = QK_DIM ** -0.5 * LOG2E
    qa, qb = qg * cos_t * q_scale, qg_p * sin_t * q_scale
    ka, kc = kg * cos_t, pltpu.roll(krope, HALF_LANES, 1) * (kg_p * sin_t)
    q_p = _dot(ql, wuqr_ref[...])
    for hd in range(A_HEADS):
        sl = slice(hd * HEAD_LANES, (hd + 1) * HEAD_LANES)
        qs = q[:, sl]
        inv = lax.rsqrt(lane_sumsq(qs) * (1.0 / QK_DIM) + RMS_EPS)
        q_ref[0, hd] = ((qs * qa + q_p[:, sl] * qb) * inv + q_one).astype(BF16)
        kns = kn[:, sl]
        inv = lax.rsqrt((lane_sumsq(kns) + kr_ss) * (1.0 / QK_DIM) + RMS_EPS)
        k_ref[0, hd] = (((kns + krope) * ka + kc) * inv + k_off).astype(BF16)
        v_ref[0, hd] = (v[:, sl] + v_one).astype(BF16)


def _front_even(x, pos, lane_consts, ln, win, qln, wuq, wuq_p, kvln, wuk, wuv, tm):
    b, s, d = x.shape
    grid = (b, s // tm)
    row = lambda i, j: (i, j, 0)
    fixed2 = lambda i, j: (0, 0)
    head_row = lambda i, j: (i, 0, j, 0)
    hw = A_HEADS * HEAD_LANES
    return pl.pallas_call(
        _front_even_kernel,
        grid=grid,
        in_specs=[
            pl.BlockSpec((1, tm, d), row),
            pl.BlockSpec((1, tm, 1), row),
            pl.BlockSpec((8, HEAD_LANES), fixed2),
            pl.BlockSpec((1, d), fixed2),
            pl.BlockSpec((d, IN_W), fixed2),
            pl.BlockSpec((1, Q_LORA), fixed2),
            pl.BlockSpec((Q_LORA, hw), fixed2),
            pl.BlockSpec((Q_LORA, hw), fixed2),
            pl.BlockSpec((1, KV_LORA), fixed2),
            pl.BlockSpec((KV_LORA, hw), fixed2),
            pl.BlockSpec((KV_LORA, hw), fixed2),
        ],
        out_specs=[
            pl.BlockSpec((1, A_HEADS, tm, HEAD_LANES), head_row),
            pl.BlockSpec((1, A_HEADS, tm, HEAD_LANES), head_row),
            pl.BlockSpec((1, A_HEADS, tm, HEAD_LANES), head_row),
            pl.BlockSpec((1, tm, D_INNER), row),
            pl.BlockSpec((1, tm, CONV_CH), row),
            pl.BlockSpec((1, tm, HEAD_LANES), row),
        ],
        out_shape=[
            jax.ShapeDtypeStruct((b, A_HEADS, s, HEAD_LANES), BF16),
            jax.ShapeDtypeStruct((b, A_HEADS, s, HEAD_LANES), BF16),
            jax.ShapeDtypeStruct((b, A_HEADS, s, HEAD_LANES), BF16),
            jax.ShapeDtypeStruct((b, s, D_INNER), BF16),
            jax.ShapeDtypeStruct((b, s, CONV_CH), BF16),
            jax.ShapeDtypeStruct((b, s, HEAD_LANES), F32),
        ],
        compiler_params=_cparams(("parallel", "parallel")),
        name="front_even",
    )(x, pos, lane_consts, ln, win, qln, wuq, wuq_p, kvln, wuk, wuv)


HEADS_PER_STEP = 8


def _attn_kernel(q_ref, k_ref, v_ref, o_ref, *, tq, online):
    qi = pl.program_id(2)
    row = lax.broadcasted_iota(I32, (tq, tq), 0)
    col = lax.broadcasted_iota(I32, (tq, tq), 1)

    def head_step(hh, j, carry, masked):
        kj = k_ref[0, hh, pl.ds(j * tq, tq), :]
        vj = v_ref[0, hh, pl.ds(j * tq, tq), :]
        s = _dot_nt(q_ref[0, hh], kj)
        if masked:
            s = jnp.where(row >= col, s, -jnp.inf)
        if online:
            m, acc = carry
            m_new = jnp.maximum(m, jnp.max(s, axis=-1, keepdims=True))
            p = jnp.exp2(s - m_new).astype(BF16)
            return m_new, jnp.exp2(m - m_new) * acc + _dot(p, vj)
        return carry + _dot(jnp.exp2(s).astype(BF16), vj)

    def step(j, carries, masked):
        return tuple(head_step(hh, j, carries[hh], masked) for hh in range(HEADS_PER_STEP))

    acc0 = jnp.zeros((tq, HEAD_LANES), F32)
    init = (jnp.full((tq, 1), -jnp.inf, F32), acc0) if online else acc0
    carries = lax.fori_loop(0, qi, functools.partial(step, masked=False), (init,) * HEADS_PER_STEP)
    carries = step(qi, carries, True)
    outs = []
    for carry in carries:
        acc = carry[1] if online else carry
        outs.append(acc / acc[:, ONES_LANE:ONES_LANE + 1])
    lane = lax.broadcasted_iota(I32, (1, HEAD_LANES), 1)
    per_group = HEAD_LANES // V_DIM
    groups = []
    for g0 in range(0, HEADS_PER_STEP, per_group):
        out = outs[g0]
        for hh in range(1, per_group):
            out = jnp.where(lane >= hh * V_DIM, pltpu.roll(outs[g0 + hh], hh * V_DIM, 1), out)
        groups.append(out.astype(BF16))
    o_ref[0] = jnp.concatenate(groups, axis=1)


def _attention(q, k, v, tq, online):
    b, nh, s, _ = q.shape
    grid = (b, nh // HEADS_PER_STEP, s // tq)
    kv_spec = pl.BlockSpec((1, HEADS_PER_STEP, s, HEAD_LANES), lambda i, h, j: (i, h, 0, 0))
    return pl.pallas_call(
        functools.partial(_attn_kernel, tq=tq, online=online),
        grid=grid,
        in_specs=[
            pl.BlockSpec((1, HEADS_PER_STEP, tq, HEAD_LANES), lambda i, h, j: (i, h, j, 0)),
            kv_spec,
            kv_spec,
        ],
        out_specs=pl.BlockSpec((1, tq, HEADS_PER_STEP * V_DIM), lambda i, h, j: (i, j, h)),
        out_shape=jax.ShapeDtypeStruct((b, s, nh * V_DIM), BF16),
        compiler_params=_cparams(("parallel", "parallel", "parallel")),
        name="mla_attention_online" if online else "mla_attention",
    )(q, k, v)


def _ssd_kernel(xbc_ref, misc_ref, z_ref, cw_ref, cb_ref, dtb_ref, alog_ref, dskip_ref, gn_ref, y_ref,
                state_ref, carry_ref):
    c = pl.program_id(1)
    t = CHUNK
    rows = y_ref.shape[1]

    @pl.when(c == 0)
    def _():
        state_ref[...] = jnp.zeros_like(state_ref)
        carry_ref[...] = jnp.zeros_like(carry_ref)

    xr = xbc_ref[0].astype(F32)
    xcat = jnp.concatenate([carry_ref[...], xr], axis=0)
    carry_ref[...] = xr[rows - CONV_CARRY:, :]
    conv = jnp.zeros((rows, CONV_CH), F32) + cb_ref[...]
    xcat16 = xcat.astype(BF16)
    s_row = lax.broadcasted_iota(I32, (t, t + CONV_CARRY), 0)
    s_col = lax.broadcasted_iota(I32, (t, t + CONV_CARRY), 1)
    for kk in range(CONV_K):
        sh = CONV_K - 1 - kk
        if sh == 0:
            shifted = xr
        else:
            pick = (s_col == s_row + (CONV_CARRY - sh)).astype(BF16)
            shifted = jnp.concatenate([_dot(pick, xcat16[ci * t:(ci + 1) * t + CONV_CARRY, :])
                                       for ci in range(rows // t)], axis=0)
        conv = conv + cw_ref[kk:kk + 1, :] * shifted
    xa = conv * _sigmoid(conv)
    xs = xa[:, :D_INNER]
    gw = SSD_GROUPS * SSD_STATE
    bmat = xa[:, D_INNER:D_INNER + gw]
    cmat = xa[:, D_INNER + gw:]

    u = misc_ref[0] + dtb_ref[...]
    dt = jnp.maximum(u, 0.0) + jnp.log(1.0 + jnp.exp(-jnp.abs(u)))
    a = -jnp.exp(alog_ref[...])
    lane = lax.broadcasted_iota(I32, (1, LANES), 1)
    adt_all = jnp.where(lane < B_HEADS, dt * a, 0.0)
    rowi = lax.broadcasted_iota(I32, (t, LANES), 0)
    tri = lax.broadcasted_iota(I32, (t, t), 0) >= lax.broadcasted_iota(I32, (t, t), 1)
    rep = B_HEADS // SSD_GROUPS
    y_chunks = []
    for ci in range(rows // t):
        sl = slice(ci * t, (ci + 1) * t)
        acs = adt_all[sl]
        sh = 1
        while sh < t:
            acs = acs + jnp.where(rowi >= sh, pltpu.roll(acs, sh, 0), 0.0)
            sh *= 2
        acs_t = acs.T
        ys = []
        for g in range(SSD_GROUPS):
            bg = bmat[sl, g * SSD_STATE:(g + 1) * SSD_STATE]
            cg = cmat[sl, g * SSD_STATE:(g + 1) * SSD_STATE]
            bg16, cg16 = bg.astype(BF16), cg.astype(BF16)
            cb = _dot_nt(cg16, bg16)
            bg_t = bg.T
            for r in range(rep):
                hd = g * rep + r
                col = acs[:, hd:hd + 1]
                rw = acs_t[hd:hd + 1, :]
                last = acs_t[hd:hd + 1, t - 1:t]
                decay = jnp.exp(jnp.where(tri, col - rw, -jnp.inf))
                xh = xs[sl, hd * SSD_HEAD_DIM:(hd + 1) * SSD_HEAD_DIM]
                xdt = (xh * dt[sl, hd:hd + 1]).astype(BF16)
                y_diag = _dot((cb * decay).astype(BF16), xdt)
                prev = state_ref[hd]
                y_off = _dot(cg16, prev.astype(BF16)) * jnp.exp(col)
                new_state = _dot((bg_t * jnp.exp(last - rw)).astype(BF16), xdt)
                state_ref[hd] = prev * jnp.exp(last) + new_state
                ys.append(y_diag + y_off)
        y_chunks.append(jnp.concatenate(ys, axis=1))
    y = jnp.concatenate(y_chunks, axis=0) + xs * dskip_ref[...]
    zf = z_ref[0].astype(F32)
    y = y * (zf * _sigmoid(zf))
    y_ref[0] = _rms(y, gn_ref[...]).astype(BF16)


SSD_CHUNKS_PER_STEP = 4


def _ssd(xbc, misc, z, cw, cb, dtb, alog, dskip, gn):
    b, s, _ = xbc.shape
    rows = CHUNK * SSD_CHUNKS_PER_STEP if s % (CHUNK * SSD_CHUNKS_PER_STEP) == 0 else CHUNK
    grid = (b, s // rows)
    row = lambda i, j: (i, j, 0)
    fixed2 = lambda i, j: (0, 0)
    return pl.pallas_call(
        _ssd_kernel,
        grid=grid,
        in_specs=[
            pl.BlockSpec((1, rows, CONV_CH), row),
            pl.BlockSpec((1, rows, LANES), row),
            pl.BlockSpec((1, rows, D_INNER), row),
            pl.BlockSpec((CONV_K, CONV_CH), fixed2),
            pl.BlockSpec((1, CONV_CH), fixed2),
            pl.BlockSpec((1, LANES), fixed2),
            pl.BlockSpec((1, LANES), fixed2),
            pl.BlockSpec((1, D_INNER), fixed2),
            pl.BlockSpec((1, D_INNER), fixed2),
        ],
        out_specs=pl.BlockSpec((1, rows, D_INNER), row),
        out_shape=jax.ShapeDtypeStruct((b, s, D_INNER), BF16),
        scratch_shapes=[
            pltpu.VMEM((B_HEADS, SSD_STATE, SSD_HEAD_DIM), F32),
            pltpu.VMEM((CONV_CARRY, CONV_CH), F32),
        ],
        compiler_params=_cparams(("parallel", "arbitrary")),
        name="ssd_scan",
    )(xbc, misc, z, cw, cb, dtb, alog, dskip, gn)


XW = X_HEADS * X_HEAD_DIM


def _mem_kv_kernel(mem_ref, ln_ref, wkv_ref, kg_ref, hsum_ref, kbd_ref, vbd_ref):
    m = mem_ref.shape[1]
    mn = _rms(mem_ref[0], ln_ref[...]).astype(BF16)
    kv = _dot(mn, wkv_ref[...])
    k, v = kv[:, :XW], kv[:, XW:]
    ss = _dot((k * k).astype(BF16), hsum_ref[...])
    kn_t = (k * lax.rsqrt(ss * (1.0 / X_HEAD_DIM) + RMS_EPS) * kg_ref[...]).T.astype(BF16)
    v16 = v.astype(BF16)
    head_of_lane = lax.shift_right_arithmetic(lax.broadcasted_iota(I32, (1, XW), 1), jnp.int32(_LOG2_XHD))
    head_of_row = lax.shift_right_arithmetic(lax.broadcasted_iota(I32, (XW, 1), 0), jnp.int32(_LOG2_XHD))
    for hd in range(X_HEADS):
        kbd_ref[0, :, hd * m:(hd + 1) * m] = jnp.where(head_of_row == hd, kn_t, jnp.zeros_like(kn_t))
        vbd_ref[0, hd * m:(hd + 1) * m, :] = jnp.where(head_of_lane == hd, v16, jnp.zeros_like(v16))


def _mem_kv(mem, ln, wkv, kg, hsum):
    b, m, d = mem.shape
    fixed2 = lambda i: (0, 0)
    return pl.pallas_call(
        _mem_kv_kernel,
        grid=(b,),
        in_specs=[
            pl.BlockSpec((1, m, d), lambda i: (i, 0, 0)),
            pl.BlockSpec((1, d), fixed2),
            pl.BlockSpec((d, 2 * XW), fixed2),
            pl.BlockSpec((1, XW), fixed2),
            pl.BlockSpec((XW, XW), fixed2),
        ],
        out_specs=[
            pl.BlockSpec((1, XW, X_HEADS * m), lambda i: (i, 0, 0)),
            pl.BlockSpec((1, X_HEADS * m, XW), lambda i: (i, 0, 0)),
        ],
        out_shape=[
            jax.ShapeDtypeStruct((b, XW, X_HEADS * m), BF16),
            jax.ShapeDtypeStruct((b, X_HEADS * m, XW), BF16),
        ],
        compiler_params=_cparams(("parallel",)),
        name="mem_kv",
    )(mem, ln, wkv, kg, hsum)


ROUTE_LANES = LANES
ROW_PARTS = 2
_GROUP_LANE0 = 0
_EXPERT_LANE0 = MOE_GROUPS
_LOG2_EPG = EXPERTS_PER_GROUP.bit_length() - 1
_LOG2_XHD = X_HEAD_DIM.bit_length() - 1


def _pack_bf16_pairs(v):
    w = v.shape[1] // 2
    r = v.astype(BF16).astype(F32)
    hi = lax.bitcast_convert_type(r[:, :w], U32)
    lo = lax.bitcast_convert_type(r[:, w:], U32)
    return (hi & jnp.uint32(0xFFFF0000)) | (lo >> jnp.uint32(16))


def _unpack_bf16_pairs(u):
    hi = lax.bitcast_convert_type(u & jnp.uint32(0xFFFF0000), F32)
    lo = lax.bitcast_convert_type(u << jnp.uint32(16), F32)
    return hi, lo


def _tail_rows(x1, kbd_ref, vbd_ref, lnq_ref, wq_ref, qg_ref, hsum_ref, wo_ref, lnf_ref, rwh_ref, rwl_ref,
               rb_ref, x2_ref, hfp_refs):
    tm = x1.shape[0]
    m = vbd_ref.shape[1] // X_HEADS
    hq = _rms(x1, lnq_ref[...]).astype(BF16)
    q = _dot(hq, wq_ref[...])
    ss = _dot((q * q).astype(BF16), hsum_ref[...])
    qn = (q * lax.rsqrt(ss * (1.0 / X_HEAD_DIM) + RMS_EPS) * qg_ref[...] * (X_HEAD_DIM ** -0.5)).astype(BF16)
    s = _dot(qn, kbd_ref[0])
    ps = []
    for hd in range(X_HEADS):
        sh = s[:, hd * m:(hd + 1) * m]
        e = jnp.exp(sh - jnp.max(sh, axis=-1, keepdims=True))
        ps.append((e / jnp.sum(e, axis=-1, keepdims=True)).astype(BF16))
    o = _dot(jnp.concatenate(ps, axis=1), vbd_ref[0]).astype(BF16)
    x2 = x1 + _dot(o, wo_ref[...])
    x2_ref[0] = x2

    hf = _rms(x2, lnf_ref[...])
    hf_hi = hf.astype(BF16)
    packed = _pack_bf16_pairs(hf)
    pw = packed.shape[1] // ROW_PARTS
    for c in range(ROW_PARTS):
        hfp_refs[c][0] = packed[:, c * pw:(c + 1) * pw]
    hf_lo = (hf - hf_hi.astype(F32)).astype(BF16)
    hi_terms = _dot(hf_hi, rwl_ref[...])
    logits = (hi_terms[:, :ROUTE_LANES] + hi_terms[:, ROUTE_LANES:] + _dot(hf_lo, rwh_ref[...]) + rb_ref[...])

    lane_i = lax.broadcasted_iota(I32, (tm, ROUTE_LANES), 1)
    lane = lane_i.astype(F32)
    big = float(ROUTE_LANES)
    neg = -jnp.inf
    gl = jnp.where(lane_i < MOE_GROUPS, logits, neg)
    gmax = jnp.max(gl, axis=-1, keepdims=True)
    gsum = jnp.sum(jnp.exp(gl - gmax), axis=-1, keepdims=True)
    g_p = 1.0 / gsum
    g_idx = jnp.min(jnp.where(gl == gmax, lane, big), axis=-1, keepdims=True)
    e_lane = lane_i - _EXPERT_LANE0
    grp_of_lane = lax.shift_right_arithmetic(e_lane, jnp.int32(_LOG2_EPG)).astype(F32)
    in_grp = (e_lane >= 0) & (e_lane < N_EXPERTS) & (grp_of_lane == g_idx)
    el = jnp.where(in_grp, logits, neg)
    emax = jnp.max(el, axis=-1, keepdims=True)
    idx1 = jnp.min(jnp.where(el == emax, lane, big), axis=-1, keepdims=True)
    el2 = jnp.where(lane == idx1, neg, el)
    emax2 = jnp.max(el2, axis=-1, keepdims=True)
    idx2 = jnp.min(jnp.where(el2 == emax2, lane, big), axis=-1, keepdims=True)
    r2 = jnp.exp(emax2 - emax)
    gate1 = g_p / (1.0 + r2)
    gate2 = g_p * r2 / (1.0 + r2)
    e1 = idx1 - float(_EXPERT_LANE0)
    e2 = idx2 - float(_EXPERT_LANE0)

    route = jnp.where(lane == 0, e1, 0.0)
    route = jnp.where(lane == 1, e2, route)
    route = jnp.where(lane == 2, gate1, route)
    route = jnp.where(lane == 3, gate2, route)
    return route, (lane == e1).astype(F32), (lane == e2).astype(F32)


def _tail(x1, kbd_ref, vbd_ref, lnq_ref, wq_ref, qg_ref, hsum_ref, wo_ref, lnf_ref, rwh_ref, rwl_ref, rb_ref,
          ltri_ref, x2_ref, *out_refs):
    hfp_refs, (route_ref, route_t_ref, cnt_ref) = out_refs[:ROW_PARTS], out_refs[ROW_PARTS:]
    tm = x1.shape[0]
    route, oh1, oh2 = _tail_rows(x1, kbd_ref, vbd_ref, lnq_ref, wq_ref, qg_ref, hsum_ref, wo_ref, lnf_ref, rwh_ref,
                                 rwl_ref, rb_ref, x2_ref, hfp_refs)
    both = oh1 + oh2
    before = _dot(ltri_ref[...], both.astype(BF16))
    rank1 = jnp.sum(before * oh1, axis=-1, keepdims=True)
    rank2 = jnp.sum(before * oh2, axis=-1, keepdims=True)
    cnt_ref[0] = jnp.broadcast_to(jnp.sum(both, axis=0, keepdims=True), cnt_ref.shape[1:])
    lane = lax.broadcasted_iota(I32, (tm, ROUTE_LANES), 1)
    route = jnp.where(lane == 4, rank1, route)
    route = jnp.where(lane == 5, rank2, route)
    route_ref[0] = route
    route_t_ref[0] = route.T[:route_t_ref.shape[1], :]


_TAIL_IN = 12


def _post_even_kernel(x_ref, a_ref, y_ref, wout_ref, *rest):
    tail_in, outs = rest[:_TAIL_IN], rest[_TAIL_IN:]
    half = wout_ref.shape[0] // 2
    x1 = x_ref[0] + _dot(a_ref[0], wout_ref[:half, :]) + _dot(y_ref[0], wout_ref[half:, :])
    _tail(x1, *tail_in, *outs)


def _add_expert_rows(x, route, ys):
    g1, g2 = route[:, 2:3], route[:, 3:4]
    his, los = [], []
    for y1, y2 in ys:
        h1, l1 = _unpack_bf16_pairs(y1)
        h2, l2 = _unpack_bf16_pairs(y2)
        his.append(h1 * g1 + h2 * g2)
        los.append(l1 * g1 + l2 * g2)
    return x + jnp.concatenate(his + los, axis=1)


N_PENDING = 1 + 2 * ROW_PARTS


def _post_pool_kernel(x_ref, *rest, pending):
    if pending:
        route_prev_ref, y_refs, rest = rest[0], rest[1:N_PENDING], rest[N_PENDING:]
    (ln_ref, pw_ref, pb_ref, ps_ref), rest = rest[:4], rest[4:]
    tail_in, outs, carry_ref = rest[:_TAIL_IN], rest[_TAIL_IN:-1], rest[-1]
    j = pl.program_id(1)
    tm = x_ref.shape[1]

    @pl.when(j == 0)
    def _():
        carry_ref[...] = jnp.zeros_like(carry_ref)

    x = x_ref[0]
    if pending:
        x = _add_expert_rows(x, route_prev_ref[0], [(y_refs[2 * c][...], y_refs[2 * c + 1][...])
                                                    for c in range(ROW_PARTS)])
    h = _rms(x, ln_ref[...])
    pos = (j * tm + 1 + lax.broadcasted_iota(I32, (tm, 1), 0)).astype(F32)
    mixed = []
    for g, w in enumerate(POOL_WINDOWS):
        sl = slice(g * POOL_GROUP, (g + 1) * POOL_GROUP)
        hg = h[:, sl]
        acc = jnp.concatenate([carry_ref[:, sl], hg], axis=0)
        sh = 1
        while sh < w:
            acc = acc + pltpu.roll(acc, sh, 0)
            sh *= 2
        win = acc[POOL_CARRY:, :]
        dlt = win / jnp.minimum(pos, float(w)) - hg
        mixed.append(_dot(dlt.astype(BF16), pw_ref[g]))
    carry_ref[...] = h[tm - POOL_CARRY:, :]
    y = (jnp.concatenate(mixed, axis=1) + pb_ref[...]) * ps_ref[...]
    _tail(x + y, *tail_in, *outs)


def _post(kind, front_args, front_specs, tail_args, b, s, tm, scratch, b0=0):
    d = D_MODEL
    m4 = tail_args[1].shape[1]
    row = lambda i, j: (i, j, 0)
    fixed2 = lambda i, j: (0, 0)
    per_b = lambda i, j: (i + b0, 0, 0)
    tail_specs = [
        pl.BlockSpec((1, XW, m4), per_b),
        pl.BlockSpec((1, m4, XW), per_b),
        pl.BlockSpec((1, d), fixed2),
        pl.BlockSpec((d, XW), fixed2),
        pl.BlockSpec((1, XW), fixed2),
        pl.BlockSpec((XW, XW), fixed2),
        pl.BlockSpec((XW, d), fixed2),
        pl.BlockSpec((1, d), fixed2),
        pl.BlockSpec((d, ROUTE_LANES), fixed2),
        pl.BlockSpec((d, 2 * ROUTE_LANES), fixed2),
        pl.BlockSpec((1, ROUTE_LANES), fixed2),
        pl.BlockSpec((tm, tm), fixed2),
    ]
    nt = s // tm
    pw = d // 2 // ROW_PARTS
    kernel = {"even": _post_even_kernel,
              "pool": functools.partial(_post_pool_kernel, pending=False),
              "pool_pending": functools.partial(_post_pool_kernel, pending=True)}[kind]
    return pl.pallas_call(
        kernel,
        grid=(b, nt),
        in_specs=front_specs + tail_specs,
        out_specs=[pl.BlockSpec((1, tm, d), row)]
        + [pl.BlockSpec((1, tm, pw), row)] * ROW_PARTS
        + [pl.BlockSpec((1, tm, ROUTE_LANES), row),
           pl.BlockSpec((1, 8, tm), lambda i, j: (i * nt + j, 0, 0)),
           pl.BlockSpec((1, 8, ROUTE_LANES), lambda i, j: (i * nt + j, 0, 0))],
        out_shape=[jax.ShapeDtypeStruct((b, s, d), F32)]
        + [jax.ShapeDtypeStruct((b, s, pw), U32)] * ROW_PARTS
        + [jax.ShapeDtypeStruct((b, s, ROUTE_LANES), F32),
           jax.ShapeDtypeStruct((b * nt, 8, tm), F32),
           jax.ShapeDtypeStruct((b * nt, 8, ROUTE_LANES), F32)],
        scratch_shapes=scratch,
        compiler_params=_cparams(("parallel", "arbitrary")),
        name="post_" + kind,
    )(*front_args, *tail_args)


FFN_ROWS = 1024
COMBINE_TOKENS = 512
MOE_STREAMS = 2
SC_GATHER_WINDOW = 128


def _sc_gather_rows(table, idx):
    m, w = idx.shape[0], table.shape[1]
    mesh = plsc.VectorSubcoreMesh(core_axis_name="core", subcore_axis_name="subcore")

    @pl.kernel(out_type=jax.ShapeDtypeStruct((m, w), table.dtype), mesh=mesh, name="moe_row_gather")
    def gather(t_hbm, i_hbm, o_hbm):
        def body(i_vmem, o_vmem):
            pltpu.sync_copy(t_hbm.at[i_vmem.at[0]], o_vmem)

        pltpu.emit_pipeline(
            body,
            grid=(m // SC_GATHER_WINDOW,),
            in_specs=[pl.BlockSpec((1, SC_GATHER_WINDOW), lambda i: (0, i))],
            out_specs=[pl.BlockSpec((SC_GATHER_WINDOW, w), lambda i: (i, 0))],
            core_axis_name=("core", "subcore"),
            dimension_semantics=(pltpu.PARALLEL,),
        )(i_hbm, o_hbm)

    return gather(table, idx.reshape(1, m))


def _sc_scatter_rows(src, dests, pad_rows, n_rows):
    n, w = src.shape
    win = SC_GATHER_WINDOW
    mesh = plsc.VectorSubcoreMesh(core_axis_name="core", subcore_axis_name="subcore")
    idx_spec = pl.BlockSpec((1, win), lambda i: (0, i))
    split = dict(core_axis_name=("core", "subcore"), dimension_semantics=(pltpu.PARALLEL,))

    @pl.kernel(out_type=jax.ShapeDtypeStruct((n_rows, w), src.dtype), mesh=mesh, name="moe_row_scatter")
    def scatter(s_hbm, z_hbm, p_hbm, *rest):
        d_hbms, o_hbm = rest[:-1], rest[-1]

        def body(s_vmem, *i_vmems):
            for i_vmem in i_vmems:
                pltpu.sync_copy(s_vmem, o_hbm.at[i_vmem.at[0]])

        pltpu.emit_pipeline(
            body, grid=(n // win,),
            in_specs=[pl.BlockSpec((win, w), lambda i: (i, 0))] + [idx_spec] * len(dests),
            out_specs=[], **split)(s_hbm, *d_hbms)

        def zero_body(z_vmem, i_vmem):
            pltpu.sync_copy(z_vmem, o_hbm.at[i_vmem.at[0]])

        pltpu.emit_pipeline(
            zero_body, grid=(pad_rows.shape[0] // win,),
            in_specs=[pl.BlockSpec((win, w), lambda i: (0, 0)), idx_spec],
            out_specs=[], **split)(z_hbm, p_hbm)

    zeros = jnp.zeros((win, w), src.dtype)
    return scatter(src, zeros, pad_rows.reshape(1, -1), *[dd.reshape(1, n) for dd in dests])


def _ffn_kernel(be_ref, bi_ref, *refs):
    xb_refs, (wg_ref, wu_ref, wd_ref) = refs[:ROW_PARTS], refs[ROW_PARTS:ROW_PARTS + 3]
    yb_refs, (wg_s, wu_s, wd_s) = refs[ROW_PARTS + 3:2 * ROW_PARTS + 3], refs[2 * ROW_PARTS + 3:]
    i = pl.program_id(0)
    changed = jnp.logical_or(i == 0, be_ref[i] != be_ref[jnp.maximum(i - 1, 0)])

    @pl.when(changed)
    def _():
        wg_s[...] = wg_ref[0, 0].astype(BF16)
        wu_s[...] = wu_ref[0, 0].astype(BF16)
        wd_s[...] = wd_ref[0, 0].astype(BF16)

    @pl.when(bi_ref[i] == i)
    def _():
        half = wg_s.shape[0] // 2
        gate = up = None
        for c in range(ROW_PARTS):
            hi, lo = _unpack_bf16_pairs(xb_refs[c][...])
            hi, lo = hi.astype(BF16), lo.astype(BF16)
            pw = hi.shape[1]
            hs, ls = slice(c * pw, (c + 1) * pw), slice(half + c * pw, half + (c + 1) * pw)
            g = _dot(hi, wg_s[hs, :]) + _dot(lo, wg_s[ls, :])
            u = _dot(hi, wu_s[hs, :]) + _dot(lo, wu_s[ls, :])
            gate, up = (g, u) if gate is None else (gate + g, up + u)
        act = (gate * _sigmoid(gate) * up).astype(BF16)
        packed = _pack_bf16_pairs(_dot(act, wd_s[...]))
        pw = packed.shape[1] // ROW_PARTS
        for c in range(ROW_PARTS):
            yb_refs[c][...] = packed[:, c * pw:(c + 1) * pw]


def _expert_ffn(block_e, block_i, xbs, wg, wu, wd, layer):
    n_rows, pw = xbs[0].shape
    d, ff = wg.shape[2], wg.shape[3]
    n_blk = n_rows // FFN_ROWS
    row_spec = pl.BlockSpec((FFN_ROWS, pw), lambda i, be, bi: (bi[i], 0))
    return pl.pallas_call(
        _ffn_kernel,
        grid_spec=pltpu.PrefetchScalarGridSpec(
            num_scalar_prefetch=2,
            grid=(n_blk,),
            in_specs=[row_spec] * ROW_PARTS + [
                pl.BlockSpec((1, 1, d, ff), lambda i, be, bi: (layer, be[i], 0, 0)),
                pl.BlockSpec((1, 1, d, ff), lambda i, be, bi: (layer, be[i], 0, 0)),
                pl.BlockSpec((1, 1, ff, d), lambda i, be, bi: (layer, be[i], 0, 0)),
            ],
            out_specs=[row_spec] * ROW_PARTS,
            scratch_shapes=[pltpu.VMEM((d, ff), BF16), pltpu.VMEM((d, ff), BF16), pltpu.VMEM((ff, d), BF16)],
        ),
        out_shape=[jax.ShapeDtypeStruct((n_rows, pw), U32)] * ROW_PARTS,
        compiler_params=_cparams(("arbitrary",)),
        name="moe_expert_ffn",
    )(block_e, block_i, *xbs, wg, wu, wd)


def _combine_kernel(x_ref, route_ref, *refs):
    y_refs, o_ref = refs[:2 * ROW_PARTS], refs[-1]
    o_ref[...] = _add_expert_rows(x_ref[...], route_ref[...],
                                  [(y_refs[2 * c][...], y_refs[2 * c + 1][...]) for c in range(ROW_PARTS)])


def _combine(x2, route, ytoks, tc, out_prev, row0, n_full):
    n, d = x2.shape
    w = ytoks[0].shape[1]
    nsteps = n // tc
    blk0 = row0 // tc
    specs = [pl.BlockSpec((tc, d), lambda i: (i, 0)), pl.BlockSpec((tc, ROUTE_LANES), lambda i: (i, 0))]
    args = [x2, route]
    for ytok in ytoks:
        specs += [pl.BlockSpec((tc, w), lambda i: (i, 0)), pl.BlockSpec((tc, w), lambda i: (i + nsteps, 0))]
        args += [ytok, ytok]
    aliases = {}
    if out_prev is not None:
        specs.append(pl.BlockSpec(memory_space=pl.ANY))
        args.append(out_prev)
        aliases = {len(args) - 1: 0}
    return pl.pallas_call(
        _combine_kernel,
        grid=(nsteps,),
        in_specs=specs,
        out_specs=pl.BlockSpec((tc, d), lambda i: (i + blk0, 0)),
        out_shape=jax.ShapeDtypeStruct((n_full, d), F32),
        input_output_aliases=aliases,
        compiler_params=_cparams(("parallel",)),
        name="moe_combine",
    )(*args)


def _moe(n, hfps, route_t, counts, wg, wu, wd, layer):
    cnt = counts[:, 0, :N_EXPERTS].astype(I32)
    total = jnp.sum(cnt, axis=0)
    padded = (total + FFN_ROWS - 1) // FFN_ROWS * FFN_ROWS
    pad_end = jnp.cumsum(padded)
    pad_start = pad_end - padded
    tile_base = pad_start[None, :] + jnp.cumsum(cnt, axis=0) - cnt
    expert_ids = jnp.arange(N_EXPERTS, dtype=I32)[:, None]
    dests = []
    for k in range(TOP_K):
        ek = route_t[:, k, :].astype(I32)
        base = jnp.sum(jnp.where(ek[:, None, :] == expert_ids, tile_base[:, :, None], 0), axis=1)
        dests.append((base + route_t[:, 4 + k, :].astype(I32)).reshape(n))
    dest_by_slot = jnp.concatenate(dests)
    n_blk = (n * TOP_K) // FFN_ROWS + N_EXPERTS
    n_rows = n_blk * FFN_ROWS
    used = pad_end[-1] // FFN_ROWS
    block_i = jnp.minimum(jnp.arange(n_blk, dtype=I32), used - 1).astype(I32)
    ended = (pad_end[None, :] <= (block_i * FFN_ROWS)[:, None]).astype(I32)
    block_e = jnp.minimum(jnp.sum(ended, axis=1), N_EXPERTS - 1).astype(I32)
    seg_len = jnp.concatenate([padded - total, (n_rows - pad_end[-1])[None]])
    seg_first = jnp.concatenate([pad_start + total, pad_end[-1:]])
    seg_end = jnp.cumsum(seg_len)
    jpad = jnp.arange(n_rows - n * TOP_K, dtype=I32)
    shift = seg_first - (seg_end - seg_len)
    step = (shift[1:] - shift[:-1])[:, None]
    pad_rows = (jpad + shift[0] + jnp.sum(jnp.where(jpad[None, :] >= seg_end[:-1, None], step, 0), axis=0)).astype(I32)

    xbs = [_sc_scatter_rows(part.reshape(n, part.shape[-1]), dests, pad_rows, n_rows) for part in hfps]
    yb = _expert_ffn(block_e, block_i, xbs, wg, wu, wd, layer)
    return [_sc_gather_rows(part, dest_by_slot) for part in yb]


def _apply_moe(x2, route, ytoks, out_prev, b0, b_full):
    b, s, d = x2.shape
    n = b * s
    prev = None if out_prev is None else out_prev.reshape(b_full * s, d)
    out = _combine(x2.reshape(n, d), route.reshape(n, ROUTE_LANES), ytoks, min(COMBINE_TOKENS, n), prev,
                   b0 * s, b_full * s)
    return out.reshape(b_full, s, d)


def _rope_lane_freq():
    inv = ROPE_THETA ** (-jnp.arange(0, ROPE_DIM // 2, dtype=F32) * 2.0 / ROPE_DIM)
    idx = np.full((HEAD_LANES,), -1, np.int64)
    for r in range(ROPE_DIM):
        idx[_head_lane(NOPE_DIM + r)] = r % ROPE_HALF
    return _gather_cols(inv[None, :], idx)


FAST_SOFTMAX_MAX_LOG2 = 60.0


def _score_bound_log2(qg, kg):
    return 1.02 * LOG2E * QK_DIM ** 0.5 * jnp.max(jnp.abs(qg)) * jnp.max(jnp.abs(kg))


def _partner_lanes(idx):
    out = np.full_like(idx, -1)
    for base in range(0, idx.shape[0], HEAD_LANES):
        for r in range(ROPE_DIM):
            lane = _head_lane(NOPE_DIM + r)
            out[base + lane] = idx[base + (lane + HALF_LANES) % HEAD_LANES]
    return out


def kernel(x, mem, positions, ln_mix, w_in, q_lat_norm, w_uq, kv_lat_norm, w_ukv, q_norm, k_norm, conv_w, conv_b,
           dt_bias, a_log, d_skip, ssd_norm, w_out, pool_w, pool_b, pool_scale, ln_xq, ln_mem, xq_w, xkv_w, xq_norm,
           xk_norm, xo_w, ln_ffn, rg_w, rg_b, re_w, re_b, exp_w_gate, exp_w_up, exp_w_down):
    b, s, d = x.shape
    depth = ln_mix.shape[0]
    tm = min(1024, s)
    tq = min(512, s)
    assert d == D_MODEL and s % tm == 0 and s % CHUNK == 0 and tm >= POOL_CARRY

    pos = positions.astype(F32)[..., None]
    invf = _rope_lane_freq()
    hsum = jnp.asarray(np.kron(np.eye(X_HEADS), np.ones((X_HEAD_DIM, X_HEAD_DIM))), BF16)
    ltri = jnp.asarray(np.tril(np.ones((tm, tm)), -1), BF16)
    row2 = lambda v: v.reshape(1, -1)
    lane_pad = lambda v: jnp.pad(v, (0, LANES - v.shape[0])).reshape(1, LANES)

    n_streams = MOE_STREAMS if b % MOE_STREAMS == 0 else 1
    nb = b // n_streams
    nt = s // tm
    streams = [dict(x=x, off=k * nb, pending=None) for k in range(n_streams)]
    for layer in range(depth):
        j = layer // 2
        kbd, vbd = _mem_kv(mem, row2(ln_mem[layer]), xkv_w[layer].astype(BF16),
                           row2(jnp.tile(xk_norm[layer], X_HEADS)), hsum)
        rw = jnp.pad(jnp.concatenate([rg_w[layer], re_w[layer]], axis=1),
                     ((0, 0), (0, ROUTE_LANES - MOE_GROUPS - N_EXPERTS)))
        rw_hi = rw.astype(BF16)
        rw_lo = (rw - rw_hi.astype(F32)).astype(BF16)
        rb = lane_pad(jnp.concatenate([rg_b[layer], re_b[layer]]))
        tail_args = [kbd, vbd, row2(ln_xq[layer]), xq_w[layer].astype(BF16), row2(jnp.tile(xq_norm[layer], X_HEADS)),
                     hsum, xo_w[layer].astype(BF16), row2(ln_ffn[layer]), rw_hi,
                     jnp.concatenate([rw_hi, rw_lo], axis=1), rb, ltri]
        row = lambda i, jj: (i, jj, 0)
        fixed2 = lambda i, jj: (0, 0)
        posts = []
        if layer % 2 == 0:
            x = streams[0]["x"]
            assert all(st["x"] is x and st["pending"] is None for st in streams)
            win = _gather_cols(w_in[j], _win_col_index()).astype(BF16)
            q_idx = _head_col_index(QK_DIM, 0, QK_DIM)
            wuq = _gather_cols(w_uq[j], q_idx).astype(BF16)
            wuq_p = _gather_cols(w_uq[j], _partner_lanes(q_idx)).astype(BF16)
            wuk = _gather_cols(w_ukv[j], _head_col_index(NOPE_DIM + V_DIM, 0, NOPE_DIM)).astype(BF16)
            v_idx = np.full((A_HEADS * HEAD_LANES,), -1, np.int64)
            for hd in range(A_HEADS):
                v_idx[hd * HEAD_LANES:hd * HEAD_LANES + V_DIM] = hd * (NOPE_DIM + V_DIM) + NOPE_DIM + np.arange(V_DIM)
            wuv = _gather_cols(w_ukv[j], v_idx).astype(BF16)
            bound = _score_bound_log2(q_norm[j], k_norm[j])
            koff = jnp.zeros((1, HEAD_LANES), F32).at[0, SCORE_PAD_LANE].set(-bound)
            gain_idx = _head_col_index(QK_DIM, 0, QK_DIM)[:HEAD_LANES]
            lane_consts = jnp.concatenate(
                [_gather_cols(g[None, :], idx) for g in (q_norm[j], k_norm[j])
                 for idx in (gain_idx, _partner_lanes(gain_idx))]
                + [koff, invf, jnp.zeros((2, HEAD_LANES), F32)], axis=0)
            q, k, v, z, xbc, misc = _front_even(
                x, pos, lane_consts, row2(ln_mix[layer]), win, row2(q_lat_norm[j]), wuq, wuq_p,
                row2(kv_lat_norm[j]), wuk, wuv, tm)
            attn = lax.cond(bound <= FAST_SOFTMAX_MAX_LOG2,
                            functools.partial(_attention, tq=tq, online=False),
                            functools.partial(_attention, tq=tq, online=True), q, k, v)
            y = _ssd(xbc, misc, z, conv_w[j], row2(conv_b[j]), lane_pad(dt_bias[j]), lane_pad(a_log[j]),
                     row2(jnp.repeat(d_skip[j], SSD_HEAD_DIM)), row2(ssd_norm[j]))
            half = A_HEADS * V_DIM
            wout = w_out[j].astype(BF16)
            for k, st in enumerate(streams):
                row_k = lambda i, jj, off=st["off"]: (i + off, jj, 0)
                front_args = [x, attn, y, wout]
                front_specs = [pl.BlockSpec((1, tm, d), row_k), pl.BlockSpec((1, tm, half), row_k),
                               pl.BlockSpec((1, tm, D_INNER), row_k), pl.BlockSpec((half + D_INNER, d), fixed2)]
                posts.append(_post("even", front_args, front_specs, tail_args, nb, s, tm, [], b0=k * nb))
        else:
            for k, st in enumerate(streams):
                row_k = lambda i, jj, off=st["off"]: (i + off, jj, 0)
                front_args, front_specs = [st["x"]], [pl.BlockSpec((1, tm, d), row_k)]
                if st["pending"] is not None:
                    route_prev, ytoks = st["pending"]
                    pw = ytoks[0].shape[1]
                    front_args += [route_prev]
                    front_specs += [pl.BlockSpec((1, tm, ROUTE_LANES), row)]
                    for ytok in ytoks:
                        front_args += [ytok, ytok]
                        front_specs += [pl.BlockSpec((tm, pw), lambda i, jj: (i * nt + jj, 0)),
                                        pl.BlockSpec((tm, pw), lambda i, jj: (i * nt + jj + nb * nt, 0))]
                front_args += [row2(ln_mix[layer]), pool_w[j].astype(BF16), row2(pool_b[j]), row2(pool_scale[j])]
                front_specs += [pl.BlockSpec((1, d), fixed2),
                                pl.BlockSpec((len(POOL_WINDOWS), POOL_GROUP, POOL_GROUP), lambda i, jj: (0, 0, 0)),
                                pl.BlockSpec((1, d), fixed2), pl.BlockSpec((1, d), fixed2)]
                posts.append(_post("pool" if st["pending"] is None else "pool_pending", front_args, front_specs,
                                   tail_args, nb, s, tm, [pltpu.VMEM((POOL_CARRY, d), F32)], b0=k * nb))
        streams = []
        for x2, *hfps, route, route_t, counts in posts:
            ytoks = _moe(nb * s, hfps, route_t, counts, exp_w_gate, exp_w_up, exp_w_down, layer)
            streams.append(dict(x=x2, off=0, pending=(route, ytoks)))
        if layer + 1 == depth or (layer + 1) % 2 == 0:
            out = None
            for k, st in enumerate(streams):
                out = _apply_moe(st["x"], *st["pending"], out, k * nb, b)
            streams = [dict(x=out, off=k * nb, pending=None) for k in range(n_streams)]
    return streams[0]["x"]
```
